```python
import numpy as np
import jax
import jax.numpy as jnp
from jax import lax

D_MODEL = 1024
BATCH = 4
SEQ = 8192
DEPTH = 1
DEC_BATCH = 16
DEC_SEQ = 16
PAST_LEN = 2048

CHUNK = 64
HEAD_DIM = 64
MIX_WIDTH = D_MODEL
RWKV_WIDTH = MIX_WIDTH // 2
FOX_WIDTH = MIX_WIDTH - RWKV_WIDTH
RWKV_HEADS = RWKV_WIDTH // HEAD_DIM
FOX_HEADS = FOX_WIDTH // HEAD_DIM
DECAY_RANK = 64
ICL_RANK = 64
GATE_RANK = 128
RWKV_SPLITS = (RWKV_WIDTH, RWKV_WIDTH, RWKV_WIDTH, DECAY_RANK, ICL_RANK, GATE_RANK)
RWKV_IN = sum(RWKV_SPLITS)
FOX_SPLITS = (FOX_WIDTH, FOX_WIDTH, FOX_WIDTH, FOX_WIDTH, FOX_HEADS)
FOX_IN = sum(FOX_SPLITS)
IN_WIDTH = RWKV_IN + FOX_IN
Q_BLOCK = 128
ATTN_SCALE = HEAD_DIM ** -0.5
N_EXPERTS = 256
N_GROUPS = 8
TOPK_GROUPS = 4
TOP_K = 8
EXPERT_FF = 256
SHARED_FF = 256
ROUTED_SCALE = 2.5
MOE_BLOCK = 128
MOE_BLOCK_SMALL = 8
PLE_DIM = 256
RMS_EPS = 1e-6
GN_EPS = 64e-5
L2_EPS = 1e-12
F32 = jnp.float32

kernel_name = "hymba_rwkv7_fox_moe_stream_step"


def _split(u, sizes):
    return jnp.split(u, np.cumsum(sizes)[:-1].tolist(), axis=-1)


def rmsnorm(x, g):
    xf = x.astype(F32)
    y = xf * lax.rsqrt(jnp.mean(xf * xf, axis=-1, keepdims=True) + RMS_EPS)
    return (y * g.astype(F32)).astype(x.dtype)


def wkv7_scan(S0, r, w, k, v, kk, b):
    def step(S, inp):
        r_t, w_t, k_t, v_t, kk_t, b_t = inp
        sa = jnp.einsum('bhvk,bhk->bhv', S, kk_t)
        S = S * w_t[:, :, None, :] - sa[..., None] * b_t[:, :, None, :] + v_t[..., None] * k_t[:, :, None, :]
        return S, jnp.einsum('bhvk,bhk->bhv', S, r_t)
    xs = tuple(jnp.swapaxes(t, 0, 1) for t in (r, w, k, v, kk, b))
    S, y = lax.scan(step, S0, xs)
    return S, jnp.swapaxes(y, 0, 1)


def rwkv7_group(u, shift_prev, S0, mu, w0, w_up, a0, a_up, g_up, k_k, k_a, r_k, ln_w, ln_b):
    nb, T, _ = u.shape
    prev = jnp.concatenate([shift_prev.astype(u.dtype), u[:, :-1]], axis=1)
    xs = u + (prev - u) * mu
    r, k, v, xw, xa, xg = _split(xs, RWKV_SPLITS)
    heads = lambda t: t.astype(F32).reshape(nb, T, RWKV_HEADS, HEAD_DIM)
    w_raw = (w0 + jnp.tanh(xw) @ w_up).astype(F32)
    decay = jnp.exp(-jnp.exp(-jax.nn.softplus(-w_raw) - 0.5))
    a = jax.nn.sigmoid((a0 + xa @ a_up).astype(F32))
    g = (jax.nn.sigmoid(xg) @ g_up).astype(F32)
    r_h, v_h, a_h = heads(r), heads(v), heads(a)
    kk = heads(k * k_k)
    kk = kk / jnp.maximum(jnp.sqrt(jnp.sum(kk * kk, axis=-1, keepdims=True)), L2_EPS)
    k_h = heads(k) * (1.0 + (a_h - 1.0) * k_a.astype(F32).reshape(RWKV_HEADS, HEAD_DIM))
    S_fin, y = wkv7_scan(S0.astype(F32), r_h, heads(decay), k_h, v_h, kk, kk * a_h)
    mean = jnp.mean(y, axis=-1, keepdims=True)
    var = jnp.mean(jnp.square(y - mean), axis=-1, keepdims=True)
    y = ((y - mean) * lax.rsqrt(var + GN_EPS)).reshape(nb, T, RWKV_WIDTH) * ln_w.astype(F32) + ln_b.astype(F32)
    bonus = jnp.sum(r_h * k_h * r_k.astype(F32), axis=-1, keepdims=True) * v_h
    out = (y + bonus.reshape(nb, T, RWKV_WIDTH)) * g
    return out.astype(u.dtype), S_fin, u[:, -1:]


def fox_project(u, q_norm, k_norm, f_bias):
    nb, T, _ = u.shape
    q, k, v, og, fl = _split(u, FOX_SPLITS)
    hsplit = lambda t: t.reshape(nb, T, FOX_HEADS, HEAD_DIM)
    q = rmsnorm(hsplit(q), q_norm)
    k = rmsnorm(hsplit(k), k_norm)
    logf = jax.nn.log_sigmoid(fl.astype(F32) + f_bias.astype(F32))
    return q, k, hsplit(v), logf, og


def fox_attend_prompt(q, k, v, logf):
    nb, S, H, Dh = q.shape
    cT = jnp.cumsum(logf, axis=1).transpose(0, 2, 1)
    n_blk = S // Q_BLOCK
    qb = q.reshape(nb, n_blk, Q_BLOCK, H, Dh).swapaxes(0, 1)
    cqb = cT.reshape(nb, H, n_blk, Q_BLOCK).transpose(2, 0, 1, 3)
    kf, vf = k.astype(F32), v.astype(F32)
    kpos = jnp.arange(S)

    def one(args):
        qi, ci, bi = args
        s = jnp.einsum('bqhd,bkhd->bhqk', qi.astype(F32), kf) * ATTN_SCALE
        s = s + ci[..., None] - cT[:, :, None, :]
        qpos = bi * Q_BLOCK + jnp.arange(Q_BLOCK)
        s = jnp.where(kpos[None, :] <= qpos[:, None], s, -jnp.inf)
        return jnp.einsum('bhqk,bkhd->bqhd', jax.nn.softmax(s, axis=-1), vf)

    o = lax.map(one, (qb, cqb, jnp.arange(n_blk)))
    return o.swapaxes(0, 1).reshape(nb, S, H * Dh)


def fox_attend_cached(q, k, v, logf, k_past, v_past, logf_past):
    nb, T, H, Dh = q.shape
    P = k_past.shape[1]
    k_all = jnp.concatenate([k_past.astype(F32), k.astype(F32)], axis=1)
    v_all = jnp.concatenate([v_past.astype(F32), v.astype(F32)], axis=1)
    c = jnp.cumsum(jnp.concatenate([logf_past.astype(F32), logf], axis=1), axis=1).transpose(0, 2, 1)
    s = jnp.einsum('bqhd,bkhd->bhqk', q.astype(F32), k_all) * ATTN_SCALE
    s = s + c[:, :, P:, None] - c[:, :, None, :]
    kpos = jnp.arange(P + T)
    qpos = P + jnp.arange(T)
    s = jnp.where(kpos[None, :] <= qpos[:, None], s, -jnp.inf)
    return jnp.einsum('bhqk,bkhd->bqhd', jax.nn.softmax(s, axis=-1), v_all).reshape(nb, T, H * Dh)


def moe_ffn(x, router_w, router_bias, e_gate, e_up, e_down, s_gate, s_up, s_down):
    shp = x.shape
    xt = x.reshape(-1, D_MODEL)
    T = xt.shape[0]
    scores = jax.nn.sigmoid(xt.astype(F32) @ router_w.astype(F32))
    sel = scores + router_bias.astype(F32)
    grp = sel.reshape(T, N_GROUPS, N_EXPERTS // N_GROUPS)
    grp_score = jnp.sum(lax.top_k(grp, 2)[0], axis=-1)
    _, gidx = lax.top_k(grp_score, TOPK_GROUPS)
    gmask = jnp.sum(jax.nn.one_hot(gidx, N_GROUPS, dtype=F32), axis=1) > 0
    emask = jnp.repeat(gmask, N_EXPERTS // N_GROUPS, axis=1)
    _, eidx = lax.top_k(jnp.where(emask, sel, -jnp.inf), TOP_K)
    wts = jnp.take_along_axis(scores, eidx, axis=1)
    wts = wts / jnp.sum(wts, axis=-1, keepdims=True) * ROUTED_SCALE
    n_assign = T * TOP_K
    blk = MOE_BLOCK if n_assign >= MOE_BLOCK * N_EXPERTS else MOE_BLOCK_SMALL
    n_blocks = -(-n_assign // blk) + N_EXPERTS
    flat_e = eidx.reshape(-1)
    order = jnp.argsort(flat_e)
    se = flat_e[order]
    stok = (order // TOP_K).astype(jnp.int32)
    sw = wts.reshape(-1)[order]
    counts = jnp.bincount(flat_e, length=N_EXPERTS)
    padded = (counts + blk - 1) // blk * blk
    pend = jnp.cumsum(padded)
    pstart = pend - padded
    cstart = jnp.cumsum(counts) - counts
    dest = pstart[se] + jnp.arange(n_assign) - cstart[se]
    rows_tok = jnp.zeros((n_blocks * blk,), jnp.int32).at[dest].set(stok)
    rows_w = jnp.zeros((n_blocks * blk,), F32).at[dest].set(sw)
    blk_e = jnp.minimum(jnp.searchsorted(pend, jnp.arange(n_blocks) * blk, side='right'), N_EXPERTS - 1)

    def body(acc, inp):
        e, tok, w = inp
        xe = xt[tok]
        hdn = jax.nn.silu(xe @ e_gate[e]) * (xe @ e_up[e])
        return acc.at[tok].add((hdn @ e_down[e]).astype(F32) * w[:, None]), None

    routed, _ = lax.scan(body, jnp.zeros((T, D_MODEL), F32),
                         (blk_e, rows_tok.reshape(n_blocks, blk), rows_w.reshape(n_blocks, blk)))
    shared = (jax.nn.silu(xt @ s_gate) * (xt @ s_up)) @ s_down
    return (routed + shared.astype(F32)).astype(x.dtype).reshape(shp)


def hybrid_layer(h, p_l, S0, shift0, fox_past,
                 norm_mix_g, w_in, rwkv_mu, rwkv_w0, rwkv_w_up, rwkv_a0, rwkv_a_up, rwkv_g_up,
                 rwkv_k_k, rwkv_k_a, rwkv_r_k, rwkv_ln_w, rwkv_ln_b,
                 fox_q_norm, fox_k_norm, fox_f_bias, w_out,
                 norm_ffn_g, router_w, router_bias, exp_w_gate, exp_w_up, exp_w_down,
                 shared_w_gate, shared_w_up, shared_w_down,
                 ple_norm_g, ple_w_gate, ple_w_proj):
    u = rmsnorm(h, norm_mix_g) @ w_in
    u_rwkv, u_fox = u[..., :RWKV_IN], u[..., RWKV_IN:]
    o_rwkv, S_fin, shift_new = rwkv7_group(u_rwkv, shift0, S0, rwkv_mu, rwkv_w0, rwkv_w_up, rwkv_a0,
                                           rwkv_a_up, rwkv_g_up, rwkv_k_k, rwkv_k_a, rwkv_r_k,
                                           rwkv_ln_w, rwkv_ln_b)
    q, k, v, logf, og = fox_project(u_fox, fox_q_norm, fox_k_norm, fox_f_bias)
    if fox_past is None:
        o = fox_attend_prompt(q, k, v, logf)
    else:
        o = fox_attend_cached(q, k, v, logf, fox_past[0], fox_past[1], fox_past[2])
    o_fox = (o * jax.nn.sigmoid(og.astype(F32))).astype(h.dtype)
    h = h + jnp.concatenate([o_rwkv, o_fox], axis=-1) @ w_out
    h = h + moe_ffn(rmsnorm(h, norm_ffn_g), router_w, router_bias, exp_w_gate, exp_w_up, exp_w_down,
                    shared_w_gate, shared_w_up, shared_w_down)
    gate = jax.nn.sigmoid(rmsnorm(h, ple_norm_g) @ ple_w_gate)
    h = h + gate * (p_l @ ple_w_proj)
    return h, (k, v, logf, S_fin, shift_new)


def setup_inputs(seed: int = 0) -> dict:
    key = jax.random.key(seed)
    ks = iter(jax.random.split(key, 48))
    nrm = lambda shape, scale: jax.random.normal(next(ks), shape, F32) * scale
    uni = lambda shape, lo, hi: jax.random.uniform(next(ks), shape, F32, lo, hi)
    L = DEPTH
    return {
        'x_prompt': nrm((BATCH, SEQ, D_MODEL), 1.0),
        'x_sample': nrm((DEC_BATCH, DEC_SEQ, D_MODEL), 1.0),
        'cache_fox_k': nrm((L, DEC_BATCH, PAST_LEN, FOX_HEADS, HEAD_DIM), 1.0),
        'cache_fox_v': nrm((L, DEC_BATCH, PAST_LEN, FOX_HEADS, HEAD_DIM), 1.0),
        'cache_fox_logf': jax.nn.log_sigmoid(nrm((L, DEC_BATCH, PAST_LEN, FOX_HEADS), 1.0) + 4.0),
        'state_rwkv_wkv': nrm((L, DEC_BATCH, RWKV_HEADS, HEAD_DIM, HEAD_DIM), 0.3),
        'state_rwkv_shift': nrm((L, DEC_BATCH, 1, RWKV_IN), 1.0),
        'p_prompt': nrm((L, BATCH, SEQ, PLE_DIM), 1.0),
        'p_sample': nrm((L, DEC_BATCH, DEC_SEQ, PLE_DIM), 1.0),
        'norm_mix_g': 1.0 + nrm((L, D_MODEL), 0.02),
        'w_in': nrm((L, D_MODEL, IN_WIDTH), D_MODEL ** -0.5),
        'rwkv_mu': uni((L, RWKV_IN), 0.1, 0.9),
        'rwkv_w0': uni((L, RWKV_WIDTH), -6.0, 1.0),
        'rwkv_w_up': nrm((L, DECAY_RANK, RWKV_WIDTH), 0.5 * DECAY_RANK ** -0.5),
        'rwkv_a0': nrm((L, RWKV_WIDTH), 0.5),
        'rwkv_a_up': nrm((L, ICL_RANK, RWKV_WIDTH), ICL_RANK ** -0.5),
        'rwkv_g_up': nrm((L, GATE_RANK, RWKV_WIDTH), GATE_RANK ** -0.5),
        'rwkv_k_k': 0.85 + nrm((L, RWKV_WIDTH), 0.05),
        'rwkv_k_a': 1.0 + nrm((L, RWKV_WIDTH), 0.05),
        'rwkv_r_k': nrm((L, RWKV_HEADS, HEAD_DIM), 0.1),
        'rwkv_ln_w': 1.0 + nrm((L, RWKV_WIDTH), 0.02),
        'rwkv_ln_b': nrm((L, RWKV_WIDTH), 0.02),
        'fox_q_norm': 1.0 + nrm((L, HEAD_DIM), 0.02),
        'fox_k_norm': 1.0 + nrm((L, HEAD_DIM), 0.02),
        'fox_f_bias': uni((L, FOX_HEADS), 1.0, 6.0),
        'w_out': nrm((L, MIX_WIDTH, D_MODEL), MIX_WIDTH ** -0.5),
        'norm_ffn_g': 1.0 + nrm((L, D_MODEL), 0.02),
        'router_w': nrm((L, D_MODEL, N_EXPERTS), D_MODEL ** -0.5),
        'router_bias': nrm((L, N_EXPERTS), 0.01),
        'exp_w_gate': nrm((L, N_EXPERTS, D_MODEL, EXPERT_FF), D_MODEL ** -0.5),
        'exp_w_up': nrm((L, N_EXPERTS, D_MODEL, EXPERT_FF), D_MODEL ** -0.5),
        'exp_w_down': nrm((L, N_EXPERTS, EXPERT_FF, D_MODEL), 0.5 * EXPERT_FF ** -0.5),
        'shared_w_gate': nrm((L, D_MODEL, SHARED_FF), D_MODEL ** -0.5),
        'shared_w_up': nrm((L, D_MODEL, SHARED_FF), D_MODEL ** -0.5),
        'shared_w_down': nrm((L, SHARED_FF, D_MODEL), SHARED_FF ** -0.5),
        'ple_norm_g': 1.0 + nrm((L, D_MODEL), 0.02),
        'ple_w_gate': nrm((L, D_MODEL, D_MODEL), D_MODEL ** -0.5),
        'ple_w_proj': nrm((L, PLE_DIM, D_MODEL), PLE_DIM ** -0.5),
        'final_norm_g': 1.0 + nrm((D_MODEL,), 0.02),
    }


def reference(x_prompt, x_sample, cache_fox_k, cache_fox_v, cache_fox_logf, state_rwkv_wkv, state_rwkv_shift,
              p_prompt, p_sample,
              norm_mix_g, w_in, rwkv_mu, rwkv_w0, rwkv_w_up, rwkv_a0, rwkv_a_up, rwkv_g_up,
              rwkv_k_k, rwkv_k_a, rwkv_r_k, rwkv_ln_w, rwkv_ln_b,
              fox_q_norm, fox_k_norm, fox_f_bias, w_out,
              norm_ffn_g, router_w, router_bias, exp_w_gate, exp_w_up, exp_w_down,
              shared_w_gate, shared_w_up, shared_w_down,
              ple_norm_g, ple_w_gate, ple_w_proj, final_norm_g):
    layer_weights = (norm_mix_g, w_in, rwkv_mu, rwkv_w0, rwkv_w_up, rwkv_a0, rwkv_a_up, rwkv_g_up,
                     rwkv_k_k, rwkv_k_a, rwkv_r_k, rwkv_ln_w, rwkv_ln_b,
                     fox_q_norm, fox_k_norm, fox_f_bias, w_out,
                     norm_ffn_g, router_w, router_bias, exp_w_gate, exp_w_up, exp_w_down,
                     shared_w_gate, shared_w_up, shared_w_down,
                     ple_norm_g, ple_w_gate, ple_w_proj)
    bp = x_prompt.shape[0]
    S0_prompt = jnp.zeros((bp, RWKV_HEADS, HEAD_DIM, HEAD_DIM), F32)
    shift0_prompt = jnp.zeros((bp, 1, RWKV_IN), x_prompt.dtype)
    hp, hs = x_prompt, x_sample
    new_p, new_s = [], []
    for i in range(DEPTH):
        lw = [w[i] for w in layer_weights]
        hp, st_p = hybrid_layer(hp, p_prompt[i], S0_prompt, shift0_prompt, None, *lw)
        hs, st_s = hybrid_layer(hs, p_sample[i], state_rwkv_wkv[i], state_rwkv_shift[i],
                                (cache_fox_k[i], cache_fox_v[i], cache_fox_logf[i]), *lw)
        new_p.append(st_p)
        new_s.append(st_s)
    y_prompt = rmsnorm(hp, final_norm_g)
    y_sample = rmsnorm(hs, final_norm_g)
    fox_k_prompt = jnp.stack([s[0] for s in new_p])
    fox_v_prompt = jnp.stack([s[1] for s in new_p])
    fox_logf_prompt = jnp.stack([s[2] for s in new_p])
    rwkv_wkv_prompt = jnp.stack([s[3] for s in new_p])
    rwkv_shift_prompt = jnp.stack([s[4] for s in new_p])
    fox_k_sample = jnp.stack([s[0] for s in new_s])
    fox_v_sample = jnp.stack([s[1] for s in new_s])
    fox_logf_sample = jnp.stack([s[2] for s in new_s])
    rwkv_wkv_sample = jnp.stack([s[3] for s in new_s])
    rwkv_shift_sample = jnp.stack([s[4] for s in new_s])
    return (y_prompt, y_sample,
            fox_k_prompt, fox_v_prompt, fox_logf_prompt, rwkv_wkv_prompt, rwkv_shift_prompt,
            fox_k_sample, fox_v_sample, fox_logf_sample, rwkv_wkv_sample, rwkv_shift_sample)
```

```python
import functools
import math

import numpy as np
import jax
import jax.numpy as jnp
from jax import lax
from jax.experimental import pallas as pl
from jax.experimental.pallas import tpu as pltpu

F32 = jnp.float32
BF16 = jnp.bfloat16

D_MODEL = 1024
HEAD_DIM = 64
RWKV_WIDTH = 512
FOX_WIDTH = 512
N_HEADS = 8
N_PAIRS = N_HEADS // 2
DECAY_RANK = 64
ICL_RANK = 64
GATE_RANK = 128
RWKV_IN = 3 * RWKV_WIDTH + DECAY_RANK + ICL_RANK + GATE_RANK
RWKV_PAD = 3 * RWKV_WIDTH + 3 * 128
FOX_PAD = 4 * FOX_WIDTH + 128
ATTN_SCALE = HEAD_DIM ** -0.5
N_EXPERTS = 256
N_GROUPS = 8
GROUP_SIZE = N_EXPERTS // N_GROUPS
TOPK_GROUPS = 4
TOP_K = 8
EXPERT_FF = 256
ROUTED_SCALE = 2.5
PLE_DIM = 256
RMS_EPS = 1e-6
GN_EPS = 64e-5
L2_EPS = 1e-12

LANES = 128
RWKV_CHUNK = 128
MOE_ROWS = 128
VMEM_LIMIT = 56 * 1024 * 1024


def _cparams(*sem):
    return pltpu.CompilerParams(dimension_semantics=sem, vmem_limit_bytes=VMEM_LIMIT)


def _bf(x):
    return x.astype(BF16)


def _dot(a, b):
    return jnp.dot(a, b, preferred_element_type=F32)


def _dot_nt(a, b):
    return lax.dot_general(a, b, (((1,), (1,)), ((), ())), preferred_element_type=F32)


def _split2(x):
    hi = _bf(x)
    return hi, _bf(x - hi.astype(F32))


def _split3(x):
    hi = _bf(x)
    r1 = x - hi.astype(F32)
    mid = _bf(r1)
    return hi, mid, _bf(r1 - mid.astype(F32))


def _dot_x01(x, w01):
    hi, lo = _split2(x)
    return _dot(hi, w01) + _dot(lo, w01)


def _dot3(a, b):
    ah, al = _split2(a)
    bh, bl = _split2(b)
    return _dot(ah, bh) + _dot(al, bh) + _dot(ah, bl)


def _dot3_nt(a, b):
    ah, al = _split2(a)
    bh, bl = _split2(b)
    return _dot_nt(ah, bh) + _dot_nt(al, bh) + _dot_nt(ah, bl)


def _softplus(x):
    return jnp.maximum(x, 0.0) + jnp.log1p(jnp.exp(-jnp.abs(x)))


def _rms(x, g):
    return x * lax.rsqrt(jnp.mean(x * x, axis=-1, keepdims=True) + RMS_EPS) * g


def _full(shape):
    return pl.BlockSpec(shape, lambda *_: (0,) * len(shape))


def _in_rwkv_kernel(x_ref, shift_ref, gmix_ref, w_ref, mu_ref, w0_ref, wup_ref, a0_ref, aup_ref, gup_ref,
                    kk_ref, ka_ref, rk_ref, gs_ref,
                    r_out, lw_out, kh_out, v_out, kkn_out, bb_out, g_out, bonus_out, last_out,
                    carry_ref):
    j = pl.program_id(1)
    tm = x_ref.shape[1]
    xn = _bf(_rms(x_ref[0], gmix_ref[...]))
    u = _dot(xn, w_ref[...])
    first = jnp.where(j == 0, shift_ref[0], carry_ref[...])
    row = lax.broadcasted_iota(jnp.int32, (tm, 1), 0)
    prev = jnp.where(row == 0, first, pltpu.roll(u, 1, axis=0))
    carry_ref[...] = u[tm - 1:tm, :]
    last_out[0] = u[tm - 1:tm, :]
    xs = u + (prev - u) * mu_ref[...]
    W = RWKV_WIDTH
    r, k, v = xs[:, :W], xs[:, W:2 * W], xs[:, 2 * W:3 * W]
    xw, xa, xg = xs[:, 3 * W:3 * W + 128], xs[:, 3 * W + 128:3 * W + 256], xs[:, 3 * W + 256:]
    w_raw = w0_ref[...] + _dot(_bf(jnp.tanh(xw)), wup_ref[...])
    lw = -jnp.exp(-_softplus(-w_raw) - 0.5)
    a = jax.nn.sigmoid(a0_ref[...] + _dot(_bf(xa), aup_ref[...]))
    g = _dot(_bf(jax.nn.sigmoid(xg)), gup_ref[...])
    gs = gs_ref[...]
    kk = k * kk_ref[...]
    kkn = kk / jnp.maximum(jnp.sqrt(_dot_x01(kk * kk, gs)), L2_EPS)
    kh = k * (1.0 + (a - 1.0) * ka_ref[...])
    r_out[0] = r
    lw_out[0] = lw
    kh_out[0] = kh
    v_out[0] = v
    kkn_out[0] = kkn
    bb_out[0] = kkn * a
    g_out[0] = g
    bonus_out[0] = _dot_x01(r * kh * rk_ref[...], gs) * v


def _in_rwkv(x, shift, gmix, w, mu, w0, wup, a0, aup, gup, k_k, k_a, r_k, gs, tm):
    nb, T, _ = x.shape
    W = RWKV_WIDTH
    tok = lambda width: pl.BlockSpec((1, tm, width), lambda b, j: (b, j, 0))
    outs = [jax.ShapeDtypeStruct((nb, T, W), F32)] * 8 + [jax.ShapeDtypeStruct((nb, 1, RWKV_PAD), F32)]
    return pl.pallas_call(
        _in_rwkv_kernel,
        grid=(nb, T // tm),
        in_specs=[tok(D_MODEL), pl.BlockSpec((1, 1, RWKV_PAD), lambda b, j: (b, 0, 0)),
                  _full((1, D_MODEL)), _full((D_MODEL, RWKV_PAD)), _full((1, RWKV_PAD)),
                  _full((1, W)), _full((128, W)), _full((1, W)), _full((128, W)), _full((128, W)),
                  _full((1, W)), _full((1, W)), _full((1, W)), _full((W, W))],
        out_specs=[tok(W)] * 8 + [pl.BlockSpec((1, 1, RWKV_PAD), lambda b, j: (b, 0, 0))],
        out_shape=outs,
        scratch_shapes=[pltpu.VMEM((1, RWKV_PAD), F32)],
        compiler_params=_cparams("parallel", "arbitrary"),
        name="in_rwkv",
    )(x, shift, gmix, w, mu, w0, wup, a0, aup, gup, k_k, k_a, r_k, gs)


def _in_fox_kernel(x_ref, gmix_ref, w_ref, qn_ref, kn_ref, fb_ref, gs_ref,
                   q_out, k_out, kb_out, v_out, vb_out, og_out, lf_out):
    xn = _bf(_rms(x_ref[0], gmix_ref[...]))
    u = _dot(xn, w_ref[...])
    W = FOX_WIDTH
    q, k, v, og, fl = u[:, :W], u[:, W:2 * W], u[:, 2 * W:3 * W], u[:, 3 * W:4 * W], u[:, 4 * W:]
    gs = gs_ref[...]
    inv_d = 1.0 / HEAD_DIM
    qn = q * lax.rsqrt(_dot_x01(q * q, gs) * inv_d + RMS_EPS) * qn_ref[...]
    kn = k * lax.rsqrt(_dot_x01(k * k, gs) * inv_d + RMS_EPS) * kn_ref[...]
    q_out[0] = _bf(qn * ATTN_SCALE)
    k_out[0] = kn
    kb_out[0] = _bf(kn)
    v_out[0] = v
    vb_out[0] = _bf(v)
    og_out[0] = og
    lf_out[0] = -_softplus(-(fl + fb_ref[...]))


def _in_fox(x, gmix, w, qn, kn, fb, gs, tm):
    nb, T, _ = x.shape
    W = FOX_WIDTH
    tok = lambda width: pl.BlockSpec((1, tm, width), lambda b, j: (b, j, 0))
    sds = lambda width, dt: jax.ShapeDtypeStruct((nb, T, width), dt)
    return pl.pallas_call(
        _in_fox_kernel,
        grid=(nb, T // tm),
        in_specs=[tok(D_MODEL), _full((1, D_MODEL)), _full((D_MODEL, FOX_PAD)),
                  _full((1, W)), _full((1, W)), _full((1, 128)), _full((W, W))],
        out_specs=[tok(W)] * 6 + [tok(128)],
        out_shape=[sds(W, BF16), sds(W, F32), sds(W, BF16), sds(W, F32), sds(W, BF16), sds(W, F32),
                   sds(128, F32)],
        compiler_params=_cparams("parallel", "parallel"),
        name="in_fox",
    )(x, gmix, w, qn, kn, fb, gs)


def _cumsum_kernel(x_ref, tri_ref, o_ref, carry_ref):
    j = pl.program_id(1)
    ts = x_ref.shape[1]

    @pl.when(j == 0)
    def _():
        carry_ref[...] = jnp.zeros_like(carry_ref)

    hi, mid, lo = _split3(x_ref[0])
    tri = tri_ref[...]
    c = _dot(tri, hi) + _dot(tri, mid) + _dot(tri, lo) + carry_ref[...]
    o_ref[0] = c
    carry_ref[...] = c[ts - 1:ts, :]


def _cumsum(x, ts):
    nb, S, L = x.shape
    tri = _bf(jnp.tril(jnp.ones((ts, ts), F32)))
    return pl.pallas_call(
        _cumsum_kernel,
        grid=(nb, S // ts),
        in_specs=[pl.BlockSpec((1, ts, L), lambda b, j: (b, j, 0)), _full((ts, ts))],
        out_specs=pl.BlockSpec((1, ts, L), lambda b, j: (b, j, 0)),
        out_shape=jax.ShapeDtypeStruct((nb, S, L), F32),
        scratch_shapes=[pltpu.VMEM((1, L), F32)],
        compiler_params=_cparams("parallel", "arbitrary"),
        name="seq_cumsum",
    )(x, tri)


def _rwkv_chunk_kernel(r_ref, lw_ref, kh_ref, v_ref, kk_ref, bb_ref, tri_ref,
                       rh_out, yh_out, g_out, sh_out):
    C = r_ref.shape[1]
    ti = lax.broadcasted_iota(jnp.int32, (C, C), 0)
    si = lax.broadcasted_iota(jnp.int32, (C, C), 1)
    tx = ti ^ si
    strict = ti > si
    incl = ti >= si
    eye_c = (ti == si).astype(F32)
    lane = lax.broadcasted_iota(jnp.int32, (1, LANES), 1)
    head0 = lane < HEAD_DIM
    pi = lax.broadcasted_iota(jnp.int32, (LANES, LANES), 0)
    pj = lax.broadcasted_iota(jnp.int32, (LANES, LANES), 1)
    same_head = (pi < HEAD_DIM) == (pj < HEAD_DIM)
    eye_p = (pi == pj).astype(F32)
    tri = tri_ref[...]

    for p in range(N_PAIRS):
        sl = slice(p * LANES, (p + 1) * LANES)
        r, lw, kh, v, kk, bb = (ref[0, :, sl] for ref in (r_ref, lw_ref, kh_ref, v_ref, kk_ref, bb_ref))
        l_hi, l_mid, l_lo = _split3(lw)
        lc = _dot(tri, l_hi) + _dot(tri, l_mid) + _dot(tri, l_lo)
        mid = lc[C // 2 - 1:C // 2, :]
        last = lc[C - 1:C, :]
        e_dn = jnp.exp(mid - lc)
        e_up = jnp.exp(lc - mid)
        e_end = jnp.exp(last - lc)
        aa = kk * jnp.exp(lc - lw - mid)
        rt = r * e_up
        bt_b, kt_b = _bf(bb * e_dn), _bf(kh * e_dn)
        bc_b, kc_b = _bf(bb * e_end), _bf(kh * e_end)
        rho = jnp.exp(mid)
        aa_b, v_b = _bf(aa), _bf(v)
        per_head = []
        for h in range(2):
            hm = head0 if h == 0 else jnp.logical_not(head0)
            aa_m = _bf(jnp.where(hm, aa, 0.0))
            rt_m = _bf(jnp.where(hm, rt, 0.0))
            lab = jnp.where(strict, _dot_nt(aa_m, bt_b), 0.0)
            lak = jnp.where(strict, _dot_nt(aa_m, kt_b), 0.0)
            mrb = _bf(jnp.where(incl, _dot_nt(rt_m, bt_b), 0.0))
            mrk = _bf(jnp.where(incl, _dot_nt(rt_m, kt_b), 0.0))
            d = eye_c - jnp.where(tx < 2, lab, 0.0)
            s = 2
            while s < C:
                ls = _bf(jnp.where((tx >= s) & (tx < 2 * s), lab, 0.0))
                d_b = _bf(d)
                d = d - _dot(_bf(_dot(d_b, ls)), d_b)
                s *= 2
            d_b = _bf(d)
            ah = _dot(d_b, aa_b) * rho
            uh = _dot(d_b, _bf(_dot(_bf(lak), v_b)))
            rh = rt * rho - _dot(mrb, _bf(ah))
            yh = _dot(mrk, v_b) - _dot(mrb, _bf(uh))
            per_head.append((ah, uh, rh, yh))
        ah, uh, rh, yh = (jnp.where(head0, x0, x1) for x0, x1 in zip(*per_head))
        rh_out[0, :, sl] = rh
        yh_out[0, :, sl] = yh
        g_full = eye_p * jnp.exp(last) - _dot(_bf(ah.T), bc_b)
        sh_full = _dot(_bf(v.T), kc_b) - _dot(_bf(uh.T), bc_b)
        g_out[0, 0, p] = jnp.where(same_head, g_full, 0.0)
        sh_out[0, 0, p] = jnp.where(same_head, sh_full, 0.0)


def _rwkv_chunks(r, lw, kh, v, kk, bb):
    nb, T, W = r.shape
    C = RWKV_CHUNK
    nc = T // C
    tri = _bf(jnp.tril(jnp.ones((C, C), F32)))
    tok = pl.BlockSpec((1, C, W), lambda b, c: (b, c, 0))
    mat = pl.BlockSpec((1, 1, N_PAIRS, LANES, LANES), lambda b, c: (b, c, 0, 0, 0))
    mat_shape = jax.ShapeDtypeStruct((nb, nc, N_PAIRS, LANES, LANES), F32)
    return pl.pallas_call(
        _rwkv_chunk_kernel,
        grid=(nb, nc),
        in_specs=[tok] * 6 + [_full((C, C))],
        out_specs=[tok, tok, mat, mat],
        out_shape=[jax.ShapeDtypeStruct((nb, T, W), F32)] * 2 + [mat_shape] * 2,
        compiler_params=_cparams("parallel", "parallel"),
        name="rwkv_chunks",
    )(r, lw, kh, v, kk, bb, tri)


def _rwkv_scan_kernel(rh_ref, yh_ref, g_ref, sh_ref, s0_ref, y_out, sfin_out, s_scr):
    c = pl.program_id(1)
    nbg = rh_ref.shape[0]

    @pl.when(c == 0)
    def _():
        s_scr[...] = s0_ref[...]

    for b in range(nbg):
        for p in range(N_PAIRS):
            sl = slice(p * LANES, (p + 1) * LANES)
            s = s_scr[b, p]
            y_out[b, :, sl] = _dot3_nt(rh_ref[b, :, sl], s) + yh_ref[b, :, sl]
            s_scr[b, p] = _dot3(s, g_ref[b, 0, p]) + sh_ref[b, 0, p]

    @pl.when(c == pl.num_programs(1) - 1)
    def _():
        sfin_out[...] = s_scr[...]


def _rwkv_scan(rh, yh, g, sh, s0, nbg):
    nb, T, W = rh.shape
    C = RWKV_CHUNK
    nc = T // C
    tok = pl.BlockSpec((nbg, C, W), lambda i, c: (i, c, 0))
    mat = pl.BlockSpec((nbg, 1, N_PAIRS, LANES, LANES), lambda i, c: (i, c, 0, 0, 0))
    st = pl.BlockSpec((nbg, N_PAIRS, LANES, LANES), lambda i, c: (i, 0, 0, 0))
    return pl.pallas_call(
        _rwkv_scan_kernel,
        grid=(nb // nbg, nc),
        in_specs=[tok, tok, mat, mat, st],
        out_specs=[tok, st],
        out_shape=[jax.ShapeDtypeStruct((nb, T, W), F32),
                   jax.ShapeDtypeStruct((nb, N_PAIRS, LANES, LANES), F32)],
        scratch_shapes=[pltpu.VMEM((nbg, N_PAIRS, LANES, LANES), F32)],
        compiler_params=_cparams("parallel", "arbitrary"),
        name="rwkv_scan",
    )(rh, yh, g, sh, s0)


def _state_to_pairs(s):
    nb = s.shape[0]
    s = s.reshape(nb, N_PAIRS, 2, HEAD_DIM, HEAD_DIM)
    z = jnp.zeros_like(s[:, :, 0])
    top = jnp.concatenate([s[:, :, 0], z], axis=-1)
    bot = jnp.concatenate([z, s[:, :, 1]], axis=-1)
    return jnp.concatenate([top, bot], axis=-2)


def _pairs_to_state(sp):
    nb = sp.shape[0]
    a = sp[:, :, :HEAD_DIM, :HEAD_DIM]
    b = sp[:, :, HEAD_DIM:, HEAD_DIM:]
    return jnp.stack([a, b], axis=2).reshape(nb, N_HEADS, HEAD_DIM, HEAD_DIM)


def _fox_kernel(q_ref, k_ref, v_ref, cq_ref, ck_ref, o_ref, m_scr, l_scr, acc_scr, *, tk, q_off):
    i = pl.program_id(2)
    tq = q_ref.shape[1]
    lane = lax.broadcasted_iota(jnp.int32, (1, LANES), 1)
    head0 = lane < HEAD_DIM
    q = q_ref[0]
    zero = jnp.zeros_like(q)
    qh = (jnp.where(head0, q, zero), jnp.where(head0, zero, q))
    cq = cq_ref[0, 0]
    qpos = q_off + i * tq + lax.broadcasted_iota(jnp.int32, (tq, 1), 0)
    m_scr[...] = jnp.full_like(m_scr, -jnp.inf)
    l_scr[...] = jnp.zeros_like(l_scr)
    acc_scr[...] = jnp.zeros_like(acc_scr)
    n_kv = (q_off + (i + 1) * tq + tk - 1) // tk

    def body(j, carry):
        ks = pl.multiple_of(j * tk, tk)
        k = k_ref[0, pl.ds(ks, tk), :]
        v = v_ref[0, pl.ds(ks, tk), :]
        ck = ck_ref[0, 0, :, pl.ds(ks, tk)]
        visible = (ks + lax.broadcasted_iota(jnp.int32, (1, tk), 1)) <= qpos
        for h in range(2):
            s = _dot_nt(qh[h], k) + cq[:, h:h + 1] - ck[h:h + 1, :]
            s = jnp.where(visible, s, -jnp.inf)
            m_old = m_scr[h]
            m_new = jnp.maximum(m_old, jnp.max(s, axis=-1, keepdims=True))
            alpha = jnp.exp(m_old - m_new)
            pr = jnp.exp(s - m_new)
            l_scr[h] = alpha * l_scr[h] + jnp.sum(pr, axis=-1, keepdims=True)
            acc_scr[h] = alpha * acc_scr[h] + _dot(_bf(pr), v)
            m_scr[h] = m_new
        return carry

    lax.fori_loop(0, n_kv, body, 0)
    o_ref[0] = jnp.where(head0, acc_scr[0] / l_scr[0], acc_scr[1] / l_scr[1])


def _fox_attention(q, k, v, cq, ck, tq, tk, q_off):
    nb, sq, W = q.shape
    sk = k.shape[1]
    return pl.pallas_call(
        functools.partial(_fox_kernel, tk=tk, q_off=q_off),
        grid=(nb, N_PAIRS, sq // tq),
        in_specs=[pl.BlockSpec((1, tq, LANES), lambda b, p, i: (b, i, p)),
                  pl.BlockSpec((1, sk, LANES), lambda b, p, i: (b, 0, p)),
                  pl.BlockSpec((1, sk, LANES), lambda b, p, i: (b, 0, p)),
                  pl.BlockSpec((1, 1, tq, 2), lambda b, p, i: (b, p, i, 0)),
                  pl.BlockSpec((1, 1, 2, sk), lambda b, p, i: (b, p, 0, 0))],
        out_specs=pl.BlockSpec((1, tq, LANES), lambda b, p, i: (b, i, p)),
        out_shape=jax.ShapeDtypeStruct((nb, sq, W), F32),
        scratch_shapes=[pltpu.VMEM((2, tq, 1), F32), pltpu.VMEM((2, tq, 1), F32),
                        pltpu.VMEM((2, tq, LANES), F32)],
        compiler_params=_cparams("parallel", "parallel", "arbitrary"),
        name="fox_attention",
    )(q, k, v, cq, ck)


def _bias_layouts(c):
    nb, S, _ = c.shape
    c8 = c[:, :, :N_HEADS].reshape(nb, S, N_PAIRS, 2)
    return c8.transpose(0, 2, 1, 3), c8.transpose(0, 2, 3, 1)


def _out_kernel(x_ref, y_ref, bonus_ref, g_ref, oa_ref, og_ref, lnw_ref, lnb_ref, gs_ref, wout_ref,
                gffn_ref, rwh_ref, rwl_ref, h_out, xn_out, sc_out):
    gs = gs_ref[...]
    inv_d = 1.0 / HEAD_DIM
    y = y_ref[...]
    mean = _dot_x01(y, gs) * inv_d
    d = y - mean
    var = _dot_x01(d * d, gs) * inv_d
    yn = d * lax.rsqrt(var + GN_EPS) * lnw_ref[...] + lnb_ref[...]
    o_rwkv = (yn + bonus_ref[...]) * g_ref[...]
    o_fox = oa_ref[...] * jax.nn.sigmoid(og_ref[...])
    mix = jnp.concatenate([_bf(o_rwkv), _bf(o_fox)], axis=-1)
    h = x_ref[...] + _dot(mix, wout_ref[...])
    h_out[...] = h
    xn = _rms(h, gffn_ref[...])
    xn_out[...] = xn
    xh, xl = _split2(xn)
    logits = _dot(xh, rwh_ref[...]) + _dot(xl, rwh_ref[...]) + _dot(xh, rwl_ref[...])
    sc_out[...] = jax.nn.sigmoid(logits)


def _out_proj(x, y, bonus, g, oa, og, lnw, lnb, gs, wout, gffn, rwh, rwl, tm):
    T = x.shape[0]
    W = RWKV_WIDTH
    tok = lambda width: pl.BlockSpec((tm, width), lambda i: (i, 0))
    return pl.pallas_call(
        _out_kernel,
        grid=(T // tm,),
        in_specs=[tok(D_MODEL)] + [tok(W)] * 5 + [_full((1, W)), _full((1, W)), _full((W, W)),
                                                  _full((D_MODEL, D_MODEL)), _full((1, D_MODEL)),
                                                  _full((D_MODEL, N_EXPERTS)), _full((D_MODEL, N_EXPERTS))],
        out_specs=[tok(D_MODEL), tok(D_MODEL), tok(N_EXPERTS)],
        out_shape=[jax.ShapeDtypeStruct((T, D_MODEL), F32), jax.ShapeDtypeStruct((T, D_MODEL), F32),
                   jax.ShapeDtypeStruct((T, N_EXPERTS), F32)],
        compiler_params=_cparams("parallel"),
        name="out_proj",
    )(x, y, bonus, g, oa, og, lnw, lnb, gs, wout, gffn, rwh, rwl)


def _route_kernel(sc_ref, bias_ref, idx_out, wt_out):
    tm = sc_ref.shape[0]
    neg = -jnp.inf
    st = sc_ref[...].T
    sel = st + bias_ref[...]
    gscore = []
    for gi in range(N_GROUPS):
        blk = sel[gi * GROUP_SIZE:(gi + 1) * GROUP_SIZE, :]
        m1 = jnp.max(blk, axis=0, keepdims=True)
        n1 = jnp.sum((blk == m1).astype(F32), axis=0, keepdims=True)
        m2 = jnp.max(jnp.where(blk < m1, blk, neg), axis=0, keepdims=True)
        gscore.append(m1 + jnp.where(n1 > 1.0, m1, m2))
    taken = [jnp.zeros((1, tm), jnp.bool_) for _ in range(N_GROUPS)]
    for _ in range(TOPK_GROUPS):
        avail = [jnp.where(taken[gi], neg, gscore[gi]) for gi in range(N_GROUPS)]
        best = functools.reduce(jnp.maximum, avail)
        found = jnp.zeros((1, tm), jnp.bool_)
        for gi in range(N_GROUPS):
            hit = (avail[gi] == best) & jnp.logical_not(found)
            taken[gi] = taken[gi] | hit
            found = found | hit
    cand = jnp.concatenate(
        [jnp.where(taken[gi], sel[gi * GROUP_SIZE:(gi + 1) * GROUP_SIZE, :], neg) for gi in range(N_GROUPS)], axis=0)
    eid = lax.broadcasted_iota(jnp.int32, (N_EXPERTS, tm), 0).astype(F32)
    idxs, wts = [], []
    for _ in range(TOP_K):
        best = jnp.max(cand, axis=0, keepdims=True)
        pick = jnp.min(jnp.where(cand == best, eid, float(N_EXPERTS)), axis=0, keepdims=True)
        chosen = eid == pick
        wts.append(jnp.sum(jnp.where(chosen, st, 0.0), axis=0, keepdims=True))
        idxs.append(pick)
        cand = jnp.where(chosen, neg, cand)
    w = jnp.concatenate(wts, axis=0)
    idx_out[...] = jnp.concatenate(idxs, axis=0).astype(jnp.int32)
    wt_out[...] = w / jnp.sum(w, axis=0, keepdims=True) * ROUTED_SCALE


def _route(scores, bias_col, tm):
    T = scores.shape[0]
    return pl.pallas_call(
        _route_kernel,
        grid=(T // tm,),
        in_specs=[pl.BlockSpec((tm, N_EXPERTS), lambda i: (i, 0)), _full((N_EXPERTS, 1))],
        out_specs=[pl.BlockSpec((TOP_K, tm), lambda i: (0, i))] * 2,
        out_shape=[jax.ShapeDtypeStruct((TOP_K, T), jnp.int32), jax.ShapeDtypeStruct((TOP_K, T), F32)],
        compiler_params=_cparams("parallel"),
        name="route",
    )(scores, bias_col)


def _gmm_kernel(be_ref, nb_ref, xs_ref, w_ref, wg_ref, wu_ref, wd_ref, o_ref):
    i = pl.program_id(0)

    @pl.when(i < nb_ref[0])
    def _():
        xe = _bf(xs_ref[...])
        hg = _dot(xe, _bf(wg_ref[0]))
        hu = _dot(xe, _bf(wu_ref[0]))
        hdn = hg * jax.nn.sigmoid(hg) * hu
        o_ref[...] = _dot(_bf(hdn), _bf(wd_ref[0])) * w_ref[...]

    @pl.when(i >= nb_ref[0])
    def _():
        o_ref[...] = jnp.zeros_like(o_ref)


def _gmm(blk_e, n_used, xs, rows_w, wg, wu, wd):
    n_rows = xs.shape[0]
    bm = MOE_ROWS
    grid_spec = pltpu.PrefetchScalarGridSpec(
        num_scalar_prefetch=2,
        grid=(n_rows // bm,),
        in_specs=[pl.BlockSpec((bm, D_MODEL), lambda i, be, nb: (i, 0)),
                  pl.BlockSpec((bm, 1), lambda i, be, nb: (i, 0)),
                  pl.BlockSpec((1, D_MODEL, EXPERT_FF), lambda i, be, nb: (be[i], 0, 0)),
                  pl.BlockSpec((1, D_MODEL, EXPERT_FF), lambda i, be, nb: (be[i], 0, 0)),
                  pl.BlockSpec((1, EXPERT_FF, D_MODEL), lambda i, be, nb: (be[i], 0, 0))],
        out_specs=pl.BlockSpec((bm, D_MODEL), lambda i, be, nb: (i, 0)),
    )
    return pl.pallas_call(
        _gmm_kernel,
        grid_spec=grid_spec,
        out_shape=jax.ShapeDtypeStruct((n_rows, D_MODEL), F32),
        compiler_params=_cparams("arbitrary"),
        name="expert_gmm",
    )(blk_e, n_used, xs, rows_w, wg, wu, wd)


def _final_kernel(h_ref, xn_ref, routed_ref, p_ref, sg_ref, su_ref, sd_ref, gple_ref, wpg_ref, wpp_ref,
                  gfin_ref, y_out):
    xb = _bf(xn_ref[...])
    hg = _dot(xb, sg_ref[...])
    hu = _dot(xb, su_ref[...])
    shared = _dot(_bf(hg * jax.nn.sigmoid(hg) * hu), sd_ref[...])
    h = h_ref[...] + (routed_ref[...] + shared)
    gate = jax.nn.sigmoid(_dot(_bf(_rms(h, gple_ref[...])), wpg_ref[...]))
    h = h + gate * _dot(_bf(p_ref[...]), wpp_ref[...])
    y_out[...] = _rms(h, gfin_ref[...])


def _final(h, xn, routed, p, sg, su, sd, gple, wpg, wpp, gfin, tm):
    T = h.shape[0]
    tok = lambda width: pl.BlockSpec((tm, width), lambda i: (i, 0))
    return pl.pallas_call(
        _final_kernel,
        grid=(T // tm,),
        in_specs=[tok(D_MODEL), tok(D_MODEL), tok(D_MODEL), tok(PLE_DIM),
                  _full((D_MODEL, EXPERT_FF)), _full((D_MODEL, EXPERT_FF)), _full((EXPERT_FF, D_MODEL)),
                  _full((1, D_MODEL)), _full((D_MODEL, D_MODEL)), _full((PLE_DIM, D_MODEL)),
                  _full((1, D_MODEL))],
        out_specs=tok(D_MODEL),
        out_shape=jax.ShapeDtypeStruct((T, D_MODEL), F32),
        compiler_params=_cparams("parallel"),
        name="ffn_tail",
    )(h, xn, routed, p, sg, su, sd, gple, wpg, wpp, gfin)


def _pad_cols(a, width):
    return jnp.pad(a, [(0, 0)] * (a.ndim - 1) + [(0, width - a.shape[-1])])


def _rwkv_pad_cols(a):
    W = RWKV_WIDTH
    o1, o2, o3 = 3 * W, 3 * W + DECAY_RANK, 3 * W + DECAY_RANK + ICL_RANK
    return jnp.concatenate([a[..., :o1], _pad_cols(a[..., o1:o2], 128), _pad_cols(a[..., o2:o3], 128),
                            a[..., o3:]], axis=-1)


def _rwkv_unpad_cols(a):
    W = RWKV_WIDTH
    return jnp.concatenate([a[..., :3 * W + DECAY_RANK], a[..., 3 * W + 128:3 * W + 128 + ICL_RANK],
                            a[..., 3 * W + 256:]], axis=-1)


def _pad_rows(a, rows):
    return jnp.pad(a, [(0, rows - a.shape[0])] + [(0, 0)] * (a.ndim - 1))


def _mixer(x, shift, s0, past, wts, tm):
    nb, T, _ = x.shape
    (r, lw, kh, v, kkn, bb, g, bonus, last) = _in_rwkv(
        x, _rwkv_pad_cols(shift), wts["gmix"], wts["w_rwkv"], wts["mu"], wts["w0"], wts["wup"], wts["a0"],
        wts["aup"], wts["gup"], wts["k_k"], wts["k_a"], wts["r_k"], wts["gs"], tm)
    q_b, k_f, k_b, v_f, v_b, og, logf = _in_fox(
        x, wts["gmix"], wts["w_fox"], wts["qn"], wts["kn"], wts["fb"], wts["gs"], tm)

    C = RWKV_CHUNK
    Tp = -(-T // C) * C
    if Tp != T:
        padt = lambda a: jnp.pad(a, ((0, 0), (0, Tp - T), (0, 0)))
        r_p, lw_p, kh_p, v_p, kk_p, bb_p = (padt(a) for a in (r, lw, kh, v, kkn, bb))
    else:
        r_p, lw_p, kh_p, v_p, kk_p, bb_p = r, lw, kh, v, kkn, bb
    rh, yh, gm, sh = _rwkv_chunks(r_p, lw_p, kh_p, v_p, kk_p, bb_p)
    y, s_fin = _rwkv_scan(rh, yh, gm, sh, _state_to_pairs(s0.astype(F32)), 4 if nb % 4 == 0 else 1)
    y = y[:, :T]

    if past is None:
        c = _cumsum(logf, 512)
        cq, ck = _bias_layouts(c)
        o_att = _fox_attention(q_b, k_b, v_b, cq, ck, 256, 256, 0)
    else:
        k_past, v_past, lf_past = past
        P = k_past.shape[1]
        tk = 256
        sk = -(-(P + T) // tk) * tk
        pads = lambda a: jnp.pad(a, ((0, 0), (0, sk - P - T), (0, 0)))
        k_all = pads(jnp.concatenate([_bf(k_past.reshape(nb, P, FOX_WIDTH)), k_b], axis=1))
        v_all = pads(jnp.concatenate([_bf(v_past.reshape(nb, P, FOX_WIDTH)), v_b], axis=1))
        lf_all = pads(jnp.concatenate([_pad_cols(lf_past.astype(F32), 128), logf], axis=1))
        c = _cumsum(lf_all, tk)
        cq, ck = _bias_layouts(c)
        o_att = _fox_attention(q_b, k_all, v_all, cq[:, :, P:P + T], ck, T, tk, P)

    n = nb * T
    flat = lambda a: a.reshape(n, a.shape[-1])
    feats = (flat(y), flat(bonus), flat(g), flat(o_att), flat(og))
    state = (k_f.reshape(nb, T, N_HEADS, HEAD_DIM), v_f.reshape(nb, T, N_HEADS, HEAD_DIM),
             logf[:, :, :N_HEADS], _pairs_to_state(s_fin), _rwkv_unpad_cols(last))
    return feats, state


def _moe_dispatch(eidx, wts, n_tok):
    blk = MOE_ROWS
    n_assign = n_tok * TOP_K
    n_blocks = -(-n_assign // blk) + N_EXPERTS
    flat_e = eidx.reshape(-1)
    order = jnp.argsort(flat_e)
    se = flat_e[order]
    stok = (order // TOP_K).astype(jnp.int32)
    sw = wts.reshape(-1)[order]
    counts = jnp.bincount(flat_e, length=N_EXPERTS)
    padded = (counts + blk - 1) // blk * blk
    pend = jnp.cumsum(padded)
    pstart = pend - padded
    cstart = jnp.cumsum(counts) - counts
    dest = pstart[se] + jnp.arange(n_assign) - cstart[se]
    rows_tok = jnp.zeros((n_blocks * blk,), jnp.int32).at[dest].set(stok)
    rows_w = jnp.zeros((n_blocks * blk,), F32).at[dest].set(sw)
    blk_e = jnp.minimum(jnp.searchsorted(pend, jnp.arange(n_blocks) * blk, side='right'), N_EXPERTS - 1)
    n_used = (pend[-1] // blk).astype(jnp.int32).reshape(1)
    return rows_tok, rows_w, blk_e.astype(jnp.int32), n_used


def kernel(x_prompt, x_sample, cache_fox_k, cache_fox_v, cache_fox_logf, state_rwkv_wkv, state_rwkv_shift, p_prompt, p_sample, norm_mix_g, w_in, rwkv_mu, rwkv_w0, rwkv_w_up, rwkv_a0, rwkv_a_up, rwkv_g_up, rwkv_k_k, rwkv_k_a, rwkv_r_k, rwkv_ln_w, rwkv_ln_b, fox_q_norm, fox_k_norm, fox_f_bias, w_out, norm_ffn_g, router_w, router_bias, exp_w_gate, exp_w_up, exp_w_down, shared_w_gate, shared_w_up, shared_w_down, ple_norm_g, ple_w_gate, ple_w_proj, final_norm_g):
    assert w_in.shape[0] == 1, "single-layer kernel"
    W = RWKV_WIDTH
    row = lambda a: a.reshape(1, -1).astype(F32)
    tile_heads = lambda a: jnp.tile(a.reshape(1, HEAD_DIM), (1, N_HEADS)).astype(F32)
    hid = jnp.arange(W) // HEAD_DIM
    w_in0 = w_in[0]
    router_hi = _bf(router_w[0])
    wts = {
        "gmix": row(norm_mix_g[0]),
        "w_rwkv": _bf(_rwkv_pad_cols(w_in0[:, :RWKV_IN])),
        "w_fox": _bf(_pad_cols(w_in0[:, RWKV_IN:], FOX_PAD)),
        "mu": row(_rwkv_pad_cols(rwkv_mu[0])),
        "w0": row(rwkv_w0[0]),
        "wup": _bf(_pad_rows(rwkv_w_up[0], 128)),
        "a0": row(rwkv_a0[0]),
        "aup": _bf(_pad_rows(rwkv_a_up[0], 128)),
        "gup": _bf(rwkv_g_up[0]),
        "k_k": row(rwkv_k_k[0]),
        "k_a": row(rwkv_k_a[0]),
        "r_k": row(rwkv_r_k[0]),
        "gs": _bf((hid[:, None] == hid[None, :]).astype(F32)),
        "qn": tile_heads(fox_q_norm[0]),
        "kn": tile_heads(fox_k_norm[0]),
        "fb": _pad_cols(row(fox_f_bias[0]), 128),
    }
    nbp, Tp, _ = x_prompt.shape
    nbs, Ts, _ = x_sample.shape
    s0_prompt = jnp.zeros((nbp, N_HEADS, HEAD_DIM, HEAD_DIM), F32)
    shift0_prompt = jnp.zeros((nbp, 1, RWKV_IN), F32)
    feats_p, st_p = _mixer(x_prompt, shift0_prompt, s0_prompt, None, wts, min(Tp, 256))
    feats_s, st_s = _mixer(x_sample, state_rwkv_shift[0], state_rwkv_wkv[0],
                           (cache_fox_k[0], cache_fox_v[0], cache_fox_logf[0]), wts, Ts)

    n_p, n_s = nbp * Tp, nbs * Ts
    n_tok = n_p + n_s
    x_all = jnp.concatenate([x_prompt.reshape(n_p, D_MODEL), x_sample.reshape(n_s, D_MODEL)], axis=0)
    feats = [jnp.concatenate([a, b], axis=0) for a, b in zip(feats_p, feats_s)]
    tm = math.gcd(n_tok, 256)
    h1, xn2, scores = _out_proj(x_all, *feats, row(rwkv_ln_w[0]), row(rwkv_ln_b[0]), wts["gs"], _bf(w_out[0]),
                                row(norm_ffn_g[0]), router_hi,
                                _bf(router_w[0] - router_hi.astype(F32)), tm)
    eidx_t, wts_t = _route(scores, router_bias[0].reshape(N_EXPERTS, 1).astype(F32), tm)
    rows_tok, rows_w, blk_e, n_used = _moe_dispatch(eidx_t.T, wts_t.T, n_tok)
    xs = xn2[rows_tok]
    y_rows = _gmm(blk_e, n_used, xs, rows_w.reshape(-1, 1), exp_w_gate[0], exp_w_up[0], exp_w_down[0])
    routed = jnp.zeros((n_tok, D_MODEL), F32).at[rows_tok].add(y_rows)
    p_all = jnp.concatenate([p_prompt[0].reshape(n_p, PLE_DIM), p_sample[0].reshape(n_s, PLE_DIM)], axis=0)
    y_all = _final(h1, xn2, routed, p_all, _bf(shared_w_gate[0]), _bf(shared_w_up[0]), _bf(shared_w_down[0]),
                   row(ple_norm_g[0]), _bf(ple_w_gate[0]), _bf(ple_w_proj[0]), row(final_norm_g), tm)
    y_prompt = y_all[:n_p].reshape(nbp, Tp, D_MODEL)
    y_sample = y_all[n_p:].reshape(nbs, Ts, D_MODEL)
    lead = lambda t: tuple(a[None] for a in t)
    return (y_prompt, y_sample) + lead(st_p) + lead(st_s)
```

```python
import functools
import math

import numpy as np
import jax
import jax.numpy as jnp
from jax import lax
from jax.experimental import pallas as pl
from jax.experimental.pallas import tpu as pltpu

F32 = jnp.float32
BF16 = jnp.bfloat16

D_MODEL = 1024
HEAD_DIM = 64
RWKV_WIDTH = 512
FOX_WIDTH = 512
N_HEADS = 8
N_PAIRS = N_HEADS // 2
DECAY_RANK = 64
ICL_RANK = 64
GATE_RANK = 128
RWKV_IN = 3 * RWKV_WIDTH + DECAY_RANK + ICL_RANK + GATE_RANK
RWKV_PAD = 3 * RWKV_WIDTH + 3 * 128
FOX_PAD = 4 * FOX_WIDTH + 128
ATTN_SCALE = HEAD_DIM ** -0.5
N_EXPERTS = 256
N_GROUPS = 8
GROUP_SIZE = N_EXPERTS // N_GROUPS
TOPK_GROUPS = 4
TOP_K = 8
EXPERT_FF = 256
ROUTED_SCALE = 2.5
PLE_DIM = 256
RMS_EPS = 1e-6
GN_EPS = 64e-5
L2_EPS = 1e-12

LANES = 128
ROW_CHUNKS = D_MODEL // LANES
RWKV_CHUNK = 128
MOE_ROWS = 128
FOX_TQ, FOX_TK = 256, 1024
FOX_TK_CACHED = 768
VMEM_LIMIT = 56 * 1024 * 1024


def _cparams(*sem):
    return pltpu.CompilerParams(dimension_semantics=sem, vmem_limit_bytes=VMEM_LIMIT)


def _bf(x):
    return x.astype(BF16)


def _dot(a, b):
    return jnp.dot(a, b, preferred_element_type=F32)


def _dot_nt(a, b):
    return lax.dot_general(a, b, (((1,), (1,)), ((), ())), preferred_element_type=F32)


def _split2(x):
    hi = _bf(x)
    return hi, _bf(x - hi.astype(F32))


def _split3(x):
    hi = _bf(x)
    r1 = x - hi.astype(F32)
    mid = _bf(r1)
    return hi, mid, _bf(r1 - mid.astype(F32))


def _dot_x01(x, w01):
    hi, lo = _split2(x)
    return _dot(hi, w01) + _dot(lo, w01)


def _dot3(a, b):
    ah, al = _split2(a)
    bh, bl = _split2(b)
    return _dot(ah, bh) + _dot(al, bh) + _dot(ah, bl)


def _dot3_nt(a, b):
    ah, al = _split2(a)
    bh, bl = _split2(b)
    return _dot_nt(ah, bh) + _dot_nt(al, bh) + _dot_nt(ah, bl)


def _softplus(x):
    return jnp.maximum(x, 0.0) + jnp.log1p(jnp.exp(-jnp.abs(x)))


def _rms(x, g):
    return x * lax.rsqrt(jnp.mean(x * x, axis=-1, keepdims=True) + RMS_EPS) * g


def _full(shape):
    return pl.BlockSpec(shape, lambda *_: (0,) * len(shape))


def _store_chunked(ref, x):
    n = x.shape[0]
    for s in range(ROW_CHUNKS):
        ref[pl.ds(s, n, stride=ROW_CHUNKS), :] = x[:, s * LANES:(s + 1) * LANES]


def _load_chunked(ref, n):
    return jnp.concatenate([ref[pl.ds(s, n, stride=ROW_CHUNKS), :] for s in range(ROW_CHUNKS)], axis=1)


def _in_rwkv_kernel(x_ref, shift_ref, gmix_ref, w_ref, mu_ref, w0_ref, wup_ref, a0_ref, aup_ref, gup_ref,
                    kk_ref, ka_ref, rk_ref, gs_ref,
                    r_out, lw_out, kh_out, v_out, kkn_out, bb_out, g_out, bonus_out, last_out,
                    carry_ref):
    j = pl.program_id(1)
    tm = x_ref.shape[1]
    xn = _bf(_rms(x_ref[0], gmix_ref[...]))
    u = _dot(xn, w_ref[...])
    first = jnp.where(j == 0, shift_ref[0], carry_ref[...])
    row = lax.broadcasted_iota(jnp.int32, (tm, 1), 0)
    prev = jnp.where(row == 0, first, pltpu.roll(u, 1, axis=0))
    carry_ref[...] = u[tm - 1:tm, :]
    last_out[0] = u[tm - 1:tm, :]
    xs = u + (prev - u) * mu_ref[...]
    W = RWKV_WIDTH
    r, k, v = xs[:, :W], xs[:, W:2 * W], xs[:, 2 * W:3 * W]
    xw, xa, xg = xs[:, 3 * W:3 * W + 128], xs[:, 3 * W + 128:3 * W + 256], xs[:, 3 * W + 256:]
    w_raw = w0_ref[...] + _dot(_bf(jnp.tanh(xw)), wup_ref[...])
    lw = -jnp.exp(-_softplus(-w_raw) - 0.5)
    a = jax.nn.sigmoid(a0_ref[...] + _dot(_bf(xa), aup_ref[...]))
    g = _dot(_bf(jax.nn.sigmoid(xg)), gup_ref[...])
    gs = gs_ref[...]
    kk = k * kk_ref[...]
    kkn = kk / jnp.maximum(jnp.sqrt(_dot_x01(kk * kk, gs)), L2_EPS)
    kh = k * (1.0 + (a - 1.0) * ka_ref[...])
    r_out[0] = r
    lw_out[0] = lw
    kh_out[0] = kh
    v_out[0] = v
    kkn_out[0] = kkn
    bb_out[0] = kkn * a
    g_out[0] = g
    bonus_out[0] = _dot_x01(r * kh * rk_ref[...], gs) * v


def _in_rwkv(x, shift, gmix, w, mu, w0, wup, a0, aup, gup, k_k, k_a, r_k, gs, tm):
    nb, T, _ = x.shape
    W = RWKV_WIDTH
    tok = lambda width: pl.BlockSpec((1, tm, width), lambda b, j: (b, j, 0))
    outs = [jax.ShapeDtypeStruct((nb, T, W), F32)] * 8 + [jax.ShapeDtypeStruct((nb, 1, RWKV_PAD), F32)]
    return pl.pallas_call(
        _in_rwkv_kernel,
        grid=(nb, T // tm),
        in_specs=[tok(D_MODEL), pl.BlockSpec((1, 1, RWKV_PAD), lambda b, j: (b, 0, 0)),
                  _full((1, D_MODEL)), _full((D_MODEL, RWKV_PAD)), _full((1, RWKV_PAD)),
                  _full((1, W)), _full((128, W)), _full((1, W)), _full((128, W)), _full((128, W)),
                  _full((1, W)), _full((1, W)), _full((1, W)), _full((W, W))],
        out_specs=[tok(W)] * 8 + [pl.BlockSpec((1, 1, RWKV_PAD), lambda b, j: (b, 0, 0))],
        out_shape=outs,
        scratch_shapes=[pltpu.VMEM((1, RWKV_PAD), F32)],
        compiler_params=_cparams("parallel", "arbitrary"),
        name="in_rwkv",
    )(x, shift, gmix, w, mu, w0, wup, a0, aup, gup, k_k, k_a, r_k, gs)


def _in_fox_kernel(x_ref, gmix_ref, w_ref, qn_ref, kn_ref, fb_ref, gs_ref,
                   q_out, k_out, kb_out, v_out, vb_out, og_out, lf_out):
    xn = _bf(_rms(x_ref[0], gmix_ref[...]))
    u = _dot(xn, w_ref[...])
    W = FOX_WIDTH
    q, k, v, og, fl = u[:, :W], u[:, W:2 * W], u[:, 2 * W:3 * W], u[:, 3 * W:4 * W], u[:, 4 * W:]
    gs = gs_ref[...]
    inv_d = 1.0 / HEAD_DIM
    qn = q * lax.rsqrt(_dot_x01(q * q, gs) * inv_d + RMS_EPS) * qn_ref[...]
    kn = k * lax.rsqrt(_dot_x01(k * k, gs) * inv_d + RMS_EPS) * kn_ref[...]
    q_out[0] = _bf(qn * ATTN_SCALE)
    k_out[0] = kn
    kb_out[0] = _bf(kn)
    v_out[0] = v
    vb_out[0] = _bf(v)
    og_out[0] = og
    lf_out[0] = -_softplus(-(fl + fb_ref[...]))


def _in_fox(x, gmix, w, qn, kn, fb, gs, tm):
    nb, T, _ = x.shape
    W = FOX_WIDTH
    tok = lambda width: pl.BlockSpec((1, tm, width), lambda b, j: (b, j, 0))
    sds = lambda width, dt: jax.ShapeDtypeStruct((nb, T, width), dt)
    return pl.pallas_call(
        _in_fox_kernel,
        grid=(nb, T // tm),
        in_specs=[tok(D_MODEL), _full((1, D_MODEL)), _full((D_MODEL, FOX_PAD)),
                  _full((1, W)), _full((1, W)), _full((1, 128)), _full((W, W))],
        out_specs=[tok(W)] * 6 + [tok(128)],
        out_shape=[sds(W, BF16), sds(W, F32), sds(W, BF16), sds(W, F32), sds(W, BF16), sds(W, F32),
                   sds(128, F32)],
        compiler_params=_cparams("parallel", "parallel"),
        name="in_fox",
    )(x, gmix, w, qn, kn, fb, gs)


def _cumsum_kernel(x_ref, tri_ref, o_ref, carry_ref):
    j = pl.program_id(1)
    ts = x_ref.shape[1]

    @pl.when(j == 0)
    def _():
        carry_ref[...] = jnp.zeros_like(carry_ref)

    hi, mid, lo = _split3(x_ref[0])
    tri = tri_ref[...]
    c = _dot(tri, hi) + _dot(tri, mid) + _dot(tri, lo) + carry_ref[...]
    o_ref[0] = c
    carry_ref[...] = c[ts - 1:ts, :]


def _cumsum(x, ts):
    nb, S, L = x.shape
    tri = _bf(jnp.tril(jnp.ones((ts, ts), F32)))
    return pl.pallas_call(
        _cumsum_kernel,
        grid=(nb, S // ts),
        in_specs=[pl.BlockSpec((1, ts, L), lambda b, j: (b, j, 0)), _full((ts, ts))],
        out_specs=pl.BlockSpec((1, ts, L), lambda b, j: (b, j, 0)),
        out_shape=jax.ShapeDtypeStruct((nb, S, L), F32),
        scratch_shapes=[pltpu.VMEM((1, L), F32)],
        compiler_params=_cparams("parallel", "arbitrary"),
        name="seq_cumsum",
    )(x, tri)


def _rwkv_chunk_kernel(r_ref, lw_ref, kh_ref, v_ref, kk_ref, bb_ref, tri_ref,
                       rh_out, yh_out, g_out, sh_out):
    C = r_ref.shape[1]
    ti = lax.broadcasted_iota(jnp.int32, (C, C), 0)
    si = lax.broadcasted_iota(jnp.int32, (C, C), 1)
    tx = ti ^ si
    strict = ti > si
    incl = ti >= si
    eye_c = (ti == si).astype(F32)
    lane = lax.broadcasted_iota(jnp.int32, (1, LANES), 1)
    head0 = lane < HEAD_DIM
    pi = lax.broadcasted_iota(jnp.int32, (LANES, LANES), 0)
    pj = lax.broadcasted_iota(jnp.int32, (LANES, LANES), 1)
    same_head = (pi < HEAD_DIM) == (pj < HEAD_DIM)
    eye_p = (pi == pj).astype(F32)
    tri = tri_ref[...]

    for p in range(N_PAIRS):
        sl = slice(p * LANES, (p + 1) * LANES)
        r, lw, kh, v, kk, bb = (ref[0, :, sl] for ref in (r_ref, lw_ref, kh_ref, v_ref, kk_ref, bb_ref))
        l_hi, l_mid, l_lo = _split3(lw)
        lc = _dot(tri, l_hi) + _dot(tri, l_mid) + _dot(tri, l_lo)
        mid = lc[C // 2 - 1:C // 2, :]
        last = lc[C - 1:C, :]
        e_dn = jnp.exp(mid - lc)
        e_up = jnp.exp(lc - mid)
        e_end = jnp.exp(last - lc)
        aa = kk * jnp.exp(lc - lw - mid)
        rt = r * e_up
        bt_b, kt_b = _bf(bb * e_dn), _bf(kh * e_dn)
        bc_b, kc_b = _bf(bb * e_end), _bf(kh * e_end)
        rho = jnp.exp(mid)
        aa_b, v_b = _bf(aa), _bf(v)
        per_head = []
        for h in range(2):
            hm = head0 if h == 0 else jnp.logical_not(head0)
            aa_m = _bf(jnp.where(hm, aa, 0.0))
            rt_m = _bf(jnp.where(hm, rt, 0.0))
            lab = jnp.where(strict, _dot_nt(aa_m, bt_b), 0.0)
            lak = jnp.where(strict, _dot_nt(aa_m, kt_b), 0.0)
            mrb = _bf(jnp.where(incl, _dot_nt(rt_m, bt_b), 0.0))
            mrk = _bf(jnp.where(incl, _dot_nt(rt_m, kt_b), 0.0))
            d = eye_c - jnp.where(tx < 2, lab, 0.0)
            s = 2
            while s < C:
                ls = _bf(jnp.where((tx >= s) & (tx < 2 * s), lab, 0.0))
                d_b = _bf(d)
                d = d - _dot(_bf(_dot(d_b, ls)), d_b)
                s *= 2
            d_b = _bf(d)
            ah = _dot(d_b, aa_b) * rho
            uh = _dot(d_b, _bf(_dot(_bf(lak), v_b)))
            rh = rt * rho - _dot(mrb, _bf(ah))
            yh = _dot(mrk, v_b) - _dot(mrb, _bf(uh))
            per_head.append((ah, uh, rh, yh))
        ah, uh, rh, yh = (jnp.where(head0, x0, x1) for x0, x1 in zip(*per_head))
        rh_out[0, :, sl] = rh
        yh_out[0, :, sl] = yh
        g_full = eye_p * jnp.exp(last) - _dot(_bf(ah.T), bc_b)
        sh_full = _dot(_bf(v.T), kc_b) - _dot(_bf(uh.T), bc_b)
        g_out[0, 0, p] = jnp.where(same_head, g_full, 0.0)
        sh_out[0, 0, p] = jnp.where(same_head, sh_full, 0.0)


def _rwkv_chunks(r, lw, kh, v, kk, bb):
    nb, T, W = r.shape
    C = RWKV_CHUNK
    nc = T // C
    tri = _bf(jnp.tril(jnp.ones((C, C), F32)))
    tok = pl.BlockSpec((1, C, W), lambda b, c: (b, c, 0))
    mat = pl.BlockSpec((1, 1, N_PAIRS, LANES, LANES), lambda b, c: (b, c, 0, 0, 0))
    mat_shape = jax.ShapeDtypeStruct((nb, nc, N_PAIRS, LANES, LANES), F32)
    return pl.pallas_call(
        _rwkv_chunk_kernel,
        grid=(nb, nc),
        in_specs=[tok] * 6 + [_full((C, C))],
        out_specs=[tok, tok, mat, mat],
        out_shape=[jax.ShapeDtypeStruct((nb, T, W), F32)] * 2 + [mat_shape] * 2,
        compiler_params=_cparams("parallel", "parallel"),
        name="rwkv_chunks",
    )(r, lw, kh, v, kk, bb, tri)


def _rwkv_scan_kernel(rh_ref, yh_ref, g_ref, sh_ref, s0_ref, y_out, sfin_out, s_scr):
    c = pl.program_id(1)
    nbg = rh_ref.shape[0]

    @pl.when(c == 0)
    def _():
        s_scr[...] = s0_ref[...]

    for b in range(nbg):
        for p in range(N_PAIRS):
            sl = slice(p * LANES, (p + 1) * LANES)
            s = s_scr[b, p]
            y_out[b, :, sl] = _dot3_nt(rh_ref[b, :, sl], s) + yh_ref[b, :, sl]
            s_new = _dot3(s, g_ref[b, 0, p]) + sh_ref[b, 0, p]
            s_scr[b, p] = s_new
            sfin_out[b, p] = s_new


def _rwkv_scan(rh, yh, g, sh, s0, nbg):
    nb, T, W = rh.shape
    C = RWKV_CHUNK
    nc = T // C
    tok = pl.BlockSpec((nbg, C, W), lambda i, c: (i, c, 0))
    mat = pl.BlockSpec((nbg, 1, N_PAIRS, LANES, LANES), lambda i, c: (i, c, 0, 0, 0))
    st = pl.BlockSpec((nbg, N_PAIRS, LANES, LANES), lambda i, c: (i, 0, 0, 0))
    return pl.pallas_call(
        _rwkv_scan_kernel,
        grid=(nb // nbg, nc),
        in_specs=[tok, tok, mat, mat, st],
        out_specs=[tok, st],
        out_shape=[jax.ShapeDtypeStruct((nb, T, W), F32),
                   jax.ShapeDtypeStruct((nb, N_PAIRS, LANES, LANES), F32)],
        scratch_shapes=[pltpu.VMEM((nbg, N_PAIRS, LANES, LANES), F32)],
        compiler_params=_cparams("parallel", "arbitrary"),
        name="rwkv_scan",
    )(rh, yh, g, sh, s0)


def _state_to_pairs(s):
    nb = s.shape[0]
    s = s.reshape(nb, N_PAIRS, 2, HEAD_DIM, HEAD_DIM)
    z = jnp.zeros_like(s[:, :, 0])
    top = jnp.concatenate([s[:, :, 0], z], axis=-1)
    bot = jnp.concatenate([z, s[:, :, 1]], axis=-1)
    return jnp.concatenate([top, bot], axis=-2)


def _pairs_to_state(sp):
    nb = sp.shape[0]
    a = sp[:, :, :HEAD_DIM, :HEAD_DIM]
    b = sp[:, :, HEAD_DIM:, HEAD_DIM:]
    return jnp.stack([a, b], axis=2).reshape(nb, N_HEADS, HEAD_DIM, HEAD_DIM)


def _fox_aug_kernel(x_ref, c_ref, o_ref, *, is_query):
    lane = lax.broadcasted_iota(jnp.int32, (1, LANES), 1)
    c = c_ref[0]
    for p in range(N_PAIRS):
        xp = x_ref[0, :, p * LANES:(p + 1) * LANES].astype(F32)
        xr = pltpu.roll(xp, HEAD_DIM, axis=1)
        for h in range(2):
            hh = 2 * p + h
            ch = c[:, hh:hh + 1]
            hi = _bf(ch).astype(F32)
            r1 = ch - hi
            mid = _bf(r1).astype(F32)
            lo = _bf(r1 - mid).astype(F32)
            one = jnp.ones_like(ch)
            cols = (hi, mid, lo, one, one, one) if is_query else (one, one, one, -hi, -mid, -lo)
            aug = jnp.zeros_like(xp)
            for n, col in enumerate(cols):
                aug = jnp.where(lane == HEAD_DIM + n, col, aug)
            o_ref[0, hh] = _bf(jnp.where(lane < HEAD_DIM, xp if h == 0 else xr, aug))


def _fox_aug(x, c, tm, is_query):
    nb, S, W = x.shape
    return pl.pallas_call(
        functools.partial(_fox_aug_kernel, is_query=is_query),
        grid=(nb, S // tm),
        in_specs=[pl.BlockSpec((1, tm, W), lambda b, j: (b, j, 0)),
                  pl.BlockSpec((1, tm, LANES), lambda b, j: (b, j, 0))],
        out_specs=pl.BlockSpec((1, N_HEADS, tm, LANES), lambda b, j: (b, 0, j, 0)),
        out_shape=jax.ShapeDtypeStruct((nb, N_HEADS, S, LANES), BF16),
        compiler_params=_cparams("parallel", "parallel"),
        name="fox_aug",
    )(x, c)


def _fox_kernel(q_ref, k_ref, v_ref, o_ref, m_scr, l_scr, acc_scr, *, tk, q_off):
    i = pl.program_id(2)
    tq = q_ref.shape[2]
    q_start = q_off + i * tq
    m_scr[...] = jnp.full_like(m_scr, -jnp.inf)
    l_scr[...] = jnp.zeros_like(l_scr)
    acc_scr[...] = jnp.zeros_like(acc_scr)
    n_full = (q_start + 1) // tk
    n_kv = (q_start + tq + tk - 1) // tk

    def step(j, masked):
        ks = pl.multiple_of(j * tk, tk)
        v = v_ref[0, pl.ds(ks, tk), :]
        if masked:
            visible = (ks + lax.broadcasted_iota(jnp.int32, (1, tk), 1)) <= (
                q_start + lax.broadcasted_iota(jnp.int32, (tq, 1), 0))
        for h in range(2):
            s = _dot_nt(q_ref[0, h], k_ref[0, h, pl.ds(ks, tk), :])
            if masked:
                s = jnp.where(visible, s, -jnp.inf)
            m_old = m_scr[h]
            m_new = jnp.maximum(m_old, jnp.max(s, axis=-1, keepdims=True))
            alpha = jnp.exp(m_old - m_new)
            pr = jnp.exp(s - m_new)
            l_scr[h] = alpha * l_scr[h] + jnp.sum(pr, axis=-1, keepdims=True)
            acc_scr[h] = alpha * acc_scr[h] + _dot(_bf(pr), v)
            m_scr[h] = m_new

    def full_body(j, carry):
        step(j, False)
        return carry

    def edge_body(j, carry):
        step(j, True)
        return carry

    lax.fori_loop(0, n_full, full_body, 0)
    lax.fori_loop(n_full, n_kv, edge_body, 0)
    head0 = lax.broadcasted_iota(jnp.int32, (1, LANES), 1) < HEAD_DIM
    o_ref[0] = jnp.where(head0, acc_scr[0] / l_scr[0], acc_scr[1] / l_scr[1])


def _fox_attention(q, k, v, tq, tk, q_off):
    nb, _, sq, _ = q.shape
    sk = k.shape[2]
    return pl.pallas_call(
        functools.partial(_fox_kernel, tk=tk, q_off=q_off),
        grid=(nb, N_PAIRS, sq // tq),
        in_specs=[pl.BlockSpec((1, 2, tq, LANES), lambda b, p, i: (b, p, i, 0)),
                  pl.BlockSpec((1, 2, sk, LANES), lambda b, p, i: (b, p, 0, 0)),
                  pl.BlockSpec((1, sk, LANES), lambda b, p, i: (b, 0, p))],
        out_specs=pl.BlockSpec((1, tq, LANES), lambda b, p, i: (b, i, p)),
        out_shape=jax.ShapeDtypeStruct((nb, sq, FOX_WIDTH), F32),
        scratch_shapes=[pltpu.VMEM((2, tq, 1), F32), pltpu.VMEM((2, tq, 1), F32),
                        pltpu.VMEM((2, tq, LANES), F32)],
        compiler_params=_cparams("parallel", "parallel", "arbitrary"),
        name="fox_attention",
    )(q, k, v)


def _out_kernel(x_ref, y_ref, bonus_ref, g_ref, oa_ref, og_ref, lnw_ref, lnb_ref, gs_ref, wout_ref,
                gffn_ref, rwh_ref, rwl_ref, h_out, xn_out, sc_out):
    gs = gs_ref[...]
    inv_d = 1.0 / HEAD_DIM
    y = y_ref[...]
    mean = _dot_x01(y, gs) * inv_d
    d = y - mean
    var = _dot_x01(d * d, gs) * inv_d
    yn = d * lax.rsqrt(var + GN_EPS) * lnw_ref[...] + lnb_ref[...]
    o_rwkv = (yn + bonus_ref[...]) * g_ref[...]
    o_fox = oa_ref[...] * jax.nn.sigmoid(og_ref[...])
    mix = jnp.concatenate([_bf(o_rwkv), _bf(o_fox)], axis=-1)
    h = x_ref[...] + _dot(mix, wout_ref[...])
    h_out[...] = h
    xn = _rms(h, gffn_ref[...])
    _store_chunked(xn_out, xn)
    xh, xl = _split2(xn)
    logits = _dot(xh, rwh_ref[...]) + _dot(xl, rwh_ref[...]) + _dot(xh, rwl_ref[...])
    sc_out[...] = jax.nn.sigmoid(logits)


def _out_proj(x, y, bonus, g, oa, og, lnw, lnb, gs, wout, gffn, rwh, rwl, tm):
    T = x.shape[0]
    W = RWKV_WIDTH
    tok = lambda width: pl.BlockSpec((tm, width), lambda i: (i, 0))
    return pl.pallas_call(
        _out_kernel,
        grid=(T // tm,),
        in_specs=[tok(D_MODEL)] + [tok(W)] * 5 + [_full((1, W)), _full((1, W)), _full((W, W)),
                                                  _full((D_MODEL, D_MODEL)), _full((1, D_MODEL)),
                                                  _full((D_MODEL, N_EXPERTS)), _full((D_MODEL, N_EXPERTS))],
        out_specs=[tok(D_MODEL), pl.BlockSpec((tm * ROW_CHUNKS, LANES), lambda i: (i, 0)), tok(N_EXPERTS)],
        out_shape=[jax.ShapeDtypeStruct((T, D_MODEL), F32),
                   jax.ShapeDtypeStruct((T * ROW_CHUNKS, LANES), F32),
                   jax.ShapeDtypeStruct((T, N_EXPERTS), F32)],
        compiler_params=_cparams("parallel"),
        name="out_proj",
    )(x, y, bonus, g, oa, og, lnw, lnb, gs, wout, gffn, rwh, rwl)


def _route_kernel(sc_ref, bias_ref, before_ref, idx_out, wt_out, rank_out, cnt_out, cnt_scr):
    tm = sc_ref.shape[0]
    neg = -jnp.inf

    @pl.when(pl.program_id(0) == 0)
    def _():
        cnt_scr[...] = jnp.zeros_like(cnt_scr)

    st = sc_ref[...].T
    sel = st + bias_ref[...]
    gscore = []
    for gi in range(N_GROUPS):
        blk = sel[gi * GROUP_SIZE:(gi + 1) * GROUP_SIZE, :]
        m1 = jnp.max(blk, axis=0, keepdims=True)
        n1 = jnp.sum((blk == m1).astype(F32), axis=0, keepdims=True)
        m2 = jnp.max(jnp.where(blk < m1, blk, neg), axis=0, keepdims=True)
        gscore.append(m1 + jnp.where(n1 > 1.0, m1, m2))
    taken = [jnp.zeros((1, tm), jnp.bool_) for _ in range(N_GROUPS)]
    for _ in range(TOPK_GROUPS):
        avail = [jnp.where(taken[gi], neg, gscore[gi]) for gi in range(N_GROUPS)]
        best = functools.reduce(jnp.maximum, avail)
        found = jnp.zeros((1, tm), jnp.bool_)
        for gi in range(N_GROUPS):
            hit = (avail[gi] == best) & jnp.logical_not(found)
            taken[gi] = taken[gi] | hit
            found = found | hit
    cand = jnp.concatenate(
        [jnp.where(taken[gi], sel[gi * GROUP_SIZE:(gi + 1) * GROUP_SIZE, :], neg) for gi in range(N_GROUPS)], axis=0)
    eid = lax.broadcasted_iota(jnp.int32, (N_EXPERTS, tm), 0).astype(F32)
    idxs, wts = [], []
    onehot = jnp.zeros((N_EXPERTS, tm), F32)
    for _ in range(TOP_K):
        best = jnp.max(cand, axis=0, keepdims=True)
        pick = jnp.min(jnp.where(cand == best, eid, float(N_EXPERTS)), axis=0, keepdims=True)
        chosen = eid == pick
        wts.append(jnp.sum(jnp.where(chosen, st, 0.0), axis=0, keepdims=True))
        idxs.append(pick)
        cand = jnp.where(chosen, neg, cand)
        onehot = jnp.where(chosen, 1.0, onehot)
    w = jnp.concatenate(wts, axis=0)
    idx_out[...] = jnp.concatenate(idxs, axis=0).astype(jnp.int32)
    wt_out[...] = w / jnp.sum(w, axis=0, keepdims=True) * ROUTED_SCALE
    earlier = _dot(_bf(onehot), before_ref[...]) + cnt_scr[...]
    rank_out[...] = jnp.concatenate(
        [jnp.sum(jnp.where(eid == pick, earlier, 0.0), axis=0, keepdims=True) for pick in idxs],
        axis=0).astype(jnp.int32)
    cnt_scr[...] += jnp.sum(onehot, axis=1, keepdims=True)
    cnt_out[...] = cnt_scr[...]


def _route(scores, bias_col, tm):
    T = scores.shape[0]
    before = _bf(jnp.triu(jnp.ones((tm, tm), F32), 1))
    tok = pl.BlockSpec((TOP_K, tm), lambda i: (0, i))
    return pl.pallas_call(
        _route_kernel,
        grid=(T // tm,),
        in_specs=[pl.BlockSpec((tm, N_EXPERTS), lambda i: (i, 0)), _full((N_EXPERTS, 1)), _full((tm, tm))],
        out_specs=[tok, tok, tok, _full((N_EXPERTS, 1))],
        out_shape=[jax.ShapeDtypeStruct((TOP_K, T), jnp.int32), jax.ShapeDtypeStruct((TOP_K, T), F32),
                   jax.ShapeDtypeStruct((TOP_K, T), jnp.int32), jax.ShapeDtypeStruct((N_EXPERTS, 1), F32)],
        scratch_shapes=[pltpu.VMEM((N_EXPERTS, 1), F32)],
        compiler_params=_cparams("arbitrary"),
        name="route",
    )(scores, bias_col, before)


def _pos_kernel(idx_ref, rank_ref, start_ref, pos_out):
    tm = idx_ref.shape[1]
    eid = lax.broadcasted_iota(jnp.int32, (N_EXPERTS, tm), 0)
    idx = idx_ref[...]
    start = start_ref[...]
    base = jnp.concatenate(
        [jnp.sum(jnp.where(eid == idx[k:k + 1, :], start, 0.0), axis=0, keepdims=True) for k in range(TOP_K)],
        axis=0)
    pos_out[...] = rank_ref[...] + base.astype(jnp.int32)


def _positions(eidx_t, rank_t, start_col, tm):
    T = eidx_t.shape[1]
    tok = pl.BlockSpec((TOP_K, tm), lambda i: (0, i))
    return pl.pallas_call(
        _pos_kernel,
        grid=(T // tm,),
        in_specs=[tok, tok, _full((N_EXPERTS, 1))],
        out_specs=tok,
        out_shape=jax.ShapeDtypeStruct((TOP_K, T), jnp.int32),
        compiler_params=_cparams("parallel"),
        name="moe_positions",
    )(eidx_t, rank_t, start_col)


def _dispatch_kernel(pos_ref, x_ref, xs_in, xs_out, sem):
    del xs_in
    tm = pos_ref.shape[1]

    def issue(t, carry):
        src = x_ref.at[pl.ds(pl.multiple_of(t * ROW_CHUNKS, ROW_CHUNKS), ROW_CHUNKS), :]
        for k in range(TOP_K):
            row = pl.multiple_of(pos_ref[k, t] * ROW_CHUNKS, ROW_CHUNKS)
            pltpu.make_async_copy(src, xs_out.at[pl.ds(row, ROW_CHUNKS), :], sem).start()
        return carry

    lax.fori_loop(0, tm, issue, 0)
    for k in range(TOP_K):
        pltpu.make_async_copy(x_ref, xs_out.at[pl.ds(0, tm * ROW_CHUNKS), :], sem).wait()


def _dispatch(pos_t, xn_chunked, n_rows, tm):
    T = pos_t.shape[1]
    xs0 = jnp.zeros((n_rows * ROW_CHUNKS, LANES), F32)
    return pl.pallas_call(
        _dispatch_kernel,
        grid=(T // tm,),
        in_specs=[pl.BlockSpec((TOP_K, tm), lambda i: (0, i), memory_space=pltpu.SMEM),
                  pl.BlockSpec((tm * ROW_CHUNKS, LANES), lambda i: (i, 0)),
                  pl.BlockSpec(memory_space=pl.ANY)],
        out_specs=pl.BlockSpec(memory_space=pl.ANY),
        out_shape=jax.ShapeDtypeStruct((n_rows * ROW_CHUNKS, LANES), F32),
        scratch_shapes=[pltpu.SemaphoreType.DMA(())],
        input_output_aliases={2: 0},
        compiler_params=_cparams("arbitrary"),
        name="moe_dispatch",
    )(pos_t, xn_chunked, xs0)


def _gmm_kernel(be_ref, nb_ref, xs_ref, wg_ref, wu_ref, wd_ref, o_ref):
    i = pl.program_id(0)
    bm = MOE_ROWS

    @pl.when(i < nb_ref[0])
    def _():
        xe = _bf(_load_chunked(xs_ref, bm))
        hg = _dot(xe, _bf(wg_ref[0]))
        hu = _dot(xe, _bf(wu_ref[0]))
        hdn = hg * jax.nn.sigmoid(hg) * hu
        _store_chunked(o_ref, _dot(_bf(hdn), _bf(wd_ref[0])))

    @pl.when(i >= nb_ref[0])
    def _():
        o_ref[...] = jnp.zeros_like(o_ref)


def _gmm(blk_e, n_used, xs, wg, wu, wd):
    bm = MOE_ROWS
    n_blocks = xs.shape[0] // (bm * ROW_CHUNKS)
    rows = pl.BlockSpec((bm * ROW_CHUNKS, LANES), lambda i, be, nb: (i, 0))
    grid_spec = pltpu.PrefetchScalarGridSpec(
        num_scalar_prefetch=2,
        grid=(n_blocks,),
        in_specs=[rows,
                  pl.BlockSpec((1, D_MODEL, EXPERT_FF), lambda i, be, nb: (be[i], 0, 0)),
                  pl.BlockSpec((1, D_MODEL, EXPERT_FF), lambda i, be, nb: (be[i], 0, 0)),
                  pl.BlockSpec((1, EXPERT_FF, D_MODEL), lambda i, be, nb: (be[i], 0, 0))],
        out_specs=rows,
    )
    return pl.pallas_call(
        _gmm_kernel,
        grid_spec=grid_spec,
        out_shape=jax.ShapeDtypeStruct(xs.shape, F32),
        compiler_params=_cparams("arbitrary"),
        name="expert_gmm",
    )(blk_e, n_used, xs, wg, wu, wd)


def _final_kernel(pos_ref, h_ref, xn_ref, w_ref, p_ref, ys_ref, sg_ref, su_ref, sd_ref, gple_ref, wpg_ref,
                  wpp_ref, gfin_ref, y_out, buf, sems):
    tm = h_ref.shape[0]

    def issue(t, carry):
        dst = pl.ds(pl.multiple_of(t * ROW_CHUNKS, ROW_CHUNKS), ROW_CHUNKS)
        for k in range(TOP_K):
            row = pl.multiple_of(pos_ref[k, t] * ROW_CHUNKS, ROW_CHUNKS)
            pltpu.make_async_copy(ys_ref.at[pl.ds(row, ROW_CHUNKS), :], buf.at[k, dst, :], sems.at[k]).start()
        return carry

    lax.fori_loop(0, tm, issue, 0)
    xb = _bf(_load_chunked(xn_ref, tm))
    hg = _dot(xb, sg_ref[...])
    hu = _dot(xb, su_ref[...])
    shared = _dot(_bf(hg * jax.nn.sigmoid(hg) * hu), sd_ref[...])
    w = w_ref[...]
    routed = jnp.zeros((tm, D_MODEL), F32)
    for k in range(TOP_K):
        pltpu.make_async_copy(ys_ref.at[pl.ds(0, tm * ROW_CHUNKS), :], buf.at[k], sems.at[k]).wait()
        routed = routed + _load_chunked(buf.at[k], tm) * w[:, k:k + 1]
    h = h_ref[...] + (routed + shared)
    gate = jax.nn.sigmoid(_dot(_bf(_rms(h, gple_ref[...])), wpg_ref[...]))
    h = h + gate * _dot(_bf(p_ref[...]), wpp_ref[...])
    y_out[...] = _rms(h, gfin_ref[...])


def _final(pos_t, h, xn_chunked, w, p, ys, sg, su, sd, gple, wpg, wpp, gfin, tm):
    T = h.shape[0]
    tok = lambda width: pl.BlockSpec((tm, width), lambda i: (i, 0))
    return pl.pallas_call(
        _final_kernel,
        grid=(T // tm,),
        in_specs=[pl.BlockSpec((TOP_K, tm), lambda i: (0, i), memory_space=pltpu.SMEM),
                  tok(D_MODEL), pl.BlockSpec((tm * ROW_CHUNKS, LANES), lambda i: (i, 0)), tok(TOP_K),
                  tok(PLE_DIM), pl.BlockSpec(memory_space=pl.ANY),
                  _full((D_MODEL, EXPERT_FF)), _full((D_MODEL, EXPERT_FF)), _full((EXPERT_FF, D_MODEL)),
                  _full((1, D_MODEL)), _full((D_MODEL, D_MODEL)), _full((PLE_DIM, D_MODEL)),
                  _full((1, D_MODEL))],
        out_specs=tok(D_MODEL),
        out_shape=jax.ShapeDtypeStruct((T, D_MODEL), F32),
        scratch_shapes=[pltpu.VMEM((TOP_K, tm * ROW_CHUNKS, LANES), F32), pltpu.SemaphoreType.DMA((TOP_K,))],
        compiler_params=_cparams("arbitrary"),
        name="ffn_tail",
    )(pos_t, h, xn_chunked, w, p, ys, sg, su, sd, gple, wpg, wpp, gfin)


def _pad_cols(a, width):
    return jnp.pad(a, [(0, 0)] * (a.ndim - 1) + [(0, width - a.shape[-1])])


def _rwkv_pad_cols(a):
    W = RWKV_WIDTH
    o1, o2, o3 = 3 * W, 3 * W + DECAY_RANK, 3 * W + DECAY_RANK + ICL_RANK
    return jnp.concatenate([a[..., :o1], _pad_cols(a[..., o1:o2], 128), _pad_cols(a[..., o2:o3], 128),
                            a[..., o3:]], axis=-1)


def _rwkv_unpad_cols(a):
    W = RWKV_WIDTH
    return jnp.concatenate([a[..., :3 * W + DECAY_RANK], a[..., 3 * W + 128:3 * W + 128 + ICL_RANK],
                            a[..., 3 * W + 256:]], axis=-1)


def _pad_rows(a, rows):
    return jnp.pad(a, [(0, rows - a.shape[0])] + [(0, 0)] * (a.ndim - 1))


def _mixer(x, shift, s0, past, wts, tm):
    nb, T, _ = x.shape
    (r, lw, kh, v, kkn, bb, g, bonus, last) = _in_rwkv(
        x, _rwkv_pad_cols(shift), wts["gmix"], wts["w_rwkv"], wts["mu"], wts["w0"], wts["wup"], wts["a0"],
        wts["aup"], wts["gup"], wts["k_k"], wts["k_a"], wts["r_k"], wts["gs"], tm)
    q_b, k_f, k_b, v_f, v_b, og, logf = _in_fox(
        x, wts["gmix"], wts["w_fox"], wts["qn"], wts["kn"], wts["fb"], wts["gs"], tm)

    C = RWKV_CHUNK
    Tp = -(-T // C) * C
    if Tp != T:
        padt = lambda a: jnp.pad(a, ((0, 0), (0, Tp - T), (0, 0)))
        r_p, lw_p, kh_p, v_p, kk_p, bb_p = (padt(a) for a in (r, lw, kh, v, kkn, bb))
    else:
        r_p, lw_p, kh_p, v_p, kk_p, bb_p = r, lw, kh, v, kkn, bb
    rh, yh, gm, sh = _rwkv_chunks(r_p, lw_p, kh_p, v_p, kk_p, bb_p)
    y, s_fin = _rwkv_scan(rh, yh, gm, sh, _state_to_pairs(s0.astype(F32)), 4 if nb % 4 == 0 else 1)
    y = y[:, :T]

    if past is None:
        c = _cumsum(logf, min(T, 512))
        tq, tk = min(T, FOX_TQ), min(T, FOX_TK)
        o_att = _fox_attention(_fox_aug(q_b, c, tq, True), _fox_aug(k_b, c, tq, False), v_b, tq, tk, 0)
    else:
        k_past, v_past, lf_past = past
        P = k_past.shape[1]
        tk = FOX_TK_CACHED
        sk = -(-(P + T) // tk) * tk
        pads = lambda a: jnp.pad(a, ((0, 0), (0, sk - P - T), (0, 0)))
        k_all = pads(jnp.concatenate([_bf(k_past.reshape(nb, P, FOX_WIDTH)), k_b], axis=1))
        v_all = pads(jnp.concatenate([_bf(v_past.reshape(nb, P, FOX_WIDTH)), v_b], axis=1))
        lf_all = pads(jnp.concatenate([_pad_cols(lf_past.astype(F32), 128), logf], axis=1))
        c = _cumsum(lf_all, tk)
        o_att = _fox_attention(_fox_aug(q_b, c[:, P:P + T], T, True), _fox_aug(k_all, c, tk, False), v_all,
                               T, tk, P)

    n = nb * T
    flat = lambda a: a.reshape(n, a.shape[-1])
    feats = (flat(y), flat(bonus), flat(g), flat(o_att), flat(og))
    state = (k_f.reshape(nb, T, N_HEADS, HEAD_DIM), v_f.reshape(nb, T, N_HEADS, HEAD_DIM),
             logf[:, :, :N_HEADS], _pairs_to_state(s_fin), _rwkv_unpad_cols(last))
    return feats, state


def _block_tables(counts, n_blocks):
    blk = MOE_ROWS
    counts = counts.reshape(N_EXPERTS).astype(jnp.int32)
    padded = (counts + blk - 1) // blk * blk
    pend = jnp.cumsum(padded)
    blk_e = jnp.minimum(jnp.searchsorted(pend, jnp.arange(n_blocks) * blk, side='right'), N_EXPERTS - 1)
    n_used = (pend[-1] // blk).astype(jnp.int32).reshape(1)
    return (pend - padded).astype(F32).reshape(N_EXPERTS, 1), blk_e.astype(jnp.int32), n_used


def kernel(x_prompt, x_sample, cache_fox_k, cache_fox_v, cache_fox_logf, state_rwkv_wkv, state_rwkv_shift, p_prompt, p_sample, norm_mix_g, w_in, rwkv_mu, rwkv_w0, rwkv_w_up, rwkv_a0, rwkv_a_up, rwkv_g_up, rwkv_k_k, rwkv_k_a, rwkv_r_k, rwkv_ln_w, rwkv_ln_b, fox_q_norm, fox_k_norm, fox_f_bias, w_out, norm_ffn_g, router_w, router_bias, exp_w_gate, exp_w_up, exp_w_down, shared_w_gate, shared_w_up, shared_w_down, ple_norm_g, ple_w_gate, ple_w_proj, final_norm_g):
    assert w_in.shape[0] == 1, "single-layer kernel"
    W = RWKV_WIDTH
    row = lambda a: a.reshape(1, -1).astype(F32)
    tile_heads = lambda a: jnp.tile(a.reshape(1, HEAD_DIM), (1, N_HEADS)).astype(F32)
    hid = jnp.arange(W) // HEAD_DIM
    w_in0 = w_in[0]
    router_hi = _bf(router_w[0])
    wts = {
        "gmix": row(norm_mix_g[0]),
        "w_rwkv": _bf(_rwkv_pad_cols(w_in0[:, :RWKV_IN])),
        "w_fox": _bf(_pad_cols(w_in0[:, RWKV_IN:], FOX_PAD)),
        "mu": row(_rwkv_pad_cols(rwkv_mu[0])),
        "w0": row(rwkv_w0[0]),
        "wup": _bf(_pad_rows(rwkv_w_up[0], 128)),
        "a0": row(rwkv_a0[0]),
        "aup": _bf(_pad_rows(rwkv_a_up[0], 128)),
        "gup": _bf(rwkv_g_up[0]),
        "k_k": row(rwkv_k_k[0]),
        "k_a": row(rwkv_k_a[0]),
        "r_k": row(rwkv_r_k[0]),
        "gs": _bf((hid[:, None] == hid[None, :]).astype(F32)),
        "qn": tile_heads(fox_q_norm[0]),
        "kn": tile_heads(fox_k_norm[0]),
        "fb": _pad_cols(row(fox_f_bias[0]), 128),
    }
    nbp, Tp, _ = x_prompt.shape
    nbs, Ts, _ = x_sample.shape
    s0_prompt = jnp.zeros((nbp, N_HEADS, HEAD_DIM, HEAD_DIM), F32)
    shift0_prompt = jnp.zeros((nbp, 1, RWKV_IN), F32)
    feats_p, st_p = _mixer(x_prompt, shift0_prompt, s0_prompt, None, wts, min(Tp, 256))
    feats_s, st_s = _mixer(x_sample, state_rwkv_shift[0], state_rwkv_wkv[0],
                           (cache_fox_k[0], cache_fox_v[0], cache_fox_logf[0]), wts, Ts)

    n_p, n_s = nbp * Tp, nbs * Ts
    n_tok = n_p + n_s
    x_all = jnp.concatenate([x_prompt.reshape(n_p, D_MODEL), x_sample.reshape(n_s, D_MODEL)], axis=0)
    feats = [jnp.concatenate([a, b], axis=0) for a, b in zip(feats_p, feats_s)]
    tm = math.gcd(n_tok, 256)
    h1, xn2, scores = _out_proj(x_all, *feats, row(rwkv_ln_w[0]), row(rwkv_ln_b[0]), wts["gs"], _bf(w_out[0]),
                                row(norm_ffn_g[0]), router_hi,
                                _bf(router_w[0] - router_hi.astype(F32)), tm)
    eidx_t, wts_t, rank_t, counts = _route(scores, router_bias[0].reshape(N_EXPERTS, 1).astype(F32), tm)
    n_blocks = -(-n_tok * TOP_K // MOE_ROWS) + N_EXPERTS
    start_col, blk_e, n_used = _block_tables(counts, n_blocks)
    pos_t = _positions(eidx_t, rank_t, start_col, tm)
    xs = _dispatch(pos_t, xn2, n_blocks * MOE_ROWS, tm)
    y_rows = _gmm(blk_e, n_used, xs, exp_w_gate[0], exp_w_up[0], exp_w_down[0])
    p_all = jnp.concatenate([p_prompt[0].reshape(n_p, PLE_DIM), p_sample[0].reshape(n_s, PLE_DIM)], axis=0)
    y_all = _final(pos_t, h1, xn2, wts_t.T, p_all, y_rows, _bf(shared_w_gate[0]), _bf(shared_w_up[0]),
                   _bf(shared_w_down[0]), row(ple_norm_g[0]), _bf(ple_w_gate[0]), _bf(ple_w_proj[0]),
                   row(final_norm_g), tm)
    y_prompt = y_all[:n_p].reshape(nbp, Tp, D_MODEL)
    y_sample = y_all[n_p:].reshape(nbs, Ts, D_MODEL)
    lead = lambda t: tuple(a[None] for a in t)
    return (y_prompt, y_sample) + lead(st_p) + lead(st_s)
```

```python
import functools
import math

import numpy as np
import jax
import jax.numpy as jnp
from jax import lax
from jax.experimental import pallas as pl
from jax.experimental.pallas import tpu as pltpu

F32 = jnp.float32
BF16 = jnp.bfloat16

D_MODEL = 1024
HEAD_DIM = 64
RWKV_WIDTH = 512
FOX_WIDTH = 512
N_HEADS = 8
N_PAIRS = N_HEADS // 2
DECAY_RANK = 64
ICL_RANK = 64
GATE_RANK = 128
RWKV_IN = 3 * RWKV_WIDTH + DECAY_RANK + ICL_RANK + GATE_RANK
RWKV_PAD = 3 * RWKV_WIDTH + 3 * 128
FOX_PAD = 4 * FOX_WIDTH + 128
ATTN_SCALE = HEAD_DIM ** -0.5
N_EXPERTS = 256
N_GROUPS = 8
GROUP_SIZE = N_EXPERTS // N_GROUPS
TOPK_GROUPS = 4
TOP_K = 8
EXPERT_FF = 256
ROUTED_SCALE = 2.5
PLE_DIM = 256
RMS_EPS = 1e-6
GN_EPS = 64e-5
L2_EPS = 1e-12

LANES = 128
ROW_CHUNKS = D_MODEL // LANES
RWKV_CHUNK = 128
MOE_ROWS = 256
FOX_TQ, FOX_TK = 256, 1024
FOX_TK_CACHED = 768
VMEM_LIMIT = 56 * 1024 * 1024


def _cparams(*sem):
    return pltpu.CompilerParams(dimension_semantics=sem, vmem_limit_bytes=VMEM_LIMIT)


def _bf(x):
    return x.astype(BF16)


def _dot(a, b):
    return jnp.dot(a, b, preferred_element_type=F32)


def _dot_nt(a, b):
    return lax.dot_general(a, b, (((1,), (1,)), ((), ())), preferred_element_type=F32)


def _split2(x):
    hi = _bf(x)
    return hi, _bf(x - hi.astype(F32))


def _split3(x):
    hi = _bf(x)
    r1 = x - hi.astype(F32)
    mid = _bf(r1)
    return hi, mid, _bf(r1 - mid.astype(F32))


def _dot_x01(x, w01):
    hi, lo = _split2(x)
    return _dot(hi, w01) + _dot(lo, w01)


def _dot3(a, b):
    ah, al = _split2(a)
    bh, bl = _split2(b)
    return _dot(ah, bh) + _dot(al, bh) + _dot(ah, bl)


def _dot3_nt(a, b):
    ah, al = _split2(a)
    bh, bl = _split2(b)
    return _dot_nt(ah, bh) + _dot_nt(al, bh) + _dot_nt(ah, bl)


def _softplus(x):
    return jnp.maximum(x, 0.0) + jnp.log1p(jnp.exp(-jnp.abs(x)))


def _rms(x, g):
    return x * lax.rsqrt(jnp.mean(x * x, axis=-1, keepdims=True) + RMS_EPS) * g


def _full(shape):
    return pl.BlockSpec(shape, lambda *_: (0,) * len(shape))


def _store_chunked(ref, x):
    n = x.shape[0]
    for s in range(ROW_CHUNKS):
        ref[pl.ds(s, n, stride=ROW_CHUNKS), :] = x[:, s * LANES:(s + 1) * LANES]


def _load_chunked(ref, n):
    return jnp.concatenate([ref[pl.ds(s, n, stride=ROW_CHUNKS), :] for s in range(ROW_CHUNKS)], axis=1)


def _in_rwkv_kernel(x_ref, shift_ref, gmix_ref, w_ref, mu_ref, w0_ref, wup_ref, a0_ref, aup_ref, gup_ref,
                    kk_ref, ka_ref, rk_ref, gs_ref,
                    r_out, lw_out, kh_out, v_out, kkn_out, bb_out, g_out, bonus_out, last_out,
                    carry_ref):
    j = pl.program_id(1)
    tm = x_ref.shape[1]
    xn = _bf(_rms(x_ref[0], gmix_ref[...]))
    u = _dot(xn, w_ref[...])
    first = jnp.where(j == 0, shift_ref[0], carry_ref[...])
    row = lax.broadcasted_iota(jnp.int32, (tm, 1), 0)
    prev = jnp.where(row == 0, first, pltpu.roll(u, 1, axis=0))
    carry_ref[...] = u[tm - 1:tm, :]
    last_out[0] = u[tm - 1:tm, :]
    xs = u + (prev - u) * mu_ref[...]
    W = RWKV_WIDTH
    r, k, v = xs[:, :W], xs[:, W:2 * W], xs[:, 2 * W:3 * W]
    xw, xa, xg = xs[:, 3 * W:3 * W + 128], xs[:, 3 * W + 128:3 * W + 256], xs[:, 3 * W + 256:]
    w_raw = w0_ref[...] + _dot(_bf(jnp.tanh(xw)), wup_ref[...])
    lw = -jnp.exp(-_softplus(-w_raw) - 0.5)
    a = jax.nn.sigmoid(a0_ref[...] + _dot(_bf(xa), aup_ref[...]))
    g = _dot(_bf(jax.nn.sigmoid(xg)), gup_ref[...])
    gs = gs_ref[...]
    kk = k * kk_ref[...]
    kkn = kk / jnp.maximum(jnp.sqrt(_dot_x01(kk * kk, gs)), L2_EPS)
    kh = k * (1.0 + (a - 1.0) * ka_ref[...])
    r_out[0] = r
    lw_out[0] = lw
    kh_out[0] = kh
    v_out[0] = v
    kkn_out[0] = kkn
    bb_out[0] = kkn * a
    g_out[0] = g
    bonus_out[0] = _dot_x01(r * kh * rk_ref[...], gs) * v


def _in_rwkv(x, shift, gmix, w, mu, w0, wup, a0, aup, gup, k_k, k_a, r_k, gs, tm):
    nb, T, _ = x.shape
    W = RWKV_WIDTH
    tok = lambda width: pl.BlockSpec((1, tm, width), lambda b, j: (b, j, 0))
    outs = [jax.ShapeDtypeStruct((nb, T, W), F32)] * 8 + [jax.ShapeDtypeStruct((nb, 1, RWKV_PAD), F32)]
    return pl.pallas_call(
        _in_rwkv_kernel,
        grid=(nb, T // tm),
        in_specs=[tok(D_MODEL), pl.BlockSpec((1, 1, RWKV_PAD), lambda b, j: (b, 0, 0)),
                  _full((1, D_MODEL)), _full((D_MODEL, RWKV_PAD)), _full((1, RWKV_PAD)),
                  _full((1, W)), _full((128, W)), _full((1, W)), _full((128, W)), _full((128, W)),
                  _full((1, W)), _full((1, W)), _full((1, W)), _full((W, W))],
        out_specs=[tok(W)] * 8 + [pl.BlockSpec((1, 1, RWKV_PAD), lambda b, j: (b, 0, 0))],
        out_shape=outs,
        scratch_shapes=[pltpu.VMEM((1, RWKV_PAD), F32)],
        compiler_params=_cparams("parallel", "arbitrary"),
        name="in_rwkv",
    )(x, shift, gmix, w, mu, w0, wup, a0, aup, gup, k_k, k_a, r_k, gs)


def _in_fox_kernel(x_ref, gmix_ref, w_ref, qn_ref, kn_ref, fb_ref, gs_ref,
                   q_out, k_out, kb_out, v_out, vb_out, og_out, lf_out):
    xn = _bf(_rms(x_ref[0], gmix_ref[...]))
    u = _dot(xn, w_ref[...])
    W = FOX_WIDTH
    q, k, v, og, fl = u[:, :W], u[:, W:2 * W], u[:, 2 * W:3 * W], u[:, 3 * W:4 * W], u[:, 4 * W:]
    gs = gs_ref[...]
    inv_d = 1.0 / HEAD_DIM
    qn = q * lax.rsqrt(_dot_x01(q * q, gs) * inv_d + RMS_EPS) * qn_ref[...]
    kn = k * lax.rsqrt(_dot_x01(k * k, gs) * inv_d + RMS_EPS) * kn_ref[...]
    q_out[0] = _bf(qn * ATTN_SCALE)
    k_out[0] = kn
    kb_out[0] = _bf(kn)
    v_out[0] = v
    vb_out[0] = _bf(v)
    og_out[0] = og
    lf_out[0] = -_softplus(-(fl + fb_ref[...]))


def _in_fox(x, gmix, w, qn, kn, fb, gs, tm):
    nb, T, _ = x.shape
    W = FOX_WIDTH
    tok = lambda width: pl.BlockSpec((1, tm, width), lambda b, j: (b, j, 0))
    sds = lambda width, dt: jax.ShapeDtypeStruct((nb, T, width), dt)
    return pl.pallas_call(
        _in_fox_kernel,
        grid=(nb, T // tm),
        in_specs=[tok(D_MODEL), _full((1, D_MODEL)), _full((D_MODEL, FOX_PAD)),
                  _full((1, W)), _full((1, W)), _full((1, 128)), _full((W, W))],
        out_specs=[tok(W)] * 6 + [tok(128)],
        out_shape=[sds(W, BF16), sds(W, F32), sds(W, BF16), sds(W, F32), sds(W, BF16), sds(W, F32),
                   sds(128, F32)],
        compiler_params=_cparams("parallel", "parallel"),
        name="in_fox",
    )(x, gmix, w, qn, kn, fb, gs)


def _cumsum_kernel(x_ref, tri_ref, o_ref, carry_ref):
    j = pl.program_id(1)
    ts = x_ref.shape[1]

    @pl.when(j == 0)
    def _():
        carry_ref[...] = jnp.zeros_like(carry_ref)

    hi, mid, lo = _split3(x_ref[0])
    tri = tri_ref[...]
    c = _dot(tri, hi) + _dot(tri, mid) + _dot(tri, lo) + carry_ref[...]
    o_ref[0] = c
    carry_ref[...] = c[ts - 1:ts, :]


def _cumsum(x, ts):
    nb, S, L = x.shape
    tri = _bf(jnp.tril(jnp.ones((ts, ts), F32)))
    return pl.pallas_call(
        _cumsum_kernel,
        grid=(nb, S // ts),
        in_specs=[pl.BlockSpec((1, ts, L), lambda b, j: (b, j, 0)), _full((ts, ts))],
        out_specs=pl.BlockSpec((1, ts, L), lambda b, j: (b, j, 0)),
        out_shape=jax.ShapeDtypeStruct((nb, S, L), F32),
        scratch_shapes=[pltpu.VMEM((1, L), F32)],
        compiler_params=_cparams("parallel", "arbitrary"),
        name="seq_cumsum",
    )(x, tri)


def _rwkv_chunk_kernel(r_ref, lw_ref, kh_ref, v_ref, kk_ref, bb_ref, tri_ref,
                       rh_out, yh_out, g_out, sh_out):
    C = r_ref.shape[1]
    ti = lax.broadcasted_iota(jnp.int32, (C, C), 0)
    si = lax.broadcasted_iota(jnp.int32, (C, C), 1)
    tx = ti ^ si
    strict = ti > si
    incl = ti >= si
    eye_c = (ti == si).astype(F32)
    lane = lax.broadcasted_iota(jnp.int32, (1, LANES), 1)
    head0 = lane < HEAD_DIM
    pi = lax.broadcasted_iota(jnp.int32, (LANES, LANES), 0)
    pj = lax.broadcasted_iota(jnp.int32, (LANES, LANES), 1)
    same_head = (pi < HEAD_DIM) == (pj < HEAD_DIM)
    eye_p = (pi == pj).astype(F32)
    tri = tri_ref[...]

    pairs = range(N_PAIRS)
    heads = [(p, h) for p in pairs for h in range(2)]
    P = []
    for p in pairs:
        sl = slice(p * LANES, (p + 1) * LANES)
        r, lw, kh, v, kk, bb = (ref[0, :, sl] for ref in (r_ref, lw_ref, kh_ref, v_ref, kk_ref, bb_ref))
        l_hi, l_mid, l_lo = _split3(lw)
        lc = _dot(tri, l_hi) + _dot(tri, l_mid) + _dot(tri, l_lo)
        mid = lc[C // 2 - 1:C // 2, :]
        last = lc[C - 1:C, :]
        e_dn = jnp.exp(mid - lc)
        e_end = jnp.exp(last - lc)
        aa = kk * jnp.exp(lc - lw - mid)
        P.append(dict(sl=sl, v=v, aa=aa, rt=r * jnp.exp(lc - mid), rho=jnp.exp(mid), g_last=jnp.exp(last),
                      bt_b=_bf(bb * e_dn), kt_b=_bf(kh * e_dn), bc_b=_bf(bb * e_end), kc_b=_bf(kh * e_end),
                      aa_b=_bf(aa), v_b=_bf(v)))
    lab, lak, mrb, mrk = {}, {}, {}, {}
    for p, h in heads:
        q = P[p]
        hm = head0 if h == 0 else jnp.logical_not(head0)
        aa_m = _bf(jnp.where(hm, q["aa"], 0.0))
        rt_m = _bf(jnp.where(hm, q["rt"], 0.0))
        lab[p, h] = jnp.where(strict, _dot_nt(aa_m, q["bt_b"]), 0.0)
        lak[p, h] = _bf(jnp.where(strict, _dot_nt(aa_m, q["kt_b"]), 0.0))
        mrb[p, h] = _bf(jnp.where(incl, _dot_nt(rt_m, q["bt_b"]), 0.0))
        mrk[p, h] = _bf(jnp.where(incl, _dot_nt(rt_m, q["kt_b"]), 0.0))
    d = {k: eye_c - jnp.where(tx < 2, lab[k], 0.0) for k in heads}
    s = 2
    while s < C:
        level = (tx >= s) & (tx < 2 * s)
        d_b = {k: _bf(d[k]) for k in heads}
        t1 = {k: _bf(_dot(d_b[k], _bf(jnp.where(level, lab[k], 0.0)))) for k in heads}
        d = {k: d[k] - _dot(t1[k], d_b[k]) for k in heads}
        s *= 2
    d_b = {k: _bf(d[k]) for k in heads}
    w = {k: _bf(_dot(lak[k], P[k[0]]["v_b"])) for k in heads}
    ah = {k: _dot(d_b[k], P[k[0]]["aa_b"]) * P[k[0]]["rho"] for k in heads}
    uh = {k: _dot(d_b[k], w[k]) for k in heads}
    rh = {k: P[k[0]]["rt"] * P[k[0]]["rho"] - _dot(mrb[k], _bf(ah[k])) for k in heads}
    yh = {k: _dot(mrk[k], P[k[0]]["v_b"]) - _dot(mrb[k], _bf(uh[k])) for k in heads}
    for p in pairs:
        q = P[p]
        both = lambda x: jnp.where(head0, x[p, 0], x[p, 1])
        ah_p, uh_p = both(ah), both(uh)
        rh_out[0, :, q["sl"]] = both(rh)
        yh_out[0, :, q["sl"]] = both(yh)
        g_full = eye_p * q["g_last"] - _dot(_bf(ah_p.T), q["bc_b"])
        sh_full = _dot(_bf(q["v"].T), q["kc_b"]) - _dot(_bf(uh_p.T), q["bc_b"])
        g_out[0, 0, p] = jnp.where(same_head, g_full, 0.0)
        sh_out[0, 0, p] = jnp.where(same_head, sh_full, 0.0)


def _rwkv_chunks(r, lw, kh, v, kk, bb):
    nb, T, W = r.shape
    C = RWKV_CHUNK
    nc = T // C
    tri = _bf(jnp.tril(jnp.ones((C, C), F32)))
    tok = pl.BlockSpec((1, C, W), lambda b, c: (b, c, 0))
    mat = pl.BlockSpec((1, 1, N_PAIRS, LANES, LANES), lambda b, c: (b, c, 0, 0, 0))
    mat_shape = jax.ShapeDtypeStruct((nb, nc, N_PAIRS, LANES, LANES), F32)
    return pl.pallas_call(
        _rwkv_chunk_kernel,
        grid=(nb, nc),
        in_specs=[tok] * 6 + [_full((C, C))],
        out_specs=[tok, tok, mat, mat],
        out_shape=[jax.ShapeDtypeStruct((nb, T, W), F32)] * 2 + [mat_shape] * 2,
        compiler_params=_cparams("parallel", "parallel"),
        name="rwkv_chunks",
    )(r, lw, kh, v, kk, bb, tri)


def _rwkv_scan_kernel(rh_ref, yh_ref, g_ref, sh_ref, s0_ref, y_out, sfin_out, s_scr):
    c = pl.program_id(1)
    nbg = rh_ref.shape[0]

    @pl.when(c == 0)
    def _():
        s_scr[...] = s0_ref[...]

    for b in range(nbg):
        for p in range(N_PAIRS):
            sl = slice(p * LANES, (p + 1) * LANES)
            s = s_scr[b, p]
            y_out[b, :, sl] = _dot3_nt(rh_ref[b, :, sl], s) + yh_ref[b, :, sl]
            s_new = _dot3(s, g_ref[b, 0, p]) + sh_ref[b, 0, p]
            s_scr[b, p] = s_new
            sfin_out[b, p] = s_new


def _rwkv_scan(rh, yh, g, sh, s0, nbg):
    nb, T, W = rh.shape
    C = RWKV_CHUNK
    nc = T // C
    tok = pl.BlockSpec((nbg, C, W), lambda i, c: (i, c, 0))
    mat = pl.BlockSpec((nbg, 1, N_PAIRS, LANES, LANES), lambda i, c: (i, c, 0, 0, 0))
    st = pl.BlockSpec((nbg, N_PAIRS, LANES, LANES), lambda i, c: (i, 0, 0, 0))
    return pl.pallas_call(
        _rwkv_scan_kernel,
        grid=(nb // nbg, nc),
        in_specs=[tok, tok, mat, mat, st],
        out_specs=[tok, st],
        out_shape=[jax.ShapeDtypeStruct((nb, T, W), F32),
                   jax.ShapeDtypeStruct((nb, N_PAIRS, LANES, LANES), F32)],
        scratch_shapes=[pltpu.VMEM((nbg, N_PAIRS, LANES, LANES), F32)],
        compiler_params=_cparams("parallel", "arbitrary"),
        name="rwkv_scan",
    )(rh, yh, g, sh, s0)


def _state_to_pairs(s):
    nb = s.shape[0]
    s = s.reshape(nb, N_PAIRS, 2, HEAD_DIM, HEAD_DIM)
    z = jnp.zeros_like(s[:, :, 0])
    top = jnp.concatenate([s[:, :, 0], z], axis=-1)
    bot = jnp.concatenate([z, s[:, :, 1]], axis=-1)
    return jnp.concatenate([top, bot], axis=-2)


def _pairs_to_state(sp):
    nb = sp.shape[0]
    a = sp[:, :, :HEAD_DIM, :HEAD_DIM]
    b = sp[:, :, HEAD_DIM:, HEAD_DIM:]
    return jnp.stack([a, b], axis=2).reshape(nb, N_HEADS, HEAD_DIM, HEAD_DIM)


def _fox_aug_kernel(x_ref, c_ref, o_ref, *, is_query):
    lane = lax.broadcasted_iota(jnp.int32, (1, LANES), 1)
    c = c_ref[0]
    for p in range(N_PAIRS):
        xp = x_ref[0, :, p * LANES:(p + 1) * LANES].astype(F32)
        xr = pltpu.roll(xp, HEAD_DIM, axis=1)
        for h in range(2):
            hh = 2 * p + h
            ch = c[:, hh:hh + 1]
            hi = _bf(ch).astype(F32)
            r1 = ch - hi
            mid = _bf(r1).astype(F32)
            lo = _bf(r1 - mid).astype(F32)
            one = jnp.ones_like(ch)
            cols = (hi, mid, lo, one, one, one) if is_query else (one, one, one, -hi, -mid, -lo)
            aug = jnp.zeros_like(xp)
            for n, col in enumerate(cols):
                aug = jnp.where(lane == HEAD_DIM + n, col, aug)
            o_ref[0, hh] = _bf(jnp.where(lane < HEAD_DIM, xp if h == 0 else xr, aug))


def _fox_aug(x, c, tm, is_query):
    nb, S, W = x.shape
    return pl.pallas_call(
        functools.partial(_fox_aug_kernel, is_query=is_query),
        grid=(nb, S // tm),
        in_specs=[pl.BlockSpec((1, tm, W), lambda b, j: (b, j, 0)),
                  pl.BlockSpec((1, tm, LANES), lambda b, j: (b, j, 0))],
        out_specs=pl.BlockSpec((1, N_HEADS, tm, LANES), lambda b, j: (b, 0, j, 0)),
        out_shape=jax.ShapeDtypeStruct((nb, N_HEADS, S, LANES), BF16),
        compiler_params=_cparams("parallel", "parallel"),
        name="fox_aug",
    )(x, c)


def _fox_kernel(q_ref, k_ref, v_ref, o_ref, m_scr, l_scr, acc_scr, sa_scr, sb_scr, *, tk, q_off):
    i = pl.program_id(2)
    tq = q_ref.shape[2]
    q_start = q_off + i * tq
    m_scr[...] = jnp.full_like(m_scr, -jnp.inf)
    l_scr[...] = jnp.zeros_like(l_scr)
    acc_scr[...] = jnp.zeros_like(acc_scr)
    n_full = (q_start + 1) // tk

    def scores_into(j, dst):
        ks = pl.multiple_of(j * tk, tk)
        for h in range(2):
            dst[h] = _dot_nt(q_ref[0, h], k_ref[0, h, pl.ds(ks, tk), :])

    def update_from(j, src, masked):
        ks = pl.multiple_of(j * tk, tk)
        v = v_ref[0, pl.ds(ks, tk), :]
        if masked:
            visible = (ks + lax.broadcasted_iota(jnp.int32, (1, tk), 1)) <= (
                q_start + lax.broadcasted_iota(jnp.int32, (tq, 1), 0))
        for h in range(2):
            s = src[h]
            if masked:
                s = jnp.where(visible, s, -jnp.inf)
            m_old = m_scr[h]
            m_new = jnp.maximum(m_old, jnp.max(s, axis=-1, keepdims=True))
            alpha = jnp.exp(m_old - m_new)
            pr = jnp.exp(s - m_new)
            l_scr[h] = alpha * l_scr[h] + jnp.sum(pr, axis=-1, keepdims=True)
            acc_scr[h] = alpha * acc_scr[h] + _dot(_bf(pr), v)
            m_scr[h] = m_new

    def step(j, src, dst):
        scores_into(j + 1, dst)
        update_from(j, src, False)

    def two_steps(jj, carry):
        step(2 * jj, sa_scr, sb_scr)
        step(2 * jj + 1, sb_scr, sa_scr)
        return carry

    scores_into(0, sa_scr)
    lax.fori_loop(0, n_full // 2, two_steps, 0)
    odd = n_full % 2 == 1

    @pl.when(odd)
    def _():
        step(n_full - 1, sa_scr, sb_scr)
        update_from(n_full, sb_scr, True)

    @pl.when(jnp.logical_not(odd))
    def _():
        update_from(n_full, sa_scr, True)

    head0 = lax.broadcasted_iota(jnp.int32, (1, LANES), 1) < HEAD_DIM
    o_ref[0] = jnp.where(head0, acc_scr[0] / l_scr[0], acc_scr[1] / l_scr[1])


def _fox_attention(q, k, v, tq, tk, q_off):
    nb, _, sq, _ = q.shape
    sk = k.shape[2]
    for q_start in range(q_off, q_off + sq, tq):
        assert (q_start + 1) // tk + 1 == -(-(q_start + tq) // tk) <= sk // tk, (q_start, tq, tk)
    return pl.pallas_call(
        functools.partial(_fox_kernel, tk=tk, q_off=q_off),
        grid=(nb, N_PAIRS, sq // tq),
        in_specs=[pl.BlockSpec((1, 2, tq, LANES), lambda b, p, i: (b, p, i, 0)),
                  pl.BlockSpec((1, 2, sk, LANES), lambda b, p, i: (b, p, 0, 0)),
                  pl.BlockSpec((1, sk, LANES), lambda b, p, i: (b, 0, p))],
        out_specs=pl.BlockSpec((1, tq, LANES), lambda b, p, i: (b, i, p)),
        out_shape=jax.ShapeDtypeStruct((nb, sq, FOX_WIDTH), F32),
        scratch_shapes=[pltpu.VMEM((2, tq, 1), F32), pltpu.VMEM((2, tq, 1), F32),
                        pltpu.VMEM((2, tq, LANES), F32), pltpu.VMEM((2, tq, tk), F32),
                        pltpu.VMEM((2, tq, tk), F32)],
        compiler_params=_cparams("parallel", "parallel", "arbitrary"),
        name="fox_attention",
    )(q, k, v)


def _out_kernel(x_ref, y_ref, bonus_ref, g_ref, oa_ref, og_ref, lnw_ref, lnb_ref, gs_ref, wout_ref,
                gffn_ref, rwh_ref, rwl_ref, h_out, xn_out, sc_out):
    gs = gs_ref[...]
    inv_d = 1.0 / HEAD_DIM
    y = y_ref[...]
    mean = _dot_x01(y, gs) * inv_d
    d = y - mean
    var = _dot_x01(d * d, gs) * inv_d
    yn = d * lax.rsqrt(var + GN_EPS) * lnw_ref[...] + lnb_ref[...]
    o_rwkv = (yn + bonus_ref[...]) * g_ref[...]
    o_fox = oa_ref[...] * jax.nn.sigmoid(og_ref[...])
    mix = jnp.concatenate([_bf(o_rwkv), _bf(o_fox)], axis=-1)
    h = x_ref[...] + _dot(mix, wout_ref[...])
    h_out[...] = h
    xn = _rms(h, gffn_ref[...])
    _store_chunked(xn_out, xn)
    xh, xl = _split2(xn)
    logits = _dot(xh, rwh_ref[...]) + _dot(xl, rwh_ref[...]) + _dot(xh, rwl_ref[...])
    sc_out[...] = jax.nn.sigmoid(logits)


def _out_proj(x, y, bonus, g, oa, og, lnw, lnb, gs, wout, gffn, rwh, rwl, tm):
    T = x.shape[0]
    W = RWKV_WIDTH
    tok = lambda width: pl.BlockSpec((tm, width), lambda i: (i, 0))
    return pl.pallas_call(
        _out_kernel,
        grid=(T // tm,),
        in_specs=[tok(D_MODEL)] + [tok(W)] * 5 + [_full((1, W)), _full((1, W)), _full((W, W)),
                                                  _full((D_MODEL, D_MODEL)), _full((1, D_MODEL)),
                                                  _full((D_MODEL, N_EXPERTS)), _full((D_MODEL, N_EXPERTS))],
        out_specs=[tok(D_MODEL), pl.BlockSpec((tm * ROW_CHUNKS, LANES), lambda i: (i, 0)), tok(N_EXPERTS)],
        out_shape=[jax.ShapeDtypeStruct((T, D_MODEL), F32),
                   jax.ShapeDtypeStruct((T * ROW_CHUNKS, LANES), F32),
                   jax.ShapeDtypeStruct((T, N_EXPERTS), F32)],
        compiler_params=_cparams("parallel"),
        name="out_proj",
    )(x, y, bonus, g, oa, og, lnw, lnb, gs, wout, gffn, rwh, rwl)


def _route_kernel(sc_ref, bias_ref, before_ref, idx_out, wt_out, rank_out, cnt_out, cnt_scr):
    tm = sc_ref.shape[0]
    neg = -jnp.inf

    @pl.when(pl.program_id(0) == 0)
    def _():
        cnt_scr[...] = jnp.zeros_like(cnt_scr)

    st = sc_ref[...].T
    sel = st + bias_ref[...]
    gscore = []
    for gi in range(N_GROUPS):
        blk = sel[gi * GROUP_SIZE:(gi + 1) * GROUP_SIZE, :]
        m1 = jnp.max(blk, axis=0, keepdims=True)
        n1 = jnp.sum((blk == m1).astype(F32), axis=0, keepdims=True)
        m2 = jnp.max(jnp.where(blk < m1, blk, neg), axis=0, keepdims=True)
        gscore.append(m1 + jnp.where(n1 > 1.0, m1, m2))
    taken = [jnp.zeros((1, tm), jnp.bool_) for _ in range(N_GROUPS)]
    for _ in range(TOPK_GROUPS):
        avail = [jnp.where(taken[gi], neg, gscore[gi]) for gi in range(N_GROUPS)]
        best = functools.reduce(jnp.maximum, avail)
        found = jnp.zeros((1, tm), jnp.bool_)
        for gi in range(N_GROUPS):
            hit = (avail[gi] == best) & jnp.logical_not(found)
            taken[gi] = taken[gi] | hit
            found = found | hit
    cand = jnp.concatenate(
        [jnp.where(taken[gi], sel[gi * GROUP_SIZE:(gi + 1) * GROUP_SIZE, :], neg) for gi in range(N_GROUPS)], axis=0)
    eid = lax.broadcasted_iota(jnp.int32, (N_EXPERTS, tm), 0).astype(F32)
    idxs, wts = [], []
    onehot = jnp.zeros((N_EXPERTS, tm), F32)
    for _ in range(TOP_K):
        best = jnp.max(cand, axis=0, keepdims=True)
        pick = jnp.min(jnp.where(cand == best, eid, float(N_EXPERTS)), axis=0, keepdims=True)
        chosen = eid == pick
        wts.append(jnp.sum(jnp.where(chosen, st, 0.0), axis=0, keepdims=True))
        idxs.append(pick)
        cand = jnp.where(chosen, neg, cand)
        onehot = jnp.where(chosen, 1.0, onehot)
    w = jnp.concatenate(wts, axis=0)
    idx_out[...] = jnp.concatenate(idxs, axis=0).astype(jnp.int32)
    wt_out[...] = w / jnp.sum(w, axis=0, keepdims=True) * ROUTED_SCALE
    earlier = _dot(_bf(onehot), before_ref[...]) + cnt_scr[...]
    rank_out[...] = jnp.concatenate(
        [jnp.sum(jnp.where(eid == pick, earlier, 0.0), axis=0, keepdims=True) for pick in idxs],
        axis=0).astype(jnp.int32)
    cnt_scr[...] += jnp.sum(onehot, axis=1, keepdims=True)
    cnt_out[...] = cnt_scr[...]


def _route(scores, bias_col, tm):
    T = scores.shape[0]
    before = _bf(jnp.triu(jnp.ones((tm, tm), F32), 1))
    tok = pl.BlockSpec((TOP_K, tm), lambda i: (0, i))
    return pl.pallas_call(
        _route_kernel,
        grid=(T // tm,),
        in_specs=[pl.BlockSpec((tm, N_EXPERTS), lambda i: (i, 0)), _full((N_EXPERTS, 1)), _full((tm, tm))],
        out_specs=[tok, tok, tok, _full((N_EXPERTS, 1))],
        out_shape=[jax.ShapeDtypeStruct((TOP_K, T), jnp.int32), jax.ShapeDtypeStruct((TOP_K, T), F32),
                   jax.ShapeDtypeStruct((TOP_K, T), jnp.int32), jax.ShapeDtypeStruct((N_EXPERTS, 1), F32)],
        scratch_shapes=[pltpu.VMEM((N_EXPERTS, 1), F32)],
        compiler_params=_cparams("arbitrary"),
        name="route",
    )(scores, bias_col, before)


def _pos_kernel(idx_ref, rank_ref, start_ref, pos_out):
    tm = idx_ref.shape[1]
    eid = lax.broadcasted_iota(jnp.int32, (N_EXPERTS, tm), 0)
    idx = idx_ref[...]
    start = start_ref[...]
    base = jnp.concatenate(
        [jnp.sum(jnp.where(eid == idx[k:k + 1, :], start, 0.0), axis=0, keepdims=True) for k in range(TOP_K)],
        axis=0)
    pos_out[...] = rank_ref[...] + base.astype(jnp.int32)


def _positions(eidx_t, rank_t, start_col, tm):
    T = eidx_t.shape[1]
    tok = pl.BlockSpec((TOP_K, tm), lambda i: (0, i))
    return pl.pallas_call(
        _pos_kernel,
        grid=(T // tm,),
        in_specs=[tok, tok, _full((N_EXPERTS, 1))],
        out_specs=tok,
        out_shape=jax.ShapeDtypeStruct((TOP_K, T), jnp.int32),
        compiler_params=_cparams("parallel"),
        name="moe_positions",
    )(eidx_t, rank_t, start_col)


def _dispatch_kernel(pos_ref, x_ref, xs_in, xs_out, sem):
    del xs_in
    tm = pos_ref.shape[1]

    def issue(t, carry):
        src = x_ref.at[pl.ds(pl.multiple_of(t * ROW_CHUNKS, ROW_CHUNKS), ROW_CHUNKS), :]
        for k in range(TOP_K):
            row = pl.multiple_of(pos_ref[k, t] * ROW_CHUNKS, ROW_CHUNKS)
            pltpu.make_async_copy(src, xs_out.at[pl.ds(row, ROW_CHUNKS), :], sem).start()
        return carry

    lax.fori_loop(0, tm, issue, 0)
    for k in range(TOP_K):
        pltpu.make_async_copy(x_ref, xs_out.at[pl.ds(0, tm * ROW_CHUNKS), :], sem).wait()


def _dispatch(pos_t, xn_chunked, n_rows, tm):
    T = pos_t.shape[1]
    xs0 = jnp.zeros((n_rows * ROW_CHUNKS, LANES), F32)
    return pl.pallas_call(
        _dispatch_kernel,
        grid=(T // tm,),
        in_specs=[pl.BlockSpec((TOP_K, tm), lambda i: (0, i), memory_space=pltpu.SMEM),
                  pl.BlockSpec((tm * ROW_CHUNKS, LANES), lambda i: (i, 0)),
                  pl.BlockSpec(memory_space=pl.ANY)],
        out_specs=pl.BlockSpec(memory_space=pl.ANY),
        out_shape=jax.ShapeDtypeStruct((n_rows * ROW_CHUNKS, LANES), F32),
        scratch_shapes=[pltpu.SemaphoreType.DMA(())],
        input_output_aliases={2: 0},
        compiler_params=_cparams("arbitrary"),
        name="moe_dispatch",
    )(pos_t, xn_chunked, xs0)


def _gmm_kernel(be_ref, nb_ref, xs_ref, wg_ref, wu_ref, wd_ref, o_ref):
    i = pl.program_id(0)
    bm = MOE_ROWS

    @pl.when(i < nb_ref[0])
    def _():
        xe = _bf(_load_chunked(xs_ref, bm))
        hg = _dot(xe, _bf(wg_ref[0]))
        hu = _dot(xe, _bf(wu_ref[0]))
        hdn = hg * jax.nn.sigmoid(hg) * hu
        _store_chunked(o_ref, _dot(_bf(hdn), _bf(wd_ref[0])))

    @pl.when(i >= nb_ref[0])
    def _():
        o_ref[...] = jnp.zeros_like(o_ref)


def _gmm(blk_e, n_used, xs, wg, wu, wd):
    bm = MOE_ROWS
    n_blocks = xs.shape[0] // (bm * ROW_CHUNKS)
    rows = pl.BlockSpec((bm * ROW_CHUNKS, LANES), lambda i, be, nb: (i, 0))
    grid_spec = pltpu.PrefetchScalarGridSpec(
        num_scalar_prefetch=2,
        grid=(n_blocks,),
        in_specs=[rows,
                  pl.BlockSpec((1, D_MODEL, EXPERT_FF), lambda i, be, nb: (be[i], 0, 0)),
                  pl.BlockSpec((1, D_MODEL, EXPERT_FF), lambda i, be, nb: (be[i], 0, 0)),
                  pl.BlockSpec((1, EXPERT_FF, D_MODEL), lambda i, be, nb: (be[i], 0, 0))],
        out_specs=rows,
    )
    return pl.pallas_call(
        _gmm_kernel,
        grid_spec=grid_spec,
        out_shape=jax.ShapeDtypeStruct(xs.shape, F32),
        compiler_params=_cparams("arbitrary"),
        name="expert_gmm",
    )(blk_e, n_used, xs, wg, wu, wd)


def _final_kernel(pos_ref, h_ref, xn_ref, w_ref, p_ref, ys_ref, sg_ref, su_ref, sd_ref, gple_ref, wpg_ref,
                  wpp_ref, gfin_ref, y_out, buf, sems):
    tm = h_ref.shape[0]

    def issue(t, carry):
        dst = pl.ds(pl.multiple_of(t * ROW_CHUNKS, ROW_CHUNKS), ROW_CHUNKS)
        for k in range(TOP_K):
            row = pl.multiple_of(pos_ref[k, t] * ROW_CHUNKS, ROW_CHUNKS)
            pltpu.make_async_copy(ys_ref.at[pl.ds(row, ROW_CHUNKS), :], buf.at[k, dst, :], sems.at[k]).start()
        return carry

    lax.fori_loop(0, tm, issue, 0)
    xb = _bf(_load_chunked(xn_ref, tm))
    hg = _dot(xb, sg_ref[...])
    hu = _dot(xb, su_ref[...])
    shared = _dot(_bf(hg * jax.nn.sigmoid(hg) * hu), sd_ref[...])
    w = w_ref[...]
    routed = jnp.zeros((tm, D_MODEL), F32)
    for k in range(TOP_K):
        pltpu.make_async_copy(ys_ref.at[pl.ds(0, tm * ROW_CHUNKS), :], buf.at[k], sems.at[k]).wait()
        routed = routed + _load_chunked(buf.at[k], tm) * w[:, k:k + 1]
    h = h_ref[...] + (routed + shared)
    gate = jax.nn.sigmoid(_dot(_bf(_rms(h, gple_ref[...])), wpg_ref[...]))
    h = h + gate * _dot(_bf(p_ref[...]), wpp_ref[...])
    y_out[...] = _rms(h, gfin_ref[...])


def _final(pos_t, h, xn_chunked, w, p, ys, sg, su, sd, gple, wpg, wpp, gfin, tm):
    T = h.shape[0]
    tok = lambda width: pl.BlockSpec((tm, width), lambda i: (i, 0))
    return pl.pallas_call(
        _final_kernel,
        grid=(T // tm,),
        in_specs=[pl.BlockSpec((TOP_K, tm), lambda i: (0, i), memory_space=pltpu.SMEM),
                  tok(D_MODEL), pl.BlockSpec((tm * ROW_CHUNKS, LANES), lambda i: (i, 0)), tok(TOP_K),
                  tok(PLE_DIM), pl.BlockSpec(memory_space=pl.ANY),
                  _full((D_MODEL, EXPERT_FF)), _full((D_MODEL, EXPERT_FF)), _full((EXPERT_FF, D_MODEL)),
                  _full((1, D_MODEL)), _full((D_MODEL, D_MODEL)), _full((PLE_DIM, D_MODEL)),
                  _full((1, D_MODEL))],
        out_specs=tok(D_MODEL),
        out_shape=jax.ShapeDtypeStruct((T, D_MODEL), F32),
        scratch_shapes=[pltpu.VMEM((TOP_K, tm * ROW_CHUNKS, LANES), F32), pltpu.SemaphoreType.DMA((TOP_K,))],
        compiler_params=_cparams("arbitrary"),
        name="ffn_tail",
    )(pos_t, h, xn_chunked, w, p, ys, sg, su, sd, gple, wpg, wpp, gfin)


def _pad_cols(a, width):
    return jnp.pad(a, [(0, 0)] * (a.ndim - 1) + [(0, width - a.shape[-1])])


def _rwkv_pad_cols(a):
    W = RWKV_WIDTH
    o1, o2, o3 = 3 * W, 3 * W + DECAY_RANK, 3 * W + DECAY_RANK + ICL_RANK
    return jnp.concatenate([a[..., :o1], _pad_cols(a[..., o1:o2], 128), _pad_cols(a[..., o2:o3], 128),
                            a[..., o3:]], axis=-1)


def _rwkv_unpad_cols(a):
    W = RWKV_WIDTH
    return jnp.concatenate([a[..., :3 * W + DECAY_RANK], a[..., 3 * W + 128:3 * W + 128 + ICL_RANK],
                            a[..., 3 * W + 256:]], axis=-1)


def _pad_rows(a, rows):
    return jnp.pad(a, [(0, rows - a.shape[0])] + [(0, 0)] * (a.ndim - 1))


def _mixer(x, shift, s0, past, wts, tm):
    nb, T, _ = x.shape
    (r, lw, kh, v, kkn, bb, g, bonus, last) = _in_rwkv(
        x, _rwkv_pad_cols(shift), wts["gmix"], wts["w_rwkv"], wts["mu"], wts["w0"], wts["wup"], wts["a0"],
        wts["aup"], wts["gup"], wts["k_k"], wts["k_a"], wts["r_k"], wts["gs"], tm)
    q_b, k_f, k_b, v_f, v_b, og, logf = _in_fox(
        x, wts["gmix"], wts["w_fox"], wts["qn"], wts["kn"], wts["fb"], wts["gs"], tm)

    C = RWKV_CHUNK
    Tp = -(-T // C) * C
    if Tp != T:
        padt = lambda a: jnp.pad(a, ((0, 0), (0, Tp - T), (0, 0)))
        r_p, lw_p, kh_p, v_p, kk_p, bb_p = (padt(a) for a in (r, lw, kh, v, kkn, bb))
    else:
        r_p, lw_p, kh_p, v_p, kk_p, bb_p = r, lw, kh, v, kkn, bb
    rh, yh, gm, sh = _rwkv_chunks(r_p, lw_p, kh_p, v_p, kk_p, bb_p)
    y, s_fin = _rwkv_scan(rh, yh, gm, sh, _state_to_pairs(s0.astype(F32)), 4 if nb % 4 == 0 else 1)
    y = y[:, :T]

    if past is None:
        c = _cumsum(logf, min(T, 512))
        tq, tk = min(T, FOX_TQ), min(T, FOX_TK)
        o_att = _fox_attention(_fox_aug(q_b, c, tq, True), _fox_aug(k_b, c, tq, False), v_b, tq, tk, 0)
    else:
        k_past, v_past, lf_past = past
        P = k_past.shape[1]
        tk = FOX_TK_CACHED
        sk = -(-(P + T) // tk) * tk
        pads = lambda a: jnp.pad(a, ((0, 0), (0, sk - P - T), (0, 0)))
        k_all = pads(jnp.concatenate([_bf(k_past.reshape(nb, P, FOX_WIDTH)), k_b], axis=1))
        v_all = pads(jnp.concatenate([_bf(v_past.reshape(nb, P, FOX_WIDTH)), v_b], axis=1))
        lf_all = pads(jnp.concatenate([_pad_cols(lf_past.astype(F32), 128), logf], axis=1))
        c = _cumsum(lf_all, tk)
        o_att = _fox_attention(_fox_aug(q_b, c[:, P:P + T], T, True), _fox_aug(k_all, c, tk, False), v_all,
                               T, tk, P)

    n = nb * T
    flat = lambda a: a.reshape(n, a.shape[-1])
    feats = (flat(y), flat(bonus), flat(g), flat(o_att), flat(og))
    state = (k_f.reshape(nb, T, N_HEADS, HEAD_DIM), v_f.reshape(nb, T, N_HEADS, HEAD_DIM),
             logf[:, :, :N_HEADS], _pairs_to_state(s_fin), _rwkv_unpad_cols(last))
    return feats, state


def _block_tables(counts, n_blocks):
    blk = MOE_ROWS
    counts = counts.reshape(N_EXPERTS).astype(jnp.int32)
    padded = (counts + blk - 1) // blk * blk
    pend = jnp.cumsum(padded)
    first_row = jnp.arange(n_blocks, dtype=jnp.int32)[:, None] * blk
    blk_e = jnp.minimum(jnp.sum((pend[None, :] <= first_row).astype(jnp.int32), axis=1), N_EXPERTS - 1)
    n_used = (pend[-1] // blk).astype(jnp.int32).reshape(1)
    return (pend - padded).astype(F32).reshape(N_EXPERTS, 1), blk_e.astype(jnp.int32), n_used


def kernel(x_prompt, x_sample, cache_fox_k, cache_fox_v, cache_fox_logf, state_rwkv_wkv, state_rwkv_shift, p_prompt, p_sample, norm_mix_g, w_in, rwkv_mu, rwkv_w0, rwkv_w_up, rwkv_a0, rwkv_a_up, rwkv_g_up, rwkv_k_k, rwkv_k_a, rwkv_r_k, rwkv_ln_w, rwkv_ln_b, fox_q_norm, fox_k_norm, fox_f_bias, w_out, norm_ffn_g, router_w, router_bias, exp_w_gate, exp_w_up, exp_w_down, shared_w_gate, shared_w_up, shared_w_down, ple_norm_g, ple_w_gate, ple_w_proj, final_norm_g):
    assert w_in.shape[0] == 1, "single-layer kernel"
    W = RWKV_WIDTH
    row = lambda a: a.reshape(1, -1).astype(F32)
    tile_heads = lambda a: jnp.tile(a.reshape(1, HEAD_DIM), (1, N_HEADS)).astype(F32)
    hid = jnp.arange(W) // HEAD_DIM
    w_in0 = w_in[0]
    router_hi = _bf(router_w[0])
    wts = {
        "gmix": row(norm_mix_g[0]),
        "w_rwkv": _bf(_rwkv_pad_cols(w_in0[:, :RWKV_IN])),
        "w_fox": _bf(_pad_cols(w_in0[:, RWKV_IN:], FOX_PAD)),
        "mu": row(_rwkv_pad_cols(rwkv_mu[0])),
        "w0": row(rwkv_w0[0]),
        "wup": _bf(_pad_rows(rwkv_w_up[0], 128)),
        "a0": row(rwkv_a0[0]),
        "aup": _bf(_pad_rows(rwkv_a_up[0], 128)),
        "gup": _bf(rwkv_g_up[0]),
        "k_k": row(rwkv_k_k[0]),
        "k_a": row(rwkv_k_a[0]),
        "r_k": row(rwkv_r_k[0]),
        "gs": _bf((hid[:, None] == hid[None, :]).astype(F32)),
        "qn": tile_heads(fox_q_norm[0]),
        "kn": tile_heads(fox_k_norm[0]),
        "fb": _pad_cols(row(fox_f_bias[0]), 128),
    }
    nbp, Tp, _ = x_prompt.shape
    nbs, Ts, _ = x_sample.shape
    s0_prompt = jnp.zeros((nbp, N_HEADS, HEAD_DIM, HEAD_DIM), F32)
    shift0_prompt = jnp.zeros((nbp, 1, RWKV_IN), F32)
    feats_p, st_p = _mixer(x_prompt, shift0_prompt, s0_prompt, None, wts, min(Tp, 256))
    feats_s, st_s = _mixer(x_sample, state_rwkv_shift[0], state_rwkv_wkv[0],
                           (cache_fox_k[0], cache_fox_v[0], cache_fox_logf[0]), wts, Ts)

    n_p, n_s = nbp * Tp, nbs * Ts
    n_tok = n_p + n_s
    x_all = jnp.concatenate([x_prompt.reshape(n_p, D_MODEL), x_sample.reshape(n_s, D_MODEL)], axis=0)
    feats = [jnp.concatenate([a, b], axis=0) for a, b in zip(feats_p, feats_s)]
    tm = math.gcd(n_tok, 256)
    h1, xn2, scores = _out_proj(x_all, *feats, row(rwkv_ln_w[0]), row(rwkv_ln_b[0]), wts["gs"], _bf(w_out[0]),
                                row(norm_ffn_g[0]), router_hi,
                                _bf(router_w[0] - router_hi.astype(F32)), tm)
    eidx_t, wts_t, rank_t, counts = _route(scores, router_bias[0].reshape(N_EXPERTS, 1).astype(F32), tm)
    n_blocks = -(-n_tok * TOP_K // MOE_ROWS) + N_EXPERTS
    start_col, blk_e, n_used = _block_tables(counts, n_blocks)
    pos_t = _positions(eidx_t, rank_t, start_col, tm)
    xs = _dispatch(pos_t, xn2, n_blocks * MOE_ROWS, tm)
    y_rows = _gmm(blk_e, n_used, xs, exp_w_gate[0], exp_w_up[0], exp_w_down[0])
    p_all = jnp.concatenate([p_prompt[0].reshape(n_p, PLE_DIM), p_sample[0].reshape(n_s, PLE_DIM)], axis=0)
    y_all = _final(pos_t, h1, xn2, wts_t.T, p_all, y_rows, _bf(shared_w_gate[0]), _bf(shared_w_up[0]),
                   _bf(shared_w_down[0]), row(ple_norm_g[0]), _bf(ple_w_gate[0]), _bf(ple_w_proj[0]),
                   row(final_norm_g), tm)
    y_prompt = y_all[:n_p].reshape(nbp, Tp, D_MODEL)
    y_sample = y_all[n_p:].reshape(nbs, Ts, D_MODEL)
    lead = lambda t: tuple(a[None] for a in t)
    return (y_prompt, y_sample) + lead(st_p) + lead(st_s)
```

```python
import functools
import math

import numpy as np
import jax
import jax.numpy as jnp
from jax import lax
from jax.experimental import pallas as pl
from jax.experimental.pallas import tpu as pltpu

F32 = jnp.float32
BF16 = jnp.bfloat16

D_MODEL = 1024
HEAD_DIM = 64
RWKV_WIDTH = 512
FOX_WIDTH = 512
N_HEADS = 8
N_PAIRS = N_HEADS // 2
DECAY_RANK = 64
ICL_RANK = 64
GATE_RANK = 128
RWKV_IN = 3 * RWKV_WIDTH + DECAY_RANK + ICL_RANK + GATE_RANK
RWKV_PAD = 3 * RWKV_WIDTH + 3 * 128
FOX_PAD = 4 * FOX_WIDTH + 128
ATTN_SCALE = HEAD_DIM ** -0.5
N_EXPERTS = 256
N_GROUPS = 8
GROUP_SIZE = N_EXPERTS // N_GROUPS
TOPK_GROUPS = 4
TOP_K = 8
EXPERT_FF = 256
ROUTED_SCALE = 2.5
PLE_DIM = 256
RMS_EPS = 1e-6
GN_EPS = 64e-5
L2_EPS = 1e-12

LANES = 128
ROW_CHUNKS = D_MODEL // LANES
RWKV_CHUNK = 128
MOE_ROWS = 256
TOKEN_TILE = 256
FOX_TQ, FOX_TK = 256, 1024
FOX_TK_CACHED = 768
VMEM_LIMIT = 56 * 1024 * 1024


def _cparams(*sem):
    return pltpu.CompilerParams(dimension_semantics=sem, vmem_limit_bytes=VMEM_LIMIT)


def _bf(x):
    return x.astype(BF16)


def _dot(a, b):
    return jnp.dot(a, b, preferred_element_type=F32)


def _dot_nt(a, b):
    return lax.dot_general(a, b, (((1,), (1,)), ((), ())), preferred_element_type=F32)


def _split2(x):
    hi = _bf(x)
    return hi, _bf(x - hi.astype(F32))


def _split3(x):
    hi = _bf(x)
    r1 = x - hi.astype(F32)
    mid = _bf(r1)
    return hi, mid, _bf(r1 - mid.astype(F32))


def _dot_x01(x, w01):
    hi, lo = _split2(x)
    return _dot(hi, w01) + _dot(lo, w01)


def _dot3(a, b):
    ah, al = _split2(a)
    bh, bl = _split2(b)
    return _dot(ah, bh) + _dot(al, bh) + _dot(ah, bl)


def _dot3_nt(a, b):
    ah, al = _split2(a)
    bh, bl = _split2(b)
    return _dot_nt(ah, bh) + _dot_nt(al, bh) + _dot_nt(ah, bl)


def _softplus(x):
    return jnp.maximum(x, 0.0) + jnp.log1p(jnp.exp(-jnp.abs(x)))


def _rms(x, g):
    return x * lax.rsqrt(jnp.mean(x * x, axis=-1, keepdims=True) + RMS_EPS) * g


def _full(shape):
    return pl.BlockSpec(shape, lambda *_: (0,) * len(shape))


def _store_chunked(ref, x):
    n = x.shape[0]
    for s in range(ROW_CHUNKS):
        ref[pl.ds(s, n, stride=ROW_CHUNKS), :] = x[:, s * LANES:(s + 1) * LANES]


def _load_chunked(ref, n):
    return jnp.concatenate([ref[pl.ds(s, n, stride=ROW_CHUNKS), :] for s in range(ROW_CHUNKS)], axis=1)


def _in_rwkv_kernel(x_ref, shift_ref, gmix_ref, w_ref, mu_ref, w0_ref, wup_ref, a0_ref, aup_ref, gup_ref,
                    kk_ref, ka_ref, rk_ref, gs_ref,
                    r_out, lw_out, kh_out, v_out, kkn_out, bb_out, g_out, bonus_out, last_out,
                    carry_ref):
    j = pl.program_id(1)
    tm = x_ref.shape[1]
    xn = _bf(_rms(x_ref[0], gmix_ref[...]))
    u = _dot(xn, w_ref[...])
    first = jnp.where(j == 0, shift_ref[0], carry_ref[...])
    row = lax.broadcasted_iota(jnp.int32, (tm, 1), 0)
    prev = jnp.where(row == 0, first, pltpu.roll(u, 1, axis=0))
    carry_ref[...] = u[tm - 1:tm, :]
    last_out[0] = u[tm - 1:tm, :]
    xs = u + (prev - u) * mu_ref[...]
    W = RWKV_WIDTH
    r, k, v = xs[:, :W], xs[:, W:2 * W], xs[:, 2 * W:3 * W]
    xw, xa, xg = xs[:, 3 * W:3 * W + 128], xs[:, 3 * W + 128:3 * W + 256], xs[:, 3 * W + 256:]
    w_raw = w0_ref[...] + _dot(_bf(jnp.tanh(xw)), wup_ref[...])
    lw = -jnp.exp(-_softplus(-w_raw) - 0.5)
    a = jax.nn.sigmoid(a0_ref[...] + _dot(_bf(xa), aup_ref[...]))
    g = _dot(_bf(jax.nn.sigmoid(xg)), gup_ref[...])
    gs = gs_ref[...]
    kk = k * kk_ref[...]
    kkn = kk / jnp.maximum(jnp.sqrt(_dot_x01(kk * kk, gs)), L2_EPS)
    kh = k * (1.0 + (a - 1.0) * ka_ref[...])
    r_out[0] = r
    lw_out[0] = lw
    kh_out[0] = kh
    v_out[0] = v
    kkn_out[0] = kkn
    bb_out[0] = kkn * a
    g_out[0] = g
    bonus_out[0] = _dot_x01(r * kh * rk_ref[...], gs) * v


def _in_rwkv(x, shift, gmix, w, mu, w0, wup, a0, aup, gup, k_k, k_a, r_k, gs, tm):
    nb, T, _ = x.shape
    W = RWKV_WIDTH
    tok = lambda width: pl.BlockSpec((1, tm, width), lambda b, j: (b, j, 0))
    outs = [jax.ShapeDtypeStruct((nb, T, W), F32)] * 8 + [jax.ShapeDtypeStruct((nb, 1, RWKV_PAD), F32)]
    return pl.pallas_call(
        _in_rwkv_kernel,
        grid=(nb, T // tm),
        in_specs=[tok(D_MODEL), pl.BlockSpec((1, 1, RWKV_PAD), lambda b, j: (b, 0, 0)),
                  _full((1, D_MODEL)), _full((D_MODEL, RWKV_PAD)), _full((1, RWKV_PAD)),
                  _full((1, W)), _full((128, W)), _full((1, W)), _full((128, W)), _full((128, W)),
                  _full((1, W)), _full((1, W)), _full((1, W)), _full((W, W))],
        out_specs=[tok(W)] * 8 + [pl.BlockSpec((1, 1, RWKV_PAD), lambda b, j: (b, 0, 0))],
        out_shape=outs,
        scratch_shapes=[pltpu.VMEM((1, RWKV_PAD), F32)],
        compiler_params=_cparams("parallel", "arbitrary"),
        name="in_rwkv",
    )(x, shift, gmix, w, mu, w0, wup, a0, aup, gup, k_k, k_a, r_k, gs)


def _in_fox_kernel(x_ref, gmix_ref, w_ref, qn_ref, kn_ref, fb_ref, gs_ref,
                   q_out, k_out, kb_out, v_out, vb_out, og_out, lf_out):
    xn = _bf(_rms(x_ref[0], gmix_ref[...]))
    u = _dot(xn, w_ref[...])
    W = FOX_WIDTH
    q, k, v, og, fl = u[:, :W], u[:, W:2 * W], u[:, 2 * W:3 * W], u[:, 3 * W:4 * W], u[:, 4 * W:]
    gs = gs_ref[...]
    inv_d = 1.0 / HEAD_DIM
    qn = q * lax.rsqrt(_dot_x01(q * q, gs) * inv_d + RMS_EPS) * qn_ref[...]
    kn = k * lax.rsqrt(_dot_x01(k * k, gs) * inv_d + RMS_EPS) * kn_ref[...]
    q_out[0] = _bf(qn * ATTN_SCALE)
    k_out[0] = kn
    kb_out[0] = _bf(kn)
    v_out[0] = v
    vb_out[0] = _bf(v)
    og_out[0] = og
    lf_out[0] = -_softplus(-(fl + fb_ref[...]))


def _in_fox(x, gmix, w, qn, kn, fb, gs, tm):
    nb, T, _ = x.shape
    W = FOX_WIDTH
    tok = lambda width: pl.BlockSpec((1, tm, width), lambda b, j: (b, j, 0))
    sds = lambda width, dt: jax.ShapeDtypeStruct((nb, T, width), dt)
    return pl.pallas_call(
        _in_fox_kernel,
        grid=(nb, T // tm),
        in_specs=[tok(D_MODEL), _full((1, D_MODEL)), _full((D_MODEL, FOX_PAD)),
                  _full((1, W)), _full((1, W)), _full((1, 128)), _full((W, W))],
        out_specs=[tok(W)] * 6 + [tok(128)],
        out_shape=[sds(W, BF16), sds(W, F32), sds(W, BF16), sds(W, F32), sds(W, BF16), sds(W, F32),
                   sds(128, F32)],
        compiler_params=_cparams("parallel", "parallel"),
        name="in_fox",
    )(x, gmix, w, qn, kn, fb, gs)


def _cumsum_kernel(x_ref, tri_ref, o_ref, carry_ref):
    j = pl.program_id(1)
    ts = x_ref.shape[1]

    @pl.when(j == 0)
    def _():
        carry_ref[...] = jnp.zeros_like(carry_ref)

    hi, mid, lo = _split3(x_ref[0])
    tri = tri_ref[...]
    c = _dot(tri, hi) + _dot(tri, mid) + _dot(tri, lo) + carry_ref[...]
    o_ref[0] = c
    carry_ref[...] = c[ts - 1:ts, :]


def _cumsum(x, ts):
    nb, S, L = x.shape
    tri = _bf(jnp.tril(jnp.ones((ts, ts), F32)))
    return pl.pallas_call(
        _cumsum_kernel,
        grid=(nb, S // ts),
        in_specs=[pl.BlockSpec((1, ts, L), lambda b, j: (b, j, 0)), _full((ts, ts))],
        out_specs=pl.BlockSpec((1, ts, L), lambda b, j: (b, j, 0)),
        out_shape=jax.ShapeDtypeStruct((nb, S, L), F32),
        scratch_shapes=[pltpu.VMEM((1, L), F32)],
        compiler_params=_cparams("parallel", "arbitrary"),
        name="seq_cumsum",
    )(x, tri)


def _rwkv_chunk_kernel(r_ref, lw_ref, kh_ref, v_ref, kk_ref, bb_ref, tri_ref,
                       rh_out, yh_out, g_out, sh_out):
    C = r_ref.shape[1]
    ti = lax.broadcasted_iota(jnp.int32, (C, C), 0)
    si = lax.broadcasted_iota(jnp.int32, (C, C), 1)
    tx = ti ^ si
    strict = ti > si
    incl = ti >= si
    eye_c = (ti == si).astype(F32)
    lane = lax.broadcasted_iota(jnp.int32, (1, LANES), 1)
    head0 = lane < HEAD_DIM
    pi = lax.broadcasted_iota(jnp.int32, (LANES, LANES), 0)
    pj = lax.broadcasted_iota(jnp.int32, (LANES, LANES), 1)
    same_head = (pi < HEAD_DIM) == (pj < HEAD_DIM)
    eye_p = (pi == pj).astype(F32)
    tri = tri_ref[...]

    pairs = range(N_PAIRS)
    heads = [(p, h) for p in pairs for h in range(2)]
    P = []
    for p in pairs:
        sl = slice(p * LANES, (p + 1) * LANES)
        r, lw, kh, v, kk, bb = (ref[0, :, sl] for ref in (r_ref, lw_ref, kh_ref, v_ref, kk_ref, bb_ref))
        l_hi, l_mid, l_lo = _split3(lw)
        lc = _dot(tri, l_hi) + _dot(tri, l_mid) + _dot(tri, l_lo)
        mid = lc[C // 2 - 1:C // 2, :]
        last = lc[C - 1:C, :]
        e_dn = jnp.exp(mid - lc)
        e_end = jnp.exp(last - lc)
        aa = kk * jnp.exp(lc - lw - mid)
        P.append(dict(sl=sl, v=v, aa=aa, rt=r * jnp.exp(lc - mid), rho=jnp.exp(mid), g_last=jnp.exp(last),
                      bt_b=_bf(bb * e_dn), kt_b=_bf(kh * e_dn), bc_b=_bf(bb * e_end), kc_b=_bf(kh * e_end),
                      aa_b=_bf(aa), v_b=_bf(v)))
    lab, lak, mrb, mrk = {}, {}, {}, {}
    for p, h in heads:
        q = P[p]
        hm = head0 if h == 0 else jnp.logical_not(head0)
        aa_m = _bf(jnp.where(hm, q["aa"], 0.0))
        rt_m = _bf(jnp.where(hm, q["rt"], 0.0))
        lab[p, h] = jnp.where(strict, _dot_nt(aa_m, q["bt_b"]), 0.0)
        lak[p, h] = _bf(jnp.where(strict, _dot_nt(aa_m, q["kt_b"]), 0.0))
        mrb[p, h] = _bf(jnp.where(incl, _dot_nt(rt_m, q["bt_b"]), 0.0))
        mrk[p, h] = _bf(jnp.where(incl, _dot_nt(rt_m, q["kt_b"]), 0.0))
    d = {k: eye_c - jnp.where(tx < 2, lab[k], 0.0) for k in heads}
    s = 2
    while s < C:
        level = (tx >= s) & (tx < 2 * s)
        d_b = {k: _bf(d[k]) for k in heads}
        t1 = {k: _bf(_dot(d_b[k], _bf(jnp.where(level, lab[k], 0.0)))) for k in heads}
        d = {k: d[k] - _dot(t1[k], d_b[k]) for k in heads}
        s *= 2
    d_b = {k: _bf(d[k]) for k in heads}
    w = {k: _bf(_dot(lak[k], P[k[0]]["v_b"])) for k in heads}
    ah = {k: _dot(d_b[k], P[k[0]]["aa_b"]) * P[k[0]]["rho"] for k in heads}
    uh = {k: _dot(d_b[k], w[k]) for k in heads}
    rh = {k: P[k[0]]["rt"] * P[k[0]]["rho"] - _dot(mrb[k], _bf(ah[k])) for k in heads}
    yh = {k: _dot(mrk[k], P[k[0]]["v_b"]) - _dot(mrb[k], _bf(uh[k])) for k in heads}
    for p in pairs:
        q = P[p]
        both = lambda x: jnp.where(head0, x[p, 0], x[p, 1])
        ah_p, uh_p = both(ah), both(uh)
        rh_out[0, :, q["sl"]] = both(rh)
        yh_out[0, :, q["sl"]] = both(yh)
        g_full = eye_p * q["g_last"] - _dot(_bf(ah_p.T), q["bc_b"])
        sh_full = _dot(_bf(q["v"].T), q["kc_b"]) - _dot(_bf(uh_p.T), q["bc_b"])
        g_out[0, 0, p] = jnp.where(same_head, g_full, 0.0)
        sh_out[0, 0, p] = jnp.where(same_head, sh_full, 0.0)


def _rwkv_chunks(r, lw, kh, v, kk, bb):
    nb, T, W = r.shape
    C = RWKV_CHUNK
    nc = T // C
    tri = _bf(jnp.tril(jnp.ones((C, C), F32)))
    tok = pl.BlockSpec((1, C, W), lambda b, c: (b, c, 0))
    mat = pl.BlockSpec((1, 1, N_PAIRS, LANES, LANES), lambda b, c: (b, c, 0, 0, 0))
    mat_shape = jax.ShapeDtypeStruct((nb, nc, N_PAIRS, LANES, LANES), F32)
    return pl.pallas_call(
        _rwkv_chunk_kernel,
        grid=(nb, nc),
        in_specs=[tok] * 6 + [_full((C, C))],
        out_specs=[tok, tok, mat, mat],
        out_shape=[jax.ShapeDtypeStruct((nb, T, W), F32)] * 2 + [mat_shape] * 2,
        compiler_params=_cparams("parallel", "parallel"),
        name="rwkv_chunks",
    )(r, lw, kh, v, kk, bb, tri)


def _rwkv_scan_kernel(rh_ref, yh_ref, g_ref, sh_ref, s0_ref, y_out, sfin_out, s_scr):
    c = pl.program_id(1)
    nbg = rh_ref.shape[0]

    @pl.when(c == 0)
    def _():
        s_scr[...] = s0_ref[...]

    for b in range(nbg):
        for p in range(N_PAIRS):
            sl = slice(p * LANES, (p + 1) * LANES)
            s = s_scr[b, p]
            y_out[b, :, sl] = _dot3_nt(rh_ref[b, :, sl], s) + yh_ref[b, :, sl]
            s_new = _dot3(s, g_ref[b, 0, p]) + sh_ref[b, 0, p]
            s_scr[b, p] = s_new
            sfin_out[b, p] = s_new


def _rwkv_scan(rh, yh, g, sh, s0, nbg):
    nb, T, W = rh.shape
    C = RWKV_CHUNK
    nc = T // C
    tok = pl.BlockSpec((nbg, C, W), lambda i, c: (i, c, 0))
    mat = pl.BlockSpec((nbg, 1, N_PAIRS, LANES, LANES), lambda i, c: (i, c, 0, 0, 0))
    st = pl.BlockSpec((nbg, N_PAIRS, LANES, LANES), lambda i, c: (i, 0, 0, 0))
    return pl.pallas_call(
        _rwkv_scan_kernel,
        grid=(nb // nbg, nc),
        in_specs=[tok, tok, mat, mat, st],
        out_specs=[tok, st],
        out_shape=[jax.ShapeDtypeStruct((nb, T, W), F32),
                   jax.ShapeDtypeStruct((nb, N_PAIRS, LANES, LANES), F32)],
        scratch_shapes=[pltpu.VMEM((nbg, N_PAIRS, LANES, LANES), F32)],
        compiler_params=_cparams("parallel", "arbitrary"),
        name="rwkv_scan",
    )(rh, yh, g, sh, s0)


def _state_to_pairs(s):
    nb = s.shape[0]
    s = s.reshape(nb, N_PAIRS, 2, HEAD_DIM, HEAD_DIM)
    z = jnp.zeros_like(s[:, :, 0])
    top = jnp.concatenate([s[:, :, 0], z], axis=-1)
    bot = jnp.concatenate([z, s[:, :, 1]], axis=-1)
    return jnp.concatenate([top, bot], axis=-2)


def _pairs_to_state(sp):
    nb = sp.shape[0]
    a = sp[:, :, :HEAD_DIM, :HEAD_DIM]
    b = sp[:, :, HEAD_DIM:, HEAD_DIM:]
    return jnp.stack([a, b], axis=2).reshape(nb, N_HEADS, HEAD_DIM, HEAD_DIM)


def _fox_aug_kernel(x_ref, c_ref, o_ref, *, is_query):
    lane = lax.broadcasted_iota(jnp.int32, (1, LANES), 1)
    c = c_ref[0]
    for p in range(N_PAIRS):
        xp = x_ref[0, :, p * LANES:(p + 1) * LANES].astype(F32)
        xr = pltpu.roll(xp, HEAD_DIM, axis=1)
        for h in range(2):
            hh = 2 * p + h
            ch = c[:, hh:hh + 1]
            hi = _bf(ch).astype(F32)
            r1 = ch - hi
            mid = _bf(r1).astype(F32)
            lo = _bf(r1 - mid).astype(F32)
            one = jnp.ones_like(ch)
            cols = (hi, mid, lo, one, one, one) if is_query else (one, one, one, -hi, -mid, -lo)
            aug = jnp.zeros_like(xp)
            for n, col in enumerate(cols):
                aug = jnp.where(lane == HEAD_DIM + n, col, aug)
            o_ref[0, hh] = _bf(jnp.where(lane < HEAD_DIM, xp if h == 0 else xr, aug))


def _fox_aug(x, c, tm, is_query):
    nb, S, W = x.shape
    return pl.pallas_call(
        functools.partial(_fox_aug_kernel, is_query=is_query),
        grid=(nb, S // tm),
        in_specs=[pl.BlockSpec((1, tm, W), lambda b, j: (b, j, 0)),
                  pl.BlockSpec((1, tm, LANES), lambda b, j: (b, j, 0))],
        out_specs=pl.BlockSpec((1, N_HEADS, tm, LANES), lambda b, j: (b, 0, j, 0)),
        out_shape=jax.ShapeDtypeStruct((nb, N_HEADS, S, LANES), BF16),
        compiler_params=_cparams("parallel", "parallel"),
        name="fox_aug",
    )(x, c)


def _fox_kernel(q_ref, k_ref, v_ref, o_ref, m_scr, l_scr, acc_scr, sa_scr, sb_scr, *, tk, q_off):
    i = pl.program_id(2)
    tq = q_ref.shape[2]
    q_start = q_off + i * tq
    m_scr[...] = jnp.full_like(m_scr, -jnp.inf)
    l_scr[...] = jnp.zeros_like(l_scr)
    acc_scr[...] = jnp.zeros_like(acc_scr)
    n_full = (q_start + 1) // tk

    def scores_into(j, dst):
        ks = pl.multiple_of(j * tk, tk)
        for h in range(2):
            dst[h] = _dot_nt(q_ref[0, h], k_ref[0, h, pl.ds(ks, tk), :])

    def update_from(j, src, masked):
        ks = pl.multiple_of(j * tk, tk)
        v = v_ref[0, pl.ds(ks, tk), :]
        if masked:
            visible = (ks + lax.broadcasted_iota(jnp.int32, (1, tk), 1)) <= (
                q_start + lax.broadcasted_iota(jnp.int32, (tq, 1), 0))
        for h in range(2):
            s = src[h]
            if masked:
                s = jnp.where(visible, s, -jnp.inf)
            m_old = m_scr[h]
            m_new = jnp.maximum(m_old, jnp.max(s, axis=-1, keepdims=True))
            alpha = jnp.exp(m_old - m_new)
            pr = jnp.exp(s - m_new)
            l_scr[h] = alpha * l_scr[h] + jnp.sum(pr, axis=-1, keepdims=True)
            acc_scr[h] = alpha * acc_scr[h] + _dot(_bf(pr), v)
            m_scr[h] = m_new

    def step(j, src, dst):
        scores_into(j + 1, dst)
        update_from(j, src, False)

    def two_steps(jj, carry):
        step(2 * jj, sa_scr, sb_scr)
        step(2 * jj + 1, sb_scr, sa_scr)
        return carry

    scores_into(0, sa_scr)
    lax.fori_loop(0, n_full // 2, two_steps, 0)
    odd = n_full % 2 == 1

    @pl.when(odd)
    def _():
        step(n_full - 1, sa_scr, sb_scr)
        update_from(n_full, sb_scr, True)

    @pl.when(jnp.logical_not(odd))
    def _():
        update_from(n_full, sa_scr, True)

    head0 = lax.broadcasted_iota(jnp.int32, (1, LANES), 1) < HEAD_DIM
    o_ref[0] = jnp.where(head0, acc_scr[0] / l_scr[0], acc_scr[1] / l_scr[1])


def _fox_attention(q, k, v, tq, tk, q_off):
    nb, _, sq, _ = q.shape
    sk = k.shape[2]
    for q_start in range(q_off, q_off + sq, tq):
        assert (q_start + 1) // tk + 1 == -(-(q_start + tq) // tk) <= sk // tk, (q_start, tq, tk)
    return pl.pallas_call(
        functools.partial(_fox_kernel, tk=tk, q_off=q_off),
        grid=(nb, N_PAIRS, sq // tq),
        in_specs=[pl.BlockSpec((1, 2, tq, LANES), lambda b, p, i: (b, p, i, 0)),
                  pl.BlockSpec((1, 2, sk, LANES), lambda b, p, i: (b, p, 0, 0)),
                  pl.BlockSpec((1, sk, LANES), lambda b, p, i: (b, 0, p))],
        out_specs=pl.BlockSpec((1, tq, LANES), lambda b, p, i: (b, i, p)),
        out_shape=jax.ShapeDtypeStruct((nb, sq, FOX_WIDTH), F32),
        scratch_shapes=[pltpu.VMEM((2, tq, 1), F32), pltpu.VMEM((2, tq, 1), F32),
                        pltpu.VMEM((2, tq, LANES), F32), pltpu.VMEM((2, tq, tk), F32),
                        pltpu.VMEM((2, tq, tk), F32)],
        compiler_params=_cparams("parallel", "parallel", "arbitrary"),
        name="fox_attention",
    )(q, k, v)


def _out_kernel(*refs, n_first):
    tok_refs, rest = refs[:12], refs[12:]
    lnw_ref, lnb_ref, gs_ref, wout_ref, gffn_ref, rwh_ref, rwl_ref, h_out, xn_out, sc_out = rest
    first = pl.program_id(0) < n_first
    x, y, bonus, g, oa, og = (jnp.where(first, tok_refs[2 * n][...], tok_refs[2 * n + 1][...]) for n in range(6))
    gs = gs_ref[...]
    inv_d = 1.0 / HEAD_DIM
    mean = _dot_x01(y, gs) * inv_d
    d = y - mean
    var = _dot_x01(d * d, gs) * inv_d
    yn = d * lax.rsqrt(var + GN_EPS) * lnw_ref[...] + lnb_ref[...]
    o_rwkv = (yn + bonus) * g
    o_fox = oa * jax.nn.sigmoid(og)
    mix = jnp.concatenate([_bf(o_rwkv), _bf(o_fox)], axis=-1)
    h = x + _dot(mix, wout_ref[...])
    h_out[...] = h
    xn = _rms(h, gffn_ref[...])
    _store_chunked(xn_out, xn)
    xh, xl = _split2(xn)
    logits = _dot(xh, rwh_ref[...]) + _dot(xl, rwh_ref[...]) + _dot(xh, rwl_ref[...])
    sc_out[...] = jax.nn.sigmoid(logits)


def _two_streams(width, tm, n_first):
    return [pl.BlockSpec((tm, width), lambda i: (jnp.minimum(i, n_first - 1), 0)),
            pl.BlockSpec((tm, width), lambda i: (jnp.maximum(i - n_first, 0), 0))]


def _out_proj(streams, lnw, lnb, gs, wout, gffn, rwh, rwl, tm):
    n_first = streams[0][0].shape[0] // tm
    T = streams[0][0].shape[0] + streams[0][1].shape[0]
    W = RWKV_WIDTH
    tok = lambda width: pl.BlockSpec((tm, width), lambda i: (i, 0))
    tok_specs, tok_args = [], []
    for a, b in streams:
        tok_specs += _two_streams(a.shape[1], tm, n_first)
        tok_args += [a, b]
    return pl.pallas_call(
        functools.partial(_out_kernel, n_first=n_first),
        grid=(T // tm,),
        in_specs=tok_specs + [_full((1, W)), _full((1, W)), _full((W, W)),
                              _full((D_MODEL, D_MODEL)), _full((1, D_MODEL)),
                              _full((D_MODEL, N_EXPERTS)), _full((D_MODEL, N_EXPERTS))],
        out_specs=[tok(D_MODEL), pl.BlockSpec((tm * ROW_CHUNKS, LANES), lambda i: (i, 0)), tok(N_EXPERTS)],
        out_shape=[jax.ShapeDtypeStruct((T, D_MODEL), F32),
                   jax.ShapeDtypeStruct((T * ROW_CHUNKS, LANES), F32),
                   jax.ShapeDtypeStruct((T, N_EXPERTS), F32)],
        compiler_params=_cparams("parallel"),
        name="out_proj",
    )(*tok_args, lnw, lnb, gs, wout, gffn, rwh, rwl)


def _route_kernel(sc_ref, bias_ref, before_ref, idx_out, wt_out, rank_out, cnt_out, cnt_scr):
    tm = sc_ref.shape[0]
    neg = -jnp.inf

    @pl.when(pl.program_id(0) == 0)
    def _():
        cnt_scr[...] = jnp.zeros_like(cnt_scr)

    st = sc_ref[...].T
    sel = st + bias_ref[...]
    gscore = []
    for gi in range(N_GROUPS):
        blk = sel[gi * GROUP_SIZE:(gi + 1) * GROUP_SIZE, :]
        m1 = jnp.max(blk, axis=0, keepdims=True)
        n1 = jnp.sum((blk == m1).astype(F32), axis=0, keepdims=True)
        m2 = jnp.max(jnp.where(blk < m1, blk, neg), axis=0, keepdims=True)
        gscore.append(m1 + jnp.where(n1 > 1.0, m1, m2))
    taken = [jnp.zeros((1, tm), jnp.bool_) for _ in range(N_GROUPS)]
    for _ in range(TOPK_GROUPS):
        avail = [jnp.where(taken[gi], neg, gscore[gi]) for gi in range(N_GROUPS)]
        best = functools.reduce(jnp.maximum, avail)
        found = jnp.zeros((1, tm), jnp.bool_)
        for gi in range(N_GROUPS):
            hit = (avail[gi] == best) & jnp.logical_not(found)
            taken[gi] = taken[gi] | hit
            found = found | hit
    cand = jnp.concatenate(
        [jnp.where(taken[gi], sel[gi * GROUP_SIZE:(gi + 1) * GROUP_SIZE, :], neg) for gi in range(N_GROUPS)], axis=0)
    eid = lax.broadcasted_iota(jnp.int32, (N_EXPERTS, tm), 0).astype(F32)
    idxs, wts = [], []
    onehot = jnp.zeros((N_EXPERTS, tm), F32)
    for _ in range(TOP_K):
        best = jnp.max(cand, axis=0, keepdims=True)
        pick = jnp.min(jnp.where(cand == best, eid, float(N_EXPERTS)), axis=0, keepdims=True)
        chosen = eid == pick
        wts.append(jnp.sum(jnp.where(chosen, st, 0.0), axis=0, keepdims=True))
        idxs.append(pick)
        cand = jnp.where(chosen, neg, cand)
        onehot = jnp.where(chosen, 1.0, onehot)
    w = jnp.concatenate(wts, axis=0)
    idx_out[...] = jnp.concatenate(idxs, axis=0).astype(jnp.int32)
    wt_out[...] = w / jnp.sum(w, axis=0, keepdims=True) * ROUTED_SCALE
    earlier = _dot(_bf(onehot), before_ref[...]) + cnt_scr[...]
    rank_out[...] = jnp.concatenate(
        [jnp.sum(jnp.where(eid == pick, earlier, 0.0), axis=0, keepdims=True) for pick in idxs],
        axis=0).astype(jnp.int32)
    cnt_scr[...] += jnp.sum(onehot, axis=1, keepdims=True)
    cnt_out[...] = cnt_scr[...]


def _route(scores, bias_col, tm):
    T = scores.shape[0]
    before = _bf(jnp.triu(jnp.ones((tm, tm), F32), 1))
    tok = pl.BlockSpec((TOP_K, tm), lambda i: (0, i))
    return pl.pallas_call(
        _route_kernel,
        grid=(T // tm,),
        in_specs=[pl.BlockSpec((tm, N_EXPERTS), lambda i: (i, 0)), _full((N_EXPERTS, 1)), _full((tm, tm))],
        out_specs=[tok, tok, tok, _full((N_EXPERTS, 1))],
        out_shape=[jax.ShapeDtypeStruct((TOP_K, T), jnp.int32), jax.ShapeDtypeStruct((TOP_K, T), F32),
                   jax.ShapeDtypeStruct((TOP_K, T), jnp.int32), jax.ShapeDtypeStruct((N_EXPERTS, 1), F32)],
        scratch_shapes=[pltpu.VMEM((N_EXPERTS, 1), F32)],
        compiler_params=_cparams("arbitrary"),
        name="route",
    )(scores, bias_col, before)


def _pos_kernel(idx_ref, rank_ref, start_ref, pos_out):
    tm = idx_ref.shape[1]
    eid = lax.broadcasted_iota(jnp.int32, (N_EXPERTS, tm), 0)
    idx = idx_ref[...]
    start = start_ref[...]
    base = jnp.concatenate(
        [jnp.sum(jnp.where(eid == idx[k:k + 1, :], start, 0.0), axis=0, keepdims=True) for k in range(TOP_K)],
        axis=0)
    pos_out[...] = rank_ref[...] + base.astype(jnp.int32)


def _positions(eidx_t, rank_t, start_col, tm):
    T = eidx_t.shape[1]
    tok = pl.BlockSpec((TOP_K, tm), lambda i: (0, i))
    return pl.pallas_call(
        _pos_kernel,
        grid=(T // tm,),
        in_specs=[tok, tok, _full((N_EXPERTS, 1))],
        out_specs=tok,
        out_shape=jax.ShapeDtypeStruct((TOP_K, T), jnp.int32),
        compiler_params=_cparams("parallel"),
        name="moe_positions",
    )(eidx_t, rank_t, start_col)


def _dispatch_kernel(pos_ref, x_ref, h_ref, sg_ref, su_ref, sd_ref, xs_in, xs_out, hs_out, sem):
    del xs_in
    tm = pos_ref.shape[1]

    def issue(t, carry):
        src = x_ref.at[pl.ds(pl.multiple_of(t * ROW_CHUNKS, ROW_CHUNKS), ROW_CHUNKS), :]
        for k in range(TOP_K):
            row = pl.multiple_of(pos_ref[k, t] * ROW_CHUNKS, ROW_CHUNKS)
            pltpu.make_async_copy(src, xs_out.at[pl.ds(row, ROW_CHUNKS), :], sem).start()
        return carry

    lax.fori_loop(0, tm, issue, 0)
    xb = _bf(_load_chunked(x_ref, tm))
    hg = _dot(xb, sg_ref[...])
    hu = _dot(xb, su_ref[...])
    hs_out[...] = h_ref[...] + _dot(_bf(hg * jax.nn.sigmoid(hg) * hu), sd_ref[...])
    for k in range(TOP_K):
        pltpu.make_async_copy(x_ref, xs_out.at[pl.ds(0, tm * ROW_CHUNKS), :], sem).wait()


def _dispatch(pos_t, xn_chunked, h, sg, su, sd, n_rows, tm):
    T = pos_t.shape[1]
    xs0 = jnp.zeros((n_rows * ROW_CHUNKS, LANES), F32)
    return pl.pallas_call(
        _dispatch_kernel,
        grid=(T // tm,),
        in_specs=[pl.BlockSpec((TOP_K, tm), lambda i: (0, i), memory_space=pltpu.SMEM),
                  pl.BlockSpec((tm * ROW_CHUNKS, LANES), lambda i: (i, 0)),
                  pl.BlockSpec((tm, D_MODEL), lambda i: (i, 0)),
                  _full((D_MODEL, EXPERT_FF)), _full((D_MODEL, EXPERT_FF)), _full((EXPERT_FF, D_MODEL)),
                  pl.BlockSpec(memory_space=pl.ANY)],
        out_specs=[pl.BlockSpec(memory_space=pl.ANY), pl.BlockSpec((tm, D_MODEL), lambda i: (i, 0))],
        out_shape=[jax.ShapeDtypeStruct((n_rows * ROW_CHUNKS, LANES), F32),
                   jax.ShapeDtypeStruct((T, D_MODEL), F32)],
        scratch_shapes=[pltpu.SemaphoreType.DMA(())],
        input_output_aliases={6: 0},
        compiler_params=_cparams("arbitrary"),
        name="moe_dispatch",
    )(pos_t, xn_chunked, h, sg, su, sd, xs0)


def _gmm_kernel(be_ref, nb_ref, xs_ref, wg_ref, wu_ref, wd_ref, o_ref):
    i = pl.program_id(0)
    bm = MOE_ROWS

    @pl.when(i < nb_ref[0])
    def _():
        xe = _bf(_load_chunked(xs_ref, bm))
        hg = _dot(xe, _bf(wg_ref[0]))
        hu = _dot(xe, _bf(wu_ref[0]))
        hdn = hg * jax.nn.sigmoid(hg) * hu
        _store_chunked(o_ref, _dot(_bf(hdn), _bf(wd_ref[0])))

    @pl.when(i >= nb_ref[0])
    def _():
        o_ref[...] = jnp.zeros_like(o_ref)


def _gmm(blk_e, n_used, xs, wg, wu, wd):
    bm = MOE_ROWS
    n_blocks = xs.shape[0] // (bm * ROW_CHUNKS)
    rows = pl.BlockSpec((bm * ROW_CHUNKS, LANES), lambda i, be, nb: (i, 0))
    grid_spec = pltpu.PrefetchScalarGridSpec(
        num_scalar_prefetch=2,
        grid=(n_blocks,),
        in_specs=[rows,
                  pl.BlockSpec((1, D_MODEL, EXPERT_FF), lambda i, be, nb: (be[i], 0, 0)),
                  pl.BlockSpec((1, D_MODEL, EXPERT_FF), lambda i, be, nb: (be[i], 0, 0)),
                  pl.BlockSpec((1, EXPERT_FF, D_MODEL), lambda i, be, nb: (be[i], 0, 0))],
        out_specs=rows,
    )
    return pl.pallas_call(
        _gmm_kernel,
        grid_spec=grid_spec,
        out_shape=jax.ShapeDtypeStruct(xs.shape, F32),
        compiler_params=_cparams("arbitrary"),
        name="expert_gmm",
    )(blk_e, n_used, xs, wg, wu, wd)


def _final_kernel(pos_ref, nxt_ref, hs_ref, w_ref, pa_ref, pb_ref, ys_ref, gple_ref, wpg_ref, wpp_ref, gfin_ref,
                  y_out, buf, sems, *, n_first):
    i = pl.program_id(0)
    n = pl.num_programs(0)
    tm = hs_ref.shape[0]

    def gather(rows_ref, slot):
        def issue(t, carry):
            dst = pl.ds(pl.multiple_of(t * ROW_CHUNKS, ROW_CHUNKS), ROW_CHUNKS)
            for k in range(TOP_K):
                row = pl.multiple_of(rows_ref[k, t] * ROW_CHUNKS, ROW_CHUNKS)
                pltpu.make_async_copy(ys_ref.at[pl.ds(row, ROW_CHUNKS), :], buf.at[slot, k, dst, :],
                                      sems.at[slot, k]).start()
            return carry

        lax.fori_loop(0, tm, issue, 0)

    slot = i % 2

    @pl.when(i == 0)
    def _():
        gather(pos_ref, 0)

    @pl.when(i + 1 < n)
    def _():
        gather(nxt_ref, 1 - slot)

    pp = _dot(_bf(jnp.where(i < n_first, pa_ref[...], pb_ref[...])), wpp_ref[...])
    w = w_ref[...]
    routed = jnp.zeros((tm, D_MODEL), F32)
    for k in range(TOP_K):
        pltpu.make_async_copy(ys_ref.at[pl.ds(0, tm * ROW_CHUNKS), :], buf.at[slot, k], sems.at[slot, k]).wait()
        routed = routed + _load_chunked(buf.at[slot, k], tm) * w[:, k:k + 1]
    h = hs_ref[...] + routed
    gate = jax.nn.sigmoid(_dot(_bf(_rms(h, gple_ref[...])), wpg_ref[...]))
    y_out[...] = _rms(h + gate * pp, gfin_ref[...])


def _final(pos_t, hs, w, p_pair, ys, gple, wpg, wpp, gfin, tm):
    T = hs.shape[0]
    n = T // tm
    n_first = p_pair[0].shape[0] // tm
    tok = lambda width: pl.BlockSpec((tm, width), lambda i: (i, 0))
    return pl.pallas_call(
        functools.partial(_final_kernel, n_first=n_first),
        grid=(n,),
        in_specs=[pl.BlockSpec((TOP_K, tm), lambda i: (0, i), memory_space=pltpu.SMEM),
                  pl.BlockSpec((TOP_K, tm), lambda i: (0, jnp.minimum(i + 1, n - 1)), memory_space=pltpu.SMEM),
                  tok(D_MODEL), tok(TOP_K)] + _two_streams(PLE_DIM, tm, n_first) + [
                  pl.BlockSpec(memory_space=pl.ANY),
                  _full((1, D_MODEL)), _full((D_MODEL, D_MODEL)), _full((PLE_DIM, D_MODEL)),
                  _full((1, D_MODEL))],
        out_specs=tok(D_MODEL),
        out_shape=jax.ShapeDtypeStruct((T, D_MODEL), F32),
        scratch_shapes=[pltpu.VMEM((2, TOP_K, tm * ROW_CHUNKS, LANES), F32),
                        pltpu.SemaphoreType.DMA((2, TOP_K))],
        compiler_params=_cparams("arbitrary"),
        name="ffn_tail",
    )(pos_t, pos_t, hs, w, *p_pair, ys, gple, wpg, wpp, gfin)


def _pad_cols(a, width):
    return jnp.pad(a, [(0, 0)] * (a.ndim - 1) + [(0, width - a.shape[-1])])


def _rwkv_pad_cols(a):
    W = RWKV_WIDTH
    o1, o2, o3 = 3 * W, 3 * W + DECAY_RANK, 3 * W + DECAY_RANK + ICL_RANK
    return jnp.concatenate([a[..., :o1], _pad_cols(a[..., o1:o2], 128), _pad_cols(a[..., o2:o3], 128),
                            a[..., o3:]], axis=-1)


def _rwkv_unpad_cols(a):
    W = RWKV_WIDTH
    return jnp.concatenate([a[..., :3 * W + DECAY_RANK], a[..., 3 * W + 128:3 * W + 128 + ICL_RANK],
                            a[..., 3 * W + 256:]], axis=-1)


def _pad_rows(a, rows):
    return jnp.pad(a, [(0, rows - a.shape[0])] + [(0, 0)] * (a.ndim - 1))


def _mixer(x, shift, s0, past, wts, tm):
    nb, T, _ = x.shape
    (r, lw, kh, v, kkn, bb, g, bonus, last) = _in_rwkv(
        x, _rwkv_pad_cols(shift), wts["gmix"], wts["w_rwkv"], wts["mu"], wts["w0"], wts["wup"], wts["a0"],
        wts["aup"], wts["gup"], wts["k_k"], wts["k_a"], wts["r_k"], wts["gs"], tm)
    q_b, k_f, k_b, v_f, v_b, og, logf = _in_fox(
        x, wts["gmix"], wts["w_fox"], wts["qn"], wts["kn"], wts["fb"], wts["gs"], tm)

    C = RWKV_CHUNK
    Tp = -(-T // C) * C
    if Tp != T:
        padt = lambda a: jnp.pad(a, ((0, 0), (0, Tp - T), (0, 0)))
        r_p, lw_p, kh_p, v_p, kk_p, bb_p = (padt(a) for a in (r, lw, kh, v, kkn, bb))
    else:
        r_p, lw_p, kh_p, v_p, kk_p, bb_p = r, lw, kh, v, kkn, bb
    rh, yh, gm, sh = _rwkv_chunks(r_p, lw_p, kh_p, v_p, kk_p, bb_p)
    y, s_fin = _rwkv_scan(rh, yh, gm, sh, _state_to_pairs(s0.astype(F32)), 4 if nb % 4 == 0 else 1)
    y = y[:, :T]

    if past is None:
        c = _cumsum(logf, min(T, 512))
        tq, tk = min(T, FOX_TQ), min(T, FOX_TK)
        o_att = _fox_attention(_fox_aug(q_b, c, tq, True), _fox_aug(k_b, c, tq, False), v_b, tq, tk, 0)
    else:
        k_past, v_past, lf_past = past
        P = k_past.shape[1]
        tk = FOX_TK_CACHED
        sk = -(-(P + T) // tk) * tk
        pads = lambda a: jnp.pad(a, ((0, 0), (0, sk - P - T), (0, 0)))
        k_all = pads(jnp.concatenate([_bf(k_past.reshape(nb, P, FOX_WIDTH)), k_b], axis=1))
        v_all = pads(jnp.concatenate([_bf(v_past.reshape(nb, P, FOX_WIDTH)), v_b], axis=1))
        lf_all = pads(jnp.concatenate([_pad_cols(lf_past.astype(F32), 128), logf], axis=1))
        c = _cumsum(lf_all, tk)
        o_att = _fox_attention(_fox_aug(q_b, c[:, P:P + T], T, True), _fox_aug(k_all, c, tk, False), v_all,
                               T, tk, P)

    n = nb * T
    flat = lambda a: a.reshape(n, a.shape[-1])
    feats = (flat(y), flat(bonus), flat(g), flat(o_att), flat(og))
    state = (k_f.reshape(nb, T, N_HEADS, HEAD_DIM), v_f.reshape(nb, T, N_HEADS, HEAD_DIM),
             logf[:, :, :N_HEADS], _pairs_to_state(s_fin), _rwkv_unpad_cols(last))
    return feats, state


def _block_tables(counts, n_blocks):
    blk = MOE_ROWS
    counts = counts.reshape(N_EXPERTS).astype(jnp.int32)
    padded = (counts + blk - 1) // blk * blk
    pend = jnp.cumsum(padded)
    first_row = jnp.arange(n_blocks, dtype=jnp.int32)[:, None] * blk
    blk_e = jnp.minimum(jnp.sum((pend[None, :] <= first_row).astype(jnp.int32), axis=1), N_EXPERTS - 1)
    n_used = (pend[-1] // blk).astype(jnp.int32).reshape(1)
    return (pend - padded).astype(F32).reshape(N_EXPERTS, 1), blk_e.astype(jnp.int32), n_used


def kernel(x_prompt, x_sample, cache_fox_k, cache_fox_v, cache_fox_logf, state_rwkv_wkv, state_rwkv_shift, p_prompt, p_sample, norm_mix_g, w_in, rwkv_mu, rwkv_w0, rwkv_w_up, rwkv_a0, rwkv_a_up, rwkv_g_up, rwkv_k_k, rwkv_k_a, rwkv_r_k, rwkv_ln_w, rwkv_ln_b, fox_q_norm, fox_k_norm, fox_f_bias, w_out, norm_ffn_g, router_w, router_bias, exp_w_gate, exp_w_up, exp_w_down, shared_w_gate, shared_w_up, shared_w_down, ple_norm_g, ple_w_gate, ple_w_proj, final_norm_g):
    assert w_in.shape[0] == 1, "single-layer kernel"
    W = RWKV_WIDTH
    row = lambda a: a.reshape(1, -1).astype(F32)
    tile_heads = lambda a: jnp.tile(a.reshape(1, HEAD_DIM), (1, N_HEADS)).astype(F32)
    hid = jnp.arange(W) // HEAD_DIM
    w_in0 = w_in[0]
    router_hi = _bf(router_w[0])
    wts = {
        "gmix": row(norm_mix_g[0]),
        "w_rwkv": _bf(_rwkv_pad_cols(w_in0[:, :RWKV_IN])),
        "w_fox": _bf(_pad_cols(w_in0[:, RWKV_IN:], FOX_PAD)),
        "mu": row(_rwkv_pad_cols(rwkv_mu[0])),
        "w0": row(rwkv_w0[0]),
        "wup": _bf(_pad_rows(rwkv_w_up[0], 128)),
        "a0": row(rwkv_a0[0]),
        "aup": _bf(_pad_rows(rwkv_a_up[0], 128)),
        "gup": _bf(rwkv_g_up[0]),
        "k_k": row(rwkv_k_k[0]),
        "k_a": row(rwkv_k_a[0]),
        "r_k": row(rwkv_r_k[0]),
        "gs": _bf((hid[:, None] == hid[None, :]).astype(F32)),
        "qn": tile_heads(fox_q_norm[0]),
        "kn": tile_heads(fox_k_norm[0]),
        "fb": _pad_cols(row(fox_f_bias[0]), 128),
    }
    nbp, Tp, _ = x_prompt.shape
    nbs, Ts, _ = x_sample.shape
    s0_prompt = jnp.zeros((nbp, N_HEADS, HEAD_DIM, HEAD_DIM), F32)
    shift0_prompt = jnp.zeros((nbp, 1, RWKV_IN), F32)
    feats_p, st_p = _mixer(x_prompt, shift0_prompt, s0_prompt, None, wts, min(Tp, 256))
    feats_s, st_s = _mixer(x_sample, state_rwkv_shift[0], state_rwkv_wkv[0],
                           (cache_fox_k[0], cache_fox_v[0], cache_fox_logf[0]), wts, Ts)

    n_p, n_s = nbp * Tp, nbs * Ts
    n_tok = n_p + n_s
    tm = math.gcd(math.gcd(n_p, n_s), TOKEN_TILE)
    streams = [(x_prompt.reshape(n_p, D_MODEL), x_sample.reshape(n_s, D_MODEL))] + list(zip(feats_p, feats_s))
    h1, xn2, scores = _out_proj(streams, row(rwkv_ln_w[0]), row(rwkv_ln_b[0]), wts["gs"], _bf(w_out[0]),
                                row(norm_ffn_g[0]), router_hi,
                                _bf(router_w[0] - router_hi.astype(F32)), tm)
    eidx_t, wts_t, rank_t, counts = _route(scores, router_bias[0].reshape(N_EXPERTS, 1).astype(F32), tm)
    n_blocks = -(-n_tok * TOP_K // MOE_ROWS) + N_EXPERTS
    start_col, blk_e, n_used = _block_tables(counts, n_blocks)
    pos_t = _positions(eidx_t, rank_t, start_col, tm)
    xs, h1s = _dispatch(pos_t, xn2, h1, _bf(shared_w_gate[0]), _bf(shared_w_up[0]), _bf(shared_w_down[0]),
                        n_blocks * MOE_ROWS, tm)
    y_rows = _gmm(blk_e, n_used, xs, exp_w_gate[0], exp_w_up[0], exp_w_down[0])
    p_pair = (p_prompt[0].reshape(n_p, PLE_DIM), p_sample[0].reshape(n_s, PLE_DIM))
    y_all = _final(pos_t, h1s, wts_t.T, p_pair, y_rows, row(ple_norm_g[0]), _bf(ple_w_gate[0]),
                   _bf(ple_w_proj[0]), row(final_norm_g), tm)
    y_prompt = y_all[:n_p].reshape(nbp, Tp, D_MODEL)
    y_sample = y_all[n_p:].reshape(nbs, Ts, D_MODEL)
    lead = lambda t: tuple(a[None] for a in t)
    return (y_prompt, y_sample) + lead(st_p) + lead(st_s)
```

```python
import functools
import math

import numpy as np
import jax
import jax.numpy as jnp
from jax import lax
from jax.experimental import pallas as pl
from jax.experimental.pallas import tpu as pltpu

F32 = jnp.float32
BF16 = jnp.bfloat16

D_MODEL = 1024
HEAD_DIM = 64
RWKV_WIDTH = 512
FOX_WIDTH = 512
N_HEADS = 8
N_PAIRS = N_HEADS // 2
DECAY_RANK = 64
ICL_RANK = 64
GATE_RANK = 128
RWKV_IN = 3 * RWKV_WIDTH + DECAY_RANK + ICL_RANK + GATE_RANK
RWKV_PAD = 3 * RWKV_WIDTH + 3 * 128
FOX_PAD = 4 * FOX_WIDTH + 128
ATTN_SCALE = HEAD_DIM ** -0.5
N_EXPERTS = 256
N_GROUPS = 8
GROUP_SIZE = N_EXPERTS // N_GROUPS
TOPK_GROUPS = 4
TOP_K = 8
EXPERT_FF = 256
ROUTED_SCALE = 2.5
PLE_DIM = 256
RMS_EPS = 1e-6
GN_EPS = 64e-5
L2_EPS = 1e-12

LANES = 128
ROW_CHUNKS = D_MODEL // LANES
RWKV_CHUNK = 128
MOE_ROWS = 256
TOKEN_TILE = 256
GMM_AHEAD = 2
FOX_TQ, FOX_TK = 256, 1024
FOX_TK_CACHED = 768
VMEM_LIMIT = 56 * 1024 * 1024


def _cparams(*sem):
    return pltpu.CompilerParams(dimension_semantics=sem, vmem_limit_bytes=VMEM_LIMIT)


def _bf(x):
    return x.astype(BF16)


def _dot(a, b):
    return jnp.dot(a, b, preferred_element_type=F32)


def _dot_nt(a, b):
    return lax.dot_general(a, b, (((1,), (1,)), ((), ())), preferred_element_type=F32)


def _split2(x):
    hi = _bf(x)
    return hi, _bf(x - hi.astype(F32))


def _split3(x):
    hi = _bf(x)
    r1 = x - hi.astype(F32)
    mid = _bf(r1)
    return hi, mid, _bf(r1 - mid.astype(F32))


def _dot_x01(x, w01):
    hi, lo = _split2(x)
    return _dot(hi, w01) + _dot(lo, w01)


def _dot3(a, b):
    ah, al = _split2(a)
    bh, bl = _split2(b)
    return _dot(ah, bh) + _dot(al, bh) + _dot(ah, bl)


def _dot3_nt(a, b):
    ah, al = _split2(a)
    bh, bl = _split2(b)
    return _dot_nt(ah, bh) + _dot_nt(al, bh) + _dot_nt(ah, bl)


def _softplus(x):
    return jnp.maximum(x, 0.0) + jnp.log1p(jnp.exp(-jnp.abs(x)))


def _rms(x, g):
    return x * lax.rsqrt(jnp.mean(x * x, axis=-1, keepdims=True) + RMS_EPS) * g


def _full(shape):
    return pl.BlockSpec(shape, lambda *_: (0,) * len(shape))


def _store_chunked(ref, x):
    n = x.shape[0]
    for s in range(ROW_CHUNKS):
        ref[pl.ds(s, n, stride=ROW_CHUNKS), :] = x[:, s * LANES:(s + 1) * LANES]


def _load_chunked(ref, n):
    return jnp.concatenate([ref[pl.ds(s, n, stride=ROW_CHUNKS), :] for s in range(ROW_CHUNKS)], axis=1)


def _in_rwkv_kernel(x_ref, shift_ref, gmix_ref, w_ref, mu_ref, w0_ref, wup_ref, a0_ref, aup_ref, gup_ref,
                    kk_ref, ka_ref, rk_ref, gs_ref,
                    r_out, lw_out, kh_out, v_out, kkn_out, bb_out, g_out, bonus_out, last_out,
                    carry_ref):
    j = pl.program_id(1)
    tm = x_ref.shape[1]
    xn = _bf(_rms(x_ref[0], gmix_ref[...]))
    u = _dot(xn, w_ref[...])
    first = jnp.where(j == 0, shift_ref[0], carry_ref[...])
    row = lax.broadcasted_iota(jnp.int32, (tm, 1), 0)
    prev = jnp.where(row == 0, first, pltpu.roll(u, 1, axis=0))
    carry_ref[...] = u[tm - 1:tm, :]
    last_out[0] = u[tm - 1:tm, :]
    xs = u + (prev - u) * mu_ref[...]
    W = RWKV_WIDTH
    r, k, v = xs[:, :W], xs[:, W:2 * W], xs[:, 2 * W:3 * W]
    xw, xa, xg = xs[:, 3 * W:3 * W + 128], xs[:, 3 * W + 128:3 * W + 256], xs[:, 3 * W + 256:]
    w_raw = w0_ref[...] + _dot(_bf(jnp.tanh(xw)), wup_ref[...])
    lw = -jnp.exp(-_softplus(-w_raw) - 0.5)
    a = jax.nn.sigmoid(a0_ref[...] + _dot(_bf(xa), aup_ref[...]))
    g = _dot(_bf(jax.nn.sigmoid(xg)), gup_ref[...])
    gs = gs_ref[...]
    kk = k * kk_ref[...]
    kkn = kk / jnp.maximum(jnp.sqrt(_dot_x01(kk * kk, gs)), L2_EPS)
    kh = k * (1.0 + (a - 1.0) * ka_ref[...])
    r_out[0] = r
    lw_out[0] = lw
    kh_out[0] = kh
    v_out[0] = v
    kkn_out[0] = kkn
    bb_out[0] = kkn * a
    g_out[0] = g
    bonus_out[0] = _dot_x01(r * kh * rk_ref[...], gs) * v


def _in_rwkv(x, shift, gmix, w, mu, w0, wup, a0, aup, gup, k_k, k_a, r_k, gs, tm):
    nb, T, _ = x.shape
    W = RWKV_WIDTH
    tok = lambda width: pl.BlockSpec((1, tm, width), lambda b, j: (b, j, 0))
    outs = [jax.ShapeDtypeStruct((nb, T, W), F32)] * 8 + [jax.ShapeDtypeStruct((nb, 1, RWKV_PAD), F32)]
    return pl.pallas_call(
        _in_rwkv_kernel,
        grid=(nb, T // tm),
        in_specs=[tok(D_MODEL), pl.BlockSpec((1, 1, RWKV_PAD), lambda b, j: (b, 0, 0)),
                  _full((1, D_MODEL)), _full((D_MODEL, RWKV_PAD)), _full((1, RWKV_PAD)),
                  _full((1, W)), _full((128, W)), _full((1, W)), _full((128, W)), _full((128, W)),
                  _full((1, W)), _full((1, W)), _full((1, W)), _full((W, W))],
        out_specs=[tok(W)] * 8 + [pl.BlockSpec((1, 1, RWKV_PAD), lambda b, j: (b, 0, 0))],
        out_shape=outs,
        scratch_shapes=[pltpu.VMEM((1, RWKV_PAD), F32)],
        compiler_params=_cparams("parallel", "arbitrary"),
        name="in_rwkv",
    )(x, shift, gmix, w, mu, w0, wup, a0, aup, gup, k_k, k_a, r_k, gs)


def _in_fox_kernel(x_ref, gmix_ref, w_ref, qn_ref, kn_ref, fb_ref, gs_ref,
                   q_out, k_out, kb_out, v_out, vb_out, og_out, lf_out):
    xn = _bf(_rms(x_ref[0], gmix_ref[...]))
    u = _dot(xn, w_ref[...])
    W = FOX_WIDTH
    q, k, v, og, fl = u[:, :W], u[:, W:2 * W], u[:, 2 * W:3 * W], u[:, 3 * W:4 * W], u[:, 4 * W:]
    gs = gs_ref[...]
    inv_d = 1.0 / HEAD_DIM
    qn = q * lax.rsqrt(_dot_x01(q * q, gs) * inv_d + RMS_EPS) * qn_ref[...]
    kn = k * lax.rsqrt(_dot_x01(k * k, gs) * inv_d + RMS_EPS) * kn_ref[...]
    q_out[0] = _bf(qn * ATTN_SCALE)
    k_out[0] = kn
    kb_out[0] = _bf(kn)
    v_out[0] = v
    vb_out[0] = _bf(v)
    og_out[0] = og
    lf_out[0] = -_softplus(-(fl + fb_ref[...]))


def _in_fox(x, gmix, w, qn, kn, fb, gs, tm):
    nb, T, _ = x.shape
    W = FOX_WIDTH
    tok = lambda width: pl.BlockSpec((1, tm, width), lambda b, j: (b, j, 0))
    sds = lambda width, dt: jax.ShapeDtypeStruct((nb, T, width), dt)
    return pl.pallas_call(
        _in_fox_kernel,
        grid=(nb, T // tm),
        in_specs=[tok(D_MODEL), _full((1, D_MODEL)), _full((D_MODEL, FOX_PAD)),
                  _full((1, W)), _full((1, W)), _full((1, 128)), _full((W, W))],
        out_specs=[tok(W)] * 6 + [tok(128)],
        out_shape=[sds(W, BF16), sds(W, F32), sds(W, BF16), sds(W, F32), sds(W, BF16), sds(W, F32),
                   sds(128, F32)],
        compiler_params=_cparams("parallel", "parallel"),
        name="in_fox",
    )(x, gmix, w, qn, kn, fb, gs)


def _cumsum_kernel(x_ref, tri_ref, o_ref, carry_ref):
    j = pl.program_id(1)
    ts = x_ref.shape[1]

    @pl.when(j == 0)
    def _():
        carry_ref[...] = jnp.zeros_like(carry_ref)

    hi, mid, lo = _split3(x_ref[0])
    tri = tri_ref[...]
    c = _dot(tri, hi) + _dot(tri, mid) + _dot(tri, lo) + carry_ref[...]
    o_ref[0] = c
    carry_ref[...] = c[ts - 1:ts, :]


def _cumsum(x, ts):
    nb, S, L = x.shape
    tri = _bf(jnp.tril(jnp.ones((ts, ts), F32)))
    return pl.pallas_call(
        _cumsum_kernel,
        grid=(nb, S // ts),
        in_specs=[pl.BlockSpec((1, ts, L), lambda b, j: (b, j, 0)), _full((ts, ts))],
        out_specs=pl.BlockSpec((1, ts, L), lambda b, j: (b, j, 0)),
        out_shape=jax.ShapeDtypeStruct((nb, S, L), F32),
        scratch_shapes=[pltpu.VMEM((1, L), F32)],
        compiler_params=_cparams("parallel", "arbitrary"),
        name="seq_cumsum",
    )(x, tri)


def _rwkv_chunk_kernel(r_ref, lw_ref, kh_ref, v_ref, kk_ref, bb_ref, tri_ref,
                       rh_out, yh_out, g_out, sh_out):
    C = r_ref.shape[1]
    ti = lax.broadcasted_iota(jnp.int32, (C, C), 0)
    si = lax.broadcasted_iota(jnp.int32, (C, C), 1)
    tx = ti ^ si
    strict = ti > si
    incl = ti >= si
    eye_c = (ti == si).astype(F32)
    lane = lax.broadcasted_iota(jnp.int32, (1, LANES), 1)
    head0 = lane < HEAD_DIM
    pi = lax.broadcasted_iota(jnp.int32, (LANES, LANES), 0)
    pj = lax.broadcasted_iota(jnp.int32, (LANES, LANES), 1)
    same_head = (pi < HEAD_DIM) == (pj < HEAD_DIM)
    eye_p = (pi == pj).astype(F32)
    tri = tri_ref[...]

    pairs = range(N_PAIRS)
    heads = [(p, h) for p in pairs for h in range(2)]
    P = []
    for p in pairs:
        sl = slice(p * LANES, (p + 1) * LANES)
        r, lw, kh, v, kk, bb = (ref[0, :, sl] for ref in (r_ref, lw_ref, kh_ref, v_ref, kk_ref, bb_ref))
        l_hi, l_mid, l_lo = _split3(lw)
        lc = _dot(tri, l_hi) + _dot(tri, l_mid) + _dot(tri, l_lo)
        mid = lc[C // 2 - 1:C // 2, :]
        last = lc[C - 1:C, :]
        e_dn = jnp.exp(mid - lc)
        e_end = jnp.exp(last - lc)
        aa = kk * jnp.exp(lc - lw - mid)
        P.append(dict(sl=sl, v=v, aa=aa, rt=r * jnp.exp(lc - mid), rho=jnp.exp(mid), g_last=jnp.exp(last),
                      bt_b=_bf(bb * e_dn), kt_b=_bf(kh * e_dn), bc_b=_bf(bb * e_end), kc_b=_bf(kh * e_end),
                      aa_b=_bf(aa), v_b=_bf(v)))
    lab, lak, mrb, mrk = {}, {}, {}, {}
    for p, h in heads:
        q = P[p]
        hm = head0 if h == 0 else jnp.logical_not(head0)
        aa_m = _bf(jnp.where(hm, q["aa"], 0.0))
        rt_m = _bf(jnp.where(hm, q["rt"], 0.0))
        lab[p, h] = jnp.where(strict, _dot_nt(aa_m, q["bt_b"]), 0.0)
        lak[p, h] = _bf(jnp.where(strict, _dot_nt(aa_m, q["kt_b"]), 0.0))
        mrb[p, h] = _bf(jnp.where(incl, _dot_nt(rt_m, q["bt_b"]), 0.0))
        mrk[p, h] = _bf(jnp.where(incl, _dot_nt(rt_m, q["kt_b"]), 0.0))
    d = {k: eye_c - jnp.where(tx < 2, lab[k], 0.0) for k in heads}
    s = 2
    while s < C:
        level = (tx >= s) & (tx < 2 * s)
        d_b = {k: _bf(d[k]) for k in heads}
        t1 = {k: _bf(_dot(d_b[k], _bf(jnp.where(level, lab[k], 0.0)))) for k in heads}
        d = {k: d[k] - _dot(t1[k], d_b[k]) for k in heads}
        s *= 2
    d_b = {k: _bf(d[k]) for k in heads}
    w = {k: _bf(_dot(lak[k], P[k[0]]["v_b"])) for k in heads}
    ah = {k: _dot(d_b[k], P[k[0]]["aa_b"]) * P[k[0]]["rho"] for k in heads}
    uh = {k: _dot(d_b[k], w[k]) for k in heads}
    rh = {k: P[k[0]]["rt"] * P[k[0]]["rho"] - _dot(mrb[k], _bf(ah[k])) for k in heads}
    yh = {k: _dot(mrk[k], P[k[0]]["v_b"]) - _dot(mrb[k], _bf(uh[k])) for k in heads}
    for p in pairs:
        q = P[p]
        both = lambda x: jnp.where(head0, x[p, 0], x[p, 1])
        ah_p, uh_p = both(ah), both(uh)
        rh_out[0, :, q["sl"]] = both(rh)
        yh_out[0, :, q["sl"]] = both(yh)
        g_full = eye_p * q["g_last"] - _dot(_bf(ah_p.T), q["bc_b"])
        sh_full = _dot(_bf(q["v"].T), q["kc_b"]) - _dot(_bf(uh_p.T), q["bc_b"])
        g_out[0, 0, p] = jnp.where(same_head, g_full, 0.0)
        sh_out[0, 0, p] = jnp.where(same_head, sh_full, 0.0)


def _rwkv_chunks(r, lw, kh, v, kk, bb):
    nb, T, W = r.shape
    C = RWKV_CHUNK
    nc = T // C
    tri = _bf(jnp.tril(jnp.ones((C, C), F32)))
    tok = pl.BlockSpec((1, C, W), lambda b, c: (b, c, 0))
    mat = pl.BlockSpec((1, 1, N_PAIRS, LANES, LANES), lambda b, c: (b, c, 0, 0, 0))
    mat_shape = jax.ShapeDtypeStruct((nb, nc, N_PAIRS, LANES, LANES), F32)
    return pl.pallas_call(
        _rwkv_chunk_kernel,
        grid=(nb, nc),
        in_specs=[tok] * 6 + [_full((C, C))],
        out_specs=[tok, tok, mat, mat],
        out_shape=[jax.ShapeDtypeStruct((nb, T, W), F32)] * 2 + [mat_shape] * 2,
        compiler_params=_cparams("parallel", "parallel"),
        name="rwkv_chunks",
    )(r, lw, kh, v, kk, bb, tri)


def _rwkv_scan_kernel(rh_ref, yh_ref, g_ref, sh_ref, s0_ref, y_out, sfin_out, s_scr):
    c = pl.program_id(1)
    nbg = rh_ref.shape[0]

    @pl.when(c == 0)
    def _():
        s_scr[...] = s0_ref[...]

    for b in range(nbg):
        for p in range(N_PAIRS):
            sl = slice(p * LANES, (p + 1) * LANES)
            s = s_scr[b, p]
            y_out[b, :, sl] = _dot3_nt(rh_ref[b, :, sl], s) + yh_ref[b, :, sl]
            s_new = _dot3(s, g_ref[b, 0, p]) + sh_ref[b, 0, p]
            s_scr[b, p] = s_new
            sfin_out[b, p] = s_new


def _rwkv_scan(rh, yh, g, sh, s0, nbg):
    nb, T, W = rh.shape
    C = RWKV_CHUNK
    nc = T // C
    tok = pl.BlockSpec((nbg, C, W), lambda i, c: (i, c, 0))
    mat = pl.BlockSpec((nbg, 1, N_PAIRS, LANES, LANES), lambda i, c: (i, c, 0, 0, 0))
    st = pl.BlockSpec((nbg, N_PAIRS, LANES, LANES), lambda i, c: (i, 0, 0, 0))
    return pl.pallas_call(
        _rwkv_scan_kernel,
        grid=(nb // nbg, nc),
        in_specs=[tok, tok, mat, mat, st],
        out_specs=[tok, st],
        out_shape=[jax.ShapeDtypeStruct((nb, T, W), F32),
                   jax.ShapeDtypeStruct((nb, N_PAIRS, LANES, LANES), F32)],
        scratch_shapes=[pltpu.VMEM((nbg, N_PAIRS, LANES, LANES), F32)],
        compiler_params=_cparams("parallel", "arbitrary"),
        name="rwkv_scan",
    )(rh, yh, g, sh, s0)


def _state_to_pairs(s):
    nb = s.shape[0]
    s = s.reshape(nb, N_PAIRS, 2, HEAD_DIM, HEAD_DIM)
    z = jnp.zeros_like(s[:, :, 0])
    top = jnp.concatenate([s[:, :, 0], z], axis=-1)
    bot = jnp.concatenate([z, s[:, :, 1]], axis=-1)
    return jnp.concatenate([top, bot], axis=-2)


def _pairs_to_state(sp):
    nb = sp.shape[0]
    a = sp[:, :, :HEAD_DIM, :HEAD_DIM]
    b = sp[:, :, HEAD_DIM:, HEAD_DIM:]
    return jnp.stack([a, b], axis=2).reshape(nb, N_HEADS, HEAD_DIM, HEAD_DIM)


def _fox_aug_kernel(x_ref, c_ref, o_ref, *, is_query):
    lane = lax.broadcasted_iota(jnp.int32, (1, LANES), 1)
    c = c_ref[0]
    for p in range(N_PAIRS):
        xp = x_ref[0, :, p * LANES:(p + 1) * LANES].astype(F32)
        xr = pltpu.roll(xp, HEAD_DIM, axis=1)
        for h in range(2):
            hh = 2 * p + h
            ch = c[:, hh:hh + 1]
            hi = _bf(ch).astype(F32)
            r1 = ch - hi
            mid = _bf(r1).astype(F32)
            lo = _bf(r1 - mid).astype(F32)
            one = jnp.ones_like(ch)
            cols = (hi, mid, lo, one, one, one) if is_query else (one, one, one, -hi, -mid, -lo)
            aug = jnp.zeros_like(xp)
            for n, col in enumerate(cols):
                aug = jnp.where(lane == HEAD_DIM + n, col, aug)
            o_ref[0, hh] = _bf(jnp.where(lane < HEAD_DIM, xp if h == 0 else xr, aug))


def _fox_aug(x, c, tm, is_query):
    nb, S, W = x.shape
    return pl.pallas_call(
        functools.partial(_fox_aug_kernel, is_query=is_query),
        grid=(nb, S // tm),
        in_specs=[pl.BlockSpec((1, tm, W), lambda b, j: (b, j, 0)),
                  pl.BlockSpec((1, tm, LANES), lambda b, j: (b, j, 0))],
        out_specs=pl.BlockSpec((1, N_HEADS, tm, LANES), lambda b, j: (b, 0, j, 0)),
        out_shape=jax.ShapeDtypeStruct((nb, N_HEADS, S, LANES), BF16),
        compiler_params=_cparams("parallel", "parallel"),
        name="fox_aug",
    )(x, c)


def _fox_kernel(q_ref, k_ref, v_ref, o_ref, m_scr, l_scr, acc_scr, sa_scr, sb_scr, *, tk, q_off):
    i = pl.program_id(2)
    tq = q_ref.shape[2]
    q_start = q_off + i * tq
    m_scr[...] = jnp.full_like(m_scr, -jnp.inf)
    l_scr[...] = jnp.zeros_like(l_scr)
    acc_scr[...] = jnp.zeros_like(acc_scr)
    n_full = (q_start + 1) // tk

    def scores_into(j, dst):
        ks = pl.multiple_of(j * tk, tk)
        for h in range(2):
            dst[h] = _dot_nt(q_ref[0, h], k_ref[0, h, pl.ds(ks, tk), :])

    def update_from(j, src, masked):
        ks = pl.multiple_of(j * tk, tk)
        v = v_ref[0, pl.ds(ks, tk), :]
        if masked:
            visible = (ks + lax.broadcasted_iota(jnp.int32, (1, tk), 1)) <= (
                q_start + lax.broadcasted_iota(jnp.int32, (tq, 1), 0))
        for h in range(2):
            s = src[h]
            if masked:
                s = jnp.where(visible, s, -jnp.inf)
            m_old = m_scr[h]
            m_new = jnp.maximum(m_old, jnp.max(s, axis=-1, keepdims=True))
            alpha = jnp.exp(m_old - m_new)
            pr = jnp.exp(s - m_new)
            l_scr[h] = alpha * l_scr[h] + jnp.sum(pr, axis=-1, keepdims=True)
            acc_scr[h] = alpha * acc_scr[h] + _dot(_bf(pr), v)
            m_scr[h] = m_new

    def step(j, src, dst):
        scores_into(j + 1, dst)
        update_from(j, src, False)

    def two_steps(jj, carry):
        step(2 * jj, sa_scr, sb_scr)
        step(2 * jj + 1, sb_scr, sa_scr)
        return carry

    scores_into(0, sa_scr)
    lax.fori_loop(0, n_full // 2, two_steps, 0)
    odd = n_full % 2 == 1

    @pl.when(odd)
    def _():
        step(n_full - 1, sa_scr, sb_scr)
        update_from(n_full, sb_scr, True)

    @pl.when(jnp.logical_not(odd))
    def _():
        update_from(n_full, sa_scr, True)

    head0 = lax.broadcasted_iota(jnp.int32, (1, LANES), 1) < HEAD_DIM
    o_ref[0] = jnp.where(head0, acc_scr[0] / l_scr[0], acc_scr[1] / l_scr[1])


def _fox_attention(q, k, v, tq, tk, q_off):
    nb, _, sq, _ = q.shape
    sk = k.shape[2]
    for q_start in range(q_off, q_off + sq, tq):
        assert (q_start + 1) // tk + 1 == -(-(q_start + tq) // tk) <= sk // tk, (q_start, tq, tk)
    return pl.pallas_call(
        functools.partial(_fox_kernel, tk=tk, q_off=q_off),
        grid=(nb, N_PAIRS, sq // tq),
        in_specs=[pl.BlockSpec((1, 2, tq, LANES), lambda b, p, i: (b, p, i, 0)),
                  pl.BlockSpec((1, 2, sk, LANES), lambda b, p, i: (b, p, 0, 0)),
                  pl.BlockSpec((1, sk, LANES), lambda b, p, i: (b, 0, p))],
        out_specs=pl.BlockSpec((1, tq, LANES), lambda b, p, i: (b, i, p)),
        out_shape=jax.ShapeDtypeStruct((nb, sq, FOX_WIDTH), F32),
        scratch_shapes=[pltpu.VMEM((2, tq, 1), F32), pltpu.VMEM((2, tq, 1), F32),
                        pltpu.VMEM((2, tq, LANES), F32), pltpu.VMEM((2, tq, tk), F32),
                        pltpu.VMEM((2, tq, tk), F32)],
        compiler_params=_cparams("parallel", "parallel", "arbitrary"),
        name="fox_attention",
    )(q, k, v)


def _out_kernel(*refs, n_first):
    tok_refs, rest = refs[:12], refs[12:]
    lnw_ref, lnb_ref, gs_ref, wout_ref, gffn_ref, rwh_ref, rwl_ref, h_out, xn_out, sc_out = rest
    first = pl.program_id(0) < n_first
    x, y, bonus, g, oa, og = (jnp.where(first, tok_refs[2 * n][...], tok_refs[2 * n + 1][...]) for n in range(6))
    gs = gs_ref[...]
    inv_d = 1.0 / HEAD_DIM
    mean = _dot_x01(y, gs) * inv_d
    d = y - mean
    var = _dot_x01(d * d, gs) * inv_d
    yn = d * lax.rsqrt(var + GN_EPS) * lnw_ref[...] + lnb_ref[...]
    o_rwkv = (yn + bonus) * g
    o_fox = oa * jax.nn.sigmoid(og)
    mix = jnp.concatenate([_bf(o_rwkv), _bf(o_fox)], axis=-1)
    h = x + _dot(mix, wout_ref[...])
    h_out[...] = h
    xn = _rms(h, gffn_ref[...])
    _store_chunked(xn_out, xn)
    xh, xl = _split2(xn)
    logits = _dot(xh, rwh_ref[...]) + _dot(xl, rwh_ref[...]) + _dot(xh, rwl_ref[...])
    sc_out[...] = jax.nn.sigmoid(logits)


def _two_streams(width, tm, n_first):
    return [pl.BlockSpec((tm, width), lambda i: (jnp.minimum(i, n_first - 1), 0)),
            pl.BlockSpec((tm, width), lambda i: (jnp.maximum(i - n_first, 0), 0))]


def _out_proj(streams, lnw, lnb, gs, wout, gffn, rwh, rwl, tm):
    n_first = streams[0][0].shape[0] // tm
    T = streams[0][0].shape[0] + streams[0][1].shape[0]
    W = RWKV_WIDTH
    tok = lambda width: pl.BlockSpec((tm, width), lambda i: (i, 0))
    tok_specs, tok_args = [], []
    for a, b in streams:
        tok_specs += _two_streams(a.shape[1], tm, n_first)
        tok_args += [a, b]
    return pl.pallas_call(
        functools.partial(_out_kernel, n_first=n_first),
        grid=(T // tm,),
        in_specs=tok_specs + [_full((1, W)), _full((1, W)), _full((W, W)),
                              _full((D_MODEL, D_MODEL)), _full((1, D_MODEL)),
                              _full((D_MODEL, N_EXPERTS)), _full((D_MODEL, N_EXPERTS))],
        out_specs=[tok(D_MODEL), pl.BlockSpec((tm * ROW_CHUNKS, LANES), lambda i: (i, 0)), tok(N_EXPERTS)],
        out_shape=[jax.ShapeDtypeStruct((T, D_MODEL), F32),
                   jax.ShapeDtypeStruct((T * ROW_CHUNKS, LANES), F32),
                   jax.ShapeDtypeStruct((T, N_EXPERTS), F32)],
        compiler_params=_cparams("parallel"),
        name="out_proj",
    )(*tok_args, lnw, lnb, gs, wout, gffn, rwh, rwl)


def _route_kernel(sc_ref, bias_ref, before_ref, idx_out, wt_out, rank_out, cnt_out, cnt_scr):
    tm = sc_ref.shape[0]
    neg = -jnp.inf

    @pl.when(pl.program_id(0) == 0)
    def _():
        cnt_scr[...] = jnp.zeros_like(cnt_scr)

    st = sc_ref[...].T
    sel = st + bias_ref[...]
    gscore = []
    for gi in range(N_GROUPS):
        blk = sel[gi * GROUP_SIZE:(gi + 1) * GROUP_SIZE, :]
        m1 = jnp.max(blk, axis=0, keepdims=True)
        n1 = jnp.sum((blk == m1).astype(F32), axis=0, keepdims=True)
        m2 = jnp.max(jnp.where(blk < m1, blk, neg), axis=0, keepdims=True)
        gscore.append(m1 + jnp.where(n1 > 1.0, m1, m2))
    taken = [jnp.zeros((1, tm), jnp.bool_) for _ in range(N_GROUPS)]
    for _ in range(TOPK_GROUPS):
        avail = [jnp.where(taken[gi], neg, gscore[gi]) for gi in range(N_GROUPS)]
        best = functools.reduce(jnp.maximum, avail)
        found = jnp.zeros((1, tm), jnp.bool_)
        for gi in range(N_GROUPS):
            hit = (avail[gi] == best) & jnp.logical_not(found)
            taken[gi] = taken[gi] | hit
            found = found | hit
    cand = jnp.concatenate(
        [jnp.where(taken[gi], sel[gi * GROUP_SIZE:(gi + 1) * GROUP_SIZE, :], neg) for gi in range(N_GROUPS)], axis=0)
    eid = lax.broadcasted_iota(jnp.int32, (N_EXPERTS, tm), 0).astype(F32)
    idxs, wts = [], []
    onehot = jnp.zeros((N_EXPERTS, tm), F32)
    for _ in range(TOP_K):
        best = jnp.max(cand, axis=0, keepdims=True)
        pick = jnp.min(jnp.where(cand == best, eid, float(N_EXPERTS)), axis=0, keepdims=True)
        chosen = eid == pick
        wts.append(jnp.sum(jnp.where(chosen, st, 0.0), axis=0, keepdims=True))
        idxs.append(pick)
        cand = jnp.where(chosen, neg, cand)
        onehot = jnp.where(chosen, 1.0, onehot)
    w = jnp.concatenate(wts, axis=0)
    idx_out[...] = jnp.concatenate(idxs, axis=0).astype(jnp.int32)
    wt_out[...] = w / jnp.sum(w, axis=0, keepdims=True) * ROUTED_SCALE
    earlier = _dot(_bf(onehot), before_ref[...]) + cnt_scr[...]
    rank_out[...] = jnp.concatenate(
        [jnp.sum(jnp.where(eid == pick, earlier, 0.0), axis=0, keepdims=True) for pick in idxs],
        axis=0).astype(jnp.int32)
    cnt_scr[...] += jnp.sum(onehot, axis=1, keepdims=True)
    cnt_out[...] = cnt_scr[...]


def _route(scores, bias_col, tm):
    T = scores.shape[0]
    before = _bf(jnp.triu(jnp.ones((tm, tm), F32), 1))
    tok = pl.BlockSpec((TOP_K, tm), lambda i: (0, i))
    return pl.pallas_call(
        _route_kernel,
        grid=(T // tm,),
        in_specs=[pl.BlockSpec((tm, N_EXPERTS), lambda i: (i, 0)), _full((N_EXPERTS, 1)), _full((tm, tm))],
        out_specs=[tok, tok, tok, _full((N_EXPERTS, 1))],
        out_shape=[jax.ShapeDtypeStruct((TOP_K, T), jnp.int32), jax.ShapeDtypeStruct((TOP_K, T), F32),
                   jax.ShapeDtypeStruct((TOP_K, T), jnp.int32), jax.ShapeDtypeStruct((N_EXPERTS, 1), F32)],
        scratch_shapes=[pltpu.VMEM((N_EXPERTS, 1), F32)],
        compiler_params=_cparams("arbitrary"),
        name="route",
    )(scores, bias_col, before)


def _pos_kernel(idx_ref, rank_ref, start_ref, pos_out):
    tm = idx_ref.shape[1]
    eid = lax.broadcasted_iota(jnp.int32, (N_EXPERTS, tm), 0)
    idx = idx_ref[...]
    start = start_ref[...]
    base = jnp.concatenate(
        [jnp.sum(jnp.where(eid == idx[k:k + 1, :], start, 0.0), axis=0, keepdims=True) for k in range(TOP_K)],
        axis=0)
    pos_out[...] = rank_ref[...] + base.astype(jnp.int32)


def _positions(eidx_t, rank_t, start_col, tm):
    T = eidx_t.shape[1]
    tok = pl.BlockSpec((TOP_K, tm), lambda i: (0, i))
    return pl.pallas_call(
        _pos_kernel,
        grid=(T // tm,),
        in_specs=[tok, tok, _full((N_EXPERTS, 1))],
        out_specs=tok,
        out_shape=jax.ShapeDtypeStruct((TOP_K, T), jnp.int32),
        compiler_params=_cparams("parallel"),
        name="moe_positions",
    )(eidx_t, rank_t, start_col)


def _dispatch_kernel(pos_ref, x_ref, h_ref, sg_ref, su_ref, sd_ref, xs_in, xs_out, hs_out, sem):
    del xs_in
    tm = pos_ref.shape[1]

    def issue(t, carry):
        src = x_ref.at[pl.ds(pl.multiple_of(t * ROW_CHUNKS, ROW_CHUNKS), ROW_CHUNKS), :]
        for k in range(TOP_K):
            row = pl.multiple_of(pos_ref[k, t] * ROW_CHUNKS, ROW_CHUNKS)
            pltpu.make_async_copy(src, xs_out.at[pl.ds(row, ROW_CHUNKS), :], sem).start()
        return carry

    lax.fori_loop(0, tm, issue, 0)
    xb = _bf(_load_chunked(x_ref, tm))
    hg = _dot(xb, sg_ref[...])
    hu = _dot(xb, su_ref[...])
    hs_out[...] = h_ref[...] + _dot(_bf(hg * jax.nn.sigmoid(hg) * hu), sd_ref[...])
    for k in range(TOP_K):
        pltpu.make_async_copy(x_ref, xs_out.at[pl.ds(0, tm * ROW_CHUNKS), :], sem).wait()


def _dispatch(pos_t, xn_chunked, h, sg, su, sd, n_rows, tm):
    T = pos_t.shape[1]
    xs0 = jnp.zeros((n_rows * ROW_CHUNKS, LANES), F32)
    return pl.pallas_call(
        _dispatch_kernel,
        grid=(T // tm,),
        in_specs=[pl.BlockSpec((TOP_K, tm), lambda i: (0, i), memory_space=pltpu.SMEM),
                  pl.BlockSpec((tm * ROW_CHUNKS, LANES), lambda i: (i, 0)),
                  pl.BlockSpec((tm, D_MODEL), lambda i: (i, 0)),
                  _full((D_MODEL, EXPERT_FF)), _full((D_MODEL, EXPERT_FF)), _full((EXPERT_FF, D_MODEL)),
                  pl.BlockSpec(memory_space=pl.ANY)],
        out_specs=[pl.BlockSpec(memory_space=pl.ANY), pl.BlockSpec((tm, D_MODEL), lambda i: (i, 0))],
        out_shape=[jax.ShapeDtypeStruct((n_rows * ROW_CHUNKS, LANES), F32),
                   jax.ShapeDtypeStruct((T, D_MODEL), F32)],
        scratch_shapes=[pltpu.SemaphoreType.DMA(())],
        input_output_aliases={6: 0},
        compiler_params=_cparams("arbitrary"),
        name="moe_dispatch",
    )(pos_t, xn_chunked, h, sg, su, sd, xs0)


def _gmm_kernel(first_ref, count_ref, used_ref, xs_hbm, wg_ref, wu_ref, wd_ref, y_hbm,
                xbuf, ybuf, wg_b, wu_b, wd_b, in_sems, out_sems):
    e = pl.program_id(0)
    bm = MOE_ROWS
    blk_rows = bm * ROW_CHUNKS
    n_used = used_ref[0]
    n_blocks = y_hbm.shape[0] // blk_rows

    def read(g, slot):
        return pltpu.make_async_copy(xs_hbm.at[pl.ds(pl.multiple_of(g * blk_rows, blk_rows), blk_rows), :],
                                     xbuf.at[slot], in_sems.at[slot])

    def write(g, slot):
        return pltpu.make_async_copy(ybuf.at[slot],
                                     y_hbm.at[pl.ds(pl.multiple_of(g * blk_rows, blk_rows), blk_rows), :],
                                     out_sems.at[slot])

    @pl.when(e == 0)
    def _():
        for g0 in range(GMM_AHEAD):
            @pl.when(g0 < n_used)
            def _():
                read(g0, g0).start()

    wg_b[...] = _bf(wg_ref[0])
    wu_b[...] = _bf(wu_ref[0])
    wd_b[...] = _bf(wd_ref[0])

    def block(g, carry):
        slot = g % (GMM_AHEAD + 1)
        oslot = g % 2
        read(g, slot).wait()

        @pl.when(g + GMM_AHEAD < n_used)
        def _():
            read(g + GMM_AHEAD, (g + GMM_AHEAD) % (GMM_AHEAD + 1)).start()

        xe = _bf(_load_chunked(xbuf.at[slot], bm))
        hg = _dot(xe, wg_b[...])
        hu = _dot(xe, wu_b[...])
        y = _dot(_bf(hg * jax.nn.sigmoid(hg) * hu), wd_b[...])

        @pl.when(g >= 2)
        def _():
            write(g - 2, oslot).wait()

        _store_chunked(ybuf.at[oslot], y)
        write(g, oslot).start()
        return carry

    lax.fori_loop(first_ref[e], first_ref[e] + count_ref[e], block, 0)

    @pl.when(e == pl.num_programs(0) - 1)
    def _():
        for back in (2, 1):
            @pl.when(n_used >= back)
            def _():
                write(n_used - back, (n_used - back) % 2).wait()
        ybuf[0] = jnp.zeros_like(ybuf[0])

        def fill(g, carry):
            write(g, 0).start()
            return carry

        def drain(g, carry):
            write(g, 0).wait()
            return carry

        lax.fori_loop(n_used, n_blocks, fill, 0)
        lax.fori_loop(n_used, n_blocks, drain, 0)


def _gmm(first_blk, blk_count, n_used, xs, wg, wu, wd):
    bm = MOE_ROWS
    blk_rows = bm * ROW_CHUNKS
    wspec = lambda shape: pl.BlockSpec((1,) + shape, lambda e, *_: (e, 0, 0))
    grid_spec = pltpu.PrefetchScalarGridSpec(
        num_scalar_prefetch=3,
        grid=(N_EXPERTS,),
        in_specs=[pl.BlockSpec(memory_space=pl.ANY), wspec((D_MODEL, EXPERT_FF)), wspec((D_MODEL, EXPERT_FF)),
                  wspec((EXPERT_FF, D_MODEL))],
        out_specs=pl.BlockSpec(memory_space=pl.ANY),
        scratch_shapes=[pltpu.VMEM((GMM_AHEAD + 1, blk_rows, LANES), F32), pltpu.VMEM((2, blk_rows, LANES), F32),
                        pltpu.VMEM((D_MODEL, EXPERT_FF), BF16), pltpu.VMEM((D_MODEL, EXPERT_FF), BF16),
                        pltpu.VMEM((EXPERT_FF, D_MODEL), BF16),
                        pltpu.SemaphoreType.DMA((GMM_AHEAD + 1,)), pltpu.SemaphoreType.DMA((2,))],
    )
    return pl.pallas_call(
        _gmm_kernel,
        grid_spec=grid_spec,
        out_shape=jax.ShapeDtypeStruct(xs.shape, F32),
        compiler_params=_cparams("arbitrary"),
        name="expert_gmm",
    )(first_blk, blk_count, n_used, xs, wg, wu, wd)


def _final_kernel(pos_ref, nxt_ref, hs_ref, w_ref, pa_ref, pb_ref, ys_ref, gple_ref, wpg_ref, wpp_ref, gfin_ref,
                  y_out, buf, sems, *, n_first):
    i = pl.program_id(0)
    n = pl.num_programs(0)
    tm = hs_ref.shape[0]

    def gather(rows_ref, slot):
        def issue(t, carry):
            dst = pl.ds(pl.multiple_of(t * ROW_CHUNKS, ROW_CHUNKS), ROW_CHUNKS)
            for k in range(TOP_K):
                row = pl.multiple_of(rows_ref[k, t] * ROW_CHUNKS, ROW_CHUNKS)
                pltpu.make_async_copy(ys_ref.at[pl.ds(row, ROW_CHUNKS), :], buf.at[slot, k, dst, :],
                                      sems.at[slot, k]).start()
            return carry

        lax.fori_loop(0, tm, issue, 0)

    slot = i % 2

    @pl.when(i == 0)
    def _():
        gather(pos_ref, 0)

    @pl.when(i + 1 < n)
    def _():
        gather(nxt_ref, 1 - slot)

    pp = _dot(_bf(jnp.where(i < n_first, pa_ref[...], pb_ref[...])), wpp_ref[...])
    w = w_ref[...]
    routed = jnp.zeros((tm, D_MODEL), F32)
    for k in range(TOP_K):
        pltpu.make_async_copy(ys_ref.at[pl.ds(0, tm * ROW_CHUNKS), :], buf.at[slot, k], sems.at[slot, k]).wait()
        routed = routed + _load_chunked(buf.at[slot, k], tm) * w[:, k:k + 1]
    h = hs_ref[...] + routed
    gate = jax.nn.sigmoid(_dot(_bf(_rms(h, gple_ref[...])), wpg_ref[...]))
    y_out[...] = _rms(h + gate * pp, gfin_ref[...])


def _final(pos_t, hs, w, p_pair, ys, gple, wpg, wpp, gfin, tm):
    T = hs.shape[0]
    n = T // tm
    n_first = p_pair[0].shape[0] // tm
    tok = lambda width: pl.BlockSpec((tm, width), lambda i: (i, 0))
    return pl.pallas_call(
        functools.partial(_final_kernel, n_first=n_first),
        grid=(n,),
        in_specs=[pl.BlockSpec((TOP_K, tm), lambda i: (0, i), memory_space=pltpu.SMEM),
                  pl.BlockSpec((TOP_K, tm), lambda i: (0, jnp.minimum(i + 1, n - 1)), memory_space=pltpu.SMEM),
                  tok(D_MODEL), tok(TOP_K)] + _two_streams(PLE_DIM, tm, n_first) + [
                  pl.BlockSpec(memory_space=pl.ANY),
                  _full((1, D_MODEL)), _full((D_MODEL, D_MODEL)), _full((PLE_DIM, D_MODEL)),
                  _full((1, D_MODEL))],
        out_specs=tok(D_MODEL),
        out_shape=jax.ShapeDtypeStruct((T, D_MODEL), F32),
        scratch_shapes=[pltpu.VMEM((2, TOP_K, tm * ROW_CHUNKS, LANES), F32),
                        pltpu.SemaphoreType.DMA((2, TOP_K))],
        compiler_params=_cparams("arbitrary"),
        name="ffn_tail",
    )(pos_t, pos_t, hs, w, *p_pair, ys, gple, wpg, wpp, gfin)


def _pad_cols(a, width):
    return jnp.pad(a, [(0, 0)] * (a.ndim - 1) + [(0, width - a.shape[-1])])


def _rwkv_pad_cols(a):
    W = RWKV_WIDTH
    o1, o2, o3 = 3 * W, 3 * W + DECAY_RANK, 3 * W + DECAY_RANK + ICL_RANK
    return jnp.concatenate([a[..., :o1], _pad_cols(a[..., o1:o2], 128), _pad_cols(a[..., o2:o3], 128),
                            a[..., o3:]], axis=-1)


def _rwkv_unpad_cols(a):
    W = RWKV_WIDTH
    return jnp.concatenate([a[..., :3 * W + DECAY_RANK], a[..., 3 * W + 128:3 * W + 128 + ICL_RANK],
                            a[..., 3 * W + 256:]], axis=-1)


def _pad_rows(a, rows):
    return jnp.pad(a, [(0, rows - a.shape[0])] + [(0, 0)] * (a.ndim - 1))


def _mixer(x, shift, s0, past, wts, tm):
    nb, T, _ = x.shape
    (r, lw, kh, v, kkn, bb, g, bonus, last) = _in_rwkv(
        x, _rwkv_pad_cols(shift), wts["gmix"], wts["w_rwkv"], wts["mu"], wts["w0"], wts["wup"], wts["a0"],
        wts["aup"], wts["gup"], wts["k_k"], wts["k_a"], wts["r_k"], wts["gs"], tm)
    q_b, k_f, k_b, v_f, v_b, og, logf = _in_fox(
        x, wts["gmix"], wts["w_fox"], wts["qn"], wts["kn"], wts["fb"], wts["gs"], tm)

    C = RWKV_CHUNK
    Tp = -(-T // C) * C
    if Tp != T:
        padt = lambda a: jnp.pad(a, ((0, 0), (0, Tp - T), (0, 0)))
        r_p, lw_p, kh_p, v_p, kk_p, bb_p = (padt(a) for a in (r, lw, kh, v, kkn, bb))
    else:
        r_p, lw_p, kh_p, v_p, kk_p, bb_p = r, lw, kh, v, kkn, bb
    rh, yh, gm, sh = _rwkv_chunks(r_p, lw_p, kh_p, v_p, kk_p, bb_p)
    y, s_fin = _rwkv_scan(rh, yh, gm, sh, _state_to_pairs(s0.astype(F32)), 4 if nb % 4 == 0 else 1)
    y = y[:, :T]

    if past is None:
        c = _cumsum(logf, min(T, 512))
        tq, tk = min(T, FOX_TQ), min(T, FOX_TK)
        o_att = _fox_attention(_fox_aug(q_b, c, tq, True), _fox_aug(k_b, c, tq, False), v_b, tq, tk, 0)
    else:
        k_past, v_past, lf_past = past
        P = k_past.shape[1]
        tk = FOX_TK_CACHED
        sk = -(-(P + T) // tk) * tk
        pads = lambda a: jnp.pad(a, ((0, 0), (0, sk - P - T), (0, 0)))
        k_all = pads(jnp.concatenate([_bf(k_past.reshape(nb, P, FOX_WIDTH)), k_b], axis=1))
        v_all = pads(jnp.concatenate([_bf(v_past.reshape(nb, P, FOX_WIDTH)), v_b], axis=1))
        lf_all = pads(jnp.concatenate([_pad_cols(lf_past.astype(F32), 128), logf], axis=1))
        c = _cumsum(lf_all, tk)
        o_att = _fox_attention(_fox_aug(q_b, c[:, P:P + T], T, True), _fox_aug(k_all, c, tk, False), v_all,
                               T, tk, P)

    n = nb * T
    flat = lambda a: a.reshape(n, a.shape[-1])
    feats = (flat(y), flat(bonus), flat(g), flat(o_att), flat(og))
    state = (k_f.reshape(nb, T, N_HEADS, HEAD_DIM), v_f.reshape(nb, T, N_HEADS, HEAD_DIM),
             logf[:, :, :N_HEADS], _pairs_to_state(s_fin), _rwkv_unpad_cols(last))
    return feats, state


def _block_tables(counts):
    blk = MOE_ROWS
    counts = counts.reshape(N_EXPERTS).astype(jnp.int32)
    blk_count = (counts + blk - 1) // blk
    blk_end = jnp.cumsum(blk_count)
    first_blk = blk_end - blk_count
    return ((first_blk * blk).astype(F32).reshape(N_EXPERTS, 1), first_blk.astype(jnp.int32),
            blk_count.astype(jnp.int32), blk_end[-1:].astype(jnp.int32))


def kernel(x_prompt, x_sample, cache_fox_k, cache_fox_v, cache_fox_logf, state_rwkv_wkv, state_rwkv_shift, p_prompt, p_sample, norm_mix_g, w_in, rwkv_mu, rwkv_w0, rwkv_w_up, rwkv_a0, rwkv_a_up, rwkv_g_up, rwkv_k_k, rwkv_k_a, rwkv_r_k, rwkv_ln_w, rwkv_ln_b, fox_q_norm, fox_k_norm, fox_f_bias, w_out, norm_ffn_g, router_w, router_bias, exp_w_gate, exp_w_up, exp_w_down, shared_w_gate, shared_w_up, shared_w_down, ple_norm_g, ple_w_gate, ple_w_proj, final_norm_g):
    assert w_in.shape[0] == 1, "single-layer kernel"
    W = RWKV_WIDTH
    row = lambda a: a.reshape(1, -1).astype(F32)
    tile_heads = lambda a: jnp.tile(a.reshape(1, HEAD_DIM), (1, N_HEADS)).astype(F32)
    hid = jnp.arange(W) // HEAD_DIM
    w_in0 = w_in[0]
    router_hi = _bf(router_w[0])
    wts = {
        "gmix": row(norm_mix_g[0]),
        "w_rwkv": _bf(_rwkv_pad_cols(w_in0[:, :RWKV_IN])),
        "w_fox": _bf(_pad_cols(w_in0[:, RWKV_IN:], FOX_PAD)),
        "mu": row(_rwkv_pad_cols(rwkv_mu[0])),
        "w0": row(rwkv_w0[0]),
        "wup": _bf(_pad_rows(rwkv_w_up[0], 128)),
        "a0": row(rwkv_a0[0]),
        "aup": _bf(_pad_rows(rwkv_a_up[0], 128)),
        "gup": _bf(rwkv_g_up[0]),
        "k_k": row(rwkv_k_k[0]),
        "k_a": row(rwkv_k_a[0]),
        "r_k": row(rwkv_r_k[0]),
        "gs": _bf((hid[:, None] == hid[None, :]).astype(F32)),
        "qn": tile_heads(fox_q_norm[0]),
        "kn": tile_heads(fox_k_norm[0]),
        "fb": _pad_cols(row(fox_f_bias[0]), 128),
    }
    nbp, Tp, _ = x_prompt.shape
    nbs, Ts, _ = x_sample.shape
    s0_prompt = jnp.zeros((nbp, N_HEADS, HEAD_DIM, HEAD_DIM), F32)
    shift0_prompt = jnp.zeros((nbp, 1, RWKV_IN), F32)
    feats_p, st_p = _mixer(x_prompt, shift0_prompt, s0_prompt, None, wts, min(Tp, 256))
    feats_s, st_s = _mixer(x_sample, state_rwkv_shift[0], state_rwkv_wkv[0],
                           (cache_fox_k[0], cache_fox_v[0], cache_fox_logf[0]), wts, Ts)

    n_p, n_s = nbp * Tp, nbs * Ts
    n_tok = n_p + n_s
    tm = math.gcd(math.gcd(n_p, n_s), TOKEN_TILE)
    streams = [(x_prompt.reshape(n_p, D_MODEL), x_sample.reshape(n_s, D_MODEL))] + list(zip(feats_p, feats_s))
    h1, xn2, scores = _out_proj(streams, row(rwkv_ln_w[0]), row(rwkv_ln_b[0]), wts["gs"], _bf(w_out[0]),
                                row(norm_ffn_g[0]), router_hi,
                                _bf(router_w[0] - router_hi.astype(F32)), tm)
    eidx_t, wts_t, rank_t, counts = _route(scores, router_bias[0].reshape(N_EXPERTS, 1).astype(F32), tm)
    n_blocks = -(-n_tok * TOP_K // MOE_ROWS) + N_EXPERTS
    start_col, first_blk, blk_count, n_used = _block_tables(counts)
    pos_t = _positions(eidx_t, rank_t, start_col, tm)
    xs, h1s = _dispatch(pos_t, xn2, h1, _bf(shared_w_gate[0]), _bf(shared_w_up[0]), _bf(shared_w_down[0]),
                        n_blocks * MOE_ROWS, tm)
    y_rows = _gmm(first_blk, blk_count, n_used, xs, exp_w_gate[0], exp_w_up[0], exp_w_down[0])
    p_pair = (p_prompt[0].reshape(n_p, PLE_DIM), p_sample[0].reshape(n_s, PLE_DIM))
    y_all = _final(pos_t, h1s, wts_t.T, p_pair, y_rows, row(ple_norm_g[0]), _bf(ple_w_gate[0]),
                   _bf(ple_w_proj[0]), row(final_norm_g), tm)
    y_prompt = y_all[:n_p].reshape(nbp, Tp, D_MODEL)
    y_sample = y_all[n_p:].reshape(nbs, Ts, D_MODEL)
    lead = lambda t: tuple(a[None] for a in t)
    return (y_prompt, y_sample) + lead(st_p) + lead(st_s)
```

```python
import functools
import math

import numpy as np
import jax
import jax.numpy as jnp
from jax import lax
from jax.experimental import pallas as pl
from jax.experimental.pallas import tpu as pltpu

F32 = jnp.float32
BF16 = jnp.bfloat16

D_MODEL = 1024
HEAD_DIM = 64
RWKV_WIDTH = 512
FOX_WIDTH = 512
N_HEADS = 8
N_PAIRS = N_HEADS // 2
DECAY_RANK = 64
ICL_RANK = 64
GATE_RANK = 128
RWKV_IN = 3 * RWKV_WIDTH + DECAY_RANK + ICL_RANK + GATE_RANK
RWKV_PAD = 3 * RWKV_WIDTH + 3 * 128
FOX_PAD = 4 * FOX_WIDTH + 128
ATTN_SCALE = HEAD_DIM ** -0.5
N_EXPERTS = 256
N_GROUPS = 8
GROUP_SIZE = N_EXPERTS // N_GROUPS
TOPK_GROUPS = 4
TOP_K = 8
EXPERT_FF = 256
ROUTED_SCALE = 2.5
PLE_DIM = 256
RMS_EPS = 1e-6
GN_EPS = 64e-5
L2_EPS = 1e-12

LANES = 128
ROW_CHUNKS = D_MODEL // LANES
PACK_CHUNKS = ROW_CHUNKS // 2
RWKV_CHUNK = 128
MOE_ROWS = 256
TOKEN_TILE = 256
GMM_AHEAD = 2
FOX_TQ, FOX_TK = 256, 1024
FOX_TK_CACHED = 768
VMEM_LIMIT = 56 * 1024 * 1024


def _cparams(*sem):
    return pltpu.CompilerParams(dimension_semantics=sem, vmem_limit_bytes=VMEM_LIMIT)


def _bf(x):
    return x.astype(BF16)


def _dot(a, b):
    return jnp.dot(a, b, preferred_element_type=F32)


def _dot_nt(a, b):
    return lax.dot_general(a, b, (((1,), (1,)), ((), ())), preferred_element_type=F32)


def _split2(x):
    hi = _bf(x)
    return hi, _bf(x - hi.astype(F32))


def _split3(x):
    hi = _bf(x)
    r1 = x - hi.astype(F32)
    mid = _bf(r1)
    return hi, mid, _bf(r1 - mid.astype(F32))


def _dot_x01(x, w01):
    hi, lo = _split2(x)
    return _dot(hi, w01) + _dot(lo, w01)


def _dot3(a, b):
    ah, al = _split2(a)
    bh, bl = _split2(b)
    return _dot(ah, bh) + _dot(al, bh) + _dot(ah, bl)


def _dot3_nt(a, b):
    ah, al = _split2(a)
    bh, bl = _split2(b)
    return _dot_nt(ah, bh) + _dot_nt(al, bh) + _dot_nt(ah, bl)


def _softplus(x):
    return jnp.maximum(x, 0.0) + jnp.log1p(jnp.exp(-jnp.abs(x)))


def _rms(x, g):
    return x * lax.rsqrt(jnp.mean(x * x, axis=-1, keepdims=True) + RMS_EPS) * g


def _full(shape):
    return pl.BlockSpec(shape, lambda *_: (0,) * len(shape))


def _store_chunked(ref, x):
    n = x.shape[0]
    for s in range(ROW_CHUNKS):
        ref[pl.ds(s, n, stride=ROW_CHUNKS), :] = x[:, s * LANES:(s + 1) * LANES]


def _load_chunked(ref, n):
    return jnp.concatenate([ref[pl.ds(s, n, stride=ROW_CHUNKS), :] for s in range(ROW_CHUNKS)], axis=1)


def _store_packed(ref, x):
    n, half = x.shape[0], D_MODEL // 2
    bits = lax.bitcast_convert_type(_bf(x).astype(F32), jnp.uint32)
    packed = (bits[:, :half] >> 16) | (bits[:, half:] & jnp.uint32(0xFFFF0000))
    for s in range(PACK_CHUNKS):
        ref[pl.ds(s, n, stride=PACK_CHUNKS), :] = packed[:, s * LANES:(s + 1) * LANES]


def _load_packed(ref, n):
    packed = jnp.concatenate([ref[pl.ds(s, n, stride=PACK_CHUNKS), :] for s in range(PACK_CHUNKS)], axis=1)
    lo = lax.bitcast_convert_type(packed << 16, F32)
    hi = lax.bitcast_convert_type(packed & jnp.uint32(0xFFFF0000), F32)
    return _bf(jnp.concatenate([lo, hi], axis=1))


def _in_rwkv_kernel(x_ref, shift_ref, gmix_ref, w_ref, mu_ref, w0_ref, wup_ref, a0_ref, aup_ref, gup_ref,
                    kk_ref, ka_ref, rk_ref, gs_ref,
                    r_out, lw_out, kh_out, v_out, kkn_out, bb_out, g_out, bonus_out, last_out,
                    carry_ref):
    j = pl.program_id(1)
    tm = x_ref.shape[1]
    xn = _bf(_rms(x_ref[0], gmix_ref[...]))
    u = _dot(xn, w_ref[...])
    first = jnp.where(j == 0, shift_ref[0], carry_ref[...])
    row = lax.broadcasted_iota(jnp.int32, (tm, 1), 0)
    prev = jnp.where(row == 0, first, pltpu.roll(u, 1, axis=0))
    carry_ref[...] = u[tm - 1:tm, :]
    last_out[0] = u[tm - 1:tm, :]
    xs = u + (prev - u) * mu_ref[...]
    W = RWKV_WIDTH
    r, k, v = xs[:, :W], xs[:, W:2 * W], xs[:, 2 * W:3 * W]
    xw, xa, xg = xs[:, 3 * W:3 * W + 128], xs[:, 3 * W + 128:3 * W + 256], xs[:, 3 * W + 256:]
    w_raw = w0_ref[...] + _dot(_bf(jnp.tanh(xw)), wup_ref[...])
    lw = -jnp.exp(-_softplus(-w_raw) - 0.5)
    a = jax.nn.sigmoid(a0_ref[...] + _dot(_bf(xa), aup_ref[...]))
    g = _dot(_bf(jax.nn.sigmoid(xg)), gup_ref[...])
    gs = gs_ref[...]
    kk = k * kk_ref[...]
    kkn = kk / jnp.maximum(jnp.sqrt(_dot_x01(kk * kk, gs)), L2_EPS)
    kh = k * (1.0 + (a - 1.0) * ka_ref[...])
    r_out[0] = r
    lw_out[0] = lw
    kh_out[0] = kh
    v_out[0] = v
    kkn_out[0] = kkn
    bb_out[0] = kkn * a
    g_out[0] = g
    bonus_out[0] = _dot_x01(r * kh * rk_ref[...], gs) * v


def _in_rwkv(x, shift, gmix, w, mu, w0, wup, a0, aup, gup, k_k, k_a, r_k, gs, tm):
    nb, T, _ = x.shape
    W = RWKV_WIDTH
    tok = lambda width: pl.BlockSpec((1, tm, width), lambda b, j: (b, j, 0))
    outs = [jax.ShapeDtypeStruct((nb, T, W), F32)] * 8 + [jax.ShapeDtypeStruct((nb, 1, RWKV_PAD), F32)]
    return pl.pallas_call(
        _in_rwkv_kernel,
        grid=(nb, T // tm),
        in_specs=[tok(D_MODEL), pl.BlockSpec((1, 1, RWKV_PAD), lambda b, j: (b, 0, 0)),
                  _full((1, D_MODEL)), _full((D_MODEL, RWKV_PAD)), _full((1, RWKV_PAD)),
                  _full((1, W)), _full((128, W)), _full((1, W)), _full((128, W)), _full((128, W)),
                  _full((1, W)), _full((1, W)), _full((1, W)), _full((W, W))],
        out_specs=[tok(W)] * 8 + [pl.BlockSpec((1, 1, RWKV_PAD), lambda b, j: (b, 0, 0))],
        out_shape=outs,
        scratch_shapes=[pltpu.VMEM((1, RWKV_PAD), F32)],
        compiler_params=_cparams("parallel", "arbitrary"),
        name="in_rwkv",
    )(x, shift, gmix, w, mu, w0, wup, a0, aup, gup, k_k, k_a, r_k, gs)


def _aug_rows(x, c, is_query):
    lane = lax.broadcasted_iota(jnp.int32, (1, LANES), 1)
    rows = []
    for p in range(N_PAIRS):
        xp = x[:, p * LANES:(p + 1) * LANES]
        xr = pltpu.roll(xp, HEAD_DIM, axis=1)
        for h in range(2):
            ch = c[:, 2 * p + h:2 * p + h + 1]
            hi = _bf(ch).astype(F32)
            r1 = ch - hi
            mid = _bf(r1).astype(F32)
            lo = _bf(r1 - mid).astype(F32)
            one = jnp.ones_like(ch)
            cols = (hi, mid, lo, one, one, one) if is_query else (one, one, one, -hi, -mid, -lo)
            aug = jnp.zeros_like(xp)
            for n, col in enumerate(cols):
                aug = jnp.where(lane == HEAD_DIM + n, col, aug)
            rows.append(_bf(jnp.where(lane < HEAD_DIM, xp if h == 0 else xr, aug)))
    return rows


def _in_fox_kernel(x_ref, gmix_ref, w_ref, qn_ref, kn_ref, fb_ref, gs_ref, tri_ref, c0_ref,
                   q_out, ka_out, k_out, v_out, vb_out, og_out, lf_out, carry_ref):
    j = pl.program_id(1)
    tm = x_ref.shape[1]

    @pl.when(j == 0)
    def _():
        carry_ref[...] = c0_ref[0]

    xn = _bf(_rms(x_ref[0], gmix_ref[...]))
    u = _dot(xn, w_ref[...])
    W = FOX_WIDTH
    q, k, v, og, fl = u[:, :W], u[:, W:2 * W], u[:, 2 * W:3 * W], u[:, 3 * W:4 * W], u[:, 4 * W:]
    gs = gs_ref[...]
    inv_d = 1.0 / HEAD_DIM
    qn = q * lax.rsqrt(_dot_x01(q * q, gs) * inv_d + RMS_EPS) * qn_ref[...]
    kn = k * lax.rsqrt(_dot_x01(k * k, gs) * inv_d + RMS_EPS) * kn_ref[...]
    logf = -_softplus(-(fl + fb_ref[...]))
    hi, mid, lo = _split3(logf)
    tri = tri_ref[...]
    c = _dot(tri, hi) + _dot(tri, mid) + _dot(tri, lo) + carry_ref[...]
    carry_ref[...] = c[tm - 1:tm, :]
    for h, row in enumerate(_aug_rows(_bf(qn * ATTN_SCALE).astype(F32), c, True)):
        q_out[0, h] = row
    for h, row in enumerate(_aug_rows(_bf(kn).astype(F32), c, False)):
        ka_out[0, h] = row
    k_out[0] = kn
    v_out[0] = v
    vb_out[0] = _bf(v)
    og_out[0] = og
    lf_out[0] = logf


def _in_fox(x, c0, gmix, w, qn, kn, fb, gs, tm):
    nb, T, _ = x.shape
    W = FOX_WIDTH
    tok = lambda width: pl.BlockSpec((1, tm, width), lambda b, j: (b, j, 0))
    aug = pl.BlockSpec((1, N_HEADS, tm, LANES), lambda b, j: (b, 0, j, 0))
    sds = lambda width, dt: jax.ShapeDtypeStruct((nb, T, width), dt)
    aug_shape = jax.ShapeDtypeStruct((nb, N_HEADS, T, LANES), BF16)
    tri = _bf(jnp.tril(jnp.ones((tm, tm), F32)))
    return pl.pallas_call(
        _in_fox_kernel,
        grid=(nb, T // tm),
        in_specs=[tok(D_MODEL), _full((1, D_MODEL)), _full((D_MODEL, FOX_PAD)),
                  _full((1, W)), _full((1, W)), _full((1, 128)), _full((W, W)), _full((tm, tm)),
                  pl.BlockSpec((1, 1, LANES), lambda b, j: (b, 0, 0))],
        out_specs=[aug, aug] + [tok(W)] * 4 + [tok(128)],
        out_shape=[aug_shape, aug_shape, sds(W, F32), sds(W, F32), sds(W, BF16), sds(W, F32), sds(128, F32)],
        scratch_shapes=[pltpu.VMEM((1, LANES), F32)],
        compiler_params=_cparams("parallel", "arbitrary"),
        name="in_fox",
    )(x, gmix, w, qn, kn, fb, gs, tri, c0)


def _cumsum_kernel(x_ref, tri_ref, o_ref, carry_ref):
    j = pl.program_id(1)
    ts = x_ref.shape[1]

    @pl.when(j == 0)
    def _():
        carry_ref[...] = jnp.zeros_like(carry_ref)

    hi, mid, lo = _split3(x_ref[0])
    tri = tri_ref[...]
    c = _dot(tri, hi) + _dot(tri, mid) + _dot(tri, lo) + carry_ref[...]
    o_ref[0] = c
    carry_ref[...] = c[ts - 1:ts, :]


def _cumsum(x, ts):
    nb, S, L = x.shape
    tri = _bf(jnp.tril(jnp.ones((ts, ts), F32)))
    return pl.pallas_call(
        _cumsum_kernel,
        grid=(nb, S // ts),
        in_specs=[pl.BlockSpec((1, ts, L), lambda b, j: (b, j, 0)), _full((ts, ts))],
        out_specs=pl.BlockSpec((1, ts, L), lambda b, j: (b, j, 0)),
        out_shape=jax.ShapeDtypeStruct((nb, S, L), F32),
        scratch_shapes=[pltpu.VMEM((1, L), F32)],
        compiler_params=_cparams("parallel", "arbitrary"),
        name="seq_cumsum",
    )(x, tri)


def _rwkv_chunk_kernel(r_ref, lw_ref, kh_ref, v_ref, kk_ref, bb_ref, tri_ref,
                       rh_out, yh_out, g_out, sh_out):
    C = r_ref.shape[1]
    ti = lax.broadcasted_iota(jnp.int32, (C, C), 0)
    si = lax.broadcasted_iota(jnp.int32, (C, C), 1)
    tx = ti ^ si
    strict = ti > si
    incl = ti >= si
    eye_c = (ti == si).astype(F32)
    lane = lax.broadcasted_iota(jnp.int32, (1, LANES), 1)
    head0 = lane < HEAD_DIM
    pi = lax.broadcasted_iota(jnp.int32, (LANES, LANES), 0)
    pj = lax.broadcasted_iota(jnp.int32, (LANES, LANES), 1)
    same_head = (pi < HEAD_DIM) == (pj < HEAD_DIM)
    eye_p = (pi == pj).astype(F32)
    tri = tri_ref[...]

    pairs = range(N_PAIRS)
    heads = [(p, h) for p in pairs for h in range(2)]
    P = []
    for p in pairs:
        sl = slice(p * LANES, (p + 1) * LANES)
        r, lw, kh, v, kk, bb = (ref[0, :, sl] for ref in (r_ref, lw_ref, kh_ref, v_ref, kk_ref, bb_ref))
        l_hi, l_mid, l_lo = _split3(lw)
        lc = _dot(tri, l_hi) + _dot(tri, l_mid) + _dot(tri, l_lo)
        mid = lc[C // 2 - 1:C // 2, :]
        last = lc[C - 1:C, :]
        e_dn = jnp.exp(mid - lc)
        e_end = jnp.exp(last - lc)
        aa = kk * jnp.exp(lc - lw - mid)
        P.append(dict(sl=sl, v=v, aa=aa, rt=r * jnp.exp(lc - mid), rho=jnp.exp(mid), g_last=jnp.exp(last),
                      bt_b=_bf(bb * e_dn), kt_b=_bf(kh * e_dn), bc_b=_bf(bb * e_end), kc_b=_bf(kh * e_end),
                      aa_b=_bf(aa), v_b=_bf(v)))
    lab, lak, mrb, mrk = {}, {}, {}, {}
    for p, h in heads:
        q = P[p]
        hm = head0 if h == 0 else jnp.logical_not(head0)
        aa_m = _bf(jnp.where(hm, q["aa"], 0.0))
        rt_m = _bf(jnp.where(hm, q["rt"], 0.0))
        lab[p, h] = jnp.where(strict, _dot_nt(aa_m, q["bt_b"]), 0.0)
        lak[p, h] = _bf(jnp.where(strict, _dot_nt(aa_m, q["kt_b"]), 0.0))
        mrb[p, h] = _bf(jnp.where(incl, _dot_nt(rt_m, q["bt_b"]), 0.0))
        mrk[p, h] = _bf(jnp.where(incl, _dot_nt(rt_m, q["kt_b"]), 0.0))
    d = {k: eye_c - jnp.where(tx < 2, lab[k], 0.0) for k in heads}
    s = 2
    while s < C:
        level = (tx >= s) & (tx < 2 * s)
        d_b = {k: _bf(d[k]) for k in heads}
        t1 = {k: _bf(_dot(d_b[k], _bf(jnp.where(level, lab[k], 0.0)))) for k in heads}
        d = {k: d[k] - _dot(t1[k], d_b[k]) for k in heads}
        s *= 2
    d_b = {k: _bf(d[k]) for k in heads}
    w = {k: _bf(_dot(lak[k], P[k[0]]["v_b"])) for k in heads}
    ah = {k: _dot(d_b[k], P[k[0]]["aa_b"]) * P[k[0]]["rho"] for k in heads}
    uh = {k: _dot(d_b[k], w[k]) for k in heads}
    rh = {k: P[k[0]]["rt"] * P[k[0]]["rho"] - _dot(mrb[k], _bf(ah[k])) for k in heads}
    yh = {k: _dot(mrk[k], P[k[0]]["v_b"]) - _dot(mrb[k], _bf(uh[k])) for k in heads}
    for p in pairs:
        q = P[p]
        both = lambda x: jnp.where(head0, x[p, 0], x[p, 1])
        ah_p, uh_p = both(ah), both(uh)
        rh_out[0, :, q["sl"]] = both(rh)
        yh_out[0, :, q["sl"]] = both(yh)
        g_full = eye_p * q["g_last"] - _dot(_bf(ah_p.T), q["bc_b"])
        sh_full = _dot(_bf(q["v"].T), q["kc_b"]) - _dot(_bf(uh_p.T), q["bc_b"])
        g_out[0, 0, p] = jnp.where(same_head, g_full, 0.0)
        sh_out[0, 0, p] = jnp.where(same_head, sh_full, 0.0)


def _rwkv_chunks(r, lw, kh, v, kk, bb):
    nb, T, W = r.shape
    C = RWKV_CHUNK
    nc = T // C
    tri = _bf(jnp.tril(jnp.ones((C, C), F32)))
    tok = pl.BlockSpec((1, C, W), lambda b, c: (b, c, 0))
    mat = pl.BlockSpec((1, 1, N_PAIRS, LANES, LANES), lambda b, c: (b, c, 0, 0, 0))
    mat_shape = jax.ShapeDtypeStruct((nb, nc, N_PAIRS, LANES, LANES), F32)
    return pl.pallas_call(
        _rwkv_chunk_kernel,
        grid=(nb, nc),
        in_specs=[tok] * 6 + [_full((C, C))],
        out_specs=[tok, tok, mat, mat],
        out_shape=[jax.ShapeDtypeStruct((nb, T, W), F32)] * 2 + [mat_shape] * 2,
        compiler_params=_cparams("parallel", "parallel"),
        name="rwkv_chunks",
    )(r, lw, kh, v, kk, bb, tri)


def _rwkv_scan_kernel(rh_ref, yh_ref, g_ref, sh_ref, s0_ref, y_out, sfin_out, s_scr):
    c = pl.program_id(1)
    nbg = rh_ref.shape[0]

    @pl.when(c == 0)
    def _():
        s_scr[...] = s0_ref[...]

    for b in range(nbg):
        for p in range(N_PAIRS):
            sl = slice(p * LANES, (p + 1) * LANES)
            s = s_scr[b, p]
            y_out[b, :, sl] = _dot3_nt(rh_ref[b, :, sl], s) + yh_ref[b, :, sl]
            s_new = _dot3(s, g_ref[b, 0, p]) + sh_ref[b, 0, p]
            s_scr[b, p] = s_new
            sfin_out[b, p] = s_new


def _rwkv_scan(rh, yh, g, sh, s0, nbg):
    nb, T, W = rh.shape
    C = RWKV_CHUNK
    nc = T // C
    tok = pl.BlockSpec((nbg, C, W), lambda i, c: (i, c, 0))
    mat = pl.BlockSpec((nbg, 1, N_PAIRS, LANES, LANES), lambda i, c: (i, c, 0, 0, 0))
    st = pl.BlockSpec((nbg, N_PAIRS, LANES, LANES), lambda i, c: (i, 0, 0, 0))
    return pl.pallas_call(
        _rwkv_scan_kernel,
        grid=(nb // nbg, nc),
        in_specs=[tok, tok, mat, mat, st],
        out_specs=[tok, st],
        out_shape=[jax.ShapeDtypeStruct((nb, T, W), F32),
                   jax.ShapeDtypeStruct((nb, N_PAIRS, LANES, LANES), F32)],
        scratch_shapes=[pltpu.VMEM((nbg, N_PAIRS, LANES, LANES), F32)],
        compiler_params=_cparams("parallel", "arbitrary"),
        name="rwkv_scan",
    )(rh, yh, g, sh, s0)


def _state_to_pairs(s):
    nb = s.shape[0]
    s = s.reshape(nb, N_PAIRS, 2, HEAD_DIM, HEAD_DIM)
    z = jnp.zeros_like(s[:, :, 0])
    top = jnp.concatenate([s[:, :, 0], z], axis=-1)
    bot = jnp.concatenate([z, s[:, :, 1]], axis=-1)
    return jnp.concatenate([top, bot], axis=-2)


def _pairs_to_state(sp):
    nb = sp.shape[0]
    a = sp[:, :, :HEAD_DIM, :HEAD_DIM]
    b = sp[:, :, HEAD_DIM:, HEAD_DIM:]
    return jnp.stack([a, b], axis=2).reshape(nb, N_HEADS, HEAD_DIM, HEAD_DIM)


def _fox_aug_kernel(x_ref, c_ref, o_ref):
    for h, row in enumerate(_aug_rows(x_ref[0].astype(F32), c_ref[0], False)):
        o_ref[0, h] = row


def _fox_aug(x, c, tm):
    nb, S, W = x.shape
    return pl.pallas_call(
        _fox_aug_kernel,
        grid=(nb, S // tm),
        in_specs=[pl.BlockSpec((1, tm, W), lambda b, j: (b, j, 0)),
                  pl.BlockSpec((1, tm, LANES), lambda b, j: (b, j, 0))],
        out_specs=pl.BlockSpec((1, N_HEADS, tm, LANES), lambda b, j: (b, 0, j, 0)),
        out_shape=jax.ShapeDtypeStruct((nb, N_HEADS, S, LANES), BF16),
        compiler_params=_cparams("parallel", "parallel"),
        name="fox_aug",
    )(x, c)


def _fox_kernel(q_ref, k_ref, v_ref, o_ref, m_scr, l_scr, acc_scr, sa_scr, sb_scr, *, tk, q_off):
    i = pl.program_id(2)
    tq = q_ref.shape[2]
    q_start = q_off + i * tq
    m_scr[...] = jnp.full_like(m_scr, -jnp.inf)
    l_scr[...] = jnp.zeros_like(l_scr)
    acc_scr[...] = jnp.zeros_like(acc_scr)
    n_full = (q_start + 1) // tk

    def scores_into(j, dst):
        ks = pl.multiple_of(j * tk, tk)
        for h in range(2):
            dst[h] = _dot_nt(q_ref[0, h], k_ref[0, h, pl.ds(ks, tk), :])

    def update_from(j, src, masked):
        ks = pl.multiple_of(j * tk, tk)
        v = v_ref[0, pl.ds(ks, tk), :]
        if masked:
            visible = (ks + lax.broadcasted_iota(jnp.int32, (1, tk), 1)) <= (
                q_start + lax.broadcasted_iota(jnp.int32, (tq, 1), 0))
        for h in range(2):
            s = src[h]
            if masked:
                s = jnp.where(visible, s, -jnp.inf)
            m_old = m_scr[h]
            m_new = jnp.maximum(m_old, jnp.max(s, axis=-1, keepdims=True))
            alpha = jnp.exp(m_old - m_new)
            pr = jnp.exp(s - m_new)
            l_scr[h] = alpha * l_scr[h] + jnp.sum(pr, axis=-1, keepdims=True)
            acc_scr[h] = alpha * acc_scr[h] + _dot(_bf(pr), v)
            m_scr[h] = m_new

    def step(j, src, dst):
        scores_into(j + 1, dst)
        update_from(j, src, False)

    def two_steps(jj, carry):
        step(2 * jj, sa_scr, sb_scr)
        step(2 * jj + 1, sb_scr, sa_scr)
        return carry

    scores_into(0, sa_scr)
    lax.fori_loop(0, n_full // 2, two_steps, 0)
    odd = n_full % 2 == 1

    @pl.when(odd)
    def _():
        step(n_full - 1, sa_scr, sb_scr)
        update_from(n_full, sb_scr, True)

    @pl.when(jnp.logical_not(odd))
    def _():
        update_from(n_full, sa_scr, True)

    head0 = lax.broadcasted_iota(jnp.int32, (1, LANES), 1) < HEAD_DIM
    o_ref[0] = jnp.where(head0, acc_scr[0] / l_scr[0], acc_scr[1] / l_scr[1])


def _fox_attention(q, k, v, tq, tk, q_off):
    nb, _, sq, _ = q.shape
    sk = k.shape[2]
    for q_start in range(q_off, q_off + sq, tq):
        assert (q_start + 1) // tk + 1 == -(-(q_start + tq) // tk) <= sk // tk, (q_start, tq, tk)
    return pl.pallas_call(
        functools.partial(_fox_kernel, tk=tk, q_off=q_off),
        grid=(nb, N_PAIRS, sq // tq),
        in_specs=[pl.BlockSpec((1, 2, tq, LANES), lambda b, p, i: (b, p, i, 0)),
                  pl.BlockSpec((1, 2, sk, LANES), lambda b, p, i: (b, p, 0, 0)),
                  pl.BlockSpec((1, sk, LANES), lambda b, p, i: (b, 0, p))],
        out_specs=pl.BlockSpec((1, tq, LANES), lambda b, p, i: (b, i, p)),
        out_shape=jax.ShapeDtypeStruct((nb, sq, FOX_WIDTH), F32),
        scratch_shapes=[pltpu.VMEM((2, tq, 1), F32), pltpu.VMEM((2, tq, 1), F32),
                        pltpu.VMEM((2, tq, LANES), F32), pltpu.VMEM((2, tq, tk), F32),
                        pltpu.VMEM((2, tq, tk), F32)],
        compiler_params=_cparams("parallel", "parallel", "arbitrary"),
        name="fox_attention",
    )(q, k, v)


def _out_kernel(*refs, n_first):
    tok_refs, rest = refs[:12], refs[12:]
    lnw_ref, lnb_ref, gs_ref, wout_ref, gffn_ref, rwh_ref, rwl_ref, h_out, xn_out, sc_out = rest
    first = pl.program_id(0) < n_first
    x, y, bonus, g, oa, og = (jnp.where(first, tok_refs[2 * n][...], tok_refs[2 * n + 1][...]) for n in range(6))
    gs = gs_ref[...]
    inv_d = 1.0 / HEAD_DIM
    mean = _dot_x01(y, gs) * inv_d
    d = y - mean
    var = _dot_x01(d * d, gs) * inv_d
    yn = d * lax.rsqrt(var + GN_EPS) * lnw_ref[...] + lnb_ref[...]
    o_rwkv = (yn + bonus) * g
    o_fox = oa * jax.nn.sigmoid(og)
    mix = jnp.concatenate([_bf(o_rwkv), _bf(o_fox)], axis=-1)
    h = x + _dot(mix, wout_ref[...])
    h_out[...] = h
    xn = _rms(h, gffn_ref[...])
    _store_packed(xn_out, xn)
    xh, xl = _split2(xn)
    logits = _dot(xh, rwh_ref[...]) + _dot(xl, rwh_ref[...]) + _dot(xh, rwl_ref[...])
    sc_out[...] = jax.nn.sigmoid(logits)


def _two_streams(width, tm, n_first):
    return [pl.BlockSpec((tm, width), lambda i: (jnp.minimum(i, n_first - 1), 0)),
            pl.BlockSpec((tm, width), lambda i: (jnp.maximum(i - n_first, 0), 0))]


def _out_proj(streams, lnw, lnb, gs, wout, gffn, rwh, rwl, tm):
    n_first = streams[0][0].shape[0] // tm
    T = streams[0][0].shape[0] + streams[0][1].shape[0]
    W = RWKV_WIDTH
    tok = lambda width: pl.BlockSpec((tm, width), lambda i: (i, 0))
    tok_specs, tok_args = [], []
    for a, b in streams:
        tok_specs += _two_streams(a.shape[1], tm, n_first)
        tok_args += [a, b]
    return pl.pallas_call(
        functools.partial(_out_kernel, n_first=n_first),
        grid=(T // tm,),
        in_specs=tok_specs + [_full((1, W)), _full((1, W)), _full((W, W)),
                              _full((D_MODEL, D_MODEL)), _full((1, D_MODEL)),
                              _full((D_MODEL, N_EXPERTS)), _full((D_MODEL, N_EXPERTS))],
        out_specs=[tok(D_MODEL), pl.BlockSpec((tm * PACK_CHUNKS, LANES), lambda i: (i, 0)), tok(N_EXPERTS)],
        out_shape=[jax.ShapeDtypeStruct((T, D_MODEL), F32),
                   jax.ShapeDtypeStruct((T * PACK_CHUNKS, LANES), jnp.uint32),
                   jax.ShapeDtypeStruct((T, N_EXPERTS), F32)],
        compiler_params=_cparams("parallel"),
        name="out_proj",
    )(*tok_args, lnw, lnb, gs, wout, gffn, rwh, rwl)


def _route_kernel(sc_ref, bias_ref, before_ref, idx_out, wt_out, rank_out, cnt_out, cnt_scr):
    tm = sc_ref.shape[0]
    neg = -jnp.inf

    @pl.when(pl.program_id(0) == 0)
    def _():
        cnt_scr[...] = jnp.zeros_like(cnt_scr)

    st = sc_ref[...].T
    sel = st + bias_ref[...]
    gscore = []
    for gi in range(N_GROUPS):
        blk = sel[gi * GROUP_SIZE:(gi + 1) * GROUP_SIZE, :]
        m1 = jnp.max(blk, axis=0, keepdims=True)
        n1 = jnp.sum((blk == m1).astype(F32), axis=0, keepdims=True)
        m2 = jnp.max(jnp.where(blk < m1, blk, neg), axis=0, keepdims=True)
        gscore.append(m1 + jnp.where(n1 > 1.0, m1, m2))
    taken = [jnp.zeros((1, tm), jnp.bool_) for _ in range(N_GROUPS)]
    for _ in range(TOPK_GROUPS):
        avail = [jnp.where(taken[gi], neg, gscore[gi]) for gi in range(N_GROUPS)]
        best = functools.reduce(jnp.maximum, avail)
        found = jnp.zeros((1, tm), jnp.bool_)
        for gi in range(N_GROUPS):
            hit = (avail[gi] == best) & jnp.logical_not(found)
            taken[gi] = taken[gi] | hit
            found = found | hit
    cand = jnp.concatenate(
        [jnp.where(taken[gi], sel[gi * GROUP_SIZE:(gi + 1) * GROUP_SIZE, :], neg) for gi in range(N_GROUPS)], axis=0)
    eid = lax.broadcasted_iota(jnp.int32, (N_EXPERTS, tm), 0).astype(F32)
    idxs, wts = [], []
    onehot = jnp.zeros((N_EXPERTS, tm), F32)
    for _ in range(TOP_K):
        best = jnp.max(cand, axis=0, keepdims=True)
        pick = jnp.min(jnp.where(cand == best, eid, float(N_EXPERTS)), axis=0, keepdims=True)
        chosen = eid == pick
        wts.append(jnp.sum(jnp.where(chosen, st, 0.0), axis=0, keepdims=True))
        idxs.append(pick)
        cand = jnp.where(chosen, neg, cand)
        onehot = jnp.where(chosen, 1.0, onehot)
    w = jnp.concatenate(wts, axis=0)
    idx_out[...] = jnp.concatenate(idxs, axis=0).astype(jnp.int32)
    wt_out[...] = w / jnp.sum(w, axis=0, keepdims=True) * ROUTED_SCALE
    earlier = _dot(_bf(onehot), before_ref[...]) + cnt_scr[...]
    rank_out[...] = jnp.concatenate(
        [jnp.sum(jnp.where(eid == pick, earlier, 0.0), axis=0, keepdims=True) for pick in idxs],
        axis=0).astype(jnp.int32)
    cnt_scr[...] += jnp.sum(onehot, axis=1, keepdims=True)
    cnt_out[...] = cnt_scr[...]


def _route(scores, bias_col, tm):
    T = scores.shape[0]
    before = _bf(jnp.triu(jnp.ones((tm, tm), F32), 1))
    tok = pl.BlockSpec((TOP_K, tm), lambda i: (0, i))
    return pl.pallas_call(
        _route_kernel,
        grid=(T // tm,),
        in_specs=[pl.BlockSpec((tm, N_EXPERTS), lambda i: (i, 0)), _full((N_EXPERTS, 1)), _full((tm, tm))],
        out_specs=[tok, tok, tok, _full((N_EXPERTS, 1))],
        out_shape=[jax.ShapeDtypeStruct((TOP_K, T), jnp.int32), jax.ShapeDtypeStruct((TOP_K, T), F32),
                   jax.ShapeDtypeStruct((TOP_K, T), jnp.int32), jax.ShapeDtypeStruct((N_EXPERTS, 1), F32)],
        scratch_shapes=[pltpu.VMEM((N_EXPERTS, 1), F32)],
        compiler_params=_cparams("arbitrary"),
        name="route",
    )(scores, bias_col, before)


def _pos_kernel(idx_ref, rank_ref, start_ref, pos_out):
    tm = idx_ref.shape[1]
    eid = lax.broadcasted_iota(jnp.int32, (N_EXPERTS, tm), 0)
    idx = idx_ref[...]
    start = start_ref[...]
    base = jnp.concatenate(
        [jnp.sum(jnp.where(eid == idx[k:k + 1, :], start, 0.0), axis=0, keepdims=True) for k in range(TOP_K)],
        axis=0)
    pos_out[...] = rank_ref[...] + base.astype(jnp.int32)


def _positions(eidx_t, rank_t, start_col, tm):
    T = eidx_t.shape[1]
    tok = pl.BlockSpec((TOP_K, tm), lambda i: (0, i))
    return pl.pallas_call(
        _pos_kernel,
        grid=(T // tm,),
        in_specs=[tok, tok, _full((N_EXPERTS, 1))],
        out_specs=tok,
        out_shape=jax.ShapeDtypeStruct((TOP_K, T), jnp.int32),
        compiler_params=_cparams("parallel"),
        name="moe_positions",
    )(eidx_t, rank_t, start_col)


def _dispatch_kernel(pos_ref, x_ref, h_ref, sg_ref, su_ref, sd_ref, xs_in, xs_out, hs_out, sem):
    del xs_in
    tm = pos_ref.shape[1]

    def issue(t, carry):
        src = x_ref.at[pl.ds(pl.multiple_of(t * PACK_CHUNKS, PACK_CHUNKS), PACK_CHUNKS), :]
        for k in range(TOP_K):
            row = pl.multiple_of(pos_ref[k, t] * PACK_CHUNKS, PACK_CHUNKS)
            pltpu.make_async_copy(src, xs_out.at[pl.ds(row, PACK_CHUNKS), :], sem).start()
        return carry

    lax.fori_loop(0, tm, issue, 0)
    xb = _load_packed(x_ref, tm)
    hg = _dot(xb, sg_ref[...])
    hu = _dot(xb, su_ref[...])
    hs_out[...] = h_ref[...] + _dot(_bf(hg * jax.nn.sigmoid(hg) * hu), sd_ref[...])
    for k in range(TOP_K):
        pltpu.make_async_copy(x_ref, xs_out.at[pl.ds(0, tm * PACK_CHUNKS), :], sem).wait()


def _dispatch(pos_t, xn_packed, h, sg, su, sd, n_rows, tm):
    T = pos_t.shape[1]
    xs0 = jnp.zeros((n_rows * PACK_CHUNKS, LANES), jnp.uint32)
    return pl.pallas_call(
        _dispatch_kernel,
        grid=(T // tm,),
        in_specs=[pl.BlockSpec((TOP_K, tm), lambda i: (0, i), memory_space=pltpu.SMEM),
                  pl.BlockSpec((tm * PACK_CHUNKS, LANES), lambda i: (i, 0)),
                  pl.BlockSpec((tm, D_MODEL), lambda i: (i, 0)),
                  _full((D_MODEL, EXPERT_FF)), _full((D_MODEL, EXPERT_FF)), _full((EXPERT_FF, D_MODEL)),
                  pl.BlockSpec(memory_space=pl.ANY)],
        out_specs=[pl.BlockSpec(memory_space=pl.ANY), pl.BlockSpec((tm, D_MODEL), lambda i: (i, 0))],
        out_shape=[jax.ShapeDtypeStruct((n_rows * PACK_CHUNKS, LANES), jnp.uint32),
                   jax.ShapeDtypeStruct((T, D_MODEL), F32)],
        scratch_shapes=[pltpu.SemaphoreType.DMA(())],
        input_output_aliases={6: 0},
        compiler_params=_cparams("arbitrary"),
        name="moe_dispatch",
    )(pos_t, xn_packed, h, sg, su, sd, xs0)


def _gmm_kernel(first_ref, count_ref, used_ref, xs_hbm, wg_ref, wu_ref, wd_ref, y_hbm,
                xbuf, ybuf, wg_b, wu_b, wd_b, in_sems, out_sems):
    e = pl.program_id(0)
    bm = MOE_ROWS
    blk_rows = bm * ROW_CHUNKS
    n_used = used_ref[0]
    n_blocks = y_hbm.shape[0] // blk_rows

    in_rows = bm * PACK_CHUNKS

    def read(g, slot):
        return pltpu.make_async_copy(xs_hbm.at[pl.ds(pl.multiple_of(g * in_rows, in_rows), in_rows), :],
                                     xbuf.at[slot], in_sems.at[slot])

    def write(g, slot):
        return pltpu.make_async_copy(ybuf.at[slot],
                                     y_hbm.at[pl.ds(pl.multiple_of(g * blk_rows, blk_rows), blk_rows), :],
                                     out_sems.at[slot])

    @pl.when(e == 0)
    def _():
        for g0 in range(GMM_AHEAD):
            @pl.when(g0 < n_used)
            def _():
                read(g0, g0).start()

    wg_b[...] = _bf(wg_ref[0])
    wu_b[...] = _bf(wu_ref[0])
    wd_b[...] = _bf(wd_ref[0])

    def block(g, carry):
        slot = g % (GMM_AHEAD + 1)
        oslot = g % 2
        read(g, slot).wait()

        @pl.when(g + GMM_AHEAD < n_used)
        def _():
            read(g + GMM_AHEAD, (g + GMM_AHEAD) % (GMM_AHEAD + 1)).start()

        @pl.when(g >= 2)
        def _():
            write(g - 2, oslot).wait()

        xe = _load_packed(xbuf.at[slot], bm)
        hg = _dot(xe, wg_b[...])
        hu = _dot(xe, wu_b[...])
        _store_chunked(ybuf.at[oslot], _dot(_bf(hg * jax.nn.sigmoid(hg) * hu), wd_b[...]))
        write(g, oslot).start()
        return carry

    lax.fori_loop(first_ref[e], first_ref[e] + count_ref[e], block, 0)

    @pl.when(e == pl.num_programs(0) - 1)
    def _():
        for back in (2, 1):
            @pl.when(n_used >= back)
            def _():
                write(n_used - back, (n_used - back) % 2).wait()
        ybuf[0] = jnp.zeros_like(ybuf[0])

        def fill(g, carry):
            write(g, 0).start()
            return carry

        def drain(g, carry):
            write(g, 0).wait()
            return carry

        lax.fori_loop(n_used, n_blocks, fill, 0)
        lax.fori_loop(n_used, n_blocks, drain, 0)


def _gmm(first_blk, blk_count, n_used, xs, wg, wu, wd):
    bm = MOE_ROWS
    blk_rows = bm * ROW_CHUNKS
    n_rows = xs.shape[0] // PACK_CHUNKS
    wspec = lambda shape: pl.BlockSpec((1,) + shape, lambda e, *_: (e, 0, 0))
    grid_spec = pltpu.PrefetchScalarGridSpec(
        num_scalar_prefetch=3,
        grid=(N_EXPERTS,),
        in_specs=[pl.BlockSpec(memory_space=pl.ANY), wspec((D_MODEL, EXPERT_FF)), wspec((D_MODEL, EXPERT_FF)),
                  wspec((EXPERT_FF, D_MODEL))],
        out_specs=pl.BlockSpec(memory_space=pl.ANY),
        scratch_shapes=[pltpu.VMEM((GMM_AHEAD + 1, bm * PACK_CHUNKS, LANES), jnp.uint32),
                        pltpu.VMEM((2, blk_rows, LANES), F32),
                        pltpu.VMEM((D_MODEL, EXPERT_FF), BF16), pltpu.VMEM((D_MODEL, EXPERT_FF), BF16),
                        pltpu.VMEM((EXPERT_FF, D_MODEL), BF16),
                        pltpu.SemaphoreType.DMA((GMM_AHEAD + 1,)), pltpu.SemaphoreType.DMA((2,))],
    )
    return pl.pallas_call(
        _gmm_kernel,
        grid_spec=grid_spec,
        out_shape=jax.ShapeDtypeStruct((n_rows * ROW_CHUNKS, LANES), F32),
        compiler_params=_cparams("arbitrary"),
        name="expert_gmm",
    )(first_blk, blk_count, n_used, xs, wg, wu, wd)


def _final_kernel(pos_ref, nxt_ref, hs_ref, w_ref, pa_ref, pb_ref, ys_ref, gple_ref, wpg_ref, wpp_ref, gfin_ref,
                  y_out, buf, sems, *, n_first):
    i = pl.program_id(0)
    n = pl.num_programs(0)
    tm = hs_ref.shape[0]

    def gather(rows_ref, slot):
        def issue(t, carry):
            dst = pl.ds(pl.multiple_of(t * ROW_CHUNKS, ROW_CHUNKS), ROW_CHUNKS)
            for k in range(TOP_K):
                row = pl.multiple_of(rows_ref[k, t] * ROW_CHUNKS, ROW_CHUNKS)
                pltpu.make_async_copy(ys_ref.at[pl.ds(row, ROW_CHUNKS), :], buf.at[slot, k, dst, :],
                                      sems.at[slot, k]).start()
            return carry

        lax.fori_loop(0, tm, issue, 0)

    slot = i % 2

    @pl.when(i == 0)
    def _():
        gather(pos_ref, 0)

    @pl.when(i + 1 < n)
    def _():
        gather(nxt_ref, 1 - slot)

    pp = _dot(_bf(jnp.where(i < n_first, pa_ref[...], pb_ref[...])), wpp_ref[...])
    w = w_ref[...]
    routed = jnp.zeros((tm, D_MODEL), F32)
    for k in range(TOP_K):
        pltpu.make_async_copy(ys_ref.at[pl.ds(0, tm * ROW_CHUNKS), :], buf.at[slot, k], sems.at[slot, k]).wait()
        routed = routed + _load_chunked(buf.at[slot, k], tm) * w[:, k:k + 1]
    h = hs_ref[...] + routed
    gate = jax.nn.sigmoid(_dot(_bf(_rms(h, gple_ref[...])), wpg_ref[...]))
    y_out[...] = _rms(h + gate * pp, gfin_ref[...])


def _final(pos_t, hs, w, p_pair, ys, gple, wpg, wpp, gfin, tm):
    T = hs.shape[0]
    n = T // tm
    n_first = p_pair[0].shape[0] // tm
    tok = lambda width: pl.BlockSpec((tm, width), lambda i: (i, 0))
    return pl.pallas_call(
        functools.partial(_final_kernel, n_first=n_first),
        grid=(n,),
        in_specs=[pl.BlockSpec((TOP_K, tm), lambda i: (0, i), memory_space=pltpu.SMEM),
                  pl.BlockSpec((TOP_K, tm), lambda i: (0, jnp.minimum(i + 1, n - 1)), memory_space=pltpu.SMEM),
                  tok(D_MODEL), tok(TOP_K)] + _two_streams(PLE_DIM, tm, n_first) + [
                  pl.BlockSpec(memory_space=pl.ANY),
                  _full((1, D_MODEL)), _full((D_MODEL, D_MODEL)), _full((PLE_DIM, D_MODEL)),
                  _full((1, D_MODEL))],
        out_specs=tok(D_MODEL),
        out_shape=jax.ShapeDtypeStruct((T, D_MODEL), F32),
        scratch_shapes=[pltpu.VMEM((2, TOP_K, tm * ROW_CHUNKS, LANES), F32),
                        pltpu.SemaphoreType.DMA((2, TOP_K))],
        compiler_params=_cparams("arbitrary"),
        name="ffn_tail",
    )(pos_t, pos_t, hs, w, *p_pair, ys, gple, wpg, wpp, gfin)


def _pad_cols(a, width):
    return jnp.pad(a, [(0, 0)] * (a.ndim - 1) + [(0, width - a.shape[-1])])


def _rwkv_pad_cols(a):
    W = RWKV_WIDTH
    o1, o2, o3 = 3 * W, 3 * W + DECAY_RANK, 3 * W + DECAY_RANK + ICL_RANK
    return jnp.concatenate([a[..., :o1], _pad_cols(a[..., o1:o2], 128), _pad_cols(a[..., o2:o3], 128),
                            a[..., o3:]], axis=-1)


def _rwkv_unpad_cols(a):
    W = RWKV_WIDTH
    return jnp.concatenate([a[..., :3 * W + DECAY_RANK], a[..., 3 * W + 128:3 * W + 128 + ICL_RANK],
                            a[..., 3 * W + 256:]], axis=-1)


def _pad_rows(a, rows):
    return jnp.pad(a, [(0, rows - a.shape[0])] + [(0, 0)] * (a.ndim - 1))


def _mixer(x, shift, s0, past, wts, tm):
    nb, T, _ = x.shape
    (r, lw, kh, v, kkn, bb, g, bonus, last) = _in_rwkv(
        x, _rwkv_pad_cols(shift), wts["gmix"], wts["w_rwkv"], wts["mu"], wts["w0"], wts["wup"], wts["a0"],
        wts["aup"], wts["gup"], wts["k_k"], wts["k_a"], wts["r_k"], wts["gs"], tm)
    if past is None:
        c0 = jnp.zeros((nb, 1, LANES), F32)
    else:
        k_past, v_past, lf_past = past
        P = k_past.shape[1]
        c_past = _cumsum(_pad_cols(lf_past.astype(F32), LANES), math.gcd(P, 512))
        c0 = c_past[:, P - 1:, :]
    q_aug, k_aug, k_f, v_f, v_b, og, logf = _in_fox(
        x, c0, wts["gmix"], wts["w_fox"], wts["qn"], wts["kn"], wts["fb"], wts["gs"], tm)

    C = RWKV_CHUNK
    Tp = -(-T // C) * C
    if Tp != T:
        padt = lambda a: jnp.pad(a, ((0, 0), (0, Tp - T), (0, 0)))
        r_p, lw_p, kh_p, v_p, kk_p, bb_p = (padt(a) for a in (r, lw, kh, v, kkn, bb))
    else:
        r_p, lw_p, kh_p, v_p, kk_p, bb_p = r, lw, kh, v, kkn, bb
    rh, yh, gm, sh = _rwkv_chunks(r_p, lw_p, kh_p, v_p, kk_p, bb_p)
    y, s_fin = _rwkv_scan(rh, yh, gm, sh, _state_to_pairs(s0.astype(F32)), 4 if nb % 4 == 0 else 1)
    y = y[:, :T]

    if past is None:
        o_att = _fox_attention(q_aug, k_aug, v_b, min(T, FOX_TQ), min(T, FOX_TK), 0)
    else:
        tk = FOX_TK_CACHED
        sk = -(-(P + T) // tk) * tk
        k_aug_past = _fox_aug(_bf(k_past.reshape(nb, P, FOX_WIDTH)), c_past, math.gcd(P, 512))
        k_all = jnp.pad(jnp.concatenate([k_aug_past, k_aug], axis=2), ((0, 0), (0, 0), (0, sk - P - T), (0, 0)))
        v_all = jnp.pad(jnp.concatenate([_bf(v_past.reshape(nb, P, FOX_WIDTH)), v_b], axis=1),
                        ((0, 0), (0, sk - P - T), (0, 0)))
        o_att = _fox_attention(q_aug, k_all, v_all, T, tk, P)

    n = nb * T
    flat = lambda a: a.reshape(n, a.shape[-1])
    feats = (flat(y), flat(bonus), flat(g), flat(o_att), flat(og))
    state = (k_f.reshape(nb, T, N_HEADS, HEAD_DIM), v_f.reshape(nb, T, N_HEADS, HEAD_DIM),
             logf[:, :, :N_HEADS], _pairs_to_state(s_fin), _rwkv_unpad_cols(last))
    return feats, state


def _block_tables(counts):
    blk = MOE_ROWS
    counts = counts.reshape(N_EXPERTS).astype(jnp.int32)
    blk_count = (counts + blk - 1) // blk
    blk_end = jnp.cumsum(blk_count)
    first_blk = blk_end - blk_count
    return ((first_blk * blk).astype(F32).reshape(N_EXPERTS, 1), first_blk.astype(jnp.int32),
            blk_count.astype(jnp.int32), blk_end[-1:].astype(jnp.int32))


def kernel(x_prompt, x_sample, cache_fox_k, cache_fox_v, cache_fox_logf, state_rwkv_wkv, state_rwkv_shift, p_prompt, p_sample, norm_mix_g, w_in, rwkv_mu, rwkv_w0, rwkv_w_up, rwkv_a0, rwkv_a_up, rwkv_g_up, rwkv_k_k, rwkv_k_a, rwkv_r_k, rwkv_ln_w, rwkv_ln_b, fox_q_norm, fox_k_norm, fox_f_bias, w_out, norm_ffn_g, router_w, router_bias, exp_w_gate, exp_w_up, exp_w_down, shared_w_gate, shared_w_up, shared_w_down, ple_norm_g, ple_w_gate, ple_w_proj, final_norm_g):
    assert w_in.shape[0] == 1, "single-layer kernel"
    W = RWKV_WIDTH
    row = lambda a: a.reshape(1, -1).astype(F32)
    tile_heads = lambda a: jnp.tile(a.reshape(1, HEAD_DIM), (1, N_HEADS)).astype(F32)
    hid = jnp.arange(W) // HEAD_DIM
    w_in0 = w_in[0]
    router_hi = _bf(router_w[0])
    wts = {
        "gmix": row(norm_mix_g[0]),
        "w_rwkv": _bf(_rwkv_pad_cols(w_in0[:, :RWKV_IN])),
        "w_fox": _bf(_pad_cols(w_in0[:, RWKV_IN:], FOX_PAD)),
        "mu": row(_rwkv_pad_cols(rwkv_mu[0])),
        "w0": row(rwkv_w0[0]),
        "wup": _bf(_pad_rows(rwkv_w_up[0], 128)),
        "a0": row(rwkv_a0[0]),
        "aup": _bf(_pad_rows(rwkv_a_up[0], 128)),
        "gup": _bf(rwkv_g_up[0]),
        "k_k": row(rwkv_k_k[0]),
        "k_a": row(rwkv_k_a[0]),
        "r_k": row(rwkv_r_k[0]),
        "gs": _bf((hid[:, None] == hid[None, :]).astype(F32)),
        "qn": tile_heads(fox_q_norm[0]),
        "kn": tile_heads(fox_k_norm[0]),
        "fb": _pad_cols(row(fox_f_bias[0]), 128),
    }
    nbp, Tp, _ = x_prompt.shape
    nbs, Ts, _ = x_sample.shape
    s0_prompt = jnp.zeros((nbp, N_HEADS, HEAD_DIM, HEAD_DIM), F32)
    shift0_prompt = jnp.zeros((nbp, 1, RWKV_IN), F32)
    feats_p, st_p = _mixer(x_prompt, shift0_prompt, s0_prompt, None, wts, min(Tp, 256))
    feats_s, st_s = _mixer(x_sample, state_rwkv_shift[0], state_rwkv_wkv[0],
                           (cache_fox_k[0], cache_fox_v[0], cache_fox_logf[0]), wts, Ts)

    n_p, n_s = nbp * Tp, nbs * Ts
    n_tok = n_p + n_s
    tm = math.gcd(math.gcd(n_p, n_s), TOKEN_TILE)
    streams = [(x_prompt.reshape(n_p, D_MODEL), x_sample.reshape(n_s, D_MODEL))] + list(zip(feats_p, feats_s))
    h1, xn2, scores = _out_proj(streams, row(rwkv_ln_w[0]), row(rwkv_ln_b[0]), wts["gs"], _bf(w_out[0]),
                                row(norm_ffn_g[0]), router_hi,
                                _bf(router_w[0] - router_hi.astype(F32)), tm)
    eidx_t, wts_t, rank_t, counts = _route(scores, router_bias[0].reshape(N_EXPERTS, 1).astype(F32), tm)
    n_blocks = -(-n_tok * TOP_K // MOE_ROWS) + N_EXPERTS
    start_col, first_blk, blk_count, n_used = _block_tables(counts)
    pos_t = _positions(eidx_t, rank_t, start_col, tm)
    xs, h1s = _dispatch(pos_t, xn2, h1, _bf(shared_w_gate[0]), _bf(shared_w_up[0]), _bf(shared_w_down[0]),
                        n_blocks * MOE_ROWS, tm)
    y_rows = _gmm(first_blk, blk_count, n_used, xs, exp_w_gate[0], exp_w_up[0], exp_w_down[0])
    p_pair = (p_prompt[0].reshape(n_p, PLE_DIM), p_sample[0].reshape(n_s, PLE_DIM))
    y_all = _final(pos_t, h1s, wts_t.T, p_pair, y_rows, row(ple_norm_g[0]), _bf(ple_w_gate[0]),
                   _bf(ple_w_proj[0]), row(final_norm_g), tm)
    y_prompt = y_all[:n_p].reshape(nbp, Tp, D_MODEL)
    y_sample = y_all[n_p:].reshape(nbs, Ts, D_MODEL)
    lead = lambda t: tuple(a[None] for a in t)
    return (y_prompt, y_sample) + lead(st_p) + lead(st_s)
```

```python
import functools
import math

import numpy as np
import jax
import jax.numpy as jnp
from jax import lax
from jax.experimental import pallas as pl
from jax.experimental.pallas import tpu as pltpu

F32 = jnp.float32
BF16 = jnp.bfloat16

D_MODEL = 1024
HEAD_DIM = 64
RWKV_WIDTH = 512
FOX_WIDTH = 512
N_HEADS = 8
N_PAIRS = N_HEADS // 2
DECAY_RANK = 64
ICL_RANK = 64
GATE_RANK = 128
RWKV_IN = 3 * RWKV_WIDTH + DECAY_RANK + ICL_RANK + GATE_RANK
RWKV_PAD = 3 * RWKV_WIDTH + 3 * 128
FOX_PAD = 4 * FOX_WIDTH + 128
ATTN_SCALE = HEAD_DIM ** -0.5
N_EXPERTS = 256
N_GROUPS = 8
GROUP_SIZE = N_EXPERTS // N_GROUPS
TOPK_GROUPS = 4
TOP_K = 8
EXPERT_FF = 256
ROUTED_SCALE = 2.5
PLE_DIM = 256
RMS_EPS = 1e-6
GN_EPS = 64e-5
L2_EPS = 1e-12

LANES = 128
MXU_TILE = 256
ROW_CHUNKS = D_MODEL // LANES
PACK_CHUNKS = ROW_CHUNKS // 2
RWKV_CHUNK = 128
MOE_ROWS = 256
TOKEN_TILE = 256
GMM_AHEAD = 2
FOX_TQ, FOX_TK = 256, 1024
FOX_TK_CACHED = 768
VMEM_LIMIT = 56 * 1024 * 1024


def _cparams(*sem):
    return pltpu.CompilerParams(dimension_semantics=sem, vmem_limit_bytes=VMEM_LIMIT)


def _bf(x):
    return x.astype(BF16)


def _dot(a, b):
    return jnp.dot(a, b, preferred_element_type=F32)


def _dot_nt(a, b):
    return lax.dot_general(a, b, (((1,), (1,)), ((), ())), preferred_element_type=F32)


def _split2(x):
    hi = _bf(x)
    return hi, _bf(x - hi.astype(F32))


def _split3(x):
    hi = _bf(x)
    r1 = x - hi.astype(F32)
    mid = _bf(r1)
    return hi, mid, _bf(r1 - mid.astype(F32))


def _dot_x01(x, w01):
    hi, lo = _split2(x)
    slabs = [slice(c, c + MXU_TILE) for c in range(0, x.shape[1], MXU_TILE)]
    return jnp.concatenate([_dot(hi[:, s], w01) + _dot(lo[:, s], w01) for s in slabs], axis=1)


def _dot3(a, b):
    ah, al = _split2(a)
    bh, bl = _split2(b)
    return _dot(ah, bh) + _dot(al, bh) + _dot(ah, bl)


def _dot3_nt(a, b):
    ah, al = _split2(a)
    bh, bl = _split2(b)
    return _dot_nt(ah, bh) + _dot_nt(al, bh) + _dot_nt(ah, bl)


def _softplus(x):
    return jnp.maximum(x, 0.0) + jnp.log1p(jnp.exp(-jnp.abs(x)))


def _rms(x, g):
    return x * lax.rsqrt(jnp.mean(x * x, axis=-1, keepdims=True) + RMS_EPS) * g


def _full(shape):
    return pl.BlockSpec(shape, lambda *_: (0,) * len(shape))


def _store_chunked(ref, x):
    n = x.shape[0]
    for s in range(ROW_CHUNKS):
        ref[pl.ds(s, n, stride=ROW_CHUNKS), :] = x[:, s * LANES:(s + 1) * LANES]


def _load_chunked(ref, n):
    return jnp.concatenate([ref[pl.ds(s, n, stride=ROW_CHUNKS), :] for s in range(ROW_CHUNKS)], axis=1)


def _store_packed(ref, x):
    n, half = x.shape[0], D_MODEL // 2
    bits = lax.bitcast_convert_type(_bf(x).astype(F32), jnp.uint32)
    packed = (bits[:, :half] >> 16) | (bits[:, half:] & jnp.uint32(0xFFFF0000))
    for s in range(PACK_CHUNKS):
        ref[pl.ds(s, n, stride=PACK_CHUNKS), :] = packed[:, s * LANES:(s + 1) * LANES]


def _load_packed(ref, n):
    packed = jnp.concatenate([ref[pl.ds(s, n, stride=PACK_CHUNKS), :] for s in range(PACK_CHUNKS)], axis=1)
    lo = lax.bitcast_convert_type(packed << 16, F32)
    hi = lax.bitcast_convert_type(packed & jnp.uint32(0xFFFF0000), F32)
    return _bf(jnp.concatenate([lo, hi], axis=1))


def _in_rwkv_kernel(x_ref, shift_ref, gmix_ref, w_ref, mu_ref, w0_ref, wup_ref, a0_ref, aup_ref, gup_ref,
                    kk_ref, ka_ref, rk_ref, gs_ref,
                    r_out, lw_out, kh_out, v_out, kkn_out, bb_out, g_out, bonus_out, last_out,
                    carry_ref):
    j = pl.program_id(1)
    tm = x_ref.shape[1]
    xn = _bf(_rms(x_ref[0], gmix_ref[...]))
    u = _dot(xn, w_ref[...])
    first = jnp.where(j == 0, shift_ref[0], carry_ref[...])
    row = lax.broadcasted_iota(jnp.int32, (tm, 1), 0)
    prev = jnp.where(row == 0, first, pltpu.roll(u, 1, axis=0))
    carry_ref[...] = u[tm - 1:tm, :]
    last_out[0] = u[tm - 1:tm, :]
    xs = u + (prev - u) * mu_ref[...]
    W = RWKV_WIDTH
    r, k, v = xs[:, :W], xs[:, W:2 * W], xs[:, 2 * W:3 * W]
    xw, xa, xg = xs[:, 3 * W:3 * W + 128], xs[:, 3 * W + 128:3 * W + 256], xs[:, 3 * W + 256:]
    w_raw = w0_ref[...] + _dot(_bf(jnp.tanh(xw)), wup_ref[...])
    lw = -jnp.exp(-_softplus(-w_raw) - 0.5)
    a = jax.nn.sigmoid(a0_ref[...] + _dot(_bf(xa), aup_ref[...]))
    g = _dot(_bf(jax.nn.sigmoid(xg)), gup_ref[...])
    gs = gs_ref[...]
    kk = k * kk_ref[...]
    kkn = kk / jnp.maximum(jnp.sqrt(_dot_x01(kk * kk, gs)), L2_EPS)
    kh = k * (1.0 + (a - 1.0) * ka_ref[...])
    r_out[0] = r
    lw_out[0] = lw
    kh_out[0] = kh
    v_out[0] = v
    kkn_out[0] = kkn
    bb_out[0] = kkn * a
    g_out[0] = g
    bonus_out[0] = _dot_x01(r * kh * rk_ref[...], gs) * v


def _in_rwkv(x, shift, gmix, w, mu, w0, wup, a0, aup, gup, k_k, k_a, r_k, gs, tm):
    nb, T, _ = x.shape
    W = RWKV_WIDTH
    tok = lambda width: pl.BlockSpec((1, tm, width), lambda b, j: (b, j, 0))
    outs = [jax.ShapeDtypeStruct((nb, T, W), F32)] * 8 + [jax.ShapeDtypeStruct((nb, 1, RWKV_PAD), F32)]
    return pl.pallas_call(
        _in_rwkv_kernel,
        grid=(nb, T // tm),
        in_specs=[tok(D_MODEL), pl.BlockSpec((1, 1, RWKV_PAD), lambda b, j: (b, 0, 0)),
                  _full((1, D_MODEL)), _full((D_MODEL, RWKV_PAD)), _full((1, RWKV_PAD)),
                  _full((1, W)), _full((128, W)), _full((1, W)), _full((128, W)), _full((128, W)),
                  _full((1, W)), _full((1, W)), _full((1, W)), _full((MXU_TILE, MXU_TILE))],
        out_specs=[tok(W)] * 8 + [pl.BlockSpec((1, 1, RWKV_PAD), lambda b, j: (b, 0, 0))],
        out_shape=outs,
        scratch_shapes=[pltpu.VMEM((1, RWKV_PAD), F32)],
        compiler_params=_cparams("parallel", "arbitrary"),
        name="in_rwkv",
    )(x, shift, gmix, w, mu, w0, wup, a0, aup, gup, k_k, k_a, r_k, gs)


def _aug_rows(x, c, is_query):
    lane = lax.broadcasted_iota(jnp.int32, (1, LANES), 1)
    rows = []
    for p in range(N_PAIRS):
        xp = x[:, p * LANES:(p + 1) * LANES]
        xr = pltpu.roll(xp, HEAD_DIM, axis=1)
        for h in range(2):
            ch = c[:, 2 * p + h:2 * p + h + 1]
            hi = _bf(ch).astype(F32)
            r1 = ch - hi
            mid = _bf(r1).astype(F32)
            lo = _bf(r1 - mid).astype(F32)
            one = jnp.ones_like(ch)
            cols = (hi, mid, lo, one, one, one) if is_query else (one, one, one, -hi, -mid, -lo)
            aug = jnp.zeros_like(xp)
            for n, col in enumerate(cols):
                aug = jnp.where(lane == HEAD_DIM + n, col, aug)
            rows.append(_bf(jnp.where(lane < HEAD_DIM, xp if h == 0 else xr, aug)))
    return rows


def _in_fox_kernel(x_ref, gmix_ref, w_ref, qn_ref, kn_ref, fb_ref, gs_ref, tri_ref, c0_ref,
                   q_out, ka_out, k_out, v_out, vb_out, og_out, lf_out, carry_ref):
    j = pl.program_id(1)
    tm = x_ref.shape[1]

    @pl.when(j == 0)
    def _():
        carry_ref[...] = c0_ref[0]

    xn = _bf(_rms(x_ref[0], gmix_ref[...]))
    u = _dot(xn, w_ref[...])
    W = FOX_WIDTH
    q, k, v, og, fl = u[:, :W], u[:, W:2 * W], u[:, 2 * W:3 * W], u[:, 3 * W:4 * W], u[:, 4 * W:]
    gs = gs_ref[...]
    inv_d = 1.0 / HEAD_DIM
    qn = q * lax.rsqrt(_dot_x01(q * q, gs) * inv_d + RMS_EPS) * qn_ref[...]
    kn = k * lax.rsqrt(_dot_x01(k * k, gs) * inv_d + RMS_EPS) * kn_ref[...]
    logf = -_softplus(-(fl + fb_ref[...]))
    hi, mid, lo = _split3(logf)
    tri = tri_ref[...]
    c = _dot(tri, hi) + _dot(tri, mid) + _dot(tri, lo) + carry_ref[...]
    carry_ref[...] = c[tm - 1:tm, :]
    for h, row in enumerate(_aug_rows(_bf(qn * ATTN_SCALE).astype(F32), c, True)):
        q_out[0, h] = row
    for h, row in enumerate(_aug_rows(_bf(kn).astype(F32), c, False)):
        ka_out[0, h] = row
    k_out[0] = kn
    v_out[0] = v
    vb_out[0] = _bf(v)
    og_out[0] = og
    lf_out[0] = logf


def _in_fox(x, c0, gmix, w, qn, kn, fb, gs, tm):
    nb, T, _ = x.shape
    W = FOX_WIDTH
    tok = lambda width: pl.BlockSpec((1, tm, width), lambda b, j: (b, j, 0))
    aug = pl.BlockSpec((1, N_HEADS, tm, LANES), lambda b, j: (b, 0, j, 0))
    sds = lambda width, dt: jax.ShapeDtypeStruct((nb, T, width), dt)
    aug_shape = jax.ShapeDtypeStruct((nb, N_HEADS, T, LANES), BF16)
    tri = _bf(jnp.tril(jnp.ones((tm, tm), F32)))
    return pl.pallas_call(
        _in_fox_kernel,
        grid=(nb, T // tm),
        in_specs=[tok(D_MODEL), _full((1, D_MODEL)), _full((D_MODEL, FOX_PAD)),
                  _full((1, W)), _full((1, W)), _full((1, 128)), _full((MXU_TILE, MXU_TILE)), _full((tm, tm)),
                  pl.BlockSpec((1, 1, LANES), lambda b, j: (b, 0, 0))],
        out_specs=[aug, aug] + [tok(W)] * 4 + [tok(128)],
        out_shape=[aug_shape, aug_shape, sds(W, F32), sds(W, F32), sds(W, BF16), sds(W, F32), sds(128, F32)],
        scratch_shapes=[pltpu.VMEM((1, LANES), F32)],
        compiler_params=_cparams("parallel", "arbitrary"),
        name="in_fox",
    )(x, gmix, w, qn, kn, fb, gs, tri, c0)


def _cumsum_kernel(x_ref, tri_ref, o_ref, carry_ref):
    j = pl.program_id(1)
    ts = x_ref.shape[1]

    @pl.when(j == 0)
    def _():
        carry_ref[...] = jnp.zeros_like(carry_ref)

    hi, mid, lo = _split3(x_ref[0])
    tri = tri_ref[...]
    c = _dot(tri, hi) + _dot(tri, mid) + _dot(tri, lo) + carry_ref[...]
    o_ref[0] = c
    carry_ref[...] = c[ts - 1:ts, :]


def _cumsum(x, ts):
    nb, S, L = x.shape
    tri = _bf(jnp.tril(jnp.ones((ts, ts), F32)))
    return pl.pallas_call(
        _cumsum_kernel,
        grid=(nb, S // ts),
        in_specs=[pl.BlockSpec((1, ts, L), lambda b, j: (b, j, 0)), _full((ts, ts))],
        out_specs=pl.BlockSpec((1, ts, L), lambda b, j: (b, j, 0)),
        out_shape=jax.ShapeDtypeStruct((nb, S, L), F32),
        scratch_shapes=[pltpu.VMEM((1, L), F32)],
        compiler_params=_cparams("parallel", "arbitrary"),
        name="seq_cumsum",
    )(x, tri)


def _rwkv_chunk_kernel(r_ref, lw_ref, kh_ref, v_ref, kk_ref, bb_ref, tri_ref,
                       rh_out, yh_out, g_out, sh_out):
    C = r_ref.shape[1]
    ti = lax.broadcasted_iota(jnp.int32, (C, C), 0)
    si = lax.broadcasted_iota(jnp.int32, (C, C), 1)
    tx = ti ^ si
    strict = ti > si
    incl = ti >= si
    eye_c = (ti == si).astype(F32)
    lane = lax.broadcasted_iota(jnp.int32, (1, LANES), 1)
    head0 = lane < HEAD_DIM
    pi = lax.broadcasted_iota(jnp.int32, (LANES, LANES), 0)
    pj = lax.broadcasted_iota(jnp.int32, (LANES, LANES), 1)
    same_head = (pi < HEAD_DIM) == (pj < HEAD_DIM)
    eye_p = (pi == pj).astype(F32)
    tri = tri_ref[...]

    pairs = range(N_PAIRS)
    heads = [(p, h) for p in pairs for h in range(2)]
    P = []
    for p in pairs:
        sl = slice(p * LANES, (p + 1) * LANES)
        r, lw, kh, v, kk, bb = (ref[0, :, sl] for ref in (r_ref, lw_ref, kh_ref, v_ref, kk_ref, bb_ref))
        l_hi, l_mid, l_lo = _split3(lw)
        lc = _dot(tri, l_hi) + _dot(tri, l_mid) + _dot(tri, l_lo)
        mid = lc[C // 2 - 1:C // 2, :]
        last = lc[C - 1:C, :]
        e_dn = jnp.exp(mid - lc)
        e_end = jnp.exp(last - lc)
        aa = kk * jnp.exp(lc - lw - mid)
        P.append(dict(sl=sl, v=v, aa=aa, rt=r * jnp.exp(lc - mid), rho=jnp.exp(mid), g_last=jnp.exp(last),
                      bt_b=_bf(bb * e_dn), kt_b=_bf(kh * e_dn), bc_b=_bf(bb * e_end), kc_b=_bf(kh * e_end),
                      aa_b=_bf(aa), v_b=_bf(v)))
    lab, lak, mrb, mrk = {}, {}, {}, {}
    for p, h in heads:
        q = P[p]
        hm = head0 if h == 0 else jnp.logical_not(head0)
        aa_m = _bf(jnp.where(hm, q["aa"], 0.0))
        rt_m = _bf(jnp.where(hm, q["rt"], 0.0))
        lab[p, h] = jnp.where(strict, _dot_nt(aa_m, q["bt_b"]), 0.0)
        lak[p, h] = _bf(jnp.where(strict, _dot_nt(aa_m, q["kt_b"]), 0.0))
        mrb[p, h] = _bf(jnp.where(incl, _dot_nt(rt_m, q["bt_b"]), 0.0))
        mrk[p, h] = _bf(jnp.where(incl, _dot_nt(rt_m, q["kt_b"]), 0.0))
    d = {k: eye_c - jnp.where(tx < 2, lab[k], 0.0) for k in heads}
    s = 2
    while s < C:
        level = (tx >= s) & (tx < 2 * s)
        d_b = {k: _bf(d[k]) for k in heads}
        t1 = {k: _bf(_dot(d_b[k], _bf(jnp.where(level, lab[k], 0.0)))) for k in heads}
        d = {k: d[k] - _dot(t1[k], d_b[k]) for k in heads}
        s *= 2
    d_b = {k: _bf(d[k]) for k in heads}
    w = {k: _bf(_dot(lak[k], P[k[0]]["v_b"])) for k in heads}
    ah = {k: _dot(d_b[k], P[k[0]]["aa_b"]) * P[k[0]]["rho"] for k in heads}
    uh = {k: _dot(d_b[k], w[k]) for k in heads}
    rh = {k: P[k[0]]["rt"] * P[k[0]]["rho"] - _dot(mrb[k], _bf(ah[k])) for k in heads}
    yh = {k: _dot(mrk[k], P[k[0]]["v_b"]) - _dot(mrb[k], _bf(uh[k])) for k in heads}
    for p in pairs:
        q = P[p]
        both = lambda x: jnp.where(head0, x[p, 0], x[p, 1])
        ah_p, uh_p = both(ah), both(uh)
        rh_out[0, :, q["sl"]] = both(rh)
        yh_out[0, :, q["sl"]] = both(yh)
        g_full = eye_p * q["g_last"] - _dot(_bf(ah_p.T), q["bc_b"])
        sh_full = _dot(_bf(q["v"].T), q["kc_b"]) - _dot(_bf(uh_p.T), q["bc_b"])
        g_out[0, 0, p] = jnp.where(same_head, g_full, 0.0)
        sh_out[0, 0, p] = jnp.where(same_head, sh_full, 0.0)


def _rwkv_chunks(r, lw, kh, v, kk, bb):
    nb, T, W = r.shape
    C = RWKV_CHUNK
    nc = T // C
    tri = _bf(jnp.tril(jnp.ones((C, C), F32)))
    tok = pl.BlockSpec((1, C, W), lambda b, c: (b, c, 0))
    mat = pl.BlockSpec((1, 1, N_PAIRS, LANES, LANES), lambda b, c: (b, c, 0, 0, 0))
    mat_shape = jax.ShapeDtypeStruct((nb, nc, N_PAIRS, LANES, LANES), F32)
    return pl.pallas_call(
        _rwkv_chunk_kernel,
        grid=(nb, nc),
        in_specs=[tok] * 6 + [_full((C, C))],
        out_specs=[tok, tok, mat, mat],
        out_shape=[jax.ShapeDtypeStruct((nb, T, W), F32)] * 2 + [mat_shape] * 2,
        compiler_params=_cparams("parallel", "parallel"),
        name="rwkv_chunks",
    )(r, lw, kh, v, kk, bb, tri)


def _rwkv_scan_kernel(rh_ref, yh_ref, g_ref, sh_ref, s0_ref, y_out, sfin_out, s_scr):
    c = pl.program_id(1)
    nbg = rh_ref.shape[0]

    @pl.when(c == 0)
    def _():
        s_scr[...] = s0_ref[...]

    for b in range(nbg):
        for p in range(N_PAIRS):
            sl = slice(p * LANES, (p + 1) * LANES)
            s = s_scr[b, p]
            y_out[b, :, sl] = _dot3_nt(rh_ref[b, :, sl], s) + yh_ref[b, :, sl]
            s_new = _dot3(s, g_ref[b, 0, p]) + sh_ref[b, 0, p]
            s_scr[b, p] = s_new
            sfin_out[b, p] = s_new


def _rwkv_scan(rh, yh, g, sh, s0, nbg):
    nb, T, W = rh.shape
    C = RWKV_CHUNK
    nc = T // C
    tok = pl.BlockSpec((nbg, C, W), lambda i, c: (i, c, 0))
    mat = pl.BlockSpec((nbg, 1, N_PAIRS, LANES, LANES), lambda i, c: (i, c, 0, 0, 0))
    st = pl.BlockSpec((nbg, N_PAIRS, LANES, LANES), lambda i, c: (i, 0, 0, 0))
    return pl.pallas_call(
        _rwkv_scan_kernel,
        grid=(nb // nbg, nc),
        in_specs=[tok, tok, mat, mat, st],
        out_specs=[tok, st],
        out_shape=[jax.ShapeDtypeStruct((nb, T, W), F32),
                   jax.ShapeDtypeStruct((nb, N_PAIRS, LANES, LANES), F32)],
        scratch_shapes=[pltpu.VMEM((nbg, N_PAIRS, LANES, LANES), F32)],
        compiler_params=_cparams("parallel", "arbitrary"),
        name="rwkv_scan",
    )(rh, yh, g, sh, s0)


def _state_to_pairs(s):
    nb = s.shape[0]
    s = s.reshape(nb, N_PAIRS, 2, HEAD_DIM, HEAD_DIM)
    z = jnp.zeros_like(s[:, :, 0])
    top = jnp.concatenate([s[:, :, 0], z], axis=-1)
    bot = jnp.concatenate([z, s[:, :, 1]], axis=-1)
    return jnp.concatenate([top, bot], axis=-2)


def _pairs_to_state(sp):
    nb = sp.shape[0]
    a = sp[:, :, :HEAD_DIM, :HEAD_DIM]
    b = sp[:, :, HEAD_DIM:, HEAD_DIM:]
    return jnp.stack([a, b], axis=2).reshape(nb, N_HEADS, HEAD_DIM, HEAD_DIM)


def _fox_aug_kernel(x_ref, c_ref, o_ref):
    for h, row in enumerate(_aug_rows(x_ref[0].astype(F32), c_ref[0], False)):
        o_ref[0, h] = row


def _fox_aug(x, c, tm):
    nb, S, W = x.shape
    return pl.pallas_call(
        _fox_aug_kernel,
        grid=(nb, S // tm),
        in_specs=[pl.BlockSpec((1, tm, W), lambda b, j: (b, j, 0)),
                  pl.BlockSpec((1, tm, LANES), lambda b, j: (b, j, 0))],
        out_specs=pl.BlockSpec((1, N_HEADS, tm, LANES), lambda b, j: (b, 0, j, 0)),
        out_shape=jax.ShapeDtypeStruct((nb, N_HEADS, S, LANES), BF16),
        compiler_params=_cparams("parallel", "parallel"),
        name="fox_aug",
    )(x, c)


def _fox_kernel(q_ref, k_ref, v_ref, o_ref, m_scr, l_scr, acc_scr, sa_scr, sb_scr, *, tk, q_off):
    i = pl.program_id(2)
    tq = q_ref.shape[2]
    q_start = q_off + i * tq
    m_scr[...] = jnp.full_like(m_scr, -jnp.inf)
    l_scr[...] = jnp.zeros_like(l_scr)
    acc_scr[...] = jnp.zeros_like(acc_scr)
    n_full = (q_start + 1) // tk

    def scores_into(j, dst):
        ks = pl.multiple_of(j * tk, tk)
        for h in range(2):
            dst[h] = _dot_nt(q_ref[0, h], k_ref[0, h, pl.ds(ks, tk), :])

    def update_from(j, src, masked):
        ks = pl.multiple_of(j * tk, tk)
        v = v_ref[0, pl.ds(ks, tk), :]
        if masked:
            visible = (ks + lax.broadcasted_iota(jnp.int32, (1, tk), 1)) <= (
                q_start + lax.broadcasted_iota(jnp.int32, (tq, 1), 0))
        for h in range(2):
            s = src[h]
            if masked:
                s = jnp.where(visible, s, -jnp.inf)
            m_old = m_scr[h]
            m_new = jnp.maximum(m_old, jnp.max(s, axis=-1, keepdims=True))
            alpha = jnp.exp(m_old - m_new)
            pr = jnp.exp(s - m_new)
            l_scr[h] = alpha * l_scr[h] + jnp.sum(pr, axis=-1, keepdims=True)
            acc_scr[h] = alpha * acc_scr[h] + _dot(_bf(pr), v)
            m_scr[h] = m_new

    def step(j, src, dst):
        scores_into(j + 1, dst)
        update_from(j, src, False)

    def two_steps(jj, carry):
        step(2 * jj, sa_scr, sb_scr)
        step(2 * jj + 1, sb_scr, sa_scr)
        return carry

    scores_into(0, sa_scr)
    lax.fori_loop(0, n_full // 2, two_steps, 0)
    odd = n_full % 2 == 1

    @pl.when(odd)
    def _():
        step(n_full - 1, sa_scr, sb_scr)
        update_from(n_full, sb_scr, True)

    @pl.when(jnp.logical_not(odd))
    def _():
        update_from(n_full, sa_scr, True)

    head0 = lax.broadcasted_iota(jnp.int32, (1, LANES), 1) < HEAD_DIM
    o_ref[0] = jnp.where(head0, acc_scr[0] / l_scr[0], acc_scr[1] / l_scr[1])


def _fox_attention(q, k, v, tq, tk, q_off):
    nb, _, sq, _ = q.shape
    sk = k.shape[2]
    for q_start in range(q_off, q_off + sq, tq):
        assert (q_start + 1) // tk + 1 == -(-(q_start + tq) // tk) <= sk // tk, (q_start, tq, tk)
    return pl.pallas_call(
        functools.partial(_fox_kernel, tk=tk, q_off=q_off),
        grid=(nb, N_PAIRS, sq // tq),
        in_specs=[pl.BlockSpec((1, 2, tq, LANES), lambda b, p, i: (b, p, i, 0)),
                  pl.BlockSpec((1, 2, sk, LANES), lambda b, p, i: (b, p, 0, 0)),
                  pl.BlockSpec((1, sk, LANES), lambda b, p, i: (b, 0, p))],
        out_specs=pl.BlockSpec((1, tq, LANES), lambda b, p, i: (b, i, p)),
        out_shape=jax.ShapeDtypeStruct((nb, sq, FOX_WIDTH), F32),
        scratch_shapes=[pltpu.VMEM((2, tq, 1), F32), pltpu.VMEM((2, tq, 1), F32),
                        pltpu.VMEM((2, tq, LANES), F32), pltpu.VMEM((2, tq, tk), F32),
                        pltpu.VMEM((2, tq, tk), F32)],
        compiler_params=_cparams("parallel", "parallel", "arbitrary"),
        name="fox_attention",
    )(q, k, v)


def _out_kernel(*refs, n_first):
    tok_refs, rest = refs[:12], refs[12:]
    lnw_ref, lnb_ref, gs_ref, wout_ref, gffn_ref, rwh_ref, rwl_ref, h_out, xn_out, sc_out = rest
    first = pl.program_id(0) < n_first
    x, y, bonus, g, oa, og = (jnp.where(first, tok_refs[2 * n][...], tok_refs[2 * n + 1][...]) for n in range(6))
    gs = gs_ref[...]
    inv_d = 1.0 / HEAD_DIM
    mean = _dot_x01(y, gs) * inv_d
    d = y - mean
    var = _dot_x01(d * d, gs) * inv_d
    yn = d * lax.rsqrt(var + GN_EPS) * lnw_ref[...] + lnb_ref[...]
    o_rwkv = (yn + bonus) * g
    o_fox = oa * jax.nn.sigmoid(og)
    mix = jnp.concatenate([_bf(o_rwkv), _bf(o_fox)], axis=-1)
    h = x + _dot(mix, wout_ref[...])
    h_out[...] = h
    xn = _rms(h, gffn_ref[...])
    _store_packed(xn_out, xn)
    xh, xl = _split2(xn)
    logits = _dot(xh, rwh_ref[...]) + _dot(xl, rwh_ref[...]) + _dot(xh, rwl_ref[...])
    sc_out[...] = jax.nn.sigmoid(logits)


def _two_streams(width, tm, n_first):
    return [pl.BlockSpec((tm, width), lambda i: (jnp.minimum(i, n_first - 1), 0)),
            pl.BlockSpec((tm, width), lambda i: (jnp.maximum(i - n_first, 0), 0))]


def _out_proj(streams, lnw, lnb, gs, wout, gffn, rwh, rwl, tm):
    n_first = streams[0][0].shape[0] // tm
    T = streams[0][0].shape[0] + streams[0][1].shape[0]
    W = RWKV_WIDTH
    tok = lambda width: pl.BlockSpec((tm, width), lambda i: (i, 0))
    tok_specs, tok_args = [], []
    for a, b in streams:
        tok_specs += _two_streams(a.shape[1], tm, n_first)
        tok_args += [a, b]
    return pl.pallas_call(
        functools.partial(_out_kernel, n_first=n_first),
        grid=(T // tm,),
        in_specs=tok_specs + [_full((1, W)), _full((1, W)), _full((MXU_TILE, MXU_TILE)),
                              _full((D_MODEL, D_MODEL)), _full((1, D_MODEL)),
                              _full((D_MODEL, N_EXPERTS)), _full((D_MODEL, N_EXPERTS))],
        out_specs=[tok(D_MODEL), pl.BlockSpec((tm * PACK_CHUNKS, LANES), lambda i: (i, 0)), tok(N_EXPERTS)],
        out_shape=[jax.ShapeDtypeStruct((T, D_MODEL), F32),
                   jax.ShapeDtypeStruct((T * PACK_CHUNKS, LANES), jnp.uint32),
                   jax.ShapeDtypeStruct((T, N_EXPERTS), F32)],
        compiler_params=_cparams("parallel"),
        name="out_proj",
    )(*tok_args, lnw, lnb, gs, wout, gffn, rwh, rwl)


def _route_kernel(sc_ref, bias_ref, before_ref, idx_out, wt_out, rank_out, cnt_out, cnt_scr):
    tm = sc_ref.shape[0]
    neg = -jnp.inf

    @pl.when(pl.program_id(0) == 0)
    def _():
        cnt_scr[...] = jnp.zeros_like(cnt_scr)

    st = sc_ref[...].T
    sel = st + bias_ref[...]
    gscore = []
    for gi in range(N_GROUPS):
        blk = sel[gi * GROUP_SIZE:(gi + 1) * GROUP_SIZE, :]
        m1 = jnp.max(blk, axis=0, keepdims=True)
        n1 = jnp.sum((blk == m1).astype(F32), axis=0, keepdims=True)
        m2 = jnp.max(jnp.where(blk < m1, blk, neg), axis=0, keepdims=True)
        gscore.append(m1 + jnp.where(n1 > 1.0, m1, m2))
    taken = [jnp.zeros((1, tm), jnp.bool_) for _ in range(N_GROUPS)]
    for _ in range(TOPK_GROUPS):
        avail = [jnp.where(taken[gi], neg, gscore[gi]) for gi in range(N_GROUPS)]
        best = functools.reduce(jnp.maximum, avail)
        found = jnp.zeros((1, tm), jnp.bool_)
        for gi in range(N_GROUPS):
            hit = (avail[gi] == best) & jnp.logical_not(found)
            taken[gi] = taken[gi] | hit
            found = found | hit
    cand = jnp.concatenate(
        [jnp.where(taken[gi], sel[gi * GROUP_SIZE:(gi + 1) * GROUP_SIZE, :], neg) for gi in range(N_GROUPS)], axis=0)
    eid = lax.broadcasted_iota(jnp.int32, (N_EXPERTS, tm), 0).astype(F32)
    idxs, wts = [], []
    onehot = jnp.zeros((N_EXPERTS, tm), F32)
    for _ in range(TOP_K):
        best = jnp.max(cand, axis=0, keepdims=True)
        pick = jnp.min(jnp.where(cand == best, eid, float(N_EXPERTS)), axis=0, keepdims=True)
        chosen = eid == pick
        wts.append(jnp.sum(jnp.where(chosen, st, 0.0), axis=0, keepdims=True))
        idxs.append(pick)
        cand = jnp.where(chosen, neg, cand)
        onehot = jnp.where(chosen, 1.0, onehot)
    w = jnp.concatenate(wts, axis=0)
    idx_out[...] = jnp.concatenate(idxs, axis=0).astype(jnp.int32)
    wt_out[...] = w / jnp.sum(w, axis=0, keepdims=True) * ROUTED_SCALE
    earlier = _dot(_bf(onehot), before_ref[...]) + cnt_scr[...]
    rank_out[...] = jnp.concatenate(
        [jnp.sum(jnp.where(eid == pick, earlier, 0.0), axis=0, keepdims=True) for pick in idxs],
        axis=0).astype(jnp.int32)
    cnt_scr[...] += jnp.sum(onehot, axis=1, keepdims=True)
    cnt_out[...] = cnt_scr[...]


def _route(scores, bias_col, tm):
    T = scores.shape[0]
    before = _bf(jnp.triu(jnp.ones((tm, tm), F32), 1))
    tok = pl.BlockSpec((TOP_K, tm), lambda i: (0, i))
    return pl.pallas_call(
        _route_kernel,
        grid=(T // tm,),
        in_specs=[pl.BlockSpec((tm, N_EXPERTS), lambda i: (i, 0)), _full((N_EXPERTS, 1)), _full((tm, tm))],
        out_specs=[tok, tok, tok, _full((N_EXPERTS, 1))],
        out_shape=[jax.ShapeDtypeStruct((TOP_K, T), jnp.int32), jax.ShapeDtypeStruct((TOP_K, T), F32),
                   jax.ShapeDtypeStruct((TOP_K, T), jnp.int32), jax.ShapeDtypeStruct((N_EXPERTS, 1), F32)],
        scratch_shapes=[pltpu.VMEM((N_EXPERTS, 1), F32)],
        compiler_params=_cparams("arbitrary"),
        name="route",
    )(scores, bias_col, before)


def _pos_kernel(idx_ref, rank_ref, start_ref, pos_out):
    tm = idx_ref.shape[1]
    eid = lax.broadcasted_iota(jnp.int32, (N_EXPERTS, tm), 0)
    idx = idx_ref[...]
    start = start_ref[...]
    base = jnp.concatenate(
        [jnp.sum(jnp.where(eid == idx[k:k + 1, :], start, 0.0), axis=0, keepdims=True) for k in range(TOP_K)],
        axis=0)
    pos_out[...] = rank_ref[...] + base.astype(jnp.int32)


def _positions(eidx_t, rank_t, start_col, tm):
    T = eidx_t.shape[1]
    tok = pl.BlockSpec((TOP_K, tm), lambda i: (0, i))
    return pl.pallas_call(
        _pos_kernel,
        grid=(T // tm,),
        in_specs=[tok, tok, _full((N_EXPERTS, 1))],
        out_specs=tok,
        out_shape=jax.ShapeDtypeStruct((TOP_K, T), jnp.int32),
        compiler_params=_cparams("parallel"),
        name="moe_positions",
    )(eidx_t, rank_t, start_col)


def _dispatch_kernel(pos_ref, x_ref, h_ref, sg_ref, su_ref, sd_ref, xs_in, xs_out, hs_out, sem):
    del xs_in
    tm = pos_ref.shape[1]

    def issue(t, carry):
        src = x_ref.at[pl.ds(pl.multiple_of(t * PACK_CHUNKS, PACK_CHUNKS), PACK_CHUNKS), :]
        for k in range(TOP_K):
            row = pl.multiple_of(pos_ref[k, t] * PACK_CHUNKS, PACK_CHUNKS)
            pltpu.make_async_copy(src, xs_out.at[pl.ds(row, PACK_CHUNKS), :], sem).start()
        return carry

    lax.fori_loop(0, tm, issue, 0)
    xb = _load_packed(x_ref, tm)
    hg = _dot(xb, sg_ref[...])
    hu = _dot(xb, su_ref[...])
    hs_out[...] = h_ref[...] + _dot(_bf(hg * jax.nn.sigmoid(hg) * hu), sd_ref[...])
    for k in range(TOP_K):
        pltpu.make_async_copy(x_ref, xs_out.at[pl.ds(0, tm * PACK_CHUNKS), :], sem).wait()


def _dispatch(pos_t, xn_packed, h, sg, su, sd, n_rows, tm):
    T = pos_t.shape[1]
    xs0 = jnp.zeros((n_rows * PACK_CHUNKS, LANES), jnp.uint32)
    return pl.pallas_call(
        _dispatch_kernel,
        grid=(T // tm,),
        in_specs=[pl.BlockSpec((TOP_K, tm), lambda i: (0, i), memory_space=pltpu.SMEM),
                  pl.BlockSpec((tm * PACK_CHUNKS, LANES), lambda i: (i, 0)),
                  pl.BlockSpec((tm, D_MODEL), lambda i: (i, 0)),
                  _full((D_MODEL, EXPERT_FF)), _full((D_MODEL, EXPERT_FF)), _full((EXPERT_FF, D_MODEL)),
                  pl.BlockSpec(memory_space=pl.ANY)],
        out_specs=[pl.BlockSpec(memory_space=pl.ANY), pl.BlockSpec((tm, D_MODEL), lambda i: (i, 0))],
        out_shape=[jax.ShapeDtypeStruct((n_rows * PACK_CHUNKS, LANES), jnp.uint32),
                   jax.ShapeDtypeStruct((T, D_MODEL), F32)],
        scratch_shapes=[pltpu.SemaphoreType.DMA(())],
        input_output_aliases={6: 0},
        compiler_params=_cparams("arbitrary"),
        name="moe_dispatch",
    )(pos_t, xn_packed, h, sg, su, sd, xs0)


def _gmm_kernel(first_ref, count_ref, used_ref, xs_hbm, wg_ref, wu_ref, wd_ref, y_hbm,
                xbuf, ybuf, wg_b, wu_b, wd_b, in_sems, out_sems):
    e = pl.program_id(0)
    bm = MOE_ROWS
    blk_rows = bm * ROW_CHUNKS
    n_used = used_ref[0]
    n_blocks = y_hbm.shape[0] // blk_rows

    in_rows = bm * PACK_CHUNKS

    def read(g, slot):
        return pltpu.make_async_copy(xs_hbm.at[pl.ds(pl.multiple_of(g * in_rows, in_rows), in_rows), :],
                                     xbuf.at[slot], in_sems.at[slot])

    def write(g, slot):
        return pltpu.make_async_copy(ybuf.at[slot],
                                     y_hbm.at[pl.ds(pl.multiple_of(g * blk_rows, blk_rows), blk_rows), :],
                                     out_sems.at[slot])

    @pl.when(e == 0)
    def _():
        for g0 in range(GMM_AHEAD):
            @pl.when(g0 < n_used)
            def _():
                read(g0, g0).start()

    wg_b[...] = _bf(wg_ref[0])
    wu_b[...] = _bf(wu_ref[0])
    wd_b[...] = _bf(wd_ref[0])

    def block(g, carry):
        slot = g % (GMM_AHEAD + 1)
        oslot = g % 2
        read(g, slot).wait()

        @pl.when(g + GMM_AHEAD < n_used)
        def _():
            read(g + GMM_AHEAD, (g + GMM_AHEAD) % (GMM_AHEAD + 1)).start()

        @pl.when(g >= 2)
        def _():
            write(g - 2, oslot).wait()

        xe = _load_packed(xbuf.at[slot], bm)
        hg = _dot(xe, wg_b[...])
        hu = _dot(xe, wu_b[...])
        _store_chunked(ybuf.at[oslot], _dot(_bf(hg * jax.nn.sigmoid(hg) * hu), wd_b[...]))
        write(g, oslot).start()
        return carry

    lax.fori_loop(first_ref[e], first_ref[e] + count_ref[e], block, 0)

    @pl.when(e == pl.num_programs(0) - 1)
    def _():
        for back in (2, 1):
            @pl.when(n_used >= back)
            def _():
                write(n_used - back, (n_used - back) % 2).wait()
        ybuf[0] = jnp.zeros_like(ybuf[0])

        def fill(g, carry):
            write(g, 0).start()
            return carry

        def drain(g, carry):
            write(g, 0).wait()
            return carry

        lax.fori_loop(n_used, n_blocks, fill, 0)
        lax.fori_loop(n_used, n_blocks, drain, 0)


def _gmm(first_blk, blk_count, n_used, xs, wg, wu, wd):
    bm = MOE_ROWS
    n_rows = xs.shape[0] // PACK_CHUNKS
    wspec = lambda shape: pl.BlockSpec((1,) + shape, lambda e, *_: (e, 0, 0))
    grid_spec = pltpu.PrefetchScalarGridSpec(
        num_scalar_prefetch=3,
        grid=(N_EXPERTS,),
        in_specs=[pl.BlockSpec(memory_space=pl.ANY), wspec((D_MODEL, EXPERT_FF)), wspec((D_MODEL, EXPERT_FF)),
                  wspec((EXPERT_FF, D_MODEL))],
        out_specs=pl.BlockSpec(memory_space=pl.ANY),
        scratch_shapes=[pltpu.VMEM((GMM_AHEAD + 1, bm * PACK_CHUNKS, LANES), jnp.uint32),
                        pltpu.VMEM((2, bm * ROW_CHUNKS, LANES), F32),
                        pltpu.VMEM((D_MODEL, EXPERT_FF), BF16), pltpu.VMEM((D_MODEL, EXPERT_FF), BF16),
                        pltpu.VMEM((EXPERT_FF, D_MODEL), BF16),
                        pltpu.SemaphoreType.DMA((GMM_AHEAD + 1,)), pltpu.SemaphoreType.DMA((2,))],
    )
    return pl.pallas_call(
        _gmm_kernel,
        grid_spec=grid_spec,
        out_shape=jax.ShapeDtypeStruct((n_rows * ROW_CHUNKS, LANES), F32),
        compiler_params=_cparams("arbitrary"),
        name="expert_gmm",
    )(first_blk, blk_count, n_used, xs, wg, wu, wd)


def _final_kernel(pos_ref, nxt_ref, hs_ref, w_ref, pa_ref, pb_ref, ys_ref, gple_ref, wpg_ref, wpp_ref, gfin_ref,
                  ya_out, yb_out, buf, sems, *, n_first):
    i = pl.program_id(0)
    n = pl.num_programs(0)
    tm = hs_ref.shape[0]

    def gather(rows_ref, slot):
        def issue(t, carry):
            dst = pl.ds(pl.multiple_of(t * ROW_CHUNKS, ROW_CHUNKS), ROW_CHUNKS)
            for k in range(TOP_K):
                row = pl.multiple_of(rows_ref[k, t] * ROW_CHUNKS, ROW_CHUNKS)
                pltpu.make_async_copy(ys_ref.at[pl.ds(row, ROW_CHUNKS), :], buf.at[slot, k, dst, :],
                                      sems.at[slot, k]).start()
            return carry

        lax.fori_loop(0, tm, issue, 0)

    slot = i % 2

    @pl.when(i == 0)
    def _():
        gather(pos_ref, 0)

    @pl.when(i + 1 < n)
    def _():
        gather(nxt_ref, 1 - slot)

    pp = _dot(_bf(jnp.where(i < n_first, pa_ref[...], pb_ref[...])), wpp_ref[...])
    w = w_ref[...]
    for k in range(TOP_K):
        pltpu.make_async_copy(ys_ref.at[pl.ds(0, tm * ROW_CHUNKS), :], buf.at[slot, k], sems.at[slot, k]).wait()
    wk = [jnp.broadcast_to(w[:, k:k + 1], (tm, LANES)) for k in range(TOP_K)]
    chunks = []
    for s in range(ROW_CHUNKS):
        acc = buf[slot, 0, pl.ds(s, tm, stride=ROW_CHUNKS), :] * wk[0]
        for k in range(1, TOP_K):
            acc = acc + buf[slot, k, pl.ds(s, tm, stride=ROW_CHUNKS), :] * wk[k]
        chunks.append(acc)
    h = hs_ref[...] + jnp.concatenate(chunks, axis=1)
    gate = jax.nn.sigmoid(_dot(_bf(_rms(h, gple_ref[...])), wpg_ref[...]))
    y = _rms(h + gate * pp, gfin_ref[...])

    @pl.when(i < n_first)
    def _():
        ya_out[...] = y

    @pl.when(i >= n_first)
    def _():
        yb_out[...] = y


def _final(pos_t, hs, w, p_pair, ys, gple, wpg, wpp, gfin, tm):
    T = hs.shape[0]
    n = T // tm
    n_first = p_pair[0].shape[0] // tm
    tok = lambda width: pl.BlockSpec((tm, width), lambda i: (i, 0))
    return pl.pallas_call(
        functools.partial(_final_kernel, n_first=n_first),
        grid=(n,),
        in_specs=[pl.BlockSpec((TOP_K, tm), lambda i: (0, i), memory_space=pltpu.SMEM),
                  pl.BlockSpec((TOP_K, tm), lambda i: (0, jnp.minimum(i + 1, n - 1)), memory_space=pltpu.SMEM),
                  tok(D_MODEL), tok(TOP_K)] + _two_streams(PLE_DIM, tm, n_first) + [
                  pl.BlockSpec(memory_space=pl.ANY),
                  _full((1, D_MODEL)), _full((D_MODEL, D_MODEL)), _full((PLE_DIM, D_MODEL)),
                  _full((1, D_MODEL))],
        out_specs=_two_streams(D_MODEL, tm, n_first),
        out_shape=[jax.ShapeDtypeStruct((n_first * tm, D_MODEL), F32),
                   jax.ShapeDtypeStruct((T - n_first * tm, D_MODEL), F32)],
        scratch_shapes=[pltpu.VMEM((2, TOP_K, tm * ROW_CHUNKS, LANES), F32),
                        pltpu.SemaphoreType.DMA((2, TOP_K))],
        compiler_params=_cparams("arbitrary"),
        name="ffn_tail",
    )(pos_t, pos_t, hs, w, *p_pair, ys, gple, wpg, wpp, gfin)


def _pad_cols(a, width):
    return jnp.pad(a, [(0, 0)] * (a.ndim - 1) + [(0, width - a.shape[-1])])


def _rwkv_pad_cols(a):
    W = RWKV_WIDTH
    o1, o2, o3 = 3 * W, 3 * W + DECAY_RANK, 3 * W + DECAY_RANK + ICL_RANK
    return jnp.concatenate([a[..., :o1], _pad_cols(a[..., o1:o2], 128), _pad_cols(a[..., o2:o3], 128),
                            a[..., o3:]], axis=-1)


def _rwkv_unpad_cols(a):
    W = RWKV_WIDTH
    return jnp.concatenate([a[..., :3 * W + DECAY_RANK], a[..., 3 * W + 128:3 * W + 128 + ICL_RANK],
                            a[..., 3 * W + 256:]], axis=-1)


def _pad_rows(a, rows):
    return jnp.pad(a, [(0, rows - a.shape[0])] + [(0, 0)] * (a.ndim - 1))


def _mixer(x, shift, s0, past, wts, tm):
    nb, T, _ = x.shape
    (r, lw, kh, v, kkn, bb, g, bonus, last) = _in_rwkv(
        x, _rwkv_pad_cols(shift), wts["gmix"], wts["w_rwkv"], wts["mu"], wts["w0"], wts["wup"], wts["a0"],
        wts["aup"], wts["gup"], wts["k_k"], wts["k_a"], wts["r_k"], wts["gs"], tm)
    if past is None:
        c0 = jnp.zeros((nb, 1, LANES), F32)
    else:
        k_past, v_past, lf_past = past
        P = k_past.shape[1]
        c_past = _cumsum(_pad_cols(lf_past.astype(F32), LANES), math.gcd(P, 512))
        c0 = c_past[:, P - 1:, :]
    q_aug, k_aug, k_f, v_f, v_b, og, logf = _in_fox(
        x, c0, wts["gmix"], wts["w_fox"], wts["qn"], wts["kn"], wts["fb"], wts["gs"], tm)

    C = RWKV_CHUNK
    Tp = -(-T // C) * C
    if Tp != T:
        padt = lambda a: jnp.pad(a, ((0, 0), (0, Tp - T), (0, 0)))
        r_p, lw_p, kh_p, v_p, kk_p, bb_p = (padt(a) for a in (r, lw, kh, v, kkn, bb))
    else:
        r_p, lw_p, kh_p, v_p, kk_p, bb_p = r, lw, kh, v, kkn, bb
    rh, yh, gm, sh = _rwkv_chunks(r_p, lw_p, kh_p, v_p, kk_p, bb_p)
    y, s_fin = _rwkv_scan(rh, yh, gm, sh, _state_to_pairs(s0.astype(F32)), 4 if nb % 4 == 0 else 1)
    y = y[:, :T]

    if past is None:
        o_att = _fox_attention(q_aug, k_aug, v_b, min(T, FOX_TQ), min(T, FOX_TK), 0)
    else:
        tk = FOX_TK_CACHED
        sk = -(-(P + T) // tk) * tk
        k_aug_past = _fox_aug(_bf(k_past.reshape(nb, P, FOX_WIDTH)), c_past, math.gcd(P, 512))
        k_all = jnp.pad(jnp.concatenate([k_aug_past, k_aug], axis=2), ((0, 0), (0, 0), (0, sk - P - T), (0, 0)))
        v_all = jnp.pad(jnp.concatenate([_bf(v_past.reshape(nb, P, FOX_WIDTH)), v_b], axis=1),
                        ((0, 0), (0, sk - P - T), (0, 0)))
        o_att = _fox_attention(q_aug, k_all, v_all, T, tk, P)

    n = nb * T
    flat = lambda a: a.reshape(n, a.shape[-1])
    feats = (flat(y), flat(bonus), flat(g), flat(o_att), flat(og))
    state = (k_f.reshape(nb, T, N_HEADS, HEAD_DIM), v_f.reshape(nb, T, N_HEADS, HEAD_DIM),
             logf[:, :, :N_HEADS], _pairs_to_state(s_fin), _rwkv_unpad_cols(last))
    return feats, state


def _block_tables(counts):
    blk = MOE_ROWS
    counts = counts.reshape(N_EXPERTS).astype(jnp.int32)
    blk_count = (counts + blk - 1) // blk
    blk_end = jnp.cumsum(blk_count)
    first_blk = blk_end - blk_count
    return ((first_blk * blk).astype(F32).reshape(N_EXPERTS, 1), first_blk.astype(jnp.int32),
            blk_count.astype(jnp.int32), blk_end[-1:].astype(jnp.int32))


def kernel(x_prompt, x_sample, cache_fox_k, cache_fox_v, cache_fox_logf, state_rwkv_wkv, state_rwkv_shift, p_prompt, p_sample, norm_mix_g, w_in, rwkv_mu, rwkv_w0, rwkv_w_up, rwkv_a0, rwkv_a_up, rwkv_g_up, rwkv_k_k, rwkv_k_a, rwkv_r_k, rwkv_ln_w, rwkv_ln_b, fox_q_norm, fox_k_norm, fox_f_bias, w_out, norm_ffn_g, router_w, router_bias, exp_w_gate, exp_w_up, exp_w_down, shared_w_gate, shared_w_up, shared_w_down, ple_norm_g, ple_w_gate, ple_w_proj, final_norm_g):
    assert w_in.shape[0] == 1, "single-layer kernel"
    W = RWKV_WIDTH
    row = lambda a: a.reshape(1, -1).astype(F32)
    tile_heads = lambda a: jnp.tile(a.reshape(1, HEAD_DIM), (1, N_HEADS)).astype(F32)
    hid = jnp.arange(MXU_TILE) // HEAD_DIM
    w_in0 = w_in[0]
    router_hi = _bf(router_w[0])
    wts = {
        "gmix": row(norm_mix_g[0]),
        "w_rwkv": _bf(_rwkv_pad_cols(w_in0[:, :RWKV_IN])),
        "w_fox": _bf(_pad_cols(w_in0[:, RWKV_IN:], FOX_PAD)),
        "mu": row(_rwkv_pad_cols(rwkv_mu[0])),
        "w0": row(rwkv_w0[0]),
        "wup": _bf(_pad_rows(rwkv_w_up[0], 128)),
        "a0": row(rwkv_a0[0]),
        "aup": _bf(_pad_rows(rwkv_a_up[0], 128)),
        "gup": _bf(rwkv_g_up[0]),
        "k_k": row(rwkv_k_k[0]),
        "k_a": row(rwkv_k_a[0]),
        "r_k": row(rwkv_r_k[0]),
        "gs": _bf((hid[:, None] == hid[None, :]).astype(F32)),
        "qn": tile_heads(fox_q_norm[0]),
        "kn": tile_heads(fox_k_norm[0]),
        "fb": _pad_cols(row(fox_f_bias[0]), 128),
    }
    nbp, Tp, _ = x_prompt.shape
    nbs, Ts, _ = x_sample.shape
    s0_prompt = jnp.zeros((nbp, N_HEADS, HEAD_DIM, HEAD_DIM), F32)
    shift0_prompt = jnp.zeros((nbp, 1, RWKV_IN), F32)
    feats_p, st_p = _mixer(x_prompt, shift0_prompt, s0_prompt, None, wts, min(Tp, 256))
    feats_s, st_s = _mixer(x_sample, state_rwkv_shift[0], state_rwkv_wkv[0],
                           (cache_fox_k[0], cache_fox_v[0], cache_fox_logf[0]), wts, Ts)

    n_p, n_s = nbp * Tp, nbs * Ts
    n_tok = n_p + n_s
    tm = math.gcd(math.gcd(n_p, n_s), TOKEN_TILE)
    streams = [(x_prompt.reshape(n_p, D_MODEL), x_sample.reshape(n_s, D_MODEL))] + list(zip(feats_p, feats_s))
    h1, xn2, scores = _out_proj(streams, row(rwkv_ln_w[0]), row(rwkv_ln_b[0]), wts["gs"], _bf(w_out[0]),
                                row(norm_ffn_g[0]), router_hi,
                                _bf(router_w[0] - router_hi.astype(F32)), tm)
    eidx_t, wts_t, rank_t, counts = _route(scores, router_bias[0].reshape(N_EXPERTS, 1).astype(F32), tm)
    n_blocks = -(-n_tok * TOP_K // MOE_ROWS) + N_EXPERTS
    start_col, first_blk, blk_count, n_used = _block_tables(counts)
    pos_t = _positions(eidx_t, rank_t, start_col, tm)
    xs, h1s = _dispatch(pos_t, xn2, h1, _bf(shared_w_gate[0]), _bf(shared_w_up[0]), _bf(shared_w_down[0]),
                        n_blocks * MOE_ROWS, tm)
    y_rows = _gmm(first_blk, blk_count, n_used, xs, exp_w_gate[0], exp_w_up[0], exp_w_down[0])
    p_pair = (p_prompt[0].reshape(n_p, PLE_DIM), p_sample[0].reshape(n_s, PLE_DIM))
    y_p, y_s = _final(pos_t, h1s, wts_t.T, p_pair, y_rows, row(ple_norm_g[0]), _bf(ple_w_gate[0]),
                      _bf(ple_w_proj[0]), row(final_norm_g), tm)
    y_prompt = y_p.reshape(nbp, Tp, D_MODEL)
    y_sample = y_s.reshape(nbs, Ts, D_MODEL)
    lead = lambda t: tuple(a[None] for a in t)
    return (y_prompt, y_sample) + lead(st_p) + lead(st_s)
```

```python
import functools
import math

import numpy as np
import jax
import jax.numpy as jnp
from jax import lax
from jax.experimental import pallas as pl
from jax.experimental.pallas import tpu as pltpu

F32 = jnp.float32
BF16 = jnp.bfloat16

D_MODEL = 1024
HEAD_DIM = 64
RWKV_WIDTH = 512
FOX_WIDTH = 512
N_HEADS = 8
N_PAIRS = N_HEADS // 2
DECAY_RANK = 64
ICL_RANK = 64
GATE_RANK = 128
RWKV_IN = 3 * RWKV_WIDTH + DECAY_RANK + ICL_RANK + GATE_RANK
RWKV_PAD = 3 * RWKV_WIDTH + 3 * 128
FOX_PAD = 4 * FOX_WIDTH + 128
ATTN_SCALE = HEAD_DIM ** -0.5
N_EXPERTS = 256
N_GROUPS = 8
GROUP_SIZE = N_EXPERTS // N_GROUPS
TOPK_GROUPS = 4
TOP_K = 8
EXPERT_FF = 256
ROUTED_SCALE = 2.5
PLE_DIM = 256
RMS_EPS = 1e-6
GN_EPS = 64e-5
L2_EPS = 1e-12

LANES = 128
MXU_TILE = 256
ROW_CHUNKS = D_MODEL // LANES
PACK_CHUNKS = ROW_CHUNKS // 2
RWKV_CHUNK = 128
MOE_ROWS = 256
TOKEN_TILE = 256
GMM_AHEAD = 2
DMA_QUEUES = 2
FOX_TQ, FOX_TK = 256, 1024
FOX_TK_CACHED = 768
VMEM_LIMIT = 56 * 1024 * 1024


def _cparams(*sem):
    return pltpu.CompilerParams(dimension_semantics=sem, vmem_limit_bytes=VMEM_LIMIT)


def _bf(x):
    return x.astype(BF16)


def _dot(a, b):
    return jnp.dot(a, b, preferred_element_type=F32)


def _dot_nt(a, b):
    return lax.dot_general(a, b, (((1,), (1,)), ((), ())), preferred_element_type=F32)


def _split2(x):
    hi = _bf(x)
    return hi, _bf(x - hi.astype(F32))


def _split3(x):
    hi = _bf(x)
    r1 = x - hi.astype(F32)
    mid = _bf(r1)
    return hi, mid, _bf(r1 - mid.astype(F32))


def _dot_x01(x, w01):
    hi, lo = _split2(x)
    slabs = [slice(c, c + MXU_TILE) for c in range(0, x.shape[1], MXU_TILE)]
    return jnp.concatenate([_dot(hi[:, s], w01) + _dot(lo[:, s], w01) for s in slabs], axis=1)


def _dot3(a, b):
    ah, al = _split2(a)
    bh, bl = _split2(b)
    return _dot(ah, bh) + _dot(al, bh) + _dot(ah, bl)


def _dot3_nt(a, b):
    ah, al = _split2(a)
    bh, bl = _split2(b)
    return _dot_nt(ah, bh) + _dot_nt(al, bh) + _dot_nt(ah, bl)


def _softplus(x):
    return jnp.maximum(x, 0.0) + jnp.log1p(jnp.exp(-jnp.abs(x)))


def _rms(x, g):
    return x * lax.rsqrt(jnp.mean(x * x, axis=-1, keepdims=True) + RMS_EPS) * g


def _full(shape):
    return pl.BlockSpec(shape, lambda *_: (0,) * len(shape))


def _store_chunked(ref, x):
    n = x.shape[0]
    for s in range(ROW_CHUNKS):
        ref[pl.ds(s, n, stride=ROW_CHUNKS), :] = x[:, s * LANES:(s + 1) * LANES]


def _load_chunked(ref, n):
    return jnp.concatenate([ref[pl.ds(s, n, stride=ROW_CHUNKS), :] for s in range(ROW_CHUNKS)], axis=1)


def _store_packed(ref, x):
    n, half = x.shape[0], D_MODEL // 2
    bits = lax.bitcast_convert_type(_bf(x).astype(F32), jnp.uint32)
    packed = (bits[:, :half] >> 16) | (bits[:, half:] & jnp.uint32(0xFFFF0000))
    for s in range(PACK_CHUNKS):
        ref[pl.ds(s, n, stride=PACK_CHUNKS), :] = packed[:, s * LANES:(s + 1) * LANES]


def _load_packed(ref, n):
    packed = jnp.concatenate([ref[pl.ds(s, n, stride=PACK_CHUNKS), :] for s in range(PACK_CHUNKS)], axis=1)
    lo = lax.bitcast_convert_type(packed << 16, F32)
    hi = lax.bitcast_convert_type(packed & jnp.uint32(0xFFFF0000), F32)
    return _bf(jnp.concatenate([lo, hi], axis=1))


def _in_rwkv_kernel(x_ref, shift_ref, gmix_ref, w_ref, mu_ref, w0_ref, wup_ref, a0_ref, aup_ref, gup_ref,
                    kk_ref, ka_ref, rk_ref, gs_ref,
                    r_out, lw_out, kh_out, v_out, kkn_out, bb_out, g_out, bonus_out, last_out,
                    carry_ref):
    j = pl.program_id(1)
    tm = x_ref.shape[1]
    xn = _bf(_rms(x_ref[0], gmix_ref[...]))
    u = _dot(xn, w_ref[...])
    first = jnp.where(j == 0, shift_ref[0], carry_ref[...])
    row = lax.broadcasted_iota(jnp.int32, (tm, 1), 0)
    prev = jnp.where(row == 0, first, pltpu.roll(u, 1, axis=0))
    carry_ref[...] = u[tm - 1:tm, :]
    last_out[0] = u[tm - 1:tm, :]
    xs = u + (prev - u) * mu_ref[...]
    W = RWKV_WIDTH
    r, k, v = xs[:, :W], xs[:, W:2 * W], xs[:, 2 * W:3 * W]
    xw, xa, xg = xs[:, 3 * W:3 * W + 128], xs[:, 3 * W + 128:3 * W + 256], xs[:, 3 * W + 256:]
    w_raw = w0_ref[...] + _dot(_bf(jnp.tanh(xw)), wup_ref[...])
    lw = -jnp.exp(-_softplus(-w_raw) - 0.5)
    a = jax.nn.sigmoid(a0_ref[...] + _dot(_bf(xa), aup_ref[...]))
    g = _dot(_bf(jax.nn.sigmoid(xg)), gup_ref[...])
    gs = gs_ref[...]
    kk = k * kk_ref[...]
    kkn = kk / jnp.maximum(jnp.sqrt(_dot_x01(kk * kk, gs)), L2_EPS)
    kh = k * (1.0 + (a - 1.0) * ka_ref[...])
    r_out[0] = r
    lw_out[0] = lw
    kh_out[0] = kh
    v_out[0] = v
    kkn_out[0] = kkn
    bb_out[0] = kkn * a
    g_out[0] = g
    bonus_out[0] = _dot_x01(r * kh * rk_ref[...], gs) * v


def _in_rwkv(x, shift, gmix, w, mu, w0, wup, a0, aup, gup, k_k, k_a, r_k, gs, tm):
    nb, T, _ = x.shape
    W = RWKV_WIDTH
    tok = lambda width: pl.BlockSpec((1, tm, width), lambda b, j: (b, j, 0))
    outs = [jax.ShapeDtypeStruct((nb, T, W), F32)] * 8 + [jax.ShapeDtypeStruct((nb, 1, RWKV_PAD), F32)]
    return pl.pallas_call(
        _in_rwkv_kernel,
        grid=(nb, T // tm),
        in_specs=[tok(D_MODEL), pl.BlockSpec((1, 1, RWKV_PAD), lambda b, j: (b, 0, 0)),
                  _full((1, D_MODEL)), _full((D_MODEL, RWKV_PAD)), _full((1, RWKV_PAD)),
                  _full((1, W)), _full((128, W)), _full((1, W)), _full((128, W)), _full((128, W)),
                  _full((1, W)), _full((1, W)), _full((1, W)), _full((MXU_TILE, MXU_TILE))],
        out_specs=[tok(W)] * 8 + [pl.BlockSpec((1, 1, RWKV_PAD), lambda b, j: (b, 0, 0))],
        out_shape=outs,
        scratch_shapes=[pltpu.VMEM((1, RWKV_PAD), F32)],
        compiler_params=_cparams("parallel", "arbitrary"),
        name="in_rwkv",
    )(x, shift, gmix, w, mu, w0, wup, a0, aup, gup, k_k, k_a, r_k, gs)


def _aug_rows(x, c, is_query):
    lane = lax.broadcasted_iota(jnp.int32, (1, LANES), 1)
    rows = []
    for p in range(N_PAIRS):
        xp = x[:, p * LANES:(p + 1) * LANES]
        xr = pltpu.roll(xp, HEAD_DIM, axis=1)
        for h in range(2):
            ch = c[:, 2 * p + h:2 * p + h + 1]
            hi = _bf(ch).astype(F32)
            r1 = ch - hi
            mid = _bf(r1).astype(F32)
            lo = _bf(r1 - mid).astype(F32)
            one = jnp.ones_like(ch)
            cols = (hi, mid, lo, one, one, one) if is_query else (one, one, one, -hi, -mid, -lo)
            aug = jnp.zeros_like(xp)
            for n, col in enumerate(cols):
                aug = jnp.where(lane == HEAD_DIM + n, col, aug)
            rows.append(_bf(jnp.where(lane < HEAD_DIM, xp if h == 0 else xr, aug)))
    return rows


def _in_fox_kernel(x_ref, gmix_ref, w_ref, qn_ref, kn_ref, fb_ref, gs_ref, tri_ref, c0_ref,
                   q_out, ka_out, k_out, v_out, vb_out, og_out, lf_out, carry_ref):
    j = pl.program_id(1)
    tm = x_ref.shape[1]

    @pl.when(j == 0)
    def _():
        carry_ref[...] = c0_ref[0]

    xn = _bf(_rms(x_ref[0], gmix_ref[...]))
    u = _dot(xn, w_ref[...])
    W = FOX_WIDTH
    q, k, v, og, fl = u[:, :W], u[:, W:2 * W], u[:, 2 * W:3 * W], u[:, 3 * W:4 * W], u[:, 4 * W:]
    gs = gs_ref[...]
    inv_d = 1.0 / HEAD_DIM
    qn = q * lax.rsqrt(_dot_x01(q * q, gs) * inv_d + RMS_EPS) * qn_ref[...]
    kn = k * lax.rsqrt(_dot_x01(k * k, gs) * inv_d + RMS_EPS) * kn_ref[...]
    logf = -_softplus(-(fl + fb_ref[...]))
    hi, mid, lo = _split3(logf)
    tri = tri_ref[...]
    c = _dot(tri, hi) + _dot(tri, mid) + _dot(tri, lo) + carry_ref[...]
    carry_ref[...] = c[tm - 1:tm, :]
    for h, row in enumerate(_aug_rows(_bf(qn * ATTN_SCALE).astype(F32), c, True)):
        q_out[0, h] = row
    for h, row in enumerate(_aug_rows(_bf(kn).astype(F32), c, False)):
        ka_out[0, h] = row
    k_out[0] = kn
    v_out[0] = v
    vb_out[0] = _bf(v)
    og_out[0] = og
    lf_out[0] = logf


def _in_fox(x, c0, gmix, w, qn, kn, fb, gs, tm):
    nb, T, _ = x.shape
    W = FOX_WIDTH
    tok = lambda width: pl.BlockSpec((1, tm, width), lambda b, j: (b, j, 0))
    aug = pl.BlockSpec((1, N_HEADS, tm, LANES), lambda b, j: (b, 0, j, 0))
    sds = lambda width, dt: jax.ShapeDtypeStruct((nb, T, width), dt)
    aug_shape = jax.ShapeDtypeStruct((nb, N_HEADS, T, LANES), BF16)
    tri = _bf(jnp.tril(jnp.ones((tm, tm), F32)))
    return pl.pallas_call(
        _in_fox_kernel,
        grid=(nb, T // tm),
        in_specs=[tok(D_MODEL), _full((1, D_MODEL)), _full((D_MODEL, FOX_PAD)),
                  _full((1, W)), _full((1, W)), _full((1, 128)), _full((MXU_TILE, MXU_TILE)), _full((tm, tm)),
                  pl.BlockSpec((1, 1, LANES), lambda b, j: (b, 0, 0))],
        out_specs=[aug, aug] + [tok(W)] * 4 + [tok(128)],
        out_shape=[aug_shape, aug_shape, sds(W, F32), sds(W, F32), sds(W, BF16), sds(W, F32), sds(128, F32)],
        scratch_shapes=[pltpu.VMEM((1, LANES), F32)],
        compiler_params=_cparams("parallel", "arbitrary"),
        name="in_fox",
    )(x, gmix, w, qn, kn, fb, gs, tri, c0)


def _cumsum_kernel(x_ref, tri_ref, o_ref, carry_ref):
    j = pl.program_id(1)
    ts = x_ref.shape[1]

    @pl.when(j == 0)
    def _():
        carry_ref[...] = jnp.zeros_like(carry_ref)

    hi, mid, lo = _split3(x_ref[0])
    tri = tri_ref[...]
    c = _dot(tri, hi) + _dot(tri, mid) + _dot(tri, lo) + carry_ref[...]
    o_ref[0] = c
    carry_ref[...] = c[ts - 1:ts, :]


def _cumsum(x, ts):
    nb, S, L = x.shape
    tri = _bf(jnp.tril(jnp.ones((ts, ts), F32)))
    return pl.pallas_call(
        _cumsum_kernel,
        grid=(nb, S // ts),
        in_specs=[pl.BlockSpec((1, ts, L), lambda b, j: (b, j, 0)), _full((ts, ts))],
        out_specs=pl.BlockSpec((1, ts, L), lambda b, j: (b, j, 0)),
        out_shape=jax.ShapeDtypeStruct((nb, S, L), F32),
        scratch_shapes=[pltpu.VMEM((1, L), F32)],
        compiler_params=_cparams("parallel", "arbitrary"),
        name="seq_cumsum",
    )(x, tri)


def _rwkv_chunk_kernel(r_ref, lw_ref, kh_ref, v_ref, kk_ref, bb_ref, tri_ref,
                       rh_out, yh_out, g_out, sh_out):
    C = r_ref.shape[1]
    ti = lax.broadcasted_iota(jnp.int32, (C, C), 0)
    si = lax.broadcasted_iota(jnp.int32, (C, C), 1)
    tx = ti ^ si
    strict = ti > si
    incl = ti >= si
    eye_c = (ti == si).astype(F32)
    lane = lax.broadcasted_iota(jnp.int32, (1, LANES), 1)
    head0 = lane < HEAD_DIM
    pi = lax.broadcasted_iota(jnp.int32, (LANES, LANES), 0)
    pj = lax.broadcasted_iota(jnp.int32, (LANES, LANES), 1)
    same_head = (pi < HEAD_DIM) == (pj < HEAD_DIM)
    eye_p = (pi == pj).astype(F32)
    tri = tri_ref[...]

    pairs = range(N_PAIRS)
    heads = [(p, h) for p in pairs for h in range(2)]
    P = []
    for p in pairs:
        sl = slice(p * LANES, (p + 1) * LANES)
        r, lw, kh, v, kk, bb = (ref[0, :, sl] for ref in (r_ref, lw_ref, kh_ref, v_ref, kk_ref, bb_ref))
        l_hi, l_mid, l_lo = _split3(lw)
        lc = _dot(tri, l_hi) + _dot(tri, l_mid) + _dot(tri, l_lo)
        mid = lc[C // 2 - 1:C // 2, :]
        last = lc[C - 1:C, :]
        e_dn = jnp.exp(mid - lc)
        e_end = jnp.exp(last - lc)
        aa = kk * jnp.exp(lc - lw - mid)
        P.append(dict(sl=sl, v=v, aa=aa, rt=r * jnp.exp(lc - mid), rho=jnp.exp(mid), g_last=jnp.exp(last),
                      bt_b=_bf(bb * e_dn), kt_b=_bf(kh * e_dn), bc_b=_bf(bb * e_end), kc_b=_bf(kh * e_end),
                      aa_b=_bf(aa), v_b=_bf(v)))
    lab, lak, mrb, mrk = {}, {}, {}, {}
    for p, h in heads:
        q = P[p]
        hm = head0 if h == 0 else jnp.logical_not(head0)
        aa_m = _bf(jnp.where(hm, q["aa"], 0.0))
        rt_m = _bf(jnp.where(hm, q["rt"], 0.0))
        lab[p, h] = jnp.where(strict, _dot_nt(aa_m, q["bt_b"]), 0.0)
        lak[p, h] = _bf(jnp.where(strict, _dot_nt(aa_m, q["kt_b"]), 0.0))
        mrb[p, h] = _bf(jnp.where(incl, _dot_nt(rt_m, q["bt_b"]), 0.0))
        mrk[p, h] = _bf(jnp.where(incl, _dot_nt(rt_m, q["kt_b"]), 0.0))
    d = {k: eye_c - jnp.where(tx < 2, lab[k], 0.0) for k in heads}
    s = 2
    while s < C:
        level = (tx >= s) & (tx < 2 * s)
        d_b = {k: _bf(d[k]) for k in heads}
        t1 = {k: _bf(_dot(d_b[k], _bf(jnp.where(level, lab[k], 0.0)))) for k in heads}
        d = {k: d[k] - _dot(t1[k], d_b[k]) for k in heads}
        s *= 2
    d_b = {k: _bf(d[k]) for k in heads}
    w = {k: _bf(_dot(lak[k], P[k[0]]["v_b"])) for k in heads}
    ah = {k: _dot(d_b[k], P[k[0]]["aa_b"]) * P[k[0]]["rho"] for k in heads}
    uh = {k: _dot(d_b[k], w[k]) for k in heads}
    rh = {k: P[k[0]]["rt"] * P[k[0]]["rho"] - _dot(mrb[k], _bf(ah[k])) for k in heads}
    yh = {k: _dot(mrk[k], P[k[0]]["v_b"]) - _dot(mrb[k], _bf(uh[k])) for k in heads}
    for p in pairs:
        q = P[p]
        both = lambda x: jnp.where(head0, x[p, 0], x[p, 1])
        ah_p, uh_p = both(ah), both(uh)
        rh_out[0, :, q["sl"]] = both(rh)
        yh_out[0, :, q["sl"]] = both(yh)
        g_full = eye_p * q["g_last"] - _dot(_bf(ah_p.T), q["bc_b"])
        sh_full = _dot(_bf(q["v"].T), q["kc_b"]) - _dot(_bf(uh_p.T), q["bc_b"])
        g_out[0, 0, p] = jnp.where(same_head, g_full, 0.0)
        sh_out[0, 0, p] = jnp.where(same_head, sh_full, 0.0)


def _rwkv_chunks(r, lw, kh, v, kk, bb):
    nb, T, W = r.shape
    C = RWKV_CHUNK
    nc = T // C
    tri = _bf(jnp.tril(jnp.ones((C, C), F32)))
    tok = pl.BlockSpec((1, C, W), lambda b, c: (b, c, 0))
    mat = pl.BlockSpec((1, 1, N_PAIRS, LANES, LANES), lambda b, c: (b, c, 0, 0, 0))
    mat_shape = jax.ShapeDtypeStruct((nb, nc, N_PAIRS, LANES, LANES), F32)
    return pl.pallas_call(
        _rwkv_chunk_kernel,
        grid=(nb, nc),
        in_specs=[tok] * 6 + [_full((C, C))],
        out_specs=[tok, tok, mat, mat],
        out_shape=[jax.ShapeDtypeStruct((nb, T, W), F32)] * 2 + [mat_shape] * 2,
        compiler_params=_cparams("parallel", "parallel"),
        name="rwkv_chunks",
    )(r, lw, kh, v, kk, bb, tri)


def _rwkv_scan_kernel(rh_ref, yh_ref, g_ref, sh_ref, s0_ref, y_out, sfin_out, s_scr):
    c = pl.program_id(1)
    nbg = rh_ref.shape[0]

    @pl.when(c == 0)
    def _():
        s_scr[...] = s0_ref[...]

    for b in range(nbg):
        for p in range(N_PAIRS):
            sl = slice(p * LANES, (p + 1) * LANES)
            s = s_scr[b, p]
            y_out[b, :, sl] = _dot3_nt(rh_ref[b, :, sl], s) + yh_ref[b, :, sl]
            s_new = _dot3(s, g_ref[b, 0, p]) + sh_ref[b, 0, p]
            s_scr[b, p] = s_new
            sfin_out[b, p] = s_new


def _rwkv_scan(rh, yh, g, sh, s0, nbg):
    nb, T, W = rh.shape
    C = RWKV_CHUNK
    nc = T // C
    tok = pl.BlockSpec((nbg, C, W), lambda i, c: (i, c, 0))
    mat = pl.BlockSpec((nbg, 1, N_PAIRS, LANES, LANES), lambda i, c: (i, c, 0, 0, 0))
    st = pl.BlockSpec((nbg, N_PAIRS, LANES, LANES), lambda i, c: (i, 0, 0, 0))
    return pl.pallas_call(
        _rwkv_scan_kernel,
        grid=(nb // nbg, nc),
        in_specs=[tok, tok, mat, mat, st],
        out_specs=[tok, st],
        out_shape=[jax.ShapeDtypeStruct((nb, T, W), F32),
                   jax.ShapeDtypeStruct((nb, N_PAIRS, LANES, LANES), F32)],
        scratch_shapes=[pltpu.VMEM((nbg, N_PAIRS, LANES, LANES), F32)],
        compiler_params=_cparams("parallel", "arbitrary"),
        name="rwkv_scan",
    )(rh, yh, g, sh, s0)


def _state_to_pairs(s):
    nb = s.shape[0]
    s = s.reshape(nb, N_PAIRS, 2, HEAD_DIM, HEAD_DIM)
    z = jnp.zeros_like(s[:, :, 0])
    top = jnp.concatenate([s[:, :, 0], z], axis=-1)
    bot = jnp.concatenate([z, s[:, :, 1]], axis=-1)
    return jnp.concatenate([top, bot], axis=-2)


def _pairs_to_state(sp):
    nb = sp.shape[0]
    a = sp[:, :, :HEAD_DIM, :HEAD_DIM]
    b = sp[:, :, HEAD_DIM:, HEAD_DIM:]
    return jnp.stack([a, b], axis=2).reshape(nb, N_HEADS, HEAD_DIM, HEAD_DIM)


def _fox_aug_kernel(x_ref, c_ref, o_ref):
    for h, row in enumerate(_aug_rows(x_ref[0].astype(F32), c_ref[0], False)):
        o_ref[0, h] = row


def _fox_aug(x, c, tm):
    nb, S, W = x.shape
    return pl.pallas_call(
        _fox_aug_kernel,
        grid=(nb, S // tm),
        in_specs=[pl.BlockSpec((1, tm, W), lambda b, j: (b, j, 0)),
                  pl.BlockSpec((1, tm, LANES), lambda b, j: (b, j, 0))],
        out_specs=pl.BlockSpec((1, N_HEADS, tm, LANES), lambda b, j: (b, 0, j, 0)),
        out_shape=jax.ShapeDtypeStruct((nb, N_HEADS, S, LANES), BF16),
        compiler_params=_cparams("parallel", "parallel"),
        name="fox_aug",
    )(x, c)


def _fox_kernel(q_ref, k_ref, v_ref, o_ref, m_scr, l_scr, acc_scr, sa_scr, sb_scr, *, tk, q_off):
    i = pl.program_id(2)
    tq = q_ref.shape[2]
    q_start = q_off + i * tq
    m_scr[...] = jnp.full_like(m_scr, -jnp.inf)
    l_scr[...] = jnp.zeros_like(l_scr)
    acc_scr[...] = jnp.zeros_like(acc_scr)
    n_full = (q_start + 1) // tk

    def scores_into(j, dst):
        ks = pl.multiple_of(j * tk, tk)
        for h in range(2):
            dst[h] = _dot_nt(q_ref[0, h], k_ref[0, h, pl.ds(ks, tk), :])

    def update_from(j, src, masked):
        ks = pl.multiple_of(j * tk, tk)
        v = v_ref[0, pl.ds(ks, tk), :]
        if masked:
            visible = (ks + lax.broadcasted_iota(jnp.int32, (1, tk), 1)) <= (
                q_start + lax.broadcasted_iota(jnp.int32, (tq, 1), 0))
        for h in range(2):
            s = src[h]
            if masked:
                s = jnp.where(visible, s, -jnp.inf)
            m_old = m_scr[h]
            m_new = jnp.maximum(m_old, jnp.max(s, axis=-1, keepdims=True))
            alpha = jnp.exp(m_old - m_new)
            pr = jnp.exp(s - m_new)
            l_scr[h] = alpha * l_scr[h] + jnp.sum(pr, axis=-1, keepdims=True)
            acc_scr[h] = alpha * acc_scr[h] + _dot(_bf(pr), v)
            m_scr[h] = m_new

    def step(j, src, dst):
        scores_into(j + 1, dst)
        update_from(j, src, False)

    def two_steps(jj, carry):
        step(2 * jj, sa_scr, sb_scr)
        step(2 * jj + 1, sb_scr, sa_scr)
        return carry

    scores_into(0, sa_scr)
    lax.fori_loop(0, n_full // 2, two_steps, 0)
    odd = n_full % 2 == 1

    @pl.when(odd)
    def _():
        step(n_full - 1, sa_scr, sb_scr)
        update_from(n_full, sb_scr, True)

    @pl.when(jnp.logical_not(odd))
    def _():
        update_from(n_full, sa_scr, True)

    head0 = lax.broadcasted_iota(jnp.int32, (1, LANES), 1) < HEAD_DIM
    o_ref[0] = jnp.where(head0, acc_scr[0] / l_scr[0], acc_scr[1] / l_scr[1])


def _fox_attention(q, k, v, tq, tk, q_off):
    nb, _, sq, _ = q.shape
    sk = k.shape[2]
    for q_start in range(q_off, q_off + sq, tq):
        assert (q_start + 1) // tk + 1 == -(-(q_start + tq) // tk) <= sk // tk, (q_start, tq, tk)
    return pl.pallas_call(
        functools.partial(_fox_kernel, tk=tk, q_off=q_off),
        grid=(nb, N_PAIRS, sq // tq),
        in_specs=[pl.BlockSpec((1, 2, tq, LANES), lambda b, p, i: (b, p, i, 0)),
                  pl.BlockSpec((1, 2, sk, LANES), lambda b, p, i: (b, p, 0, 0)),
                  pl.BlockSpec((1, sk, LANES), lambda b, p, i: (b, 0, p))],
        out_specs=pl.BlockSpec((1, tq, LANES), lambda b, p, i: (b, i, p)),
        out_shape=jax.ShapeDtypeStruct((nb, sq, FOX_WIDTH), F32),
        scratch_shapes=[pltpu.VMEM((2, tq, 1), F32), pltpu.VMEM((2, tq, 1), F32),
                        pltpu.VMEM((2, tq, LANES), F32), pltpu.VMEM((2, tq, tk), F32),
                        pltpu.VMEM((2, tq, tk), F32)],
        compiler_params=_cparams("parallel", "parallel", "arbitrary"),
        name="fox_attention",
    )(q, k, v)


def _out_kernel(*refs, n_first):
    tok_refs, rest = refs[:12], refs[12:]
    lnw_ref, lnb_ref, gs_ref, wout_ref, gffn_ref, rwh_ref, rwl_ref, h_out, xn_out, sc_out = rest
    first = pl.program_id(0) < n_first
    x, y, bonus, g, oa, og = (jnp.where(first, tok_refs[2 * n][...], tok_refs[2 * n + 1][...]) for n in range(6))
    gs = gs_ref[...]
    inv_d = 1.0 / HEAD_DIM
    mean = _dot_x01(y, gs) * inv_d
    d = y - mean
    var = _dot_x01(d * d, gs) * inv_d
    yn = d * lax.rsqrt(var + GN_EPS) * lnw_ref[...] + lnb_ref[...]
    o_rwkv = (yn + bonus) * g
    o_fox = oa * jax.nn.sigmoid(og)
    mix = jnp.concatenate([_bf(o_rwkv), _bf(o_fox)], axis=-1)
    h = x + _dot(mix, wout_ref[...])
    h_out[...] = h
    xn = _rms(h, gffn_ref[...])
    _store_packed(xn_out, xn)
    xh, xl = _split2(xn)
    logits = _dot(xh, rwh_ref[...]) + _dot(xl, rwh_ref[...]) + _dot(xh, rwl_ref[...])
    sc_out[...] = jax.nn.sigmoid(logits)


def _two_streams(width, tm, n_first):
    return [pl.BlockSpec((tm, width), lambda i: (jnp.minimum(i, n_first - 1), 0)),
            pl.BlockSpec((tm, width), lambda i: (jnp.maximum(i - n_first, 0), 0))]


def _out_proj(streams, lnw, lnb, gs, wout, gffn, rwh, rwl, tm):
    n_first = streams[0][0].shape[0] // tm
    T = streams[0][0].shape[0] + streams[0][1].shape[0]
    W = RWKV_WIDTH
    tok = lambda width: pl.BlockSpec((tm, width), lambda i: (i, 0))
    tok_specs, tok_args = [], []
    for a, b in streams:
        tok_specs += _two_streams(a.shape[1], tm, n_first)
        tok_args += [a, b]
    return pl.pallas_call(
        functools.partial(_out_kernel, n_first=n_first),
        grid=(T // tm,),
        in_specs=tok_specs + [_full((1, W)), _full((1, W)), _full((MXU_TILE, MXU_TILE)),
                              _full((D_MODEL, D_MODEL)), _full((1, D_MODEL)),
                              _full((D_MODEL, N_EXPERTS)), _full((D_MODEL, N_EXPERTS))],
        out_specs=[tok(D_MODEL), pl.BlockSpec((tm * PACK_CHUNKS, LANES), lambda i: (i, 0)), tok(N_EXPERTS)],
        out_shape=[jax.ShapeDtypeStruct((T, D_MODEL), F32),
                   jax.ShapeDtypeStruct((T * PACK_CHUNKS, LANES), jnp.uint32),
                   jax.ShapeDtypeStruct((T, N_EXPERTS), F32)],
        compiler_params=_cparams("parallel"),
        name="out_proj",
    )(*tok_args, lnw, lnb, gs, wout, gffn, rwh, rwl)


def _route_kernel(sc_ref, bias_ref, before_ref, idx_out, wt_out, rank_out, cnt_out, cnt_scr):
    tm = sc_ref.shape[0]
    neg = -jnp.inf

    @pl.when(pl.program_id(0) == 0)
    def _():
        cnt_scr[...] = jnp.zeros_like(cnt_scr)

    st = sc_ref[...].T
    sel = st + bias_ref[...]
    gscore = []
    for gi in range(N_GROUPS):
        blk = sel[gi * GROUP_SIZE:(gi + 1) * GROUP_SIZE, :]
        m1 = jnp.max(blk, axis=0, keepdims=True)
        n1 = jnp.sum((blk == m1).astype(F32), axis=0, keepdims=True)
        m2 = jnp.max(jnp.where(blk < m1, blk, neg), axis=0, keepdims=True)
        gscore.append(m1 + jnp.where(n1 > 1.0, m1, m2))
    taken = [jnp.zeros((1, tm), jnp.bool_) for _ in range(N_GROUPS)]
    for _ in range(TOPK_GROUPS):
        avail = [jnp.where(taken[gi], neg, gscore[gi]) for gi in range(N_GROUPS)]
        best = functools.reduce(jnp.maximum, avail)
        found = jnp.zeros((1, tm), jnp.bool_)
        for gi in range(N_GROUPS):
            hit = (avail[gi] == best) & jnp.logical_not(found)
            taken[gi] = taken[gi] | hit
            found = found | hit
    cand = jnp.concatenate(
        [jnp.where(taken[gi], sel[gi * GROUP_SIZE:(gi + 1) * GROUP_SIZE, :], neg) for gi in range(N_GROUPS)], axis=0)
    eid = lax.broadcasted_iota(jnp.int32, (N_EXPERTS, tm), 0).astype(F32)
    idxs, wts = [], []
    onehot = jnp.zeros((N_EXPERTS, tm), F32)
    for _ in range(TOP_K):
        best = jnp.max(cand, axis=0, keepdims=True)
        pick = jnp.min(jnp.where(cand == best, eid, float(N_EXPERTS)), axis=0, keepdims=True)
        chosen = eid == pick
        wts.append(jnp.sum(jnp.where(chosen, st, 0.0), axis=0, keepdims=True))
        idxs.append(pick)
        cand = jnp.where(chosen, neg, cand)
        onehot = jnp.where(chosen, 1.0, onehot)
    w = jnp.concatenate(wts, axis=0)
    idx_out[...] = jnp.concatenate(idxs, axis=0).astype(jnp.int32)
    wt_out[...] = w / jnp.sum(w, axis=0, keepdims=True) * ROUTED_SCALE
    earlier = _dot(_bf(onehot), before_ref[...]) + cnt_scr[...]
    rank_out[...] = jnp.concatenate(
        [jnp.sum(jnp.where(eid == pick, earlier, 0.0), axis=0, keepdims=True) for pick in idxs],
        axis=0).astype(jnp.int32)
    cnt_scr[...] += jnp.sum(onehot, axis=1, keepdims=True)
    cnt_out[...] = cnt_scr[...]


def _route(scores, bias_col, tm):
    T = scores.shape[0]
    before = _bf(jnp.triu(jnp.ones((tm, tm), F32), 1))
    tok = pl.BlockSpec((TOP_K, tm), lambda i: (0, i))
    return pl.pallas_call(
        _route_kernel,
        grid=(T // tm,),
        in_specs=[pl.BlockSpec((tm, N_EXPERTS), lambda i: (i, 0)), _full((N_EXPERTS, 1)), _full((tm, tm))],
        out_specs=[tok, tok, tok, _full((N_EXPERTS, 1))],
        out_shape=[jax.ShapeDtypeStruct((TOP_K, T), jnp.int32), jax.ShapeDtypeStruct((TOP_K, T), F32),
                   jax.ShapeDtypeStruct((TOP_K, T), jnp.int32), jax.ShapeDtypeStruct((N_EXPERTS, 1), F32)],
        scratch_shapes=[pltpu.VMEM((N_EXPERTS, 1), F32)],
        compiler_params=_cparams("arbitrary"),
        name="route",
    )(scores, bias_col, before)


def _pos_kernel(idx_ref, rank_ref, start_ref, pos_out):
    tm = idx_ref.shape[1]
    eid = lax.broadcasted_iota(jnp.int32, (N_EXPERTS, tm), 0)
    idx = idx_ref[...]
    start = start_ref[...]
    base = jnp.concatenate(
        [jnp.sum(jnp.where(eid == idx[k:k + 1, :], start, 0.0), axis=0, keepdims=True) for k in range(TOP_K)],
        axis=0)
    pos_out[...] = rank_ref[...] + base.astype(jnp.int32)


def _positions(eidx_t, rank_t, start_col, tm):
    T = eidx_t.shape[1]
    tok = pl.BlockSpec((TOP_K, tm), lambda i: (0, i))
    return pl.pallas_call(
        _pos_kernel,
        grid=(T // tm,),
        in_specs=[tok, tok, _full((N_EXPERTS, 1))],
        out_specs=tok,
        out_shape=jax.ShapeDtypeStruct((TOP_K, T), jnp.int32),
        compiler_params=_cparams("parallel"),
        name="moe_positions",
    )(eidx_t, rank_t, start_col)


def _dispatch_kernel(pos_ref, x_ref, h_ref, sg_ref, su_ref, sd_ref, xs_in, xs_out, hs_out, sem):
    del xs_in
    tm = pos_ref.shape[1]

    def issue(t, carry):
        src = x_ref.at[pl.ds(pl.multiple_of(t * PACK_CHUNKS, PACK_CHUNKS), PACK_CHUNKS), :]
        for k in range(TOP_K):
            row = pl.multiple_of(pos_ref[k, t] * PACK_CHUNKS, PACK_CHUNKS)
            pltpu.make_async_copy(src, xs_out.at[pl.ds(row, PACK_CHUNKS), :], sem).start(priority=k % DMA_QUEUES)
        return carry

    lax.fori_loop(0, tm, issue, 0)
    xb = _load_packed(x_ref, tm)
    hg = _dot(xb, sg_ref[...])
    hu = _dot(xb, su_ref[...])
    hs_out[...] = h_ref[...] + _dot(_bf(hg * jax.nn.sigmoid(hg) * hu), sd_ref[...])
    for k in range(TOP_K):
        pltpu.make_async_copy(x_ref, xs_out.at[pl.ds(0, tm * PACK_CHUNKS), :], sem).wait()


def _dispatch(pos_t, xn_packed, h, sg, su, sd, n_rows, tm):
    T = pos_t.shape[1]
    xs0 = jnp.zeros((n_rows * PACK_CHUNKS, LANES), jnp.uint32)
    return pl.pallas_call(
        _dispatch_kernel,
        grid=(T // tm,),
        in_specs=[pl.BlockSpec((TOP_K, tm), lambda i: (0, i), memory_space=pltpu.SMEM),
                  pl.BlockSpec((tm * PACK_CHUNKS, LANES), lambda i: (i, 0)),
                  pl.BlockSpec((tm, D_MODEL), lambda i: (i, 0)),
                  _full((D_MODEL, EXPERT_FF)), _full((D_MODEL, EXPERT_FF)), _full((EXPERT_FF, D_MODEL)),
                  pl.BlockSpec(memory_space=pl.ANY)],
        out_specs=[pl.BlockSpec(memory_space=pl.ANY), pl.BlockSpec((tm, D_MODEL), lambda i: (i, 0))],
        out_shape=[jax.ShapeDtypeStruct((n_rows * PACK_CHUNKS, LANES), jnp.uint32),
                   jax.ShapeDtypeStruct((T, D_MODEL), F32)],
        scratch_shapes=[pltpu.SemaphoreType.DMA(())],
        input_output_aliases={6: 0},
        compiler_params=_cparams("arbitrary"),
        name="moe_dispatch",
    )(pos_t, xn_packed, h, sg, su, sd, xs0)


def _gmm_kernel(first_ref, count_ref, used_ref, xs_hbm, wg_ref, wu_ref, wd_ref, y_hbm,
                xbuf, ybuf, wg_b, wu_b, wd_b, in_sems, out_sems):
    e = pl.program_id(0)
    bm = MOE_ROWS
    blk_rows = bm * ROW_CHUNKS
    n_used = used_ref[0]
    n_blocks = y_hbm.shape[0] // blk_rows

    in_rows = bm * PACK_CHUNKS

    def read(g, slot):
        return pltpu.make_async_copy(xs_hbm.at[pl.ds(pl.multiple_of(g * in_rows, in_rows), in_rows), :],
                                     xbuf.at[slot], in_sems.at[slot])

    def write(g, slot):
        return pltpu.make_async_copy(ybuf.at[slot],
                                     y_hbm.at[pl.ds(pl.multiple_of(g * blk_rows, blk_rows), blk_rows), :],
                                     out_sems.at[slot])

    @pl.when(e == 0)
    def _():
        for g0 in range(GMM_AHEAD):
            @pl.when(g0 < n_used)
            def _():
                read(g0, g0).start()

    wg_b[...] = _bf(wg_ref[0])
    wu_b[...] = _bf(wu_ref[0])
    wd_b[...] = _bf(wd_ref[0])

    def block(g, carry):
        slot = g % (GMM_AHEAD + 1)
        oslot = g % 2
        read(g, slot).wait()

        @pl.when(g + GMM_AHEAD < n_used)
        def _():
            read(g + GMM_AHEAD, (g + GMM_AHEAD) % (GMM_AHEAD + 1)).start()

        @pl.when(g >= 2)
        def _():
            write(g - 2, oslot).wait()

        xe = _load_packed(xbuf.at[slot], bm)
        hg = _dot(xe, wg_b[...])
        hu = _dot(xe, wu_b[...])
        _store_chunked(ybuf.at[oslot], _dot(_bf(hg * jax.nn.sigmoid(hg) * hu), wd_b[...]))
        write(g, oslot).start()
        return carry

    lax.fori_loop(first_ref[e], first_ref[e] + count_ref[e], block, 0)

    @pl.when(e == pl.num_programs(0) - 1)
    def _():
        for back in (2, 1):
            @pl.when(n_used >= back)
            def _():
                write(n_used - back, (n_used - back) % 2).wait()
        ybuf[0] = jnp.zeros_like(ybuf[0])

        def fill(g, carry):
            write(g, 0).start()
            return carry

        def drain(g, carry):
            write(g, 0).wait()
            return carry

        lax.fori_loop(n_used, n_blocks, fill, 0)
        lax.fori_loop(n_used, n_blocks, drain, 0)


def _gmm(first_blk, blk_count, n_used, xs, wg, wu, wd):
    bm = MOE_ROWS
    n_rows = xs.shape[0] // PACK_CHUNKS
    wspec = lambda shape: pl.BlockSpec((1,) + shape, lambda e, *_: (e, 0, 0))
    grid_spec = pltpu.PrefetchScalarGridSpec(
        num_scalar_prefetch=3,
        grid=(N_EXPERTS,),
        in_specs=[pl.BlockSpec(memory_space=pl.ANY), wspec((D_MODEL, EXPERT_FF)), wspec((D_MODEL, EXPERT_FF)),
                  wspec((EXPERT_FF, D_MODEL))],
        out_specs=pl.BlockSpec(memory_space=pl.ANY),
        scratch_shapes=[pltpu.VMEM((GMM_AHEAD + 1, bm * PACK_CHUNKS, LANES), jnp.uint32),
                        pltpu.VMEM((2, bm * ROW_CHUNKS, LANES), F32),
                        pltpu.VMEM((D_MODEL, EXPERT_FF), BF16), pltpu.VMEM((D_MODEL, EXPERT_FF), BF16),
                        pltpu.VMEM((EXPERT_FF, D_MODEL), BF16),
                        pltpu.SemaphoreType.DMA((GMM_AHEAD + 1,)), pltpu.SemaphoreType.DMA((2,))],
    )
    return pl.pallas_call(
        _gmm_kernel,
        grid_spec=grid_spec,
        out_shape=jax.ShapeDtypeStruct((n_rows * ROW_CHUNKS, LANES), F32),
        compiler_params=_cparams("arbitrary"),
        name="expert_gmm",
    )(first_blk, blk_count, n_used, xs, wg, wu, wd)


def _final_kernel(pos_ref, nxt_ref, hs_ref, w_ref, pa_ref, pb_ref, ys_ref, gple_ref, wpg_ref, wpp_ref, gfin_ref,
                  ya_out, yb_out, buf, sems, *, n_first):
    i = pl.program_id(0)
    n = pl.num_programs(0)
    tm = hs_ref.shape[0]

    def gather(rows_ref, slot):
        def issue(t, carry):
            dst = pl.ds(pl.multiple_of(t * ROW_CHUNKS, ROW_CHUNKS), ROW_CHUNKS)
            for k in range(TOP_K):
                row = pl.multiple_of(rows_ref[k, t] * ROW_CHUNKS, ROW_CHUNKS)
                pltpu.make_async_copy(ys_ref.at[pl.ds(row, ROW_CHUNKS), :], buf.at[slot, k, dst, :],
                                      sems.at[slot, k]).start(priority=k % DMA_QUEUES)
            return carry

        lax.fori_loop(0, tm, issue, 0)

    slot = i % 2

    @pl.when(i == 0)
    def _():
        gather(pos_ref, 0)

    @pl.when(i + 1 < n)
    def _():
        gather(nxt_ref, 1 - slot)

    pp = _dot(_bf(jnp.where(i < n_first, pa_ref[...], pb_ref[...])), wpp_ref[...])
    w = w_ref[...]
    for k in range(TOP_K):
        pltpu.make_async_copy(ys_ref.at[pl.ds(0, tm * ROW_CHUNKS), :], buf.at[slot, k], sems.at[slot, k]).wait()
    wk = [jnp.broadcast_to(w[:, k:k + 1], (tm, LANES)) for k in range(TOP_K)]
    chunks = []
    for s in range(ROW_CHUNKS):
        acc = buf[slot, 0, pl.ds(s, tm, stride=ROW_CHUNKS), :] * wk[0]
        for k in range(1, TOP_K):
            acc = acc + buf[slot, k, pl.ds(s, tm, stride=ROW_CHUNKS), :] * wk[k]
        chunks.append(acc)
    h = hs_ref[...] + jnp.concatenate(chunks, axis=1)
    gate = jax.nn.sigmoid(_dot(_bf(_rms(h, gple_ref[...])), wpg_ref[...]))
    y = _rms(h + gate * pp, gfin_ref[...])

    @pl.when(i < n_first)
    def _():
        ya_out[...] = y

    @pl.when(i >= n_first)
    def _():
        yb_out[...] = y


def _final(pos_t, hs, w, p_pair, ys, gple, wpg, wpp, gfin, tm):
    T = hs.shape[0]
    n = T // tm
    n_first = p_pair[0].shape[0] // tm
    tok = lambda width: pl.BlockSpec((tm, width), lambda i: (i, 0))
    return pl.pallas_call(
        functools.partial(_final_kernel, n_first=n_first),
        grid=(n,),
        in_specs=[pl.BlockSpec((TOP_K, tm), lambda i: (0, i), memory_space=pltpu.SMEM),
                  pl.BlockSpec((TOP_K, tm), lambda i: (0, jnp.minimum(i + 1, n - 1)), memory_space=pltpu.SMEM),
                  tok(D_MODEL), tok(TOP_K)] + _two_streams(PLE_DIM, tm, n_first) + [
                  pl.BlockSpec(memory_space=pl.ANY),
                  _full((1, D_MODEL)), _full((D_MODEL, D_MODEL)), _full((PLE_DIM, D_MODEL)),
                  _full((1, D_MODEL))],
        out_specs=_two_streams(D_MODEL, tm, n_first),
        out_shape=[jax.ShapeDtypeStruct((n_first * tm, D_MODEL), F32),
                   jax.ShapeDtypeStruct((T - n_first * tm, D_MODEL), F32)],
        scratch_shapes=[pltpu.VMEM((2, TOP_K, tm * ROW_CHUNKS, LANES), F32),
                        pltpu.SemaphoreType.DMA((2, TOP_K))],
        compiler_params=_cparams("arbitrary"),
        name="ffn_tail",
    )(pos_t, pos_t, hs, w, *p_pair, ys, gple, wpg, wpp, gfin)


def _pad_cols(a, width):
    return jnp.pad(a, [(0, 0)] * (a.ndim - 1) + [(0, width - a.shape[-1])])


def _rwkv_pad_cols(a):
    W = RWKV_WIDTH
    o1, o2, o3 = 3 * W, 3 * W + DECAY_RANK, 3 * W + DECAY_RANK + ICL_RANK
    return jnp.concatenate([a[..., :o1], _pad_cols(a[..., o1:o2], 128), _pad_cols(a[..., o2:o3], 128),
                            a[..., o3:]], axis=-1)


def _rwkv_unpad_cols(a):
    W = RWKV_WIDTH
    return jnp.concatenate([a[..., :3 * W + DECAY_RANK], a[..., 3 * W + 128:3 * W + 128 + ICL_RANK],
                            a[..., 3 * W + 256:]], axis=-1)


def _pad_rows(a, rows):
    return jnp.pad(a, [(0, rows - a.shape[0])] + [(0, 0)] * (a.ndim - 1))


def _mixer(x, shift, s0, past, wts, tm):
    nb, T, _ = x.shape
    (r, lw, kh, v, kkn, bb, g, bonus, last) = _in_rwkv(
        x, _rwkv_pad_cols(shift), wts["gmix"], wts["w_rwkv"], wts["mu"], wts["w0"], wts["wup"], wts["a0"],
        wts["aup"], wts["gup"], wts["k_k"], wts["k_a"], wts["r_k"], wts["gs"], tm)
    if past is None:
        c0 = jnp.zeros((nb, 1, LANES), F32)
    else:
        k_past, v_past, lf_past = past
        P = k_past.shape[1]
        c_past = _cumsum(_pad_cols(lf_past.astype(F32), LANES), math.gcd(P, 512))
        c0 = c_past[:, P - 1:, :]
    q_aug, k_aug, k_f, v_f, v_b, og, logf = _in_fox(
        x, c0, wts["gmix"], wts["w_fox"], wts["qn"], wts["kn"], wts["fb"], wts["gs"], tm)

    C = RWKV_CHUNK
    Tp = -(-T // C) * C
    if Tp != T:
        padt = lambda a: jnp.pad(a, ((0, 0), (0, Tp - T), (0, 0)))
        r_p, lw_p, kh_p, v_p, kk_p, bb_p = (padt(a) for a in (r, lw, kh, v, kkn, bb))
    else:
        r_p, lw_p, kh_p, v_p, kk_p, bb_p = r, lw, kh, v, kkn, bb
    rh, yh, gm, sh = _rwkv_chunks(r_p, lw_p, kh_p, v_p, kk_p, bb_p)
    y, s_fin = _rwkv_scan(rh, yh, gm, sh, _state_to_pairs(s0.astype(F32)), 4 if nb % 4 == 0 else 1)
    y = y[:, :T]

    if past is None:
        o_att = _fox_attention(q_aug, k_aug, v_b, min(T, FOX_TQ), min(T, FOX_TK), 0)
    else:
        tk = FOX_TK_CACHED
        sk = -(-(P + T) // tk) * tk
        k_aug_past = _fox_aug(_bf(k_past.reshape(nb, P, FOX_WIDTH)), c_past, math.gcd(P, 512))
        k_all = jnp.pad(jnp.concatenate([k_aug_past, k_aug], axis=2), ((0, 0), (0, 0), (0, sk - P - T), (0, 0)))
        v_all = jnp.pad(jnp.concatenate([_bf(v_past.reshape(nb, P, FOX_WIDTH)), v_b], axis=1),
                        ((0, 0), (0, sk - P - T), (0, 0)))
        o_att = _fox_attention(q_aug, k_all, v_all, T, tk, P)

    n = nb * T
    flat = lambda a: a.reshape(n, a.shape[-1])
    feats = (flat(y), flat(bonus), flat(g), flat(o_att), flat(og))
    state = (k_f.reshape(nb, T, N_HEADS, HEAD_DIM), v_f.reshape(nb, T, N_HEADS, HEAD_DIM),
             logf[:, :, :N_HEADS], _pairs_to_state(s_fin), _rwkv_unpad_cols(last))
    return feats, state


def _block_tables(counts):
    blk = MOE_ROWS
    counts = counts.reshape(N_EXPERTS).astype(jnp.int32)
    blk_count = (counts + blk - 1) // blk
    blk_end = jnp.cumsum(blk_count)
    first_blk = blk_end - blk_count
    return ((first_blk * blk).astype(F32).reshape(N_EXPERTS, 1), first_blk.astype(jnp.int32),
            blk_count.astype(jnp.int32), blk_end[-1:].astype(jnp.int32))


def kernel(x_prompt, x_sample, cache_fox_k, cache_fox_v, cache_fox_logf, state_rwkv_wkv, state_rwkv_shift, p_prompt, p_sample, norm_mix_g, w_in, rwkv_mu, rwkv_w0, rwkv_w_up, rwkv_a0, rwkv_a_up, rwkv_g_up, rwkv_k_k, rwkv_k_a, rwkv_r_k, rwkv_ln_w, rwkv_ln_b, fox_q_norm, fox_k_norm, fox_f_bias, w_out, norm_ffn_g, router_w, router_bias, exp_w_gate, exp_w_up, exp_w_down, shared_w_gate, shared_w_up, shared_w_down, ple_norm_g, ple_w_gate, ple_w_proj, final_norm_g):
    assert w_in.shape[0] == 1, "single-layer kernel"
    W = RWKV_WIDTH
    row = lambda a: a.reshape(1, -1).astype(F32)
    tile_heads = lambda a: jnp.tile(a.reshape(1, HEAD_DIM), (1, N_HEADS)).astype(F32)
    hid = jnp.arange(MXU_TILE) // HEAD_DIM
    w_in0 = w_in[0]
    router_hi = _bf(router_w[0])
    wts = {
        "gmix": row(norm_mix_g[0]),
        "w_rwkv": _bf(_rwkv_pad_cols(w_in0[:, :RWKV_IN])),
        "w_fox": _bf(_pad_cols(w_in0[:, RWKV_IN:], FOX_PAD)),
        "mu": row(_rwkv_pad_cols(rwkv_mu[0])),
        "w0": row(rwkv_w0[0]),
        "wup": _bf(_pad_rows(rwkv_w_up[0], 128)),
        "a0": row(rwkv_a0[0]),
        "aup": _bf(_pad_rows(rwkv_a_up[0], 128)),
        "gup": _bf(rwkv_g_up[0]),
        "k_k": row(rwkv_k_k[0]),
        "k_a": row(rwkv_k_a[0]),
        "r_k": row(rwkv_r_k[0]),
        "gs": _bf((hid[:, None] == hid[None, :]).astype(F32)),
        "qn": tile_heads(fox_q_norm[0]),
        "kn": tile_heads(fox_k_norm[0]),
        "fb": _pad_cols(row(fox_f_bias[0]), 128),
    }
    nbp, Tp, _ = x_prompt.shape
    nbs, Ts, _ = x_sample.shape
    s0_prompt = jnp.zeros((nbp, N_HEADS, HEAD_DIM, HEAD_DIM), F32)
    shift0_prompt = jnp.zeros((nbp, 1, RWKV_IN), F32)
    feats_p, st_p = _mixer(x_prompt, shift0_prompt, s0_prompt, None, wts, min(Tp, 256))
    feats_s, st_s = _mixer(x_sample, state_rwkv_shift[0], state_rwkv_wkv[0],
                           (cache_fox_k[0], cache_fox_v[0], cache_fox_logf[0]), wts, Ts)

    n_p, n_s = nbp * Tp, nbs * Ts
    n_tok = n_p + n_s
    tm = math.gcd(math.gcd(n_p, n_s), TOKEN_TILE)
    streams = [(x_prompt.reshape(n_p, D_MODEL), x_sample.reshape(n_s, D_MODEL))] + list(zip(feats_p, feats_s))
    h1, xn2, scores = _out_proj(streams, row(rwkv_ln_w[0]), row(rwkv_ln_b[0]), wts["gs"], _bf(w_out[0]),
                                row(norm_ffn_g[0]), router_hi,
                                _bf(router_w[0] - router_hi.astype(F32)), tm)
    eidx_t, wts_t, rank_t, counts = _route(scores, router_bias[0].reshape(N_EXPERTS, 1).astype(F32), tm)
    n_blocks = -(-n_tok * TOP_K // MOE_ROWS) + N_EXPERTS
    start_col, first_blk, blk_count, n_used = _block_tables(counts)
    pos_t = _positions(eidx_t, rank_t, start_col, tm)
    xs, h1s = _dispatch(pos_t, xn2, h1, _bf(shared_w_gate[0]), _bf(shared_w_up[0]), _bf(shared_w_down[0]),
                        n_blocks * MOE_ROWS, tm)
    y_rows = _gmm(first_blk, blk_count, n_used, xs, exp_w_gate[0], exp_w_up[0], exp_w_down[0])
    p_pair = (p_prompt[0].reshape(n_p, PLE_DIM), p_sample[0].reshape(n_s, PLE_DIM))
    y_p, y_s = _final(pos_t, h1s, wts_t.T, p_pair, y_rows, row(ple_norm_g[0]), _bf(ple_w_gate[0]),
                      _bf(ple_w_proj[0]), row(final_norm_g), tm)
    y_prompt = y_p.reshape(nbp, Tp, D_MODEL)
    y_sample = y_s.reshape(nbs, Ts, D_MODEL)
    lead = lambda t: tuple(a[None] for a in t)
    return (y_prompt, y_sample) + lead(st_p) + lead(st_s)
```

```python
import functools
import math

import numpy as np
import jax
import jax.numpy as jnp
from jax import lax
from jax.experimental import pallas as pl
from jax.experimental.pallas import tpu as pltpu

F32 = jnp.float32
BF16 = jnp.bfloat16

D_MODEL = 1024
HEAD_DIM = 64
RWKV_WIDTH = 512
FOX_WIDTH = 512
N_HEADS = 8
N_PAIRS = N_HEADS // 2
DECAY_RANK = 64
ICL_RANK = 64
GATE_RANK = 128
RWKV_IN = 3 * RWKV_WIDTH + DECAY_RANK + ICL_RANK + GATE_RANK
RWKV_PAD = 3 * RWKV_WIDTH + 3 * 128
FOX_PAD = 4 * FOX_WIDTH + 128
ATTN_SCALE = HEAD_DIM ** -0.5
N_EXPERTS = 256
N_GROUPS = 8
GROUP_SIZE = N_EXPERTS // N_GROUPS
TOPK_GROUPS = 4
TOP_K = 8
EXPERT_FF = 256
ROUTED_SCALE = 2.5
PLE_DIM = 256
RMS_EPS = 1e-6
GN_EPS = 64e-5
L2_EPS = 1e-12

LANES = 128
MXU_TILE = 256
ROW_CHUNKS = D_MODEL // LANES
PACK_CHUNKS = ROW_CHUNKS // 2
RWKV_CHUNK = 128
MOE_ROWS = 256
TOKEN_TILE = 256
GMM_AHEAD = 4
GMM_IN_SLOTS = GMM_AHEAD + 2
GMM_OUT_SLOTS = 4
DMA_QUEUES = 2
FOX_TQ, FOX_TK = 256, 1024
FOX_TK_CACHED = 768
VMEM_LIMIT = 56 * 1024 * 1024


def _cparams(*sem):
    return pltpu.CompilerParams(dimension_semantics=sem, vmem_limit_bytes=VMEM_LIMIT)


def _bf(x):
    return x.astype(BF16)


def _dot(a, b):
    return jnp.dot(a, b, preferred_element_type=F32)


def _dot_nt(a, b):
    return lax.dot_general(a, b, (((1,), (1,)), ((), ())), preferred_element_type=F32)


def _split2(x):
    hi = _bf(x)
    return hi, _bf(x - hi.astype(F32))


def _split3(x):
    hi = _bf(x)
    r1 = x - hi.astype(F32)
    mid = _bf(r1)
    return hi, mid, _bf(r1 - mid.astype(F32))


def _dot_x01(x, w01):
    hi, lo = _split2(x)
    slabs = [slice(c, c + MXU_TILE) for c in range(0, x.shape[1], MXU_TILE)]
    return jnp.concatenate([_dot(hi[:, s], w01) + _dot(lo[:, s], w01) for s in slabs], axis=1)


def _dot3(a, b):
    ah, al = _split2(a)
    bh, bl = _split2(b)
    return _dot(ah, bh) + _dot(al, bh) + _dot(ah, bl)


def _dot3_nt(a, b):
    ah, al = _split2(a)
    bh, bl = _split2(b)
    return _dot_nt(ah, bh) + _dot_nt(al, bh) + _dot_nt(ah, bl)


def _softplus(x):
    return jnp.maximum(x, 0.0) + jnp.log1p(jnp.exp(-jnp.abs(x)))


def _rms(x, g):
    return x * lax.rsqrt(jnp.mean(x * x, axis=-1, keepdims=True) + RMS_EPS) * g


def _full(shape):
    return pl.BlockSpec(shape, lambda *_: (0,) * len(shape))


def _store_chunked(ref, x):
    n = x.shape[0]
    for s in range(ROW_CHUNKS):
        ref[pl.ds(s, n, stride=ROW_CHUNKS), :] = x[:, s * LANES:(s + 1) * LANES]


def _load_chunked(ref, n):
    return jnp.concatenate([ref[pl.ds(s, n, stride=ROW_CHUNKS), :] for s in range(ROW_CHUNKS)], axis=1)


def _store_packed(ref, x):
    n, half = x.shape[0], D_MODEL // 2
    bits = lax.bitcast_convert_type(_bf(x).astype(F32), jnp.uint32)
    packed = (bits[:, :half] >> 16) | (bits[:, half:] & jnp.uint32(0xFFFF0000))
    for s in range(PACK_CHUNKS):
        ref[pl.ds(s, n, stride=PACK_CHUNKS), :] = packed[:, s * LANES:(s + 1) * LANES]


def _load_packed(ref, n):
    packed = jnp.concatenate([ref[pl.ds(s, n, stride=PACK_CHUNKS), :] for s in range(PACK_CHUNKS)], axis=1)
    lo = lax.bitcast_convert_type(packed << 16, F32)
    hi = lax.bitcast_convert_type(packed & jnp.uint32(0xFFFF0000), F32)
    return _bf(jnp.concatenate([lo, hi], axis=1))


def _in_rwkv_kernel(x_ref, shift_ref, gmix_ref, w_ref, mu_ref, w0_ref, wup_ref, a0_ref, aup_ref, gup_ref,
                    kk_ref, ka_ref, rk_ref, gs_ref,
                    r_out, lw_out, kh_out, v_out, kkn_out, bb_out, g_out, bonus_out, last_out,
                    carry_ref):
    j = pl.program_id(1)
    tm = x_ref.shape[1]
    xn = _bf(_rms(x_ref[0], gmix_ref[...]))
    u = _dot(xn, w_ref[...])
    first = jnp.where(j == 0, shift_ref[0], carry_ref[...])
    row = lax.broadcasted_iota(jnp.int32, (tm, 1), 0)
    prev = jnp.where(row == 0, first, pltpu.roll(u, 1, axis=0))
    carry_ref[...] = u[tm - 1:tm, :]
    last_out[0] = u[tm - 1:tm, :]
    xs = u + (prev - u) * mu_ref[...]
    W = RWKV_WIDTH
    r, k, v = xs[:, :W], xs[:, W:2 * W], xs[:, 2 * W:3 * W]
    xw, xa, xg = xs[:, 3 * W:3 * W + 128], xs[:, 3 * W + 128:3 * W + 256], xs[:, 3 * W + 256:]
    w_raw = w0_ref[...] + _dot(_bf(jnp.tanh(xw)), wup_ref[...])
    lw = -jnp.exp(-_softplus(-w_raw) - 0.5)
    a = jax.nn.sigmoid(a0_ref[...] + _dot(_bf(xa), aup_ref[...]))
    g = _dot(_bf(jax.nn.sigmoid(xg)), gup_ref[...])
    gs = gs_ref[...]
    kk = k * kk_ref[...]
    kkn = kk / jnp.maximum(jnp.sqrt(_dot_x01(kk * kk, gs)), L2_EPS)
    kh = k * (1.0 + (a - 1.0) * ka_ref[...])
    r_out[0] = r
    lw_out[0] = lw
    kh_out[0] = kh
    v_out[0] = v
    kkn_out[0] = kkn
    bb_out[0] = kkn * a
    g_out[0] = g
    bonus_out[0] = _dot_x01(r * kh * rk_ref[...], gs) * v


def _in_rwkv(x, shift, gmix, w, mu, w0, wup, a0, aup, gup, k_k, k_a, r_k, gs, tm):
    nb, T, _ = x.shape
    W = RWKV_WIDTH
    tok = lambda width: pl.BlockSpec((1, tm, width), lambda b, j: (b, j, 0))
    outs = [jax.ShapeDtypeStruct((nb, T, W), F32)] * 8 + [jax.ShapeDtypeStruct((nb, 1, RWKV_PAD), F32)]
    return pl.pallas_call(
        _in_rwkv_kernel,
        grid=(nb, T // tm),
        in_specs=[tok(D_MODEL), pl.BlockSpec((1, 1, RWKV_PAD), lambda b, j: (b, 0, 0)),
                  _full((1, D_MODEL)), _full((D_MODEL, RWKV_PAD)), _full((1, RWKV_PAD)),
                  _full((1, W)), _full((128, W)), _full((1, W)), _full((128, W)), _full((128, W)),
                  _full((1, W)), _full((1, W)), _full((1, W)), _full((MXU_TILE, MXU_TILE))],
        out_specs=[tok(W)] * 8 + [pl.BlockSpec((1, 1, RWKV_PAD), lambda b, j: (b, 0, 0))],
        out_shape=outs,
        scratch_shapes=[pltpu.VMEM((1, RWKV_PAD), F32)],
        compiler_params=_cparams("parallel", "arbitrary"),
        name="in_rwkv",
    )(x, shift, gmix, w, mu, w0, wup, a0, aup, gup, k_k, k_a, r_k, gs)


def _aug_rows(x, c, is_query):
    lane = lax.broadcasted_iota(jnp.int32, (1, LANES), 1)
    rows = []
    for p in range(N_PAIRS):
        xp = x[:, p * LANES:(p + 1) * LANES]
        xr = pltpu.roll(xp, HEAD_DIM, axis=1)
        for h in range(2):
            ch = c[:, 2 * p + h:2 * p + h + 1]
            hi = _bf(ch).astype(F32)
            r1 = ch - hi
            mid = _bf(r1).astype(F32)
            lo = _bf(r1 - mid).astype(F32)
            one = jnp.ones_like(ch)
            cols = (hi, mid, lo, one, one, one) if is_query else (one, one, one, -hi, -mid, -lo)
            aug = jnp.zeros_like(xp)
            for n, col in enumerate(cols):
                aug = jnp.where(lane == HEAD_DIM + n, col, aug)
            rows.append(_bf(jnp.where(lane < HEAD_DIM, xp if h == 0 else xr, aug)))
    return rows


def _in_fox_kernel(x_ref, gmix_ref, w_ref, qn_ref, kn_ref, fb_ref, gs_ref, tri_ref, c0_ref,
                   q_out, ka_out, k_out, v_out, vb_out, og_out, lf_out, carry_ref):
    j = pl.program_id(1)
    tm = x_ref.shape[1]

    @pl.when(j == 0)
    def _():
        carry_ref[...] = c0_ref[0]

    xn = _bf(_rms(x_ref[0], gmix_ref[...]))
    u = _dot(xn, w_ref[...])
    W = FOX_WIDTH
    q, k, v, og, fl = u[:, :W], u[:, W:2 * W], u[:, 2 * W:3 * W], u[:, 3 * W:4 * W], u[:, 4 * W:]
    gs = gs_ref[...]
    inv_d = 1.0 / HEAD_DIM
    qn = q * lax.rsqrt(_dot_x01(q * q, gs) * inv_d + RMS_EPS) * qn_ref[...]
    kn = k * lax.rsqrt(_dot_x01(k * k, gs) * inv_d + RMS_EPS) * kn_ref[...]
    logf = -_softplus(-(fl + fb_ref[...]))
    hi, mid, lo = _split3(logf)
    tri = tri_ref[...]
    c = _dot(tri, hi) + _dot(tri, mid) + _dot(tri, lo) + carry_ref[...]
    carry_ref[...] = c[tm - 1:tm, :]
    for h, row in enumerate(_aug_rows(_bf(qn * ATTN_SCALE).astype(F32), c, True)):
        q_out[0, h] = row
    for h, row in enumerate(_aug_rows(_bf(kn).astype(F32), c, False)):
        ka_out[0, h] = row
    k_out[0] = kn
    v_out[0] = v
    vb_out[0] = _bf(v)
    og_out[0] = og
    lf_out[0] = logf


def _in_fox(x, c0, gmix, w, qn, kn, fb, gs, tm):
    nb, T, _ = x.shape
    W = FOX_WIDTH
    tok = lambda width: pl.BlockSpec((1, tm, width), lambda b, j: (b, j, 0))
    aug = pl.BlockSpec((1, N_HEADS, tm, LANES), lambda b, j: (b, 0, j, 0))
    sds = lambda width, dt: jax.ShapeDtypeStruct((nb, T, width), dt)
    aug_shape = jax.ShapeDtypeStruct((nb, N_HEADS, T, LANES), BF16)
    tri = _bf(jnp.tril(jnp.ones((tm, tm), F32)))
    return pl.pallas_call(
        _in_fox_kernel,
        grid=(nb, T // tm),
        in_specs=[tok(D_MODEL), _full((1, D_MODEL)), _full((D_MODEL, FOX_PAD)),
                  _full((1, W)), _full((1, W)), _full((1, 128)), _full((MXU_TILE, MXU_TILE)), _full((tm, tm)),
                  pl.BlockSpec((1, 1, LANES), lambda b, j: (b, 0, 0))],
        out_specs=[aug, aug] + [tok(W)] * 4 + [tok(128)],
        out_shape=[aug_shape, aug_shape, sds(W, F32), sds(W, F32), sds(W, BF16), sds(W, F32), sds(128, F32)],
        scratch_shapes=[pltpu.VMEM((1, LANES), F32)],
        compiler_params=_cparams("parallel", "arbitrary"),
        name="in_fox",
    )(x, gmix, w, qn, kn, fb, gs, tri, c0)


def _cumsum_kernel(x_ref, tri_ref, o_ref, carry_ref):
    j = pl.program_id(1)
    ts = x_ref.shape[1]

    @pl.when(j == 0)
    def _():
        carry_ref[...] = jnp.zeros_like(carry_ref)

    hi, mid, lo = _split3(x_ref[0])
    tri = tri_ref[...]
    c = _dot(tri, hi) + _dot(tri, mid) + _dot(tri, lo) + carry_ref[...]
    o_ref[0] = c
    carry_ref[...] = c[ts - 1:ts, :]


def _cumsum(x, ts):
    nb, S, L = x.shape
    tri = _bf(jnp.tril(jnp.ones((ts, ts), F32)))
    return pl.pallas_call(
        _cumsum_kernel,
        grid=(nb, S // ts),
        in_specs=[pl.BlockSpec((1, ts, L), lambda b, j: (b, j, 0)), _full((ts, ts))],
        out_specs=pl.BlockSpec((1, ts, L), lambda b, j: (b, j, 0)),
        out_shape=jax.ShapeDtypeStruct((nb, S, L), F32),
        scratch_shapes=[pltpu.VMEM((1, L), F32)],
        compiler_params=_cparams("parallel", "arbitrary"),
        name="seq_cumsum",
    )(x, tri)


def _rwkv_chunk_kernel(r_ref, lw_ref, kh_ref, v_ref, kk_ref, bb_ref, tri_ref,
                       rh_out, yh_out, g_out, sh_out):
    C = r_ref.shape[1]
    ti = lax.broadcasted_iota(jnp.int32, (C, C), 0)
    si = lax.broadcasted_iota(jnp.int32, (C, C), 1)
    tx = ti ^ si
    strict = ti > si
    incl = ti >= si
    eye_c = (ti == si).astype(F32)
    lane = lax.broadcasted_iota(jnp.int32, (1, LANES), 1)
    head0 = lane < HEAD_DIM
    pi = lax.broadcasted_iota(jnp.int32, (LANES, LANES), 0)
    pj = lax.broadcasted_iota(jnp.int32, (LANES, LANES), 1)
    same_head = (pi < HEAD_DIM) == (pj < HEAD_DIM)
    eye_p = (pi == pj).astype(F32)
    tri = tri_ref[...]

    pairs = range(N_PAIRS)
    heads = [(p, h) for p in pairs for h in range(2)]
    P = []
    for p in pairs:
        sl = slice(p * LANES, (p + 1) * LANES)
        r, lw, kh, v, kk, bb = (ref[0, :, sl] for ref in (r_ref, lw_ref, kh_ref, v_ref, kk_ref, bb_ref))
        l_hi, l_mid, l_lo = _split3(lw)
        lc = _dot(tri, l_hi) + _dot(tri, l_mid) + _dot(tri, l_lo)
        mid = lc[C // 2 - 1:C // 2, :]
        last = lc[C - 1:C, :]
        e_dn = jnp.exp(mid - lc)
        e_end = jnp.exp(last - lc)
        aa = kk * jnp.exp(lc - lw - mid)
        P.append(dict(sl=sl, v=v, aa=aa, rt=r * jnp.exp(lc - mid), rho=jnp.exp(mid), g_last=jnp.exp(last),
                      bt_b=_bf(bb * e_dn), kt_b=_bf(kh * e_dn), bc_b=_bf(bb * e_end), kc_b=_bf(kh * e_end),
                      aa_b=_bf(aa), v_b=_bf(v)))
    lab, lak, mrb, mrk = {}, {}, {}, {}
    for p, h in heads:
        q = P[p]
        hm = head0 if h == 0 else jnp.logical_not(head0)
        aa_m = _bf(jnp.where(hm, q["aa"], 0.0))
        rt_m = _bf(jnp.where(hm, q["rt"], 0.0))
        lab[p, h] = jnp.where(strict, _dot_nt(aa_m, q["bt_b"]), 0.0)
        lak[p, h] = _bf(jnp.where(strict, _dot_nt(aa_m, q["kt_b"]), 0.0))
        mrb[p, h] = _bf(jnp.where(incl, _dot_nt(rt_m, q["bt_b"]), 0.0))
        mrk[p, h] = _bf(jnp.where(incl, _dot_nt(rt_m, q["kt_b"]), 0.0))
    d = {k: eye_c - jnp.where(tx < 2, lab[k], 0.0) for k in heads}
    s = 2
    while s < C:
        level = (tx >= s) & (tx < 2 * s)
        d_b = {k: _bf(d[k]) for k in heads}
        t1 = {k: _bf(_dot(d_b[k], _bf(jnp.where(level, lab[k], 0.0)))) for k in heads}
        d = {k: d[k] - _dot(t1[k], d_b[k]) for k in heads}
        s *= 2
    d_b = {k: _bf(d[k]) for k in heads}
    w = {k: _bf(_dot(lak[k], P[k[0]]["v_b"])) for k in heads}
    ah = {k: _dot(d_b[k], P[k[0]]["aa_b"]) * P[k[0]]["rho"] for k in heads}
    uh = {k: _dot(d_b[k], w[k]) for k in heads}
    rh = {k: P[k[0]]["rt"] * P[k[0]]["rho"] - _dot(mrb[k], _bf(ah[k])) for k in heads}
    yh = {k: _dot(mrk[k], P[k[0]]["v_b"]) - _dot(mrb[k], _bf(uh[k])) for k in heads}
    for p in pairs:
        q = P[p]
        both = lambda x: jnp.where(head0, x[p, 0], x[p, 1])
        ah_p, uh_p = both(ah), both(uh)
        rh_out[0, :, q["sl"]] = both(rh)
        yh_out[0, :, q["sl"]] = both(yh)
        g_full = eye_p * q["g_last"] - _dot(_bf(ah_p.T), q["bc_b"])
        sh_full = _dot(_bf(q["v"].T), q["kc_b"]) - _dot(_bf(uh_p.T), q["bc_b"])
        g_out[0, 0, p] = jnp.where(same_head, g_full, 0.0)
        sh_out[0, 0, p] = jnp.where(same_head, sh_full, 0.0)


def _rwkv_chunks(r, lw, kh, v, kk, bb):
    nb, T, W = r.shape
    C = RWKV_CHUNK
    nc = T // C
    tri = _bf(jnp.tril(jnp.ones((C, C), F32)))
    tok = pl.BlockSpec((1, C, W), lambda b, c: (b, c, 0))
    mat = pl.BlockSpec((1, 1, N_PAIRS, LANES, LANES), lambda b, c: (b, c, 0, 0, 0))
    mat_shape = jax.ShapeDtypeStruct((nb, nc, N_PAIRS, LANES, LANES), F32)
    return pl.pallas_call(
        _rwkv_chunk_kernel,
        grid=(nb, nc),
        in_specs=[tok] * 6 + [_full((C, C))],
        out_specs=[tok, tok, mat, mat],
        out_shape=[jax.ShapeDtypeStruct((nb, T, W), F32)] * 2 + [mat_shape] * 2,
        compiler_params=_cparams("parallel", "parallel"),
        name="rwkv_chunks",
    )(r, lw, kh, v, kk, bb, tri)


def _rwkv_scan_kernel(rh_ref, yh_ref, g_ref, sh_ref, s0_ref, y_out, sfin_out, s_scr):
    c = pl.program_id(1)
    nbg = rh_ref.shape[0]

    @pl.when(c == 0)
    def _():
        s_scr[...] = s0_ref[...]

    for b in range(nbg):
        for p in range(N_PAIRS):
            sl = slice(p * LANES, (p + 1) * LANES)
            s = s_scr[b, p]
            y_out[b, :, sl] = _dot3_nt(rh_ref[b, :, sl], s) + yh_ref[b, :, sl]
            s_new = _dot3(s, g_ref[b, 0, p]) + sh_ref[b, 0, p]
            s_scr[b, p] = s_new
            sfin_out[b, p] = s_new


def _rwkv_scan(rh, yh, g, sh, s0, nbg):
    nb, T, W = rh.shape
    C = RWKV_CHUNK
    nc = T // C
    tok = pl.BlockSpec((nbg, C, W), lambda i, c: (i, c, 0))
    mat = pl.BlockSpec((nbg, 1, N_PAIRS, LANES, LANES), lambda i, c: (i, c, 0, 0, 0))
    st = pl.BlockSpec((nbg, N_PAIRS, LANES, LANES), lambda i, c: (i, 0, 0, 0))
    return pl.pallas_call(
        _rwkv_scan_kernel,
        grid=(nb // nbg, nc),
        in_specs=[tok, tok, mat, mat, st],
        out_specs=[tok, st],
        out_shape=[jax.ShapeDtypeStruct((nb, T, W), F32),
                   jax.ShapeDtypeStruct((nb, N_PAIRS, LANES, LANES), F32)],
        scratch_shapes=[pltpu.VMEM((nbg, N_PAIRS, LANES, LANES), F32)],
        compiler_params=_cparams("parallel", "arbitrary"),
        name="rwkv_scan",
    )(rh, yh, g, sh, s0)


def _state_to_pairs(s):
    nb = s.shape[0]
    s = s.reshape(nb, N_PAIRS, 2, HEAD_DIM, HEAD_DIM)
    z = jnp.zeros_like(s[:, :, 0])
    top = jnp.concatenate([s[:, :, 0], z], axis=-1)
    bot = jnp.concatenate([z, s[:, :, 1]], axis=-1)
    return jnp.concatenate([top, bot], axis=-2)


def _pairs_to_state(sp):
    nb = sp.shape[0]
    a = sp[:, :, :HEAD_DIM, :HEAD_DIM]
    b = sp[:, :, HEAD_DIM:, HEAD_DIM:]
    return jnp.stack([a, b], axis=2).reshape(nb, N_HEADS, HEAD_DIM, HEAD_DIM)


def _fox_aug_kernel(x_ref, c_ref, o_ref):
    for h, row in enumerate(_aug_rows(x_ref[0].astype(F32), c_ref[0], False)):
        o_ref[0, h] = row


def _fox_aug(x, c, tm):
    nb, S, W = x.shape
    return pl.pallas_call(
        _fox_aug_kernel,
        grid=(nb, S // tm),
        in_specs=[pl.BlockSpec((1, tm, W), lambda b, j: (b, j, 0)),
                  pl.BlockSpec((1, tm, LANES), lambda b, j: (b, j, 0))],
        out_specs=pl.BlockSpec((1, N_HEADS, tm, LANES), lambda b, j: (b, 0, j, 0)),
        out_shape=jax.ShapeDtypeStruct((nb, N_HEADS, S, LANES), BF16),
        compiler_params=_cparams("parallel", "parallel"),
        name="fox_aug",
    )(x, c)


def _fox_kernel(q_ref, k_ref, v_ref, o_ref, m_scr, l_scr, acc_scr, sa_scr, sb_scr, *, tk, q_off):
    i = pl.program_id(2)
    tq = q_ref.shape[2]
    q_start = q_off + i * tq
    m_scr[...] = jnp.full_like(m_scr, -jnp.inf)
    l_scr[...] = jnp.zeros_like(l_scr)
    acc_scr[...] = jnp.zeros_like(acc_scr)
    n_full = (q_start + 1) // tk

    def scores_into(j, dst):
        ks = pl.multiple_of(j * tk, tk)
        for h in range(2):
            dst[h] = _dot_nt(q_ref[0, h], k_ref[0, h, pl.ds(ks, tk), :])

    def update_from(j, src, masked):
        ks = pl.multiple_of(j * tk, tk)
        v = v_ref[0, pl.ds(ks, tk), :]
        if masked:
            visible = (ks + lax.broadcasted_iota(jnp.int32, (1, tk), 1)) <= (
                q_start + lax.broadcasted_iota(jnp.int32, (tq, 1), 0))
        for h in range(2):
            s = src[h]
            if masked:
                s = jnp.where(visible, s, -jnp.inf)
            m_old = m_scr[h]
            m_new = jnp.maximum(m_old, jnp.max(s, axis=-1, keepdims=True))
            alpha = jnp.exp(m_old - m_new)
            pr = jnp.exp(s - m_new)
            l_scr[h] = alpha * l_scr[h] + jnp.sum(pr, axis=-1, keepdims=True)
            acc_scr[h] = alpha * acc_scr[h] + _dot(_bf(pr), v)
            m_scr[h] = m_new

    def step(j, src, dst):
        scores_into(j + 1, dst)
        update_from(j, src, False)

    def two_steps(jj, carry):
        step(2 * jj, sa_scr, sb_scr)
        step(2 * jj + 1, sb_scr, sa_scr)
        return carry

    scores_into(0, sa_scr)
    lax.fori_loop(0, n_full // 2, two_steps, 0)
    odd = n_full % 2 == 1

    @pl.when(odd)
    def _():
        step(n_full - 1, sa_scr, sb_scr)
        update_from(n_full, sb_scr, True)

    @pl.when(jnp.logical_not(odd))
    def _():
        update_from(n_full, sa_scr, True)

    head0 = lax.broadcasted_iota(jnp.int32, (1, LANES), 1) < HEAD_DIM
    o_ref[0] = jnp.where(head0, acc_scr[0] / l_scr[0], acc_scr[1] / l_scr[1])


def _fox_attention(q, k, v, tq, tk, q_off):
    nb, _, sq, _ = q.shape
    sk = k.shape[2]
    for q_start in range(q_off, q_off + sq, tq):
        assert (q_start + 1) // tk + 1 == -(-(q_start + tq) // tk) <= sk // tk, (q_start, tq, tk)
    return pl.pallas_call(
        functools.partial(_fox_kernel, tk=tk, q_off=q_off),
        grid=(nb, N_PAIRS, sq // tq),
        in_specs=[pl.BlockSpec((1, 2, tq, LANES), lambda b, p, i: (b, p, i, 0)),
                  pl.BlockSpec((1, 2, sk, LANES), lambda b, p, i: (b, p, 0, 0)),
                  pl.BlockSpec((1, sk, LANES), lambda b, p, i: (b, 0, p))],
        out_specs=pl.BlockSpec((1, tq, LANES), lambda b, p, i: (b, i, p)),
        out_shape=jax.ShapeDtypeStruct((nb, sq, FOX_WIDTH), F32),
        scratch_shapes=[pltpu.VMEM((2, tq, 1), F32), pltpu.VMEM((2, tq, 1), F32),
                        pltpu.VMEM((2, tq, LANES), F32), pltpu.VMEM((2, tq, tk), F32),
                        pltpu.VMEM((2, tq, tk), F32)],
        compiler_params=_cparams("parallel", "parallel", "arbitrary"),
        name="fox_attention",
    )(q, k, v)


def _out_kernel(*refs, n_first):
    tok_refs, rest = refs[:12], refs[12:]
    lnw_ref, lnb_ref, gs_ref, wout_ref, gffn_ref, rwh_ref, rwl_ref, h_out, xn_out, sc_out = rest
    first = pl.program_id(0) < n_first
    x, y, bonus, g, oa, og = (jnp.where(first, tok_refs[2 * n][...], tok_refs[2 * n + 1][...]) for n in range(6))
    gs = gs_ref[...]
    inv_d = 1.0 / HEAD_DIM
    mean = _dot_x01(y, gs) * inv_d
    d = y - mean
    var = _dot_x01(d * d, gs) * inv_d
    yn = d * lax.rsqrt(var + GN_EPS) * lnw_ref[...] + lnb_ref[...]
    o_rwkv = (yn + bonus) * g
    o_fox = oa * jax.nn.sigmoid(og)
    mix = jnp.concatenate([_bf(o_rwkv), _bf(o_fox)], axis=-1)
    h = x + _dot(mix, wout_ref[...])
    h_out[...] = h
    xn = _rms(h, gffn_ref[...])
    _store_packed(xn_out, xn)
    xh, xl = _split2(xn)
    logits = _dot(xh, rwh_ref[...]) + _dot(xl, rwh_ref[...]) + _dot(xh, rwl_ref[...])
    sc_out[...] = jax.nn.sigmoid(logits)


def _two_streams(width, tm, n_first):
    return [pl.BlockSpec((tm, width), lambda i: (jnp.minimum(i, n_first - 1), 0)),
            pl.BlockSpec((tm, width), lambda i: (jnp.maximum(i - n_first, 0), 0))]


def _out_proj(streams, lnw, lnb, gs, wout, gffn, rwh, rwl, tm):
    n_first = streams[0][0].shape[0] // tm
    T = streams[0][0].shape[0] + streams[0][1].shape[0]
    W = RWKV_WIDTH
    tok = lambda width: pl.BlockSpec((tm, width), lambda i: (i, 0))
    tok_specs, tok_args = [], []
    for a, b in streams:
        tok_specs += _two_streams(a.shape[1], tm, n_first)
        tok_args += [a, b]
    return pl.pallas_call(
        functools.partial(_out_kernel, n_first=n_first),
        grid=(T // tm,),
        in_specs=tok_specs + [_full((1, W)), _full((1, W)), _full((MXU_TILE, MXU_TILE)),
                              _full((D_MODEL, D_MODEL)), _full((1, D_MODEL)),
                              _full((D_MODEL, N_EXPERTS)), _full((D_MODEL, N_EXPERTS))],
        out_specs=[tok(D_MODEL), pl.BlockSpec((tm * PACK_CHUNKS, LANES), lambda i: (i, 0)), tok(N_EXPERTS)],
        out_shape=[jax.ShapeDtypeStruct((T, D_MODEL), F32),
                   jax.ShapeDtypeStruct((T * PACK_CHUNKS, LANES), jnp.uint32),
                   jax.ShapeDtypeStruct((T, N_EXPERTS), F32)],
        compiler_params=_cparams("parallel"),
        name="out_proj",
    )(*tok_args, lnw, lnb, gs, wout, gffn, rwh, rwl)


def _route_kernel(sc_ref, bias_ref, before_ref, idx_out, wt_out, rank_out, cnt_out, cnt_scr):
    tm = sc_ref.shape[0]
    neg = -jnp.inf

    @pl.when(pl.program_id(0) == 0)
    def _():
        cnt_scr[...] = jnp.zeros_like(cnt_scr)

    st = sc_ref[...].T
    sel = st + bias_ref[...]
    gscore = []
    for gi in range(N_GROUPS):
        blk = sel[gi * GROUP_SIZE:(gi + 1) * GROUP_SIZE, :]
        m1 = jnp.max(blk, axis=0, keepdims=True)
        n1 = jnp.sum((blk == m1).astype(F32), axis=0, keepdims=True)
        m2 = jnp.max(jnp.where(blk < m1, blk, neg), axis=0, keepdims=True)
        gscore.append(m1 + jnp.where(n1 > 1.0, m1, m2))
    taken = [jnp.zeros((1, tm), jnp.bool_) for _ in range(N_GROUPS)]
    for _ in range(TOPK_GROUPS):
        avail = [jnp.where(taken[gi], neg, gscore[gi]) for gi in range(N_GROUPS)]
        best = functools.reduce(jnp.maximum, avail)
        found = jnp.zeros((1, tm), jnp.bool_)
        for gi in range(N_GROUPS):
            hit = (avail[gi] == best) & jnp.logical_not(found)
            taken[gi] = taken[gi] | hit
            found = found | hit
    cand = jnp.concatenate(
        [jnp.where(taken[gi], sel[gi * GROUP_SIZE:(gi + 1) * GROUP_SIZE, :], neg) for gi in range(N_GROUPS)], axis=0)
    eid = lax.broadcasted_iota(jnp.int32, (N_EXPERTS, tm), 0).astype(F32)
    idxs, wts = [], []
    onehot = jnp.zeros((N_EXPERTS, tm), F32)
    for _ in range(TOP_K):
        best = jnp.max(cand, axis=0, keepdims=True)
        pick = jnp.min(jnp.where(cand == best, eid, float(N_EXPERTS)), axis=0, keepdims=True)
        chosen = eid == pick
        wts.append(jnp.sum(jnp.where(chosen, st, 0.0), axis=0, keepdims=True))
        idxs.append(pick)
        cand = jnp.where(chosen, neg, cand)
        onehot = jnp.where(chosen, 1.0, onehot)
    w = jnp.concatenate(wts, axis=0)
    idx_out[...] = jnp.concatenate(idxs, axis=0).astype(jnp.int32)
    wt_out[...] = w / jnp.sum(w, axis=0, keepdims=True) * ROUTED_SCALE
    earlier = _dot(_bf(onehot), before_ref[...]) + cnt_scr[...]
    rank_out[...] = jnp.concatenate(
        [jnp.sum(jnp.where(eid == pick, earlier, 0.0), axis=0, keepdims=True) for pick in idxs],
        axis=0).astype(jnp.int32)
    cnt_scr[...] += jnp.sum(onehot, axis=1, keepdims=True)
    cnt_out[...] = cnt_scr[...]


def _route(scores, bias_col, tm):
    T = scores.shape[0]
    before = _bf(jnp.triu(jnp.ones((tm, tm), F32), 1))
    tok = pl.BlockSpec((TOP_K, tm), lambda i: (0, i))
    return pl.pallas_call(
        _route_kernel,
        grid=(T // tm,),
        in_specs=[pl.BlockSpec((tm, N_EXPERTS), lambda i: (i, 0)), _full((N_EXPERTS, 1)), _full((tm, tm))],
        out_specs=[tok, tok, tok, _full((N_EXPERTS, 1))],
        out_shape=[jax.ShapeDtypeStruct((TOP_K, T), jnp.int32), jax.ShapeDtypeStruct((TOP_K, T), F32),
                   jax.ShapeDtypeStruct((TOP_K, T), jnp.int32), jax.ShapeDtypeStruct((N_EXPERTS, 1), F32)],
        scratch_shapes=[pltpu.VMEM((N_EXPERTS, 1), F32)],
        compiler_params=_cparams("arbitrary"),
        name="route",
    )(scores, bias_col, before)


def _pos_kernel(idx_ref, rank_ref, start_ref, pos_out):
    tm = idx_ref.shape[1]
    eid = lax.broadcasted_iota(jnp.int32, (N_EXPERTS, tm), 0)
    idx = idx_ref[...]
    start = start_ref[...]
    base = jnp.concatenate(
        [jnp.sum(jnp.where(eid == idx[k:k + 1, :], start, 0.0), axis=0, keepdims=True) for k in range(TOP_K)],
        axis=0)
    pos_out[...] = rank_ref[...] + base.astype(jnp.int32)


def _positions(eidx_t, rank_t, start_col, tm):
    T = eidx_t.shape[1]
    tok = pl.BlockSpec((TOP_K, tm), lambda i: (0, i))
    return pl.pallas_call(
        _pos_kernel,
        grid=(T // tm,),
        in_specs=[tok, tok, _full((N_EXPERTS, 1))],
        out_specs=tok,
        out_shape=jax.ShapeDtypeStruct((TOP_K, T), jnp.int32),
        compiler_params=_cparams("parallel"),
        name="moe_positions",
    )(eidx_t, rank_t, start_col)


def _dispatch_kernel(pos_ref, x_ref, h_ref, sg_ref, su_ref, sd_ref, xs_in, xs_out, hs_out, sem):
    del xs_in
    tm = pos_ref.shape[1]

    def issue(t, carry):
        src = x_ref.at[pl.ds(pl.multiple_of(t * PACK_CHUNKS, PACK_CHUNKS), PACK_CHUNKS), :]
        for k in range(TOP_K):
            row = pl.multiple_of(pos_ref[k, t] * PACK_CHUNKS, PACK_CHUNKS)
            pltpu.make_async_copy(src, xs_out.at[pl.ds(row, PACK_CHUNKS), :], sem).start(priority=k % DMA_QUEUES)
        return carry

    lax.fori_loop(0, tm, issue, 0)
    xb = _load_packed(x_ref, tm)
    hg = _dot(xb, sg_ref[...])
    hu = _dot(xb, su_ref[...])
    hs_out[...] = h_ref[...] + _dot(_bf(hg * jax.nn.sigmoid(hg) * hu), sd_ref[...])
    for k in range(TOP_K):
        pltpu.make_async_copy(x_ref, xs_out.at[pl.ds(0, tm * PACK_CHUNKS), :], sem).wait()


def _dispatch(pos_t, xn_packed, h, sg, su, sd, n_rows, tm):
    T = pos_t.shape[1]
    xs0 = jnp.zeros((n_rows * PACK_CHUNKS, LANES), jnp.uint32)
    return pl.pallas_call(
        _dispatch_kernel,
        grid=(T // tm,),
        in_specs=[pl.BlockSpec((TOP_K, tm), lambda i: (0, i), memory_space=pltpu.SMEM),
                  pl.BlockSpec((tm * PACK_CHUNKS, LANES), lambda i: (i, 0)),
                  pl.BlockSpec((tm, D_MODEL), lambda i: (i, 0)),
                  _full((D_MODEL, EXPERT_FF)), _full((D_MODEL, EXPERT_FF)), _full((EXPERT_FF, D_MODEL)),
                  pl.BlockSpec(memory_space=pl.ANY)],
        out_specs=[pl.BlockSpec(memory_space=pl.ANY), pl.BlockSpec((tm, D_MODEL), lambda i: (i, 0))],
        out_shape=[jax.ShapeDtypeStruct((n_rows * PACK_CHUNKS, LANES), jnp.uint32),
                   jax.ShapeDtypeStruct((T, D_MODEL), F32)],
        scratch_shapes=[pltpu.SemaphoreType.DMA(())],
        input_output_aliases={6: 0},
        compiler_params=_cparams("arbitrary"),
        name="moe_dispatch",
    )(pos_t, xn_packed, h, sg, su, sd, xs0)


def _gmm_kernel(first_ref, count_ref, used_ref, xs_hbm, wg_ref, wu_ref, wd_ref, y_hbm,
                xbuf, ybuf, wg_b, wu_b, wd_b, in_sems, out_sems):
    e = pl.program_id(0)
    bm = MOE_ROWS
    blk_rows = bm * ROW_CHUNKS
    n_used = used_ref[0]
    n_blocks = y_hbm.shape[0] // blk_rows

    in_rows = bm * PACK_CHUNKS

    def read(g, slot):
        return pltpu.make_async_copy(xs_hbm.at[pl.ds(pl.multiple_of(g * in_rows, in_rows), in_rows), :],
                                     xbuf.at[slot], in_sems.at[slot])

    def write(g, slot):
        return pltpu.make_async_copy(ybuf.at[slot],
                                     y_hbm.at[pl.ds(pl.multiple_of(g * blk_rows, blk_rows), blk_rows), :],
                                     out_sems.at[slot])

    @pl.when(e == 0)
    def _():
        for g0 in range(GMM_AHEAD):
            @pl.when(g0 < n_used)
            def _():
                read(g0, g0 % GMM_IN_SLOTS).start()

    wg_b[...] = _bf(wg_ref[0])
    wu_b[...] = _bf(wu_ref[0])
    wd_b[...] = _bf(wd_ref[0])

    def blocks(g, width):
        for j in range(width):
            read(g + j, (g + j) % GMM_IN_SLOTS).wait()
        for j in range(width):
            @pl.when(g + j + GMM_AHEAD < n_used)
            def _():
                read(g + j + GMM_AHEAD, (g + j + GMM_AHEAD) % GMM_IN_SLOTS).start()

            @pl.when(g + j >= GMM_OUT_SLOTS)
            def _():
                write(g + j - GMM_OUT_SLOTS, (g + j) % GMM_OUT_SLOTS).wait()
        for j in range(width):
            xe = _load_packed(xbuf.at[(g + j) % GMM_IN_SLOTS], bm)
            hg = _dot(xe, wg_b[...])
            hu = _dot(xe, wu_b[...])
            _store_chunked(ybuf.at[(g + j) % GMM_OUT_SLOTS], _dot(_bf(hg * jax.nn.sigmoid(hg) * hu), wd_b[...]))
        for j in range(width):
            write(g + j, (g + j) % GMM_OUT_SLOTS).start()

    first, count = first_ref[e], count_ref[e]

    def pair(jj, carry):
        blocks(first + 2 * jj, 2)
        return carry

    lax.fori_loop(0, count // 2, pair, 0)

    @pl.when(count % 2 == 1)
    def _():
        blocks(first + count - 1, 1)

    @pl.when(e == pl.num_programs(0) - 1)
    def _():
        for back in range(GMM_OUT_SLOTS, 0, -1):
            @pl.when(n_used >= back)
            def _():
                write(n_used - back, (n_used - back) % GMM_OUT_SLOTS).wait()
        ybuf[0] = jnp.zeros_like(ybuf[0])

        def fill(g, carry):
            write(g, 0).start()
            return carry

        def drain(g, carry):
            write(g, 0).wait()
            return carry

        lax.fori_loop(n_used, n_blocks, fill, 0)
        lax.fori_loop(n_used, n_blocks, drain, 0)


def _gmm(first_blk, blk_count, n_used, xs, wg, wu, wd):
    bm = MOE_ROWS
    n_rows = xs.shape[0] // PACK_CHUNKS
    wspec = lambda shape: pl.BlockSpec((1,) + shape, lambda e, *_: (e, 0, 0))
    grid_spec = pltpu.PrefetchScalarGridSpec(
        num_scalar_prefetch=3,
        grid=(N_EXPERTS,),
        in_specs=[pl.BlockSpec(memory_space=pl.ANY), wspec((D_MODEL, EXPERT_FF)), wspec((D_MODEL, EXPERT_FF)),
                  wspec((EXPERT_FF, D_MODEL))],
        out_specs=pl.BlockSpec(memory_space=pl.ANY),
        scratch_shapes=[pltpu.VMEM((GMM_IN_SLOTS, bm * PACK_CHUNKS, LANES), jnp.uint32),
                        pltpu.VMEM((GMM_OUT_SLOTS, bm * ROW_CHUNKS, LANES), F32),
                        pltpu.VMEM((D_MODEL, EXPERT_FF), BF16), pltpu.VMEM((D_MODEL, EXPERT_FF), BF16),
                        pltpu.VMEM((EXPERT_FF, D_MODEL), BF16),
                        pltpu.SemaphoreType.DMA((GMM_IN_SLOTS,)), pltpu.SemaphoreType.DMA((GMM_OUT_SLOTS,))],
    )
    return pl.pallas_call(
        _gmm_kernel,
        grid_spec=grid_spec,
        out_shape=jax.ShapeDtypeStruct((n_rows * ROW_CHUNKS, LANES), F32),
        compiler_params=_cparams("arbitrary"),
        name="expert_gmm",
    )(first_blk, blk_count, n_used, xs, wg, wu, wd)


def _final_kernel(pos_ref, nxt_ref, hs_ref, w_ref, pa_ref, pb_ref, ys_ref, gple_ref, wpg_ref, wpp_ref, gfin_ref,
                  ya_out, yb_out, buf, sems, *, n_first):
    i = pl.program_id(0)
    n = pl.num_programs(0)
    tm = hs_ref.shape[0]

    def gather(rows_ref, slot):
        def issue(t, carry):
            dst = pl.ds(pl.multiple_of(t * ROW_CHUNKS, ROW_CHUNKS), ROW_CHUNKS)
            for k in range(TOP_K):
                row = pl.multiple_of(rows_ref[k, t] * ROW_CHUNKS, ROW_CHUNKS)
                pltpu.make_async_copy(ys_ref.at[pl.ds(row, ROW_CHUNKS), :], buf.at[slot, k, dst, :],
                                      sems.at[slot, k]).start(priority=k % DMA_QUEUES)
            return carry

        lax.fori_loop(0, tm, issue, 0)

    slot = i % 2

    @pl.when(i == 0)
    def _():
        gather(pos_ref, 0)

    for nxt in range(2):
        @pl.when((i + 1 < n) & (slot != nxt))
        def _():
            gather(nxt_ref, nxt)

    pp = _dot(_bf(jnp.where(i < n_first, pa_ref[...], pb_ref[...])), wpp_ref[...])
    for k in range(TOP_K):
        pltpu.make_async_copy(ys_ref.at[pl.ds(0, tm * ROW_CHUNKS), :], buf.at[slot, k], sems.at[slot, k]).wait()
    w = w_ref[...]
    wk = [jnp.broadcast_to(w[:, k:k + 1], (tm, LANES)) for k in range(TOP_K)]
    chunks = []
    for s in range(ROW_CHUNKS):
        acc = buf[slot, 0, pl.ds(s, tm, stride=ROW_CHUNKS), :] * wk[0]
        for k in range(1, TOP_K):
            acc = acc + buf[slot, k, pl.ds(s, tm, stride=ROW_CHUNKS), :] * wk[k]
        chunks.append(acc)
    h = hs_ref[...] + jnp.concatenate(chunks, axis=1)
    gate = jax.nn.sigmoid(_dot(_bf(_rms(h, gple_ref[...])), wpg_ref[...]))
    y = _rms(h + gate * pp, gfin_ref[...])

    @pl.when(i < n_first)
    def _():
        ya_out[...] = y

    @pl.when(i >= n_first)
    def _():
        yb_out[...] = y


def _final(pos_t, hs, w, p_pair, ys, gple, wpg, wpp, gfin, tm):
    T = hs.shape[0]
    n = T // tm
    n_first = p_pair[0].shape[0] // tm
    tok = lambda width: pl.BlockSpec((tm, width), lambda i: (i, 0))
    return pl.pallas_call(
        functools.partial(_final_kernel, n_first=n_first),
        grid=(n,),
        in_specs=[pl.BlockSpec((TOP_K, tm), lambda i: (0, i), memory_space=pltpu.SMEM),
                  pl.BlockSpec((TOP_K, tm), lambda i: (0, jnp.minimum(i + 1, n - 1)), memory_space=pltpu.SMEM),
                  tok(D_MODEL), tok(TOP_K)] + _two_streams(PLE_DIM, tm, n_first) + [
                  pl.BlockSpec(memory_space=pl.ANY),
                  _full((1, D_MODEL)), _full((D_MODEL, D_MODEL)), _full((PLE_DIM, D_MODEL)),
                  _full((1, D_MODEL))],
        out_specs=_two_streams(D_MODEL, tm, n_first),
        out_shape=[jax.ShapeDtypeStruct((n_first * tm, D_MODEL), F32),
                   jax.ShapeDtypeStruct((T - n_first * tm, D_MODEL), F32)],
        scratch_shapes=[pltpu.VMEM((2, TOP_K, tm * ROW_CHUNKS, LANES), F32),
                        pltpu.SemaphoreType.DMA((2, TOP_K))],
        compiler_params=_cparams("arbitrary"),
        name="ffn_tail",
    )(pos_t, pos_t, hs, w, *p_pair, ys, gple, wpg, wpp, gfin)


def _pad_cols(a, width):
    return jnp.pad(a, [(0, 0)] * (a.ndim - 1) + [(0, width - a.shape[-1])])


def _rwkv_pad_cols(a):
    W = RWKV_WIDTH
    o1, o2, o3 = 3 * W, 3 * W + DECAY_RANK, 3 * W + DECAY_RANK + ICL_RANK
    return jnp.concatenate([a[..., :o1], _pad_cols(a[..., o1:o2], 128), _pad_cols(a[..., o2:o3], 128),
                            a[..., o3:]], axis=-1)


def _rwkv_unpad_cols(a):
    W = RWKV_WIDTH
    return jnp.concatenate([a[..., :3 * W + DECAY_RANK], a[..., 3 * W + 128:3 * W + 128 + ICL_RANK],
                            a[..., 3 * W + 256:]], axis=-1)


def _pad_rows(a, rows):
    return jnp.pad(a, [(0, rows - a.shape[0])] + [(0, 0)] * (a.ndim - 1))


def _mixer(x, shift, s0, past, wts, tm):
    nb, T, _ = x.shape
    (r, lw, kh, v, kkn, bb, g, bonus, last) = _in_rwkv(
        x, _rwkv_pad_cols(shift), wts["gmix"], wts["w_rwkv"], wts["mu"], wts["w0"], wts["wup"], wts["a0"],
        wts["aup"], wts["gup"], wts["k_k"], wts["k_a"], wts["r_k"], wts["gs"], tm)
    if past is None:
        c0 = jnp.zeros((nb, 1, LANES), F32)
    else:
        k_past, v_past, lf_past = past
        P = k_past.shape[1]
        c_past = _cumsum(_pad_cols(lf_past.astype(F32), LANES), math.gcd(P, 512))
        c0 = c_past[:, P - 1:, :]
    q_aug, k_aug, k_f, v_f, v_b, og, logf = _in_fox(
        x, c0, wts["gmix"], wts["w_fox"], wts["qn"], wts["kn"], wts["fb"], wts["gs"], tm)

    C = RWKV_CHUNK
    Tp = -(-T // C) * C
    if Tp != T:
        padt = lambda a: jnp.pad(a, ((0, 0), (0, Tp - T), (0, 0)))
        r_p, lw_p, kh_p, v_p, kk_p, bb_p = (padt(a) for a in (r, lw, kh, v, kkn, bb))
    else:
        r_p, lw_p, kh_p, v_p, kk_p, bb_p = r, lw, kh, v, kkn, bb
    rh, yh, gm, sh = _rwkv_chunks(r_p, lw_p, kh_p, v_p, kk_p, bb_p)
    y, s_fin = _rwkv_scan(rh, yh, gm, sh, _state_to_pairs(s0.astype(F32)), 4 if nb % 4 == 0 else 1)
    y = y[:, :T]

    if past is None:
        o_att = _fox_attention(q_aug, k_aug, v_b, min(T, FOX_TQ), min(T, FOX_TK), 0)
    else:
        tk = FOX_TK_CACHED
        sk = -(-(P + T) // tk) * tk
        k_aug_past = _fox_aug(_bf(k_past.reshape(nb, P, FOX_WIDTH)), c_past, math.gcd(P, 512))
        k_all = jnp.pad(jnp.concatenate([k_aug_past, k_aug], axis=2), ((0, 0), (0, 0), (0, sk - P - T), (0, 0)))
        v_all = jnp.pad(jnp.concatenate([_bf(v_past.reshape(nb, P, FOX_WIDTH)), v_b], axis=1),
                        ((0, 0), (0, sk - P - T), (0, 0)))
        o_att = _fox_attention(q_aug, k_all, v_all, T, tk, P)

    n = nb * T
    flat = lambda a: a.reshape(n, a.shape[-1])
    feats = (flat(y), flat(bonus), flat(g), flat(o_att), flat(og))
    state = (k_f.reshape(nb, T, N_HEADS, HEAD_DIM), v_f.reshape(nb, T, N_HEADS, HEAD_DIM),
             logf[:, :, :N_HEADS], _pairs_to_state(s_fin), _rwkv_unpad_cols(last))
    return feats, state


def _block_tables(counts):
    blk = MOE_ROWS
    counts = counts.reshape(N_EXPERTS).astype(jnp.int32)
    blk_count = (counts + blk - 1) // blk
    blk_end = jnp.cumsum(blk_count)
    first_blk = blk_end - blk_count
    return ((first_blk * blk).astype(F32).reshape(N_EXPERTS, 1), first_blk.astype(jnp.int32),
            blk_count.astype(jnp.int32), blk_end[-1:].astype(jnp.int32))


def kernel(x_prompt, x_sample, cache_fox_k, cache_fox_v, cache_fox_logf, state_rwkv_wkv, state_rwkv_shift, p_prompt, p_sample, norm_mix_g, w_in, rwkv_mu, rwkv_w0, rwkv_w_up, rwkv_a0, rwkv_a_up, rwkv_g_up, rwkv_k_k, rwkv_k_a, rwkv_r_k, rwkv_ln_w, rwkv_ln_b, fox_q_norm, fox_k_norm, fox_f_bias, w_out, norm_ffn_g, router_w, router_bias, exp_w_gate, exp_w_up, exp_w_down, shared_w_gate, shared_w_up, shared_w_down, ple_norm_g, ple_w_gate, ple_w_proj, final_norm_g):
    assert w_in.shape[0] == 1, "single-layer kernel"
    W = RWKV_WIDTH
    row = lambda a: a.reshape(1, -1).astype(F32)
    tile_heads = lambda a: jnp.tile(a.reshape(1, HEAD_DIM), (1, N_HEADS)).astype(F32)
    hid = jnp.arange(MXU_TILE) // HEAD_DIM
    w_in0 = w_in[0]
    router_hi = _bf(router_w[0])
    wts = {
        "gmix": row(norm_mix_g[0]),
        "w_rwkv": _bf(_rwkv_pad_cols(w_in0[:, :RWKV_IN])),
        "w_fox": _bf(_pad_cols(w_in0[:, RWKV_IN:], FOX_PAD)),
        "mu": row(_rwkv_pad_cols(rwkv_mu[0])),
        "w0": row(rwkv_w0[0]),
        "wup": _bf(_pad_rows(rwkv_w_up[0], 128)),
        "a0": row(rwkv_a0[0]),
        "aup": _bf(_pad_rows(rwkv_a_up[0], 128)),
        "gup": _bf(rwkv_g_up[0]),
        "k_k": row(rwkv_k_k[0]),
        "k_a": row(rwkv_k_a[0]),
        "r_k": row(rwkv_r_k[0]),
        "gs": _bf((hid[:, None] == hid[None, :]).astype(F32)),
        "qn": tile_heads(fox_q_norm[0]),
        "kn": tile_heads(fox_k_norm[0]),
        "fb": _pad_cols(row(fox_f_bias[0]), 128),
    }
    nbp, Tp, _ = x_prompt.shape
    nbs, Ts, _ = x_sample.shape
    s0_prompt = jnp.zeros((nbp, N_HEADS, HEAD_DIM, HEAD_DIM), F32)
    shift0_prompt = jnp.zeros((nbp, 1, RWKV_IN), F32)
    feats_p, st_p = _mixer(x_prompt, shift0_prompt, s0_prompt, None, wts, min(Tp, 256))
    feats_s, st_s = _mixer(x_sample, state_rwkv_shift[0], state_rwkv_wkv[0],
                           (cache_fox_k[0], cache_fox_v[0], cache_fox_logf[0]), wts, Ts)

    n_p, n_s = nbp * Tp, nbs * Ts
    n_tok = n_p + n_s
    tm = math.gcd(math.gcd(n_p, n_s), TOKEN_TILE)
    streams = [(x_prompt.reshape(n_p, D_MODEL), x_sample.reshape(n_s, D_MODEL))] + list(zip(feats_p, feats_s))
    h1, xn2, scores = _out_proj(streams, row(rwkv_ln_w[0]), row(rwkv_ln_b[0]), wts["gs"], _bf(w_out[0]),
                                row(norm_ffn_g[0]), router_hi,
                                _bf(router_w[0] - router_hi.astype(F32)), tm)
    eidx_t, wts_t, rank_t, counts = _route(scores, router_bias[0].reshape(N_EXPERTS, 1).astype(F32), tm)
    n_blocks = -(-n_tok * TOP_K // MOE_ROWS) + N_EXPERTS
    start_col, first_blk, blk_count, n_used = _block_tables(counts)
    pos_t = _positions(eidx_t, rank_t, start_col, tm)
    xs, h1s = _dispatch(pos_t, xn2, h1, _bf(shared_w_gate[0]), _bf(shared_w_up[0]), _bf(shared_w_down[0]),
                        n_blocks * MOE_ROWS, tm)
    y_rows = _gmm(first_blk, blk_count, n_used, xs, exp_w_gate[0], exp_w_up[0], exp_w_down[0])
    p_pair = (p_prompt[0].reshape(n_p, PLE_DIM), p_sample[0].reshape(n_s, PLE_DIM))
    y_p, y_s = _final(pos_t, h1s, wts_t.T, p_pair, y_rows, row(ple_norm_g[0]), _bf(ple_w_gate[0]),
                      _bf(ple_w_proj[0]), row(final_norm_g), tm)
    y_prompt = y_p.reshape(nbp, Tp, D_MODEL)
    y_sample = y_s.reshape(nbs, Ts, D_MODEL)
    lead = lambda t: tuple(a[None] for a in t)
    return (y_prompt, y_sample) + lead(st_p) + lead(st_s)
```

```python
import functools
import math

import numpy as np
import jax
import jax.numpy as jnp
from jax import lax
from jax.experimental import pallas as pl
from jax.experimental.pallas import tpu as pltpu

F32 = jnp.float32
BF16 = jnp.bfloat16

D_MODEL = 1024
HEAD_DIM = 64
RWKV_WIDTH = 512
FOX_WIDTH = 512
N_HEADS = 8
N_PAIRS = N_HEADS // 2
DECAY_RANK = 64
ICL_RANK = 64
GATE_RANK = 128
RWKV_IN = 3 * RWKV_WIDTH + DECAY_RANK + ICL_RANK + GATE_RANK
RWKV_PAD = 3 * RWKV_WIDTH + 3 * 128
FOX_PAD = 4 * FOX_WIDTH + 128
ATTN_SCALE = HEAD_DIM ** -0.5
N_EXPERTS = 256
N_GROUPS = 8
GROUP_SIZE = N_EXPERTS // N_GROUPS
TOPK_GROUPS = 4
TOP_K = 8
EXPERT_FF = 256
ROUTED_SCALE = 2.5
PLE_DIM = 256
RMS_EPS = 1e-6
GN_EPS = 64e-5
L2_EPS = 1e-12

LANES = 128
MXU_TILE = 256
ROW_CHUNKS = D_MODEL // LANES
PACK_CHUNKS = ROW_CHUNKS // 2
RWKV_CHUNK = 128
MOE_ROWS = 256
TOKEN_TILE = 256
GMM_AHEAD = 4
GMM_IN_SLOTS = GMM_AHEAD + 2
GMM_OUT_SLOTS = 4
DMA_QUEUES = 2
FOX_TQ, FOX_TK = 512, 1024
FOX_TK_CACHED = 768
VMEM_LIMIT = 56 * 1024 * 1024


def _cparams(*sem):
    return pltpu.CompilerParams(dimension_semantics=sem, vmem_limit_bytes=VMEM_LIMIT)


def _bf(x):
    return x.astype(BF16)


def _dot(a, b):
    return jnp.dot(a, b, preferred_element_type=F32)


def _dot_nt(a, b):
    return lax.dot_general(a, b, (((1,), (1,)), ((), ())), preferred_element_type=F32)


def _split2(x):
    hi = _bf(x)
    return hi, _bf(x - hi.astype(F32))


def _split3(x):
    hi = _bf(x)
    r1 = x - hi.astype(F32)
    mid = _bf(r1)
    return hi, mid, _bf(r1 - mid.astype(F32))


def _dot_x01(x, w01):
    hi, lo = _split2(x)
    slabs = [slice(c, c + MXU_TILE) for c in range(0, x.shape[1], MXU_TILE)]
    return jnp.concatenate([_dot(hi[:, s], w01) + _dot(lo[:, s], w01) for s in slabs], axis=1)


def _dot3(a, b):
    ah, al = _split2(a)
    bh, bl = _split2(b)
    return _dot(ah, bh) + _dot(al, bh) + _dot(ah, bl)


def _dot3_nt(a, b):
    ah, al = _split2(a)
    bh, bl = _split2(b)
    return _dot_nt(ah, bh) + _dot_nt(al, bh) + _dot_nt(ah, bl)


def _softplus(x):
    return jnp.maximum(x, 0.0) + jnp.log1p(jnp.exp(-jnp.abs(x)))


def _rms(x, g):
    return x * lax.rsqrt(jnp.mean(x * x, axis=-1, keepdims=True) + RMS_EPS) * g


def _full(shape):
    return pl.BlockSpec(shape, lambda *_: (0,) * len(shape))


def _store_chunked(ref, x):
    n = x.shape[0]
    for s in range(ROW_CHUNKS):
        ref[pl.ds(s, n, stride=ROW_CHUNKS), :] = x[:, s * LANES:(s + 1) * LANES]


def _load_chunked(ref, n):
    return jnp.concatenate([ref[pl.ds(s, n, stride=ROW_CHUNKS), :] for s in range(ROW_CHUNKS)], axis=1)


def _store_packed(ref, x):
    n, half = x.shape[0], D_MODEL // 2
    bits = lax.bitcast_convert_type(_bf(x).astype(F32), jnp.uint32)
    packed = (bits[:, :half] >> 16) | (bits[:, half:] & jnp.uint32(0xFFFF0000))
    for s in range(PACK_CHUNKS):
        ref[pl.ds(s, n, stride=PACK_CHUNKS), :] = packed[:, s * LANES:(s + 1) * LANES]


def _load_packed(ref, n):
    packed = jnp.concatenate([ref[pl.ds(s, n, stride=PACK_CHUNKS), :] for s in range(PACK_CHUNKS)], axis=1)
    lo = lax.bitcast_convert_type(packed << 16, F32)
    hi = lax.bitcast_convert_type(packed & jnp.uint32(0xFFFF0000), F32)
    return _bf(jnp.concatenate([lo, hi], axis=1))


def _in_rwkv_kernel(x_ref, shift_ref, gmix_ref, w_ref, mu_ref, w0_ref, wup_ref, a0_ref, aup_ref, gup_ref,
                    kk_ref, ka_ref, rk_ref, gs_ref,
                    r_out, lw_out, kh_out, v_out, kkn_out, bb_out, g_out, bonus_out, last_out,
                    carry_ref):
    j = pl.program_id(1)
    tm = x_ref.shape[1]
    xn = _bf(_rms(x_ref[0], gmix_ref[...]))
    u = _dot(xn, w_ref[...])
    first = jnp.where(j == 0, shift_ref[0], carry_ref[...])
    row = lax.broadcasted_iota(jnp.int32, (tm, 1), 0)
    prev = jnp.where(row == 0, first, pltpu.roll(u, 1, axis=0))
    carry_ref[...] = u[tm - 1:tm, :]
    last_out[0] = u[tm - 1:tm, :]
    xs = u + (prev - u) * mu_ref[...]
    W = RWKV_WIDTH
    r, k, v = xs[:, :W], xs[:, W:2 * W], xs[:, 2 * W:3 * W]
    xw, xa, xg = xs[:, 3 * W:3 * W + 128], xs[:, 3 * W + 128:3 * W + 256], xs[:, 3 * W + 256:]
    w_raw = w0_ref[...] + _dot(_bf(jnp.tanh(xw)), wup_ref[...])
    lw = -jnp.exp(-_softplus(-w_raw) - 0.5)
    a = jax.nn.sigmoid(a0_ref[...] + _dot(_bf(xa), aup_ref[...]))
    g = _dot(_bf(jax.nn.sigmoid(xg)), gup_ref[...])
    gs = gs_ref[...]
    kk = k * kk_ref[...]
    kkn = kk / jnp.maximum(jnp.sqrt(_dot_x01(kk * kk, gs)), L2_EPS)
    kh = k * (1.0 + (a - 1.0) * ka_ref[...])
    r_out[0] = r
    lw_out[0] = lw
    kh_out[0] = kh
    v_out[0] = v
    kkn_out[0] = kkn
    bb_out[0] = kkn * a
    g_out[0] = g
    bonus_out[0] = _dot_x01(r * kh * rk_ref[...], gs) * v


def _in_rwkv(x, shift, gmix, w, mu, w0, wup, a0, aup, gup, k_k, k_a, r_k, gs, tm):
    nb, T, _ = x.shape
    W = RWKV_WIDTH
    tok = lambda width: pl.BlockSpec((1, tm, width), lambda b, j: (b, j, 0))
    outs = [jax.ShapeDtypeStruct((nb, T, W), F32)] * 8 + [jax.ShapeDtypeStruct((nb, 1, RWKV_PAD), F32)]
    return pl.pallas_call(
        _in_rwkv_kernel,
        grid=(nb, T // tm),
        in_specs=[tok(D_MODEL), pl.BlockSpec((1, 1, RWKV_PAD), lambda b, j: (b, 0, 0)),
                  _full((1, D_MODEL)), _full((D_MODEL, RWKV_PAD)), _full((1, RWKV_PAD)),
                  _full((1, W)), _full((128, W)), _full((1, W)), _full((128, W)), _full((128, W)),
                  _full((1, W)), _full((1, W)), _full((1, W)), _full((MXU_TILE, MXU_TILE))],
        out_specs=[tok(W)] * 8 + [pl.BlockSpec((1, 1, RWKV_PAD), lambda b, j: (b, 0, 0))],
        out_shape=outs,
        scratch_shapes=[pltpu.VMEM((1, RWKV_PAD), F32)],
        compiler_params=_cparams("parallel", "arbitrary"),
        name="in_rwkv",
    )(x, shift, gmix, w, mu, w0, wup, a0, aup, gup, k_k, k_a, r_k, gs)


def _aug_rows(x, c, is_query):
    lane = lax.broadcasted_iota(jnp.int32, (1, LANES), 1)
    rows = []
    for p in range(N_PAIRS):
        xp = x[:, p * LANES:(p + 1) * LANES]
        xr = pltpu.roll(xp, HEAD_DIM, axis=1)
        for h in range(2):
            ch = c[:, 2 * p + h:2 * p + h + 1]
            hi = _bf(ch).astype(F32)
            r1 = ch - hi
            mid = _bf(r1).astype(F32)
            lo = _bf(r1 - mid).astype(F32)
            one = jnp.ones_like(ch)
            cols = (hi, mid, lo, one, one, one) if is_query else (one, one, one, -hi, -mid, -lo)
            aug = jnp.zeros_like(xp)
            for n, col in enumerate(cols):
                aug = jnp.where(lane == HEAD_DIM + n, col, aug)
            rows.append(_bf(jnp.where(lane < HEAD_DIM, xp if h == 0 else xr, aug)))
    return rows


def _in_fox_kernel(x_ref, gmix_ref, w_ref, qn_ref, kn_ref, fb_ref, gs_ref, tri_ref, c0_ref,
                   q_out, ka_out, k_out, v_out, vb_out, og_out, lf_out, carry_ref):
    j = pl.program_id(1)
    tm = x_ref.shape[1]

    @pl.when(j == 0)
    def _():
        carry_ref[...] = c0_ref[0]

    xn = _bf(_rms(x_ref[0], gmix_ref[...]))
    u = _dot(xn, w_ref[...])
    W = FOX_WIDTH
    q, k, v, og, fl = u[:, :W], u[:, W:2 * W], u[:, 2 * W:3 * W], u[:, 3 * W:4 * W], u[:, 4 * W:]
    gs = gs_ref[...]
    inv_d = 1.0 / HEAD_DIM
    qn = q * lax.rsqrt(_dot_x01(q * q, gs) * inv_d + RMS_EPS) * qn_ref[...]
    kn = k * lax.rsqrt(_dot_x01(k * k, gs) * inv_d + RMS_EPS) * kn_ref[...]
    logf = -_softplus(-(fl + fb_ref[...]))
    hi, mid, lo = _split3(logf)
    tri = tri_ref[...]
    c = _dot(tri, hi) + _dot(tri, mid) + _dot(tri, lo) + carry_ref[...]
    carry_ref[...] = c[tm - 1:tm, :]
    for h, row in enumerate(_aug_rows(_bf(qn * ATTN_SCALE).astype(F32), c, True)):
        q_out[0, h] = row
    for h, row in enumerate(_aug_rows(_bf(kn).astype(F32), c, False)):
        ka_out[0, h] = row
    k_out[0] = kn
    v_out[0] = v
    vb_out[0] = _bf(v)
    og_out[0] = og
    lf_out[0] = logf


def _in_fox(x, c0, gmix, w, qn, kn, fb, gs, tm):
    nb, T, _ = x.shape
    W = FOX_WIDTH
    tok = lambda width: pl.BlockSpec((1, tm, width), lambda b, j: (b, j, 0))
    aug = pl.BlockSpec((1, N_HEADS, tm, LANES), lambda b, j: (b, 0, j, 0))
    sds = lambda width, dt: jax.ShapeDtypeStruct((nb, T, width), dt)
    aug_shape = jax.ShapeDtypeStruct((nb, N_HEADS, T, LANES), BF16)
    tri = _bf(jnp.tril(jnp.ones((tm, tm), F32)))
    return pl.pallas_call(
        _in_fox_kernel,
        grid=(nb, T // tm),
        in_specs=[tok(D_MODEL), _full((1, D_MODEL)), _full((D_MODEL, FOX_PAD)),
                  _full((1, W)), _full((1, W)), _full((1, 128)), _full((MXU_TILE, MXU_TILE)), _full((tm, tm)),
                  pl.BlockSpec((1, 1, LANES), lambda b, j: (b, 0, 0))],
        out_specs=[aug, aug] + [tok(W)] * 4 + [tok(128)],
        out_shape=[aug_shape, aug_shape, sds(W, F32), sds(W, F32), sds(W, BF16), sds(W, F32), sds(128, F32)],
        scratch_shapes=[pltpu.VMEM((1, LANES), F32)],
        compiler_params=_cparams("parallel", "arbitrary"),
        name="in_fox",
    )(x, gmix, w, qn, kn, fb, gs, tri, c0)


def _cumsum_kernel(x_ref, tri_ref, o_ref, carry_ref):
    j = pl.program_id(1)
    ts = x_ref.shape[1]

    @pl.when(j == 0)
    def _():
        carry_ref[...] = jnp.zeros_like(carry_ref)

    hi, mid, lo = _split3(x_ref[0])
    tri = tri_ref[...]
    c = _dot(tri, hi) + _dot(tri, mid) + _dot(tri, lo) + carry_ref[...]
    o_ref[0] = c
    carry_ref[...] = c[ts - 1:ts, :]


def _cumsum(x, ts):
    nb, S, L = x.shape
    tri = _bf(jnp.tril(jnp.ones((ts, ts), F32)))
    return pl.pallas_call(
        _cumsum_kernel,
        grid=(nb, S // ts),
        in_specs=[pl.BlockSpec((1, ts, L), lambda b, j: (b, j, 0)), _full((ts, ts))],
        out_specs=pl.BlockSpec((1, ts, L), lambda b, j: (b, j, 0)),
        out_shape=jax.ShapeDtypeStruct((nb, S, L), F32),
        scratch_shapes=[pltpu.VMEM((1, L), F32)],
        compiler_params=_cparams("parallel", "arbitrary"),
        name="seq_cumsum",
    )(x, tri)


def _rwkv_chunk_kernel(r_ref, lw_ref, kh_ref, v_ref, kk_ref, bb_ref, tri_ref,
                       rh_out, yh_out, g_out, sh_out):
    C = r_ref.shape[1]
    ti = lax.broadcasted_iota(jnp.int32, (C, C), 0)
    si = lax.broadcasted_iota(jnp.int32, (C, C), 1)
    tx = ti ^ si
    strict = ti > si
    incl = ti >= si
    eye_c = (ti == si).astype(F32)
    lane = lax.broadcasted_iota(jnp.int32, (1, LANES), 1)
    head0 = lane < HEAD_DIM
    pi = lax.broadcasted_iota(jnp.int32, (LANES, LANES), 0)
    pj = lax.broadcasted_iota(jnp.int32, (LANES, LANES), 1)
    same_head = (pi < HEAD_DIM) == (pj < HEAD_DIM)
    eye_p = (pi == pj).astype(F32)
    tri = tri_ref[...]

    pairs = range(N_PAIRS)
    heads = [(p, h) for p in pairs for h in range(2)]
    P = []
    for p in pairs:
        sl = slice(p * LANES, (p + 1) * LANES)
        r, lw, kh, v, kk, bb = (ref[0, :, sl] for ref in (r_ref, lw_ref, kh_ref, v_ref, kk_ref, bb_ref))
        l_hi, l_mid, l_lo = _split3(lw)
        lc = _dot(tri, l_hi) + _dot(tri, l_mid) + _dot(tri, l_lo)
        mid = lc[C // 2 - 1:C // 2, :]
        last = lc[C - 1:C, :]
        e_dn = jnp.exp(mid - lc)
        e_end = jnp.exp(last - lc)
        aa = kk * jnp.exp(lc - lw - mid)
        P.append(dict(sl=sl, v=v, aa=aa, rt=r * jnp.exp(lc - mid), rho=jnp.exp(mid), g_last=jnp.exp(last),
                      bt_b=_bf(bb * e_dn), kt_b=_bf(kh * e_dn), bc_b=_bf(bb * e_end), kc_b=_bf(kh * e_end),
                      aa_b=_bf(aa), v_b=_bf(v)))
    lab, lak, mrb, mrk = {}, {}, {}, {}
    for p, h in heads:
        q = P[p]
        hm = head0 if h == 0 else jnp.logical_not(head0)
        aa_m = _bf(jnp.where(hm, q["aa"], 0.0))
        rt_m = _bf(jnp.where(hm, q["rt"], 0.0))
        lab[p, h] = jnp.where(strict, _dot_nt(aa_m, q["bt_b"]), 0.0)
        lak[p, h] = _bf(jnp.where(strict, _dot_nt(aa_m, q["kt_b"]), 0.0))
        mrb[p, h] = _bf(jnp.where(incl, _dot_nt(rt_m, q["bt_b"]), 0.0))
        mrk[p, h] = _bf(jnp.where(incl, _dot_nt(rt_m, q["kt_b"]), 0.0))
    d = {k: eye_c - jnp.where(tx < 2, lab[k], 0.0) for k in heads}
    s = 2
    while s < C:
        level = (tx >= s) & (tx < 2 * s)
        d_b = {k: _bf(d[k]) for k in heads}
        t1 = {k: _bf(_dot(d_b[k], _bf(jnp.where(level, lab[k], 0.0)))) for k in heads}
        d = {k: d[k] - _dot(t1[k], d_b[k]) for k in heads}
        s *= 2
    d_b = {k: _bf(d[k]) for k in heads}
    w = {k: _bf(_dot(lak[k], P[k[0]]["v_b"])) for k in heads}
    ah = {k: _dot(d_b[k], P[k[0]]["aa_b"]) * P[k[0]]["rho"] for k in heads}
    uh = {k: _dot(d_b[k], w[k]) for k in heads}
    rh = {k: P[k[0]]["rt"] * P[k[0]]["rho"] - _dot(mrb[k], _bf(ah[k])) for k in heads}
    yh = {k: _dot(mrk[k], P[k[0]]["v_b"]) - _dot(mrb[k], _bf(uh[k])) for k in heads}
    for p in pairs:
        q = P[p]
        both = lambda x: jnp.where(head0, x[p, 0], x[p, 1])
        ah_p, uh_p = both(ah), both(uh)
        rh_out[0, :, q["sl"]] = both(rh)
        yh_out[0, :, q["sl"]] = both(yh)
        g_full = eye_p * q["g_last"] - _dot(_bf(ah_p.T), q["bc_b"])
        sh_full = _dot(_bf(q["v"].T), q["kc_b"]) - _dot(_bf(uh_p.T), q["bc_b"])
        g_out[0, 0, p] = jnp.where(same_head, g_full, 0.0)
        sh_out[0, 0, p] = jnp.where(same_head, sh_full, 0.0)


def _rwkv_chunks(r, lw, kh, v, kk, bb):
    nb, T, W = r.shape
    C = RWKV_CHUNK
    nc = T // C
    tri = _bf(jnp.tril(jnp.ones((C, C), F32)))
    tok = pl.BlockSpec((1, C, W), lambda b, c: (b, c, 0))
    mat = pl.BlockSpec((1, 1, N_PAIRS, LANES, LANES), lambda b, c: (b, c, 0, 0, 0))
    mat_shape = jax.ShapeDtypeStruct((nb, nc, N_PAIRS, LANES, LANES), F32)
    return pl.pallas_call(
        _rwkv_chunk_kernel,
        grid=(nb, nc),
        in_specs=[tok] * 6 + [_full((C, C))],
        out_specs=[tok, tok, mat, mat],
        out_shape=[jax.ShapeDtypeStruct((nb, T, W), F32)] * 2 + [mat_shape] * 2,
        compiler_params=_cparams("parallel", "parallel"),
        name="rwkv_chunks",
    )(r, lw, kh, v, kk, bb, tri)


def _rwkv_scan_kernel(rh_ref, yh_ref, g_ref, sh_ref, s0_ref, y_out, sfin_out, s_scr):
    c = pl.program_id(1)
    nbg = rh_ref.shape[0]

    @pl.when(c == 0)
    def _():
        s_scr[...] = s0_ref[...]

    for b in range(nbg):
        for p in range(N_PAIRS):
            sl = slice(p * LANES, (p + 1) * LANES)
            s = s_scr[b, p]
            y_out[b, :, sl] = _dot3_nt(rh_ref[b, :, sl], s) + yh_ref[b, :, sl]
            s_new = _dot3(s, g_ref[b, 0, p]) + sh_ref[b, 0, p]
            s_scr[b, p] = s_new
            sfin_out[b, p] = s_new


def _rwkv_scan(rh, yh, g, sh, s0, nbg):
    nb, T, W = rh.shape
    C = RWKV_CHUNK
    nc = T // C
    tok = pl.BlockSpec((nbg, C, W), lambda i, c: (i, c, 0))
    mat = pl.BlockSpec((nbg, 1, N_PAIRS, LANES, LANES), lambda i, c: (i, c, 0, 0, 0))
    st = pl.BlockSpec((nbg, N_PAIRS, LANES, LANES), lambda i, c: (i, 0, 0, 0))
    return pl.pallas_call(
        _rwkv_scan_kernel,
        grid=(nb // nbg, nc),
        in_specs=[tok, tok, mat, mat, st],
        out_specs=[tok, st],
        out_shape=[jax.ShapeDtypeStruct((nb, T, W), F32),
                   jax.ShapeDtypeStruct((nb, N_PAIRS, LANES, LANES), F32)],
        scratch_shapes=[pltpu.VMEM((nbg, N_PAIRS, LANES, LANES), F32)],
        compiler_params=_cparams("parallel", "arbitrary"),
        name="rwkv_scan",
    )(rh, yh, g, sh, s0)


def _state_to_pairs(s):
    nb = s.shape[0]
    s = s.reshape(nb, N_PAIRS, 2, HEAD_DIM, HEAD_DIM)
    z = jnp.zeros_like(s[:, :, 0])
    top = jnp.concatenate([s[:, :, 0], z], axis=-1)
    bot = jnp.concatenate([z, s[:, :, 1]], axis=-1)
    return jnp.concatenate([top, bot], axis=-2)


def _pairs_to_state(sp):
    nb = sp.shape[0]
    a = sp[:, :, :HEAD_DIM, :HEAD_DIM]
    b = sp[:, :, HEAD_DIM:, HEAD_DIM:]
    return jnp.stack([a, b], axis=2).reshape(nb, N_HEADS, HEAD_DIM, HEAD_DIM)


def _fox_aug_kernel(x_ref, c_ref, o_ref):
    for h, row in enumerate(_aug_rows(x_ref[0].astype(F32), c_ref[0], False)):
        o_ref[0, h] = row


def _fox_aug(x, c, tm):
    nb, S, W = x.shape
    return pl.pallas_call(
        _fox_aug_kernel,
        grid=(nb, S // tm),
        in_specs=[pl.BlockSpec((1, tm, W), lambda b, j: (b, j, 0)),
                  pl.BlockSpec((1, tm, LANES), lambda b, j: (b, j, 0))],
        out_specs=pl.BlockSpec((1, N_HEADS, tm, LANES), lambda b, j: (b, 0, j, 0)),
        out_shape=jax.ShapeDtypeStruct((nb, N_HEADS, S, LANES), BF16),
        compiler_params=_cparams("parallel", "parallel"),
        name="fox_aug",
    )(x, c)


def _fox_kernel(q_ref, k_ref, v_ref, o_ref, m_scr, l_scr, acc_scr, sa_scr, sb_scr, *, tk, q_off):
    i = pl.program_id(2)
    tq = q_ref.shape[2]
    q_start = q_off + i * tq
    m_scr[...] = jnp.full_like(m_scr, -jnp.inf)
    l_scr[...] = jnp.zeros_like(l_scr)
    acc_scr[...] = jnp.zeros_like(acc_scr)
    n_full = (q_start + 1) // tk

    def scores_into(j, dst):
        ks = pl.multiple_of(j * tk, tk)
        for h in range(2):
            dst[h] = _dot_nt(q_ref[0, h], k_ref[0, h, pl.ds(ks, tk), :])

    def update_from(j, src, masked, width=tk):
        ks = pl.multiple_of(j * tk, tk)
        v = v_ref[0, pl.ds(ks, width), :]
        if masked:
            visible = (ks + lax.broadcasted_iota(jnp.int32, (1, width), 1)) <= (
                q_start + lax.broadcasted_iota(jnp.int32, (tq, 1), 0))
        for h in range(2):
            s = src[h, :, :width]
            if masked:
                s = jnp.where(visible, s, -jnp.inf)
            m_old = m_scr[h]
            m_new = jnp.maximum(m_old, jnp.max(s, axis=-1, keepdims=True))
            alpha = jnp.exp(m_old - m_new)
            pr = jnp.exp(s - m_new)
            l_scr[h] = alpha * l_scr[h] + jnp.sum(pr, axis=-1, keepdims=True)
            acc_scr[h] = alpha * acc_scr[h] + _dot(_bf(pr), v)
            m_scr[h] = m_new

    def step(j, src, dst):
        scores_into(j + 1, dst)
        update_from(j, src, False)

    def two_steps(jj, carry):
        step(2 * jj, sa_scr, sb_scr)
        step(2 * jj + 1, sb_scr, sa_scr)
        return carry

    scores_into(0, sa_scr)
    lax.fori_loop(0, n_full // 2, two_steps, 0)
    odd = n_full % 2 == 1

    def diagonal_block(src):
        half = q_start + tq - n_full * tk <= tk // 2

        @pl.when(half)
        def _():
            update_from(n_full, src, True, tk // 2)

        @pl.when(jnp.logical_not(half))
        def _():
            update_from(n_full, src, True)

    @pl.when(odd)
    def _():
        step(n_full - 1, sa_scr, sb_scr)
        diagonal_block(sb_scr)

    @pl.when(jnp.logical_not(odd))
    def _():
        diagonal_block(sa_scr)

    head0 = lax.broadcasted_iota(jnp.int32, (1, LANES), 1) < HEAD_DIM
    o_ref[0] = jnp.where(head0, acc_scr[0] / l_scr[0], acc_scr[1] / l_scr[1])


def _fox_attention(q, k, v, tq, tk, q_off):
    nb, _, sq, _ = q.shape
    sk = k.shape[2]
    for q_start in range(q_off, q_off + sq, tq):
        assert (q_start + 1) // tk + 1 == -(-(q_start + tq) // tk) <= sk // tk, (q_start, tq, tk)
    return pl.pallas_call(
        functools.partial(_fox_kernel, tk=tk, q_off=q_off),
        grid=(nb, N_PAIRS, sq // tq),
        in_specs=[pl.BlockSpec((1, 2, tq, LANES), lambda b, p, i: (b, p, i, 0)),
                  pl.BlockSpec((1, 2, sk, LANES), lambda b, p, i: (b, p, 0, 0)),
                  pl.BlockSpec((1, sk, LANES), lambda b, p, i: (b, 0, p))],
        out_specs=pl.BlockSpec((1, tq, LANES), lambda b, p, i: (b, i, p)),
        out_shape=jax.ShapeDtypeStruct((nb, sq, FOX_WIDTH), F32),
        scratch_shapes=[pltpu.VMEM((2, tq, 1), F32), pltpu.VMEM((2, tq, 1), F32),
                        pltpu.VMEM((2, tq, LANES), F32), pltpu.VMEM((2, tq, tk), F32),
                        pltpu.VMEM((2, tq, tk), F32)],
        compiler_params=_cparams("parallel", "parallel", "arbitrary"),
        name="fox_attention",
    )(q, k, v)


def _out_kernel(*refs, n_first):
    tok_refs, rest = refs[:12], refs[12:]
    lnw_ref, lnb_ref, gs_ref, wout_ref, gffn_ref, rwh_ref, rwl_ref, h_out, xn_out, sc_out = rest
    first = pl.program_id(0) < n_first
    x, y, bonus, g, oa, og = (jnp.where(first, tok_refs[2 * n][...], tok_refs[2 * n + 1][...]) for n in range(6))
    gs = gs_ref[...]
    inv_d = 1.0 / HEAD_DIM
    mean = _dot_x01(y, gs) * inv_d
    d = y - mean
    var = _dot_x01(d * d, gs) * inv_d
    yn = d * lax.rsqrt(var + GN_EPS) * lnw_ref[...] + lnb_ref[...]
    o_rwkv = (yn + bonus) * g
    o_fox = oa * jax.nn.sigmoid(og)
    mix = jnp.concatenate([_bf(o_rwkv), _bf(o_fox)], axis=-1)
    h = x + _dot(mix, wout_ref[...])
    h_out[...] = h
    xn = _rms(h, gffn_ref[...])
    _store_packed(xn_out, xn)
    xh, xl = _split2(xn)
    logits = _dot(xh, rwh_ref[...]) + _dot(xl, rwh_ref[...]) + _dot(xh, rwl_ref[...])
    sc_out[...] = jax.nn.sigmoid(logits)


def _two_streams(width, tm, n_first):
    return [pl.BlockSpec((tm, width), lambda i: (jnp.minimum(i, n_first - 1), 0)),
            pl.BlockSpec((tm, width), lambda i: (jnp.maximum(i - n_first, 0), 0))]


def _out_proj(streams, lnw, lnb, gs, wout, gffn, rwh, rwl, tm):
    n_first = streams[0][0].shape[0] // tm
    T = streams[0][0].shape[0] + streams[0][1].shape[0]
    W = RWKV_WIDTH
    tok = lambda width: pl.BlockSpec((tm, width), lambda i: (i, 0))
    tok_specs, tok_args = [], []
    for a, b in streams:
        tok_specs += _two_streams(a.shape[1], tm, n_first)
        tok_args += [a, b]
    return pl.pallas_call(
        functools.partial(_out_kernel, n_first=n_first),
        grid=(T // tm,),
        in_specs=tok_specs + [_full((1, W)), _full((1, W)), _full((MXU_TILE, MXU_TILE)),
                              _full((D_MODEL, D_MODEL)), _full((1, D_MODEL)),
                              _full((D_MODEL, N_EXPERTS)), _full((D_MODEL, N_EXPERTS))],
        out_specs=[tok(D_MODEL), pl.BlockSpec((tm * PACK_CHUNKS, LANES), lambda i: (i, 0)), tok(N_EXPERTS)],
        out_shape=[jax.ShapeDtypeStruct((T, D_MODEL), F32),
                   jax.ShapeDtypeStruct((T * PACK_CHUNKS, LANES), jnp.uint32),
                   jax.ShapeDtypeStruct((T, N_EXPERTS), F32)],
        compiler_params=_cparams("parallel"),
        name="out_proj",
    )(*tok_args, lnw, lnb, gs, wout, gffn, rwh, rwl)


def _route_kernel(sc_ref, bias_ref, before_ref, idx_out, wt_out, rank_out, cnt_out, cnt_scr):
    tm = sc_ref.shape[0]
    neg = -jnp.inf

    @pl.when(pl.program_id(0) == 0)
    def _():
        cnt_scr[...] = jnp.zeros_like(cnt_scr)

    st = sc_ref[...].T
    sel = st + bias_ref[...]
    gscore = []
    for gi in range(N_GROUPS):
        blk = sel[gi * GROUP_SIZE:(gi + 1) * GROUP_SIZE, :]
        m1 = jnp.max(blk, axis=0, keepdims=True)
        n1 = jnp.sum((blk == m1).astype(F32), axis=0, keepdims=True)
        m2 = jnp.max(jnp.where(blk < m1, blk, neg), axis=0, keepdims=True)
        gscore.append(m1 + jnp.where(n1 > 1.0, m1, m2))
    taken = [jnp.zeros((1, tm), jnp.bool_) for _ in range(N_GROUPS)]
    for _ in range(TOPK_GROUPS):
        avail = [jnp.where(taken[gi], neg, gscore[gi]) for gi in range(N_GROUPS)]
        best = functools.reduce(jnp.maximum, avail)
        found = jnp.zeros((1, tm), jnp.bool_)
        for gi in range(N_GROUPS):
            hit = (avail[gi] == best) & jnp.logical_not(found)
            taken[gi] = taken[gi] | hit
            found = found | hit
    cand = jnp.concatenate(
        [jnp.where(taken[gi], sel[gi * GROUP_SIZE:(gi + 1) * GROUP_SIZE, :], neg) for gi in range(N_GROUPS)], axis=0)
    eid = lax.broadcasted_iota(jnp.int32, (N_EXPERTS, tm), 0).astype(F32)
    idxs, wts = [], []
    onehot = jnp.zeros((N_EXPERTS, tm), F32)
    for _ in range(TOP_K):
        best = jnp.max(cand, axis=0, keepdims=True)
        pick = jnp.min(jnp.where(cand == best, eid, float(N_EXPERTS)), axis=0, keepdims=True)
        chosen = eid == pick
        wts.append(jnp.sum(jnp.where(chosen, st, 0.0), axis=0, keepdims=True))
        idxs.append(pick)
        cand = jnp.where(chosen, neg, cand)
        onehot = jnp.where(chosen, 1.0, onehot)
    w = jnp.concatenate(wts, axis=0)
    idx_out[...] = jnp.concatenate(idxs, axis=0).astype(jnp.int32)
    wt_out[...] = w / jnp.sum(w, axis=0, keepdims=True) * ROUTED_SCALE
    earlier = _dot(_bf(onehot), before_ref[...]) + cnt_scr[...]
    rank_out[...] = jnp.concatenate(
        [jnp.sum(jnp.where(eid == pick, earlier, 0.0), axis=0, keepdims=True) for pick in idxs],
        axis=0).astype(jnp.int32)
    cnt_scr[...] += jnp.sum(onehot, axis=1, keepdims=True)
    cnt_out[...] = cnt_scr[...]


def _route(scores, bias_col, tm):
    T = scores.shape[0]
    before = _bf(jnp.triu(jnp.ones((tm, tm), F32), 1))
    tok = pl.BlockSpec((TOP_K, tm), lambda i: (0, i))
    return pl.pallas_call(
        _route_kernel,
        grid=(T // tm,),
        in_specs=[pl.BlockSpec((tm, N_EXPERTS), lambda i: (i, 0)), _full((N_EXPERTS, 1)), _full((tm, tm))],
        out_specs=[tok, tok, tok, _full((N_EXPERTS, 1))],
        out_shape=[jax.ShapeDtypeStruct((TOP_K, T), jnp.int32), jax.ShapeDtypeStruct((TOP_K, T), F32),
                   jax.ShapeDtypeStruct((TOP_K, T), jnp.int32), jax.ShapeDtypeStruct((N_EXPERTS, 1), F32)],
        scratch_shapes=[pltpu.VMEM((N_EXPERTS, 1), F32)],
        compiler_params=_cparams("arbitrary"),
        name="route",
    )(scores, bias_col, before)


def _pos_kernel(idx_ref, rank_ref, start_ref, pos_out):
    tm = idx_ref.shape[1]
    eid = lax.broadcasted_iota(jnp.int32, (N_EXPERTS, tm), 0)
    idx = idx_ref[...]
    start = start_ref[...]
    base = jnp.concatenate(
        [jnp.sum(jnp.where(eid == idx[k:k + 1, :], start, 0.0), axis=0, keepdims=True) for k in range(TOP_K)],
        axis=0)
    pos_out[...] = rank_ref[...] + base.astype(jnp.int32)


def _positions(eidx_t, rank_t, start_col, tm):
    T = eidx_t.shape[1]
    tok = pl.BlockSpec((TOP_K, tm), lambda i: (0, i))
    return pl.pallas_call(
        _pos_kernel,
        grid=(T // tm,),
        in_specs=[tok, tok, _full((N_EXPERTS, 1))],
        out_specs=tok,
        out_shape=jax.ShapeDtypeStruct((TOP_K, T), jnp.int32),
        compiler_params=_cparams("parallel"),
        name="moe_positions",
    )(eidx_t, rank_t, start_col)


def _dispatch_kernel(pos_ref, x_ref, h_ref, sg_ref, su_ref, sd_ref, xs_in, xs_out, hs_out, sem):
    del xs_in
    tm = pos_ref.shape[1]

    def issue(t, carry):
        src = x_ref.at[pl.ds(pl.multiple_of(t * PACK_CHUNKS, PACK_CHUNKS), PACK_CHUNKS), :]
        for k in range(TOP_K):
            row = pl.multiple_of(pos_ref[k, t] * PACK_CHUNKS, PACK_CHUNKS)
            pltpu.make_async_copy(src, xs_out.at[pl.ds(row, PACK_CHUNKS), :], sem).start(priority=k % DMA_QUEUES)
        return carry

    lax.fori_loop(0, tm, issue, 0)
    xb = _load_packed(x_ref, tm)
    hg = _dot(xb, sg_ref[...])
    hu = _dot(xb, su_ref[...])
    hs_out[...] = h_ref[...] + _dot(_bf(hg * jax.nn.sigmoid(hg) * hu), sd_ref[...])
    for k in range(TOP_K):
        pltpu.make_async_copy(x_ref, xs_out.at[pl.ds(0, tm * PACK_CHUNKS), :], sem).wait()


def _dispatch(pos_t, xn_packed, h, sg, su, sd, n_rows, tm):
    T = pos_t.shape[1]
    xs0 = jnp.zeros((n_rows * PACK_CHUNKS, LANES), jnp.uint32)
    return pl.pallas_call(
        _dispatch_kernel,
        grid=(T // tm,),
        in_specs=[pl.BlockSpec((TOP_K, tm), lambda i: (0, i), memory_space=pltpu.SMEM),
                  pl.BlockSpec((tm * PACK_CHUNKS, LANES), lambda i: (i, 0)),
                  pl.BlockSpec((tm, D_MODEL), lambda i: (i, 0)),
                  _full((D_MODEL, EXPERT_FF)), _full((D_MODEL, EXPERT_FF)), _full((EXPERT_FF, D_MODEL)),
                  pl.BlockSpec(memory_space=pl.ANY)],
        out_specs=[pl.BlockSpec(memory_space=pl.ANY), pl.BlockSpec((tm, D_MODEL), lambda i: (i, 0))],
        out_shape=[jax.ShapeDtypeStruct((n_rows * PACK_CHUNKS, LANES), jnp.uint32),
                   jax.ShapeDtypeStruct((T, D_MODEL), F32)],
        scratch_shapes=[pltpu.SemaphoreType.DMA(())],
        input_output_aliases={6: 0},
        compiler_params=_cparams("arbitrary"),
        name="moe_dispatch",
    )(pos_t, xn_packed, h, sg, su, sd, xs0)


def _gmm_kernel(first_ref, count_ref, used_ref, xs_hbm, wg_ref, wu_ref, wd_ref, y_hbm,
                xbuf, ybuf, wg_b, wu_b, wd_b, in_sems, out_sems):
    e = pl.program_id(0)
    bm = MOE_ROWS
    blk_rows = bm * ROW_CHUNKS
    n_used = used_ref[0]
    n_blocks = y_hbm.shape[0] // blk_rows

    in_rows = bm * PACK_CHUNKS

    def read(g, slot):
        return pltpu.make_async_copy(xs_hbm.at[pl.ds(pl.multiple_of(g * in_rows, in_rows), in_rows), :],
                                     xbuf.at[slot], in_sems.at[slot])

    def write(g, slot):
        return pltpu.make_async_copy(ybuf.at[slot],
                                     y_hbm.at[pl.ds(pl.multiple_of(g * blk_rows, blk_rows), blk_rows), :],
                                     out_sems.at[slot])

    @pl.when(e == 0)
    def _():
        for g0 in range(GMM_AHEAD):
            @pl.when(g0 < n_used)
            def _():
                read(g0, g0 % GMM_IN_SLOTS).start()

    wg_b[...] = _bf(wg_ref[0])
    wu_b[...] = _bf(wu_ref[0])
    wd_b[...] = _bf(wd_ref[0])

    def blocks(g, width):
        for j in range(width):
            read(g + j, (g + j) % GMM_IN_SLOTS).wait()
        for j in range(width):
            @pl.when(g + j + GMM_AHEAD < n_used)
            def _():
                read(g + j + GMM_AHEAD, (g + j + GMM_AHEAD) % GMM_IN_SLOTS).start()

            @pl.when(g + j >= GMM_OUT_SLOTS)
            def _():
                write(g + j - GMM_OUT_SLOTS, (g + j) % GMM_OUT_SLOTS).wait()
        for j in range(width):
            xe = _load_packed(xbuf.at[(g + j) % GMM_IN_SLOTS], bm)
            hg = _dot(xe, wg_b[...])
            hu = _dot(xe, wu_b[...])
            _store_chunked(ybuf.at[(g + j) % GMM_OUT_SLOTS], _dot(_bf(hg * jax.nn.sigmoid(hg) * hu), wd_b[...]))
        for j in range(width):
            write(g + j, (g + j) % GMM_OUT_SLOTS).start()

    first, count = first_ref[e], count_ref[e]

    def pair(jj, carry):
        blocks(first + 2 * jj, 2)
        return carry

    lax.fori_loop(0, count // 2, pair, 0)

    @pl.when(count % 2 == 1)
    def _():
        blocks(first + count - 1, 1)

    @pl.when(e == pl.num_programs(0) - 1)
    def _():
        for back in range(GMM_OUT_SLOTS, 0, -1):
            @pl.when(n_used >= back)
            def _():
                write(n_used - back, (n_used - back) % GMM_OUT_SLOTS).wait()
        ybuf[0] = jnp.zeros_like(ybuf[0])

        def fill(g, carry):
            write(g, 0).start()
            return carry

        def drain(g, carry):
            write(g, 0).wait()
            return carry

        lax.fori_loop(n_used, n_blocks, fill, 0)
        lax.fori_loop(n_used, n_blocks, drain, 0)


def _gmm(first_blk, blk_count, n_used, xs, wg, wu, wd):
    bm = MOE_ROWS
    n_rows = xs.shape[0] // PACK_CHUNKS
    wspec = lambda shape: pl.BlockSpec((1,) + shape, lambda e, *_: (e, 0, 0))
    grid_spec = pltpu.PrefetchScalarGridSpec(
        num_scalar_prefetch=3,
        grid=(N_EXPERTS,),
        in_specs=[pl.BlockSpec(memory_space=pl.ANY), wspec((D_MODEL, EXPERT_FF)), wspec((D_MODEL, EXPERT_FF)),
                  wspec((EXPERT_FF, D_MODEL))],
        out_specs=pl.BlockSpec(memory_space=pl.ANY),
        scratch_shapes=[pltpu.VMEM((GMM_IN_SLOTS, bm * PACK_CHUNKS, LANES), jnp.uint32),
                        pltpu.VMEM((GMM_OUT_SLOTS, bm * ROW_CHUNKS, LANES), F32),
                        pltpu.VMEM((D_MODEL, EXPERT_FF), BF16), pltpu.VMEM((D_MODEL, EXPERT_FF), BF16),
                        pltpu.VMEM((EXPERT_FF, D_MODEL), BF16),
                        pltpu.SemaphoreType.DMA((GMM_IN_SLOTS,)), pltpu.SemaphoreType.DMA((GMM_OUT_SLOTS,))],
    )
    return pl.pallas_call(
        _gmm_kernel,
        grid_spec=grid_spec,
        out_shape=jax.ShapeDtypeStruct((n_rows * ROW_CHUNKS, LANES), F32),
        compiler_params=_cparams("arbitrary"),
        name="expert_gmm",
    )(first_blk, blk_count, n_used, xs, wg, wu, wd)


def _final_kernel(pos_ref, nxt_ref, hs_ref, w_ref, pa_ref, pb_ref, ys_ref, gple_ref, wpg_ref, wpp_ref, gfin_ref,
                  ya_out, yb_out, buf, sems, *, n_first):
    i = pl.program_id(0)
    n = pl.num_programs(0)
    tm = hs_ref.shape[0]

    def gather(rows_ref, slot):
        def issue(t, carry):
            dst = pl.ds(pl.multiple_of(t * ROW_CHUNKS, ROW_CHUNKS), ROW_CHUNKS)
            for k in range(TOP_K):
                row = pl.multiple_of(rows_ref[k, t] * ROW_CHUNKS, ROW_CHUNKS)
                pltpu.make_async_copy(ys_ref.at[pl.ds(row, ROW_CHUNKS), :], buf.at[slot, k, dst, :],
                                      sems.at[slot, k]).start(priority=k % DMA_QUEUES)
            return carry

        lax.fori_loop(0, tm, issue, 0)

    slot = i % 2

    @pl.when(i == 0)
    def _():
        gather(pos_ref, 0)

    for nxt in range(2):
        @pl.when((i + 1 < n) & (slot != nxt))
        def _():
            gather(nxt_ref, nxt)

    pp = _dot(_bf(jnp.where(i < n_first, pa_ref[...], pb_ref[...])), wpp_ref[...])
    for k in range(TOP_K):
        pltpu.make_async_copy(ys_ref.at[pl.ds(0, tm * ROW_CHUNKS), :], buf.at[slot, k], sems.at[slot, k]).wait()
    w = w_ref[...]
    wk = [jnp.broadcast_to(w[:, k:k + 1], (tm, LANES)) for k in range(TOP_K)]
    chunks = []
    for s in range(ROW_CHUNKS):
        acc = buf[slot, 0, pl.ds(s, tm, stride=ROW_CHUNKS), :] * wk[0]
        for k in range(1, TOP_K):
            acc = acc + buf[slot, k, pl.ds(s, tm, stride=ROW_CHUNKS), :] * wk[k]
        chunks.append(acc)
    h = hs_ref[...] + jnp.concatenate(chunks, axis=1)
    gate = jax.nn.sigmoid(_dot(_bf(_rms(h, gple_ref[...])), wpg_ref[...]))
    y = _rms(h + gate * pp, gfin_ref[...])

    @pl.when(i < n_first)
    def _():
        ya_out[...] = y

    @pl.when(i >= n_first)
    def _():
        yb_out[...] = y


def _final(pos_t, hs, w, p_pair, ys, gple, wpg, wpp, gfin, tm):
    T = hs.shape[0]
    n = T // tm
    n_first = p_pair[0].shape[0] // tm
    tok = lambda width: pl.BlockSpec((tm, width), lambda i: (i, 0))
    return pl.pallas_call(
        functools.partial(_final_kernel, n_first=n_first),
        grid=(n,),
        in_specs=[pl.BlockSpec((TOP_K, tm), lambda i: (0, i), memory_space=pltpu.SMEM),
                  pl.BlockSpec((TOP_K, tm), lambda i: (0, jnp.minimum(i + 1, n - 1)), memory_space=pltpu.SMEM),
                  tok(D_MODEL), tok(TOP_K)] + _two_streams(PLE_DIM, tm, n_first) + [
                  pl.BlockSpec(memory_space=pl.ANY),
                  _full((1, D_MODEL)), _full((D_MODEL, D_MODEL)), _full((PLE_DIM, D_MODEL)),
                  _full((1, D_MODEL))],
        out_specs=_two_streams(D_MODEL, tm, n_first),
        out_shape=[jax.ShapeDtypeStruct((n_first * tm, D_MODEL), F32),
                   jax.ShapeDtypeStruct((T - n_first * tm, D_MODEL), F32)],
        scratch_shapes=[pltpu.VMEM((2, TOP_K, tm * ROW_CHUNKS, LANES), F32),
                        pltpu.SemaphoreType.DMA((2, TOP_K))],
        compiler_params=_cparams("arbitrary"),
        name="ffn_tail",
    )(pos_t, pos_t, hs, w, *p_pair, ys, gple, wpg, wpp, gfin)


def _pad_cols(a, width):
    return jnp.pad(a, [(0, 0)] * (a.ndim - 1) + [(0, width - a.shape[-1])])


def _rwkv_pad_cols(a):
    W = RWKV_WIDTH
    o1, o2, o3 = 3 * W, 3 * W + DECAY_RANK, 3 * W + DECAY_RANK + ICL_RANK
    return jnp.concatenate([a[..., :o1], _pad_cols(a[..., o1:o2], 128), _pad_cols(a[..., o2:o3], 128),
                            a[..., o3:]], axis=-1)


def _rwkv_unpad_cols(a):
    W = RWKV_WIDTH
    return jnp.concatenate([a[..., :3 * W + DECAY_RANK], a[..., 3 * W + 128:3 * W + 128 + ICL_RANK],
                            a[..., 3 * W + 256:]], axis=-1)


def _pad_rows(a, rows):
    return jnp.pad(a, [(0, rows - a.shape[0])] + [(0, 0)] * (a.ndim - 1))


def _mixer(x, shift, s0, past, wts, tm):
    nb, T, _ = x.shape
    (r, lw, kh, v, kkn, bb, g, bonus, last) = _in_rwkv(
        x, _rwkv_pad_cols(shift), wts["gmix"], wts["w_rwkv"], wts["mu"], wts["w0"], wts["wup"], wts["a0"],
        wts["aup"], wts["gup"], wts["k_k"], wts["k_a"], wts["r_k"], wts["gs"], tm)
    if past is None:
        c0 = jnp.zeros((nb, 1, LANES), F32)
    else:
        k_past, v_past, lf_past = past
        P = k_past.shape[1]
        c_past = _cumsum(_pad_cols(lf_past.astype(F32), LANES), math.gcd(P, 512))
        c0 = c_past[:, P - 1:, :]
    q_aug, k_aug, k_f, v_f, v_b, og, logf = _in_fox(
        x, c0, wts["gmix"], wts["w_fox"], wts["qn"], wts["kn"], wts["fb"], wts["gs"], tm)

    C = RWKV_CHUNK
    Tp = -(-T // C) * C
    if Tp != T:
        padt = lambda a: jnp.pad(a, ((0, 0), (0, Tp - T), (0, 0)))
        r_p, lw_p, kh_p, v_p, kk_p, bb_p = (padt(a) for a in (r, lw, kh, v, kkn, bb))
    else:
        r_p, lw_p, kh_p, v_p, kk_p, bb_p = r, lw, kh, v, kkn, bb
    rh, yh, gm, sh = _rwkv_chunks(r_p, lw_p, kh_p, v_p, kk_p, bb_p)
    y, s_fin = _rwkv_scan(rh, yh, gm, sh, _state_to_pairs(s0.astype(F32)), 4 if nb % 4 == 0 else 1)
    y = y[:, :T]

    if past is None:
        o_att = _fox_attention(q_aug, k_aug, v_b, min(T, FOX_TQ), min(T, FOX_TK), 0)
    else:
        tk = FOX_TK_CACHED
        sk = -(-(P + T) // tk) * tk
        k_aug_past = _fox_aug(_bf(k_past.reshape(nb, P, FOX_WIDTH)), c_past, math.gcd(P, 512))
        k_all = jnp.pad(jnp.concatenate([k_aug_past, k_aug], axis=2), ((0, 0), (0, 0), (0, sk - P - T), (0, 0)))
        v_all = jnp.pad(jnp.concatenate([_bf(v_past.reshape(nb, P, FOX_WIDTH)), v_b], axis=1),
                        ((0, 0), (0, sk - P - T), (0, 0)))
        o_att = _fox_attention(q_aug, k_all, v_all, T, tk, P)

    n = nb * T
    flat = lambda a: a.reshape(n, a.shape[-1])
    feats = (flat(y), flat(bonus), flat(g), flat(o_att), flat(og))
    state = (k_f.reshape(nb, T, N_HEADS, HEAD_DIM), v_f.reshape(nb, T, N_HEADS, HEAD_DIM),
             logf[:, :, :N_HEADS], _pairs_to_state(s_fin), _rwkv_unpad_cols(last))
    return feats, state


def _block_tables(counts):
    blk = MOE_ROWS
    counts = counts.reshape(N_EXPERTS).astype(jnp.int32)
    blk_count = (counts + blk - 1) // blk
    blk_end = jnp.cumsum(blk_count)
    first_blk = blk_end - blk_count
    return ((first_blk * blk).astype(F32).reshape(N_EXPERTS, 1), first_blk.astype(jnp.int32),
            blk_count.astype(jnp.int32), blk_end[-1:].astype(jnp.int32))


def kernel(x_prompt, x_sample, cache_fox_k, cache_fox_v, cache_fox_logf, state_rwkv_wkv, state_rwkv_shift, p_prompt, p_sample, norm_mix_g, w_in, rwkv_mu, rwkv_w0, rwkv_w_up, rwkv_a0, rwkv_a_up, rwkv_g_up, rwkv_k_k, rwkv_k_a, rwkv_r_k, rwkv_ln_w, rwkv_ln_b, fox_q_norm, fox_k_norm, fox_f_bias, w_out, norm_ffn_g, router_w, router_bias, exp_w_gate, exp_w_up, exp_w_down, shared_w_gate, shared_w_up, shared_w_down, ple_norm_g, ple_w_gate, ple_w_proj, final_norm_g):
    assert w_in.shape[0] == 1, "single-layer kernel"
    W = RWKV_WIDTH
    row = lambda a: a.reshape(1, -1).astype(F32)
    tile_heads = lambda a: jnp.tile(a.reshape(1, HEAD_DIM), (1, N_HEADS)).astype(F32)
    hid = jnp.arange(MXU_TILE) // HEAD_DIM
    w_in0 = w_in[0]
    router_hi = _bf(router_w[0])
    wts = {
        "gmix": row(norm_mix_g[0]),
        "w_rwkv": _bf(_rwkv_pad_cols(w_in0[:, :RWKV_IN])),
        "w_fox": _bf(_pad_cols(w_in0[:, RWKV_IN:], FOX_PAD)),
        "mu": row(_rwkv_pad_cols(rwkv_mu[0])),
        "w0": row(rwkv_w0[0]),
        "wup": _bf(_pad_rows(rwkv_w_up[0], 128)),
        "a0": row(rwkv_a0[0]),
        "aup": _bf(_pad_rows(rwkv_a_up[0], 128)),
        "gup": _bf(rwkv_g_up[0]),
        "k_k": row(rwkv_k_k[0]),
        "k_a": row(rwkv_k_a[0]),
        "r_k": row(rwkv_r_k[0]),
        "gs": _bf((hid[:, None] == hid[None, :]).astype(F32)),
        "qn": tile_heads(fox_q_norm[0]),
        "kn": tile_heads(fox_k_norm[0]),
        "fb": _pad_cols(row(fox_f_bias[0]), 128),
    }
    nbp, Tp, _ = x_prompt.shape
    nbs, Ts, _ = x_sample.shape
    s0_prompt = jnp.zeros((nbp, N_HEADS, HEAD_DIM, HEAD_DIM), F32)
    shift0_prompt = jnp.zeros((nbp, 1, RWKV_IN), F32)
    feats_p, st_p = _mixer(x_prompt, shift0_prompt, s0_prompt, None, wts, min(Tp, 256))
    feats_s, st_s = _mixer(x_sample, state_rwkv_shift[0], state_rwkv_wkv[0],
                           (cache_fox_k[0], cache_fox_v[0], cache_fox_logf[0]), wts, Ts)

    n_p, n_s = nbp * Tp, nbs * Ts
    n_tok = n_p + n_s
    tm = math.gcd(math.gcd(n_p, n_s), TOKEN_TILE)
    streams = [(x_prompt.reshape(n_p, D_MODEL), x_sample.reshape(n_s, D_MODEL))] + list(zip(feats_p, feats_s))
    h1, xn2, scores = _out_proj(streams, row(rwkv_ln_w[0]), row(rwkv_ln_b[0]), wts["gs"], _bf(w_out[0]),
                                row(norm_ffn_g[0]), router_hi,
                                _bf(router_w[0] - router_hi.astype(F32)), tm)
    eidx_t, wts_t, rank_t, counts = _route(scores, router_bias[0].reshape(N_EXPERTS, 1).astype(F32), tm)
    n_blocks = -(-n_tok * TOP_K // MOE_ROWS) + N_EXPERTS
    start_col, first_blk, blk_count, n_used = _block_tables(counts)
    pos_t = _positions(eidx_t, rank_t, start_col, tm)
    xs, h1s = _dispatch(pos_t, xn2, h1, _bf(shared_w_gate[0]), _bf(shared_w_up[0]), _bf(shared_w_down[0]),
                        n_blocks * MOE_ROWS, tm)
    y_rows = _gmm(first_blk, blk_count, n_used, xs, exp_w_gate[0], exp_w_up[0], exp_w_down[0])
    p_pair = (p_prompt[0].reshape(n_p, PLE_DIM), p_sample[0].reshape(n_s, PLE_DIM))
    y_p, y_s = _final(pos_t, h1s, wts_t.T, p_pair, y_rows, row(ple_norm_g[0]), _bf(ple_w_gate[0]),
                      _bf(ple_w_proj[0]), row(final_norm_g), tm)
    y_prompt = y_p.reshape(nbp, Tp, D_MODEL)
    y_sample = y_s.reshape(nbs, Ts, D_MODEL)
    lead = lambda t: tuple(a[None] for a in t)
    return (y_prompt, y_sample) + lead(st_p) + lead(st_s)
```

```python
import functools
import math

import jax
import jax.numpy as jnp
from jax import lax
from jax.experimental import pallas as pl
from jax.experimental.pallas import tpu as pltpu

F32 = jnp.float32
BF16 = jnp.bfloat16

D_MODEL = 1024
HEAD_DIM = 64
RWKV_WIDTH = 512
FOX_WIDTH = 512
N_HEADS = 8
N_PAIRS = N_HEADS // 2
DECAY_RANK = 64
ICL_RANK = 64
GATE_RANK = 128
RWKV_IN = 3 * RWKV_WIDTH + DECAY_RANK + ICL_RANK + GATE_RANK
RWKV_PAD = 3 * RWKV_WIDTH + 3 * 128
FOX_PAD = 4 * FOX_WIDTH + 128
ATTN_SCALE = HEAD_DIM ** -0.5
N_EXPERTS = 256
N_GROUPS = 8
GROUP_SIZE = N_EXPERTS // N_GROUPS
TOPK_GROUPS = 4
TOP_K = 8
EXPERT_FF = 256
ROUTED_SCALE = 2.5
PLE_DIM = 256
RMS_EPS = 1e-6
GN_EPS = 64e-5
L2_EPS = 1e-12

LANES = 128
MXU_TILE = 256
ROW_CHUNKS = D_MODEL // LANES
PACK_CHUNKS = ROW_CHUNKS // 2
RWKV_CHUNK = 128
MOE_ROWS = 256
TOKEN_TILE = 256
CACHE_TILE = 512
SCAN_BATCH = 4
GMM_AHEAD = 4
GMM_IN_SLOTS = GMM_AHEAD + 2
GMM_OUT_SLOTS = 4
DMA_QUEUES = 2
COMBINE_UNROLL = 8
FOX_TQ, FOX_TK = 512, 1024
FOX_TK_CACHED = 768
VMEM_LIMIT = 56 * 1024 * 1024


def _cparams(*sem):
    return pltpu.CompilerParams(dimension_semantics=sem, vmem_limit_bytes=VMEM_LIMIT)


def _bf(x):
    return x.astype(BF16)


def _dot(a, b):
    return jnp.dot(a, b, preferred_element_type=F32)


def _dot_nt(a, b):
    return lax.dot_general(a, b, (((1,), (1,)), ((), ())), preferred_element_type=F32)


def _split2(x):
    hi = _bf(x)
    return hi, _bf(x - hi.astype(F32))


def _split3(x):
    hi = _bf(x)
    r1 = x - hi.astype(F32)
    mid = _bf(r1)
    return hi, mid, _bf(r1 - mid.astype(F32))


def _dot_x01(x, w01):
    hi, lo = _split2(x)
    slabs = [slice(c, c + MXU_TILE) for c in range(0, x.shape[1], MXU_TILE)]
    return jnp.concatenate([_dot(hi[:, s], w01) + _dot(lo[:, s], w01) for s in slabs], axis=1)


def _dot3(a, b):
    ah, al = _split2(a)
    bh, bl = _split2(b)
    return _dot(ah, bh) + _dot(al, bh) + _dot(ah, bl)


def _dot3_nt(a, b):
    ah, al = _split2(a)
    bh, bl = _split2(b)
    return _dot_nt(ah, bh) + _dot_nt(al, bh) + _dot_nt(ah, bl)


def _softplus(x):
    return jnp.maximum(x, 0.0) + jnp.log1p(jnp.exp(-jnp.abs(x)))


def _rms(x, g):
    return x * lax.rsqrt(jnp.mean(x * x, axis=-1, keepdims=True) + RMS_EPS) * g


def _full(shape):
    return pl.BlockSpec(shape, lambda *_: (0,) * len(shape))


def _store_chunked(ref, x):
    n = x.shape[0]
    for s in range(ROW_CHUNKS):
        ref[pl.ds(s, n, stride=ROW_CHUNKS), :] = x[:, s * LANES:(s + 1) * LANES]


def _load_chunked(ref, n):
    return jnp.concatenate([ref[pl.ds(s, n, stride=ROW_CHUNKS), :] for s in range(ROW_CHUNKS)], axis=1)


def _store_packed(ref, x):
    n, half = x.shape[0], D_MODEL // 2
    bits = lax.bitcast_convert_type(_bf(x).astype(F32), jnp.uint32)
    packed = (bits[:, :half] >> 16) | (bits[:, half:] & jnp.uint32(0xFFFF0000))
    for s in range(PACK_CHUNKS):
        ref[pl.ds(s, n, stride=PACK_CHUNKS), :] = packed[:, s * LANES:(s + 1) * LANES]


def _load_packed(ref, n):
    packed = jnp.concatenate([ref[pl.ds(s, n, stride=PACK_CHUNKS), :] for s in range(PACK_CHUNKS)], axis=1)
    lo = lax.bitcast_convert_type(packed << 16, F32)
    hi = lax.bitcast_convert_type(packed & jnp.uint32(0xFFFF0000), F32)
    return _bf(jnp.concatenate([lo, hi], axis=1))


def _in_rwkv_kernel(x_ref, shift_ref, gmix_ref, w_ref, mu_ref, w0_ref, wup_ref, a0_ref, aup_ref, gup_ref,
                    kk_ref, ka_ref, rk_ref, gs_ref,
                    r_out, lw_out, kh_out, v_out, kkn_out, bb_out, g_out, bonus_out, last_out,
                    carry_ref):
    j = pl.program_id(1)
    tm = x_ref.shape[1]
    xn = _bf(_rms(x_ref[0], gmix_ref[...]))
    u = _dot(xn, w_ref[...])
    first = jnp.where(j == 0, shift_ref[0], carry_ref[...])
    row = lax.broadcasted_iota(jnp.int32, (tm, 1), 0)
    prev = jnp.where(row == 0, first, pltpu.roll(u, 1, axis=0))
    carry_ref[...] = u[tm - 1:tm, :]
    last_out[0] = u[tm - 1:tm, :]
    xs = u + (prev - u) * mu_ref[...]
    W = RWKV_WIDTH
    r, k, v = xs[:, :W], xs[:, W:2 * W], xs[:, 2 * W:3 * W]
    xw, xa, xg = xs[:, 3 * W:3 * W + 128], xs[:, 3 * W + 128:3 * W + 256], xs[:, 3 * W + 256:]
    w_raw = w0_ref[...] + _dot(_bf(jnp.tanh(xw)), wup_ref[...])
    lw = -jnp.exp(-_softplus(-w_raw) - 0.5)
    a = jax.nn.sigmoid(a0_ref[...] + _dot(_bf(xa), aup_ref[...]))
    g = _dot(_bf(jax.nn.sigmoid(xg)), gup_ref[...])
    gs = gs_ref[...]
    kk = k * kk_ref[...]
    kkn = kk / jnp.maximum(jnp.sqrt(_dot_x01(kk * kk, gs)), L2_EPS)
    kh = k * (1.0 + (a - 1.0) * ka_ref[...])
    r_out[0] = r
    lw_out[0] = lw
    kh_out[0] = kh
    v_out[0] = v
    kkn_out[0] = kkn
    bb_out[0] = kkn * a
    g_out[0] = g
    bonus_out[0] = _dot_x01(r * kh * rk_ref[...], gs) * v


def _in_rwkv(x, shift, gmix, w, mu, w0, wup, a0, aup, gup, k_k, k_a, r_k, gs, tm):
    nb, T, _ = x.shape
    W = RWKV_WIDTH
    tok = lambda width: pl.BlockSpec((1, tm, width), lambda b, j: (b, j, 0))
    outs = [jax.ShapeDtypeStruct((nb, T, W), F32)] * 8 + [jax.ShapeDtypeStruct((nb, 1, RWKV_PAD), F32)]
    return pl.pallas_call(
        _in_rwkv_kernel,
        grid=(nb, T // tm),
        in_specs=[tok(D_MODEL), pl.BlockSpec((1, 1, RWKV_PAD), lambda b, j: (b, 0, 0)),
                  _full((1, D_MODEL)), _full((D_MODEL, RWKV_PAD)), _full((1, RWKV_PAD)),
                  _full((1, W)), _full((128, W)), _full((1, W)), _full((128, W)), _full((128, W)),
                  _full((1, W)), _full((1, W)), _full((1, W)), _full((MXU_TILE, MXU_TILE))],
        out_specs=[tok(W)] * 8 + [pl.BlockSpec((1, 1, RWKV_PAD), lambda b, j: (b, 0, 0))],
        out_shape=outs,
        scratch_shapes=[pltpu.VMEM((1, RWKV_PAD), F32)],
        compiler_params=_cparams("parallel", "arbitrary"),
        name="in_rwkv",
    )(x, shift, gmix, w, mu, w0, wup, a0, aup, gup, k_k, k_a, r_k, gs)


def _aug_rows(x, c, is_query):
    lane = lax.broadcasted_iota(jnp.int32, (1, LANES), 1)
    rows = []
    for p in range(N_PAIRS):
        xp = x[:, p * LANES:(p + 1) * LANES]
        xr = pltpu.roll(xp, HEAD_DIM, axis=1)
        for h in range(2):
            ch = c[:, 2 * p + h:2 * p + h + 1]
            hi = _bf(ch).astype(F32)
            r1 = ch - hi
            mid = _bf(r1).astype(F32)
            lo = _bf(r1 - mid).astype(F32)
            one = jnp.ones_like(ch)
            cols = (hi, mid, lo, one, one, one) if is_query else (one, one, one, -hi, -mid, -lo)
            aug = jnp.zeros_like(xp)
            for n, col in enumerate(cols):
                aug = jnp.where(lane == HEAD_DIM + n, col, aug)
            rows.append(_bf(jnp.where(lane < HEAD_DIM, xp if h == 0 else xr, aug)))
    return rows


def _in_fox_kernel(x_ref, gmix_ref, w_ref, qn_ref, kn_ref, fb_ref, gs_ref, tri_ref, c0_ref,
                   q_out, ka_out, k_out, v_out, vb_out, og_out, lf_out, carry_ref):
    j = pl.program_id(1)
    tm = x_ref.shape[1]

    @pl.when(j == 0)
    def _():
        carry_ref[...] = c0_ref[0]

    xn = _bf(_rms(x_ref[0], gmix_ref[...]))
    u = _dot(xn, w_ref[...])
    W = FOX_WIDTH
    q, k, v, og, fl = u[:, :W], u[:, W:2 * W], u[:, 2 * W:3 * W], u[:, 3 * W:4 * W], u[:, 4 * W:]
    gs = gs_ref[...]
    inv_d = 1.0 / HEAD_DIM
    qn = q * lax.rsqrt(_dot_x01(q * q, gs) * inv_d + RMS_EPS) * qn_ref[...]
    kn = k * lax.rsqrt(_dot_x01(k * k, gs) * inv_d + RMS_EPS) * kn_ref[...]
    logf = -_softplus(-(fl + fb_ref[...]))
    hi, mid, lo = _split3(logf)
    tri = tri_ref[...]
    c = _dot(tri, hi) + _dot(tri, mid) + _dot(tri, lo) + carry_ref[...]
    carry_ref[...] = c[tm - 1:tm, :]
    for h, row in enumerate(_aug_rows(_bf(qn * ATTN_SCALE).astype(F32), c, True)):
        q_out[0, h] = row
    for h, row in enumerate(_aug_rows(_bf(kn).astype(F32), c, False)):
        ka_out[0, h] = row
    k_out[0] = kn
    v_out[0] = v
    vb_out[0] = _bf(v)
    og_out[0] = og
    lf_out[0] = logf


def _in_fox(x, c0, gmix, w, qn, kn, fb, gs, tm):
    nb, T, _ = x.shape
    W = FOX_WIDTH
    tok = lambda width: pl.BlockSpec((1, tm, width), lambda b, j: (b, j, 0))
    aug = pl.BlockSpec((1, N_HEADS, tm, LANES), lambda b, j: (b, 0, j, 0))
    sds = lambda width, dt: jax.ShapeDtypeStruct((nb, T, width), dt)
    aug_shape = jax.ShapeDtypeStruct((nb, N_HEADS, T, LANES), BF16)
    tri = _bf(jnp.tril(jnp.ones((tm, tm), F32)))
    return pl.pallas_call(
        _in_fox_kernel,
        grid=(nb, T // tm),
        in_specs=[tok(D_MODEL), _full((1, D_MODEL)), _full((D_MODEL, FOX_PAD)),
                  _full((1, W)), _full((1, W)), _full((1, 128)), _full((MXU_TILE, MXU_TILE)), _full((tm, tm)),
                  pl.BlockSpec((1, 1, LANES), lambda b, j: (b, 0, 0))],
        out_specs=[aug, aug] + [tok(W)] * 4 + [tok(128)],
        out_shape=[aug_shape, aug_shape, sds(W, F32), sds(W, F32), sds(W, BF16), sds(W, F32), sds(128, F32)],
        scratch_shapes=[pltpu.VMEM((1, LANES), F32)],
        compiler_params=_cparams("parallel", "arbitrary"),
        name="in_fox",
    )(x, gmix, w, qn, kn, fb, gs, tri, c0)


def _cumsum_kernel(x_ref, tri_ref, o_ref, carry_ref):
    j = pl.program_id(1)
    ts = x_ref.shape[1]

    @pl.when(j == 0)
    def _():
        carry_ref[...] = jnp.zeros_like(carry_ref)

    hi, mid, lo = _split3(x_ref[0])
    tri = tri_ref[...]
    c = _dot(tri, hi) + _dot(tri, mid) + _dot(tri, lo) + carry_ref[...]
    o_ref[0] = c
    carry_ref[...] = c[ts - 1:ts, :]


def _cumsum(x, ts):
    nb, S, L = x.shape
    tri = _bf(jnp.tril(jnp.ones((ts, ts), F32)))
    return pl.pallas_call(
        _cumsum_kernel,
        grid=(nb, S // ts),
        in_specs=[pl.BlockSpec((1, ts, L), lambda b, j: (b, j, 0)), _full((ts, ts))],
        out_specs=pl.BlockSpec((1, ts, L), lambda b, j: (b, j, 0)),
        out_shape=jax.ShapeDtypeStruct((nb, S, L), F32),
        scratch_shapes=[pltpu.VMEM((1, L), F32)],
        compiler_params=_cparams("parallel", "arbitrary"),
        name="seq_cumsum",
    )(x, tri)


def _rwkv_chunk_kernel(r_ref, lw_ref, kh_ref, v_ref, kk_ref, bb_ref, tri_ref,
                       rh_out, yh_out, g_out, sh_out):
    C = r_ref.shape[1]
    ti = lax.broadcasted_iota(jnp.int32, (C, C), 0)
    si = lax.broadcasted_iota(jnp.int32, (C, C), 1)
    tx = ti ^ si
    strict = ti > si
    incl = ti >= si
    eye_c = (ti == si).astype(F32)
    lane = lax.broadcasted_iota(jnp.int32, (1, LANES), 1)
    head0 = lane < HEAD_DIM
    pi = lax.broadcasted_iota(jnp.int32, (LANES, LANES), 0)
    pj = lax.broadcasted_iota(jnp.int32, (LANES, LANES), 1)
    same_head = (pi < HEAD_DIM) == (pj < HEAD_DIM)
    eye_p = (pi == pj).astype(F32)
    tri = tri_ref[...]

    pairs = range(N_PAIRS)
    heads = [(p, h) for p in pairs for h in range(2)]
    P = []
    for p in pairs:
        sl = slice(p * LANES, (p + 1) * LANES)
        r, lw, kh, v, kk, bb = (ref[0, :, sl] for ref in (r_ref, lw_ref, kh_ref, v_ref, kk_ref, bb_ref))
        l_hi, l_mid, l_lo = _split3(lw)
        lc = _dot(tri, l_hi) + _dot(tri, l_mid) + _dot(tri, l_lo)
        mid = lc[C // 2 - 1:C // 2, :]
        last = lc[C - 1:C, :]
        e_dn = jnp.exp(mid - lc)
        e_end = jnp.exp(last - lc)
        aa = kk * jnp.exp(lc - lw - mid)
        P.append(dict(sl=sl, v=v, aa=aa, rt=r * jnp.exp(lc - mid), rho=jnp.exp(mid), g_last=jnp.exp(last),
                      bt_b=_bf(bb * e_dn), kt_b=_bf(kh * e_dn), bc_b=_bf(bb * e_end), kc_b=_bf(kh * e_end),
                      aa_b=_bf(aa), v_b=_bf(v)))
    lab, lak, mrb, mrk = {}, {}, {}, {}
    for p, h in heads:
        q = P[p]
        hm = head0 if h == 0 else jnp.logical_not(head0)
        aa_m = _bf(jnp.where(hm, q["aa"], 0.0))
        rt_m = _bf(jnp.where(hm, q["rt"], 0.0))
        lab[p, h] = jnp.where(strict, _dot_nt(aa_m, q["bt_b"]), 0.0)
        lak[p, h] = _bf(jnp.where(strict, _dot_nt(aa_m, q["kt_b"]), 0.0))
        mrb[p, h] = _bf(jnp.where(incl, _dot_nt(rt_m, q["bt_b"]), 0.0))
        mrk[p, h] = _bf(jnp.where(incl, _dot_nt(rt_m, q["kt_b"]), 0.0))
    d = {k: eye_c - jnp.where(tx < 2, lab[k], 0.0) for k in heads}
    s = 2
    while s < C:
        level = (tx >= s) & (tx < 2 * s)
        d_b = {k: _bf(d[k]) for k in heads}
        t1 = {k: _bf(_dot(d_b[k], _bf(jnp.where(level, lab[k], 0.0)))) for k in heads}
        d = {k: d[k] - _dot(t1[k], d_b[k]) for k in heads}
        s *= 2
    d_b = {k: _bf(d[k]) for k in heads}
    w = {k: _bf(_dot(lak[k], P[k[0]]["v_b"])) for k in heads}
    ah = {k: _dot(d_b[k], P[k[0]]["aa_b"]) * P[k[0]]["rho"] for k in heads}
    uh = {k: _dot(d_b[k], w[k]) for k in heads}
    rh = {k: P[k[0]]["rt"] * P[k[0]]["rho"] - _dot(mrb[k], _bf(ah[k])) for k in heads}
    yh = {k: _dot(mrk[k], P[k[0]]["v_b"]) - _dot(mrb[k], _bf(uh[k])) for k in heads}
    for p in pairs:
        q = P[p]
        both = lambda x: jnp.where(head0, x[p, 0], x[p, 1])
        ah_p, uh_p = both(ah), both(uh)
        rh_out[0, :, q["sl"]] = both(rh)
        yh_out[0, :, q["sl"]] = both(yh)
        g_full = eye_p * q["g_last"] - _dot(_bf(ah_p.T), q["bc_b"])
        sh_full = _dot(_bf(q["v"].T), q["kc_b"]) - _dot(_bf(uh_p.T), q["bc_b"])
        g_out[0, 0, p] = jnp.where(same_head, g_full, 0.0)
        sh_out[0, 0, p] = jnp.where(same_head, sh_full, 0.0)


def _rwkv_chunks(r, lw, kh, v, kk, bb):
    nb, T, W = r.shape
    C = RWKV_CHUNK
    nc = T // C
    tri = _bf(jnp.tril(jnp.ones((C, C), F32)))
    tok = pl.BlockSpec((1, C, W), lambda b, c: (b, c, 0))
    mat = pl.BlockSpec((1, 1, N_PAIRS, LANES, LANES), lambda b, c: (b, c, 0, 0, 0))
    mat_shape = jax.ShapeDtypeStruct((nb, nc, N_PAIRS, LANES, LANES), F32)
    return pl.pallas_call(
        _rwkv_chunk_kernel,
        grid=(nb, nc),
        in_specs=[tok] * 6 + [_full((C, C))],
        out_specs=[tok, tok, mat, mat],
        out_shape=[jax.ShapeDtypeStruct((nb, T, W), F32)] * 2 + [mat_shape] * 2,
        compiler_params=_cparams("parallel", "parallel"),
        name="rwkv_chunks",
    )(r, lw, kh, v, kk, bb, tri)


def _rwkv_scan_kernel(rh_ref, yh_ref, g_ref, sh_ref, s0_ref, y_out, sfin_out, s_scr):
    c = pl.program_id(1)
    nbg = rh_ref.shape[0]

    @pl.when(c == 0)
    def _():
        s_scr[...] = s0_ref[...]

    for b in range(nbg):
        for p in range(N_PAIRS):
            sl = slice(p * LANES, (p + 1) * LANES)
            s = s_scr[b, p]
            y_out[b, :, sl] = _dot3_nt(rh_ref[b, :, sl], s) + yh_ref[b, :, sl]
            s_new = _dot3(s, g_ref[b, 0, p]) + sh_ref[b, 0, p]
            s_scr[b, p] = s_new
            sfin_out[b, p] = s_new


def _rwkv_scan(rh, yh, g, sh, s0, nbg):
    nb, T, W = rh.shape
    C = RWKV_CHUNK
    nc = T // C
    tok = pl.BlockSpec((nbg, C, W), lambda i, c: (i, c, 0))
    mat = pl.BlockSpec((nbg, 1, N_PAIRS, LANES, LANES), lambda i, c: (i, c, 0, 0, 0))
    st = pl.BlockSpec((nbg, N_PAIRS, LANES, LANES), lambda i, c: (i, 0, 0, 0))
    return pl.pallas_call(
        _rwkv_scan_kernel,
        grid=(nb // nbg, nc),
        in_specs=[tok, tok, mat, mat, st],
        out_specs=[tok, st],
        out_shape=[jax.ShapeDtypeStruct((nb, T, W), F32),
                   jax.ShapeDtypeStruct((nb, N_PAIRS, LANES, LANES), F32)],
        scratch_shapes=[pltpu.VMEM((nbg, N_PAIRS, LANES, LANES), F32)],
        compiler_params=_cparams("parallel", "arbitrary"),
        name="rwkv_scan",
    )(rh, yh, g, sh, s0)


def _state_to_pairs(s):
    nb = s.shape[0]
    s = s.reshape(nb, N_PAIRS, 2, HEAD_DIM, HEAD_DIM)
    z = jnp.zeros_like(s[:, :, 0])
    top = jnp.concatenate([s[:, :, 0], z], axis=-1)
    bot = jnp.concatenate([z, s[:, :, 1]], axis=-1)
    return jnp.concatenate([top, bot], axis=-2)


def _pairs_to_state(sp):
    nb = sp.shape[0]
    a = sp[:, :, :HEAD_DIM, :HEAD_DIM]
    b = sp[:, :, HEAD_DIM:, HEAD_DIM:]
    return jnp.stack([a, b], axis=2).reshape(nb, N_HEADS, HEAD_DIM, HEAD_DIM)


def _fox_aug_kernel(x_ref, c_ref, o_ref):
    for h, row in enumerate(_aug_rows(x_ref[0].astype(F32), c_ref[0], False)):
        o_ref[0, h] = row


def _fox_aug(x, c, tm):
    nb, S, W = x.shape
    return pl.pallas_call(
        _fox_aug_kernel,
        grid=(nb, S // tm),
        in_specs=[pl.BlockSpec((1, tm, W), lambda b, j: (b, j, 0)),
                  pl.BlockSpec((1, tm, LANES), lambda b, j: (b, j, 0))],
        out_specs=pl.BlockSpec((1, N_HEADS, tm, LANES), lambda b, j: (b, 0, j, 0)),
        out_shape=jax.ShapeDtypeStruct((nb, N_HEADS, S, LANES), BF16),
        compiler_params=_cparams("parallel", "parallel"),
        name="fox_aug",
    )(x, c)


def _fox_kernel(q_ref, k_ref, v_ref, o_ref, m_scr, l_scr, acc_scr, sa_scr, sb_scr, *, tk, q_off):
    i = pl.program_id(2)
    tq = q_ref.shape[2]
    q_start = q_off + i * tq
    m_scr[...] = jnp.full_like(m_scr, -jnp.inf)
    l_scr[...] = jnp.zeros_like(l_scr)
    acc_scr[...] = jnp.zeros_like(acc_scr)
    n_full = (q_start + 1) // tk

    def scores_into(j, dst):
        ks = pl.multiple_of(j * tk, tk)
        for h in range(2):
            dst[h] = _dot_nt(q_ref[0, h], k_ref[0, h, pl.ds(ks, tk), :])

    def update_from(j, src, masked, width=tk):
        ks = pl.multiple_of(j * tk, tk)
        v = v_ref[0, pl.ds(ks, width), :]
        if masked:
            visible = (ks + lax.broadcasted_iota(jnp.int32, (1, width), 1)) <= (
                q_start + lax.broadcasted_iota(jnp.int32, (tq, 1), 0))
        for h in range(2):
            s = src[h, :, :width]
            if masked:
                s = jnp.where(visible, s, -jnp.inf)
            m_old = m_scr[h]
            m_new = jnp.maximum(m_old, jnp.max(s, axis=-1, keepdims=True))
            alpha = jnp.exp(m_old - m_new)
            pr = jnp.exp(s - m_new)
            l_scr[h] = alpha * l_scr[h] + jnp.sum(pr, axis=-1, keepdims=True)
            acc_scr[h] = alpha * acc_scr[h] + _dot(_bf(pr), v)
            m_scr[h] = m_new

    def step(j, src, dst):
        scores_into(j + 1, dst)
        update_from(j, src, False)

    def two_steps(jj, carry):
        step(2 * jj, sa_scr, sb_scr)
        step(2 * jj + 1, sb_scr, sa_scr)
        return carry

    scores_into(0, sa_scr)
    lax.fori_loop(0, n_full // 2, two_steps, 0)
    odd = n_full % 2 == 1

    def diagonal_block(src):
        half = q_start + tq - n_full * tk <= tk // 2

        @pl.when(half)
        def _():
            update_from(n_full, src, True, tk // 2)

        @pl.when(jnp.logical_not(half))
        def _():
            update_from(n_full, src, True)

    @pl.when(odd)
    def _():
        step(n_full - 1, sa_scr, sb_scr)
        diagonal_block(sb_scr)

    @pl.when(jnp.logical_not(odd))
    def _():
        diagonal_block(sa_scr)

    head0 = lax.broadcasted_iota(jnp.int32, (1, LANES), 1) < HEAD_DIM
    o_ref[0] = jnp.where(head0, acc_scr[0] / l_scr[0], acc_scr[1] / l_scr[1])


def _fox_attention(q, k, v, tq, tk, q_off):
    nb, _, sq, _ = q.shape
    sk = k.shape[2]
    for q_start in range(q_off, q_off + sq, tq):
        assert (q_start + 1) // tk + 1 == -(-(q_start + tq) // tk) <= sk // tk, (q_start, tq, tk)
    return pl.pallas_call(
        functools.partial(_fox_kernel, tk=tk, q_off=q_off),
        grid=(nb, N_PAIRS, sq // tq),
        in_specs=[pl.BlockSpec((1, 2, tq, LANES), lambda b, p, i: (b, p, i, 0)),
                  pl.BlockSpec((1, 2, sk, LANES), lambda b, p, i: (b, p, 0, 0)),
                  pl.BlockSpec((1, sk, LANES), lambda b, p, i: (b, 0, p))],
        out_specs=pl.BlockSpec((1, tq, LANES), lambda b, p, i: (b, i, p)),
        out_shape=jax.ShapeDtypeStruct((nb, sq, FOX_WIDTH), F32),
        scratch_shapes=[pltpu.VMEM((2, tq, 1), F32), pltpu.VMEM((2, tq, 1), F32),
                        pltpu.VMEM((2, tq, LANES), F32), pltpu.VMEM((2, tq, tk), F32),
                        pltpu.VMEM((2, tq, tk), F32)],
        compiler_params=_cparams("parallel", "parallel", "arbitrary"),
        name="fox_attention",
    )(q, k, v)


def _out_kernel(*refs, n_first):
    tok_refs, rest = refs[:12], refs[12:]
    lnw_ref, lnb_ref, gs_ref, wout_ref, gffn_ref, rwh_ref, rwl_ref, h_out, xn_out, sc_out = rest
    first = pl.program_id(0) < n_first
    x, y, bonus, g, oa, og = (jnp.where(first, tok_refs[2 * n][...], tok_refs[2 * n + 1][...]) for n in range(6))
    gs = gs_ref[...]
    inv_d = 1.0 / HEAD_DIM
    mean = _dot_x01(y, gs) * inv_d
    d = y - mean
    var = _dot_x01(d * d, gs) * inv_d
    yn = d * lax.rsqrt(var + GN_EPS) * lnw_ref[...] + lnb_ref[...]
    o_rwkv = (yn + bonus) * g
    o_fox = oa * jax.nn.sigmoid(og)
    mix = jnp.concatenate([_bf(o_rwkv), _bf(o_fox)], axis=-1)
    h = x + _dot(mix, wout_ref[...])
    h_out[...] = h
    xn = _rms(h, gffn_ref[...])
    _store_packed(xn_out, xn)
    xh, xl = _split2(xn)
    logits = _dot(xh, rwh_ref[...]) + _dot(xl, rwh_ref[...]) + _dot(xh, rwl_ref[...])
    sc_out[...] = jax.nn.sigmoid(logits)


def _two_streams(width, tm, n_first):
    return [pl.BlockSpec((tm, width), lambda i: (jnp.minimum(i, n_first - 1), 0)),
            pl.BlockSpec((tm, width), lambda i: (jnp.maximum(i - n_first, 0), 0))]


def _out_proj(streams, lnw, lnb, gs, wout, gffn, rwh, rwl, tm):
    n_first = streams[0][0].shape[0] // tm
    T = streams[0][0].shape[0] + streams[0][1].shape[0]
    W = RWKV_WIDTH
    tok = lambda width: pl.BlockSpec((tm, width), lambda i: (i, 0))
    tok_specs, tok_args = [], []
    for a, b in streams:
        tok_specs += _two_streams(a.shape[1], tm, n_first)
        tok_args += [a, b]
    return pl.pallas_call(
        functools.partial(_out_kernel, n_first=n_first),
        grid=(T // tm,),
        in_specs=tok_specs + [_full((1, W)), _full((1, W)), _full((MXU_TILE, MXU_TILE)),
                              _full((D_MODEL, D_MODEL)), _full((1, D_MODEL)),
                              _full((D_MODEL, N_EXPERTS)), _full((D_MODEL, N_EXPERTS))],
        out_specs=[tok(D_MODEL), pl.BlockSpec((tm * PACK_CHUNKS, LANES), lambda i: (i, 0)), tok(N_EXPERTS)],
        out_shape=[jax.ShapeDtypeStruct((T, D_MODEL), F32),
                   jax.ShapeDtypeStruct((T * PACK_CHUNKS, LANES), jnp.uint32),
                   jax.ShapeDtypeStruct((T, N_EXPERTS), F32)],
        compiler_params=_cparams("parallel"),
        name="out_proj",
    )(*tok_args, lnw, lnb, gs, wout, gffn, rwh, rwl)


def _route_kernel(sc_ref, bias_ref, before_ref, idx_out, wt_out, rank_out, cnt_out, cnt_scr):
    tm = sc_ref.shape[0]
    neg = -jnp.inf

    @pl.when(pl.program_id(0) == 0)
    def _():
        cnt_scr[...] = jnp.zeros_like(cnt_scr)

    st = sc_ref[...].T
    sel = st + bias_ref[...]
    gscore = []
    for gi in range(N_GROUPS):
        blk = sel[gi * GROUP_SIZE:(gi + 1) * GROUP_SIZE, :]
        m1 = jnp.max(blk, axis=0, keepdims=True)
        n1 = jnp.sum((blk == m1).astype(F32), axis=0, keepdims=True)
        m2 = jnp.max(jnp.where(blk < m1, blk, neg), axis=0, keepdims=True)
        gscore.append(m1 + jnp.where(n1 > 1.0, m1, m2))
    taken = [jnp.zeros((1, tm), jnp.bool_) for _ in range(N_GROUPS)]
    for _ in range(TOPK_GROUPS):
        avail = [jnp.where(taken[gi], neg, gscore[gi]) for gi in range(N_GROUPS)]
        best = functools.reduce(jnp.maximum, avail)
        found = jnp.zeros((1, tm), jnp.bool_)
        for gi in range(N_GROUPS):
            hit = (avail[gi] == best) & jnp.logical_not(found)
            taken[gi] = taken[gi] | hit
            found = found | hit
    cand = jnp.concatenate(
        [jnp.where(taken[gi], sel[gi * GROUP_SIZE:(gi + 1) * GROUP_SIZE, :], neg) for gi in range(N_GROUPS)], axis=0)
    eid = lax.broadcasted_iota(jnp.int32, (N_EXPERTS, tm), 0).astype(F32)
    idxs, wts = [], []
    onehot = jnp.zeros((N_EXPERTS, tm), F32)
    for _ in range(TOP_K):
        best = jnp.max(cand, axis=0, keepdims=True)
        pick = jnp.min(jnp.where(cand == best, eid, float(N_EXPERTS)), axis=0, keepdims=True)
        chosen = eid == pick
        wts.append(jnp.sum(jnp.where(chosen, st, 0.0), axis=0, keepdims=True))
        idxs.append(pick)
        cand = jnp.where(chosen, neg, cand)
        onehot = jnp.where(chosen, 1.0, onehot)
    w = jnp.concatenate(wts, axis=0)
    idx_out[...] = jnp.concatenate(idxs, axis=0).astype(jnp.int32)
    wt_out[...] = w / jnp.sum(w, axis=0, keepdims=True) * ROUTED_SCALE
    earlier = _dot(_bf(onehot), before_ref[...]) + cnt_scr[...]
    rank_out[...] = jnp.concatenate(
        [jnp.sum(jnp.where(eid == pick, earlier, 0.0), axis=0, keepdims=True) for pick in idxs],
        axis=0).astype(jnp.int32)
    cnt_scr[...] += jnp.sum(onehot, axis=1, keepdims=True)
    cnt_out[...] = cnt_scr[...]


def _route(scores, bias_col, tm):
    T = scores.shape[0]
    before = _bf(jnp.triu(jnp.ones((tm, tm), F32), 1))
    tok = pl.BlockSpec((TOP_K, tm), lambda i: (0, i))
    return pl.pallas_call(
        _route_kernel,
        grid=(T // tm,),
        in_specs=[pl.BlockSpec((tm, N_EXPERTS), lambda i: (i, 0)), _full((N_EXPERTS, 1)), _full((tm, tm))],
        out_specs=[tok, tok, tok, _full((N_EXPERTS, 1))],
        out_shape=[jax.ShapeDtypeStruct((TOP_K, T), jnp.int32), jax.ShapeDtypeStruct((TOP_K, T), F32),
                   jax.ShapeDtypeStruct((TOP_K, T), jnp.int32), jax.ShapeDtypeStruct((N_EXPERTS, 1), F32)],
        scratch_shapes=[pltpu.VMEM((N_EXPERTS, 1), F32)],
        compiler_params=_cparams("arbitrary"),
        name="route",
    )(scores, bias_col, before)


def _pos_kernel(idx_ref, rank_ref, start_ref, pos_out):
    tm = idx_ref.shape[1]
    eid = lax.broadcasted_iota(jnp.int32, (N_EXPERTS, tm), 0)
    idx = idx_ref[...]
    start = start_ref[...]
    base = jnp.concatenate(
        [jnp.sum(jnp.where(eid == idx[k:k + 1, :], start, 0.0), axis=0, keepdims=True) for k in range(TOP_K)],
        axis=0)
    pos_out[...] = rank_ref[...] + base.astype(jnp.int32)


def _positions(eidx_t, rank_t, start_col, tm):
    T = eidx_t.shape[1]
    tok = pl.BlockSpec((TOP_K, tm), lambda i: (0, i))
    return pl.pallas_call(
        _pos_kernel,
        grid=(T // tm,),
        in_specs=[tok, tok, _full((N_EXPERTS, 1))],
        out_specs=tok,
        out_shape=jax.ShapeDtypeStruct((TOP_K, T), jnp.int32),
        compiler_params=_cparams("parallel"),
        name="moe_positions",
    )(eidx_t, rank_t, start_col)


def _dispatch_kernel(pos_ref, x_ref, h_ref, sg_ref, su_ref, sd_ref, xs_in, xs_out, hs_out, sem):
    del xs_in
    tm = pos_ref.shape[1]

    def issue(t, carry):
        src = x_ref.at[pl.ds(pl.multiple_of(t * PACK_CHUNKS, PACK_CHUNKS), PACK_CHUNKS), :]
        for k in range(TOP_K):
            row = pl.multiple_of(pos_ref[k, t] * PACK_CHUNKS, PACK_CHUNKS)
            pltpu.make_async_copy(src, xs_out.at[pl.ds(row, PACK_CHUNKS), :], sem).start(priority=k % DMA_QUEUES)
        return carry

    lax.fori_loop(0, tm, issue, 0)
    xb = _load_packed(x_ref, tm)
    hg = _dot(xb, sg_ref[...])
    hu = _dot(xb, su_ref[...])
    hs_out[...] = h_ref[...] + _dot(_bf(hg * jax.nn.sigmoid(hg) * hu), sd_ref[...])
    for k in range(TOP_K):
        pltpu.make_async_copy(x_ref, xs_out.at[pl.ds(0, tm * PACK_CHUNKS), :], sem).wait()


def _dispatch(pos_t, xn_packed, h, sg, su, sd, n_rows, tm):
    T = pos_t.shape[1]
    xs0 = jnp.zeros((n_rows * PACK_CHUNKS, LANES), jnp.uint32)
    return pl.pallas_call(
        _dispatch_kernel,
        grid=(T // tm,),
        in_specs=[pl.BlockSpec((TOP_K, tm), lambda i: (0, i), memory_space=pltpu.SMEM),
                  pl.BlockSpec((tm * PACK_CHUNKS, LANES), lambda i: (i, 0)),
                  pl.BlockSpec((tm, D_MODEL), lambda i: (i, 0)),
                  _full((D_MODEL, EXPERT_FF)), _full((D_MODEL, EXPERT_FF)), _full((EXPERT_FF, D_MODEL)),
                  pl.BlockSpec(memory_space=pl.ANY)],
        out_specs=[pl.BlockSpec(memory_space=pl.ANY), pl.BlockSpec((tm, D_MODEL), lambda i: (i, 0))],
        out_shape=[jax.ShapeDtypeStruct((n_rows * PACK_CHUNKS, LANES), jnp.uint32),
                   jax.ShapeDtypeStruct((T, D_MODEL), F32)],
        scratch_shapes=[pltpu.SemaphoreType.DMA(())],
        input_output_aliases={6: 0},
        compiler_params=_cparams("arbitrary"),
        name="moe_dispatch",
    )(pos_t, xn_packed, h, sg, su, sd, xs0)


def _gmm_kernel(first_ref, count_ref, used_ref, xs_hbm, wg_ref, wu_ref, wd_ref, y_hbm,
                xbuf, ybuf, wg_b, wu_b, wd_b, in_sems, out_sems):
    e = pl.program_id(0)
    bm = MOE_ROWS
    blk_rows = bm * ROW_CHUNKS
    n_used = used_ref[0]
    n_blocks = y_hbm.shape[0] // blk_rows

    in_rows = bm * PACK_CHUNKS

    def read(g, slot):
        return pltpu.make_async_copy(xs_hbm.at[pl.ds(pl.multiple_of(g * in_rows, in_rows), in_rows), :],
                                     xbuf.at[slot], in_sems.at[slot])

    def write(g, slot):
        return pltpu.make_async_copy(ybuf.at[slot],
                                     y_hbm.at[pl.ds(pl.multiple_of(g * blk_rows, blk_rows), blk_rows), :],
                                     out_sems.at[slot])

    @pl.when(e == 0)
    def _():
        for g0 in range(GMM_AHEAD):
            @pl.when(g0 < n_used)
            def _():
                read(g0, g0 % GMM_IN_SLOTS).start()

    wg_b[...] = _bf(wg_ref[0])
    wu_b[...] = _bf(wu_ref[0])
    wd_b[...] = _bf(wd_ref[0])

    def blocks(g, width):
        for j in range(width):
            read(g + j, (g + j) % GMM_IN_SLOTS).wait()
        for j in range(width):
            @pl.when(g + j + GMM_AHEAD < n_used)
            def _():
                read(g + j + GMM_AHEAD, (g + j + GMM_AHEAD) % GMM_IN_SLOTS).start()

            @pl.when(g + j >= GMM_OUT_SLOTS)
            def _():
                write(g + j - GMM_OUT_SLOTS, (g + j) % GMM_OUT_SLOTS).wait()
        for j in range(width):
            xe = _load_packed(xbuf.at[(g + j) % GMM_IN_SLOTS], bm)
            hg = _dot(xe, wg_b[...])
            hu = _dot(xe, wu_b[...])
            _store_chunked(ybuf.at[(g + j) % GMM_OUT_SLOTS], _dot(_bf(hg * jax.nn.sigmoid(hg) * hu), wd_b[...]))
        for j in range(width):
            write(g + j, (g + j) % GMM_OUT_SLOTS).start()

    first, count = first_ref[e], count_ref[e]

    def pair(jj, carry):
        blocks(first + 2 * jj, 2)
        return carry

    lax.fori_loop(0, count // 2, pair, 0)

    @pl.when(count % 2 == 1)
    def _():
        blocks(first + count - 1, 1)

    @pl.when(e == pl.num_programs(0) - 1)
    def _():
        for back in range(GMM_OUT_SLOTS, 0, -1):
            @pl.when(n_used >= back)
            def _():
                write(n_used - back, (n_used - back) % GMM_OUT_SLOTS).wait()
        ybuf[0] = jnp.zeros_like(ybuf[0])

        def fill(g, carry):
            write(g, 0).start()
            return carry

        def drain(g, carry):
            write(g, 0).wait()
            return carry

        lax.fori_loop(n_used, n_blocks, fill, 0)
        lax.fori_loop(n_used, n_blocks, drain, 0)


def _gmm(first_blk, blk_count, n_used, xs, wg, wu, wd):
    bm = MOE_ROWS
    n_rows = xs.shape[0] // PACK_CHUNKS
    wspec = lambda shape: pl.BlockSpec((1,) + shape, lambda e, *_: (e, 0, 0))
    grid_spec = pltpu.PrefetchScalarGridSpec(
        num_scalar_prefetch=3,
        grid=(N_EXPERTS,),
        in_specs=[pl.BlockSpec(memory_space=pl.ANY), wspec((D_MODEL, EXPERT_FF)), wspec((D_MODEL, EXPERT_FF)),
                  wspec((EXPERT_FF, D_MODEL))],
        out_specs=pl.BlockSpec(memory_space=pl.ANY),
        scratch_shapes=[pltpu.VMEM((GMM_IN_SLOTS, bm * PACK_CHUNKS, LANES), jnp.uint32),
                        pltpu.VMEM((GMM_OUT_SLOTS, bm * ROW_CHUNKS, LANES), F32),
                        pltpu.VMEM((D_MODEL, EXPERT_FF), BF16), pltpu.VMEM((D_MODEL, EXPERT_FF), BF16),
                        pltpu.VMEM((EXPERT_FF, D_MODEL), BF16),
                        pltpu.SemaphoreType.DMA((GMM_IN_SLOTS,)), pltpu.SemaphoreType.DMA((GMM_OUT_SLOTS,))],
    )
    return pl.pallas_call(
        _gmm_kernel,
        grid_spec=grid_spec,
        out_shape=jax.ShapeDtypeStruct((n_rows * ROW_CHUNKS, LANES), F32),
        compiler_params=_cparams("arbitrary"),
        name="expert_gmm",
    )(first_blk, blk_count, n_used, xs, wg, wu, wd)


def _final_kernel(pos_ref, nxt_ref, hs_ref, w_ref, pa_ref, pb_ref, ys_ref, gple_ref, wpg_ref, wpp_ref, gfin_ref,
                  ya_out, yb_out, buf, routed_scr, sems, *, n_first):
    i = pl.program_id(0)
    n = pl.num_programs(0)
    tm = hs_ref.shape[0]

    def gather(rows_ref, slot):
        def issue(t, carry):
            dst = pl.ds(pl.multiple_of(t * ROW_CHUNKS, ROW_CHUNKS), ROW_CHUNKS)
            for k in range(TOP_K):
                row = pl.multiple_of(rows_ref[k, t] * ROW_CHUNKS, ROW_CHUNKS)
                pltpu.make_async_copy(ys_ref.at[pl.ds(row, ROW_CHUNKS), :], buf.at[slot, k, dst, :],
                                      sems.at[slot, k]).start(priority=k % DMA_QUEUES)
            return carry

        lax.fori_loop(0, tm, issue, 0)

    slot = i % 2

    @pl.when(i == 0)
    def _():
        gather(pos_ref, 0)

    for nxt in range(2):
        @pl.when((i + 1 < n) & (slot != nxt))
        def _():
            gather(nxt_ref, nxt)

    pp = _dot(_bf(jnp.where(i < n_first, pa_ref[...], pb_ref[...])), wpp_ref[...])
    for k in range(TOP_K):
        pltpu.make_async_copy(ys_ref.at[pl.ds(0, tm * ROW_CHUNKS), :], buf.at[slot, k], sems.at[slot, k]).wait()
    def combine(t, carry):
        tile = pl.ds(pl.multiple_of(t * ROW_CHUNKS, ROW_CHUNKS), ROW_CHUNKS)
        acc = buf[slot, 0, tile, :] * w_ref[0, t]
        for k in range(1, TOP_K):
            acc = acc + buf[slot, k, tile, :] * w_ref[k, t]
        routed_scr[tile, :] = acc
        return carry

    lax.fori_loop(0, tm, combine, 0, unroll=COMBINE_UNROLL)
    h = hs_ref[...] + _load_chunked(routed_scr, tm)
    gate = jax.nn.sigmoid(_dot(_bf(_rms(h, gple_ref[...])), wpg_ref[...]))
    y = _rms(h + gate * pp, gfin_ref[...])

    @pl.when(i < n_first)
    def _():
        ya_out[...] = y

    @pl.when(i >= n_first)
    def _():
        yb_out[...] = y


def _final(pos_t, hs, w, p_pair, ys, gple, wpg, wpp, gfin, tm):
    T = hs.shape[0]
    n = T // tm
    n_first = p_pair[0].shape[0] // tm
    tok = lambda width: pl.BlockSpec((tm, width), lambda i: (i, 0))
    return pl.pallas_call(
        functools.partial(_final_kernel, n_first=n_first),
        grid=(n,),
        in_specs=[pl.BlockSpec((TOP_K, tm), lambda i: (0, i), memory_space=pltpu.SMEM),
                  pl.BlockSpec((TOP_K, tm), lambda i: (0, jnp.minimum(i + 1, n - 1)), memory_space=pltpu.SMEM),
                  tok(D_MODEL), pl.BlockSpec((TOP_K, tm), lambda i: (0, i), memory_space=pltpu.SMEM)]
                 + _two_streams(PLE_DIM, tm, n_first) + [
                  pl.BlockSpec(memory_space=pl.ANY),
                  _full((1, D_MODEL)), _full((D_MODEL, D_MODEL)), _full((PLE_DIM, D_MODEL)),
                  _full((1, D_MODEL))],
        out_specs=_two_streams(D_MODEL, tm, n_first),
        out_shape=[jax.ShapeDtypeStruct((n_first * tm, D_MODEL), F32),
                   jax.ShapeDtypeStruct((T - n_first * tm, D_MODEL), F32)],
        scratch_shapes=[pltpu.VMEM((2, TOP_K, tm * ROW_CHUNKS, LANES), F32),
                        pltpu.VMEM((tm * ROW_CHUNKS, LANES), F32),
                        pltpu.SemaphoreType.DMA((2, TOP_K))],
        compiler_params=_cparams("arbitrary"),
        name="ffn_tail",
    )(pos_t, pos_t, hs, w, *p_pair, ys, gple, wpg, wpp, gfin)


def _pad_cols(a, width):
    return jnp.pad(a, [(0, 0)] * (a.ndim - 1) + [(0, width - a.shape[-1])])


def _rwkv_pad_cols(a):
    W = RWKV_WIDTH
    o1, o2, o3 = 3 * W, 3 * W + DECAY_RANK, 3 * W + DECAY_RANK + ICL_RANK
    return jnp.concatenate([a[..., :o1], _pad_cols(a[..., o1:o2], 128), _pad_cols(a[..., o2:o3], 128),
                            a[..., o3:]], axis=-1)


def _rwkv_unpad_cols(a):
    W = RWKV_WIDTH
    return jnp.concatenate([a[..., :3 * W + DECAY_RANK], a[..., 3 * W + 128:3 * W + 128 + ICL_RANK],
                            a[..., 3 * W + 256:]], axis=-1)


def _pad_rows(a, rows):
    return jnp.pad(a, [(0, rows - a.shape[0])] + [(0, 0)] * (a.ndim - 1))


def _mixer(x, shift, s0, past, wts, tm):
    nb, T, _ = x.shape
    (r, lw, kh, v, kkn, bb, g, bonus, last) = _in_rwkv(
        x, _rwkv_pad_cols(shift), wts["gmix"], wts["w_rwkv"], wts["mu"], wts["w0"], wts["wup"], wts["a0"],
        wts["aup"], wts["gup"], wts["k_k"], wts["k_a"], wts["r_k"], wts["gs"], tm)
    if past is None:
        c0 = jnp.zeros((nb, 1, LANES), F32)
    else:
        k_past, v_past, lf_past = past
        P = k_past.shape[1]
        c_past = _cumsum(_pad_cols(lf_past.astype(F32), LANES), math.gcd(P, CACHE_TILE))
        c0 = c_past[:, P - 1:, :]
    q_aug, k_aug, k_f, v_f, v_b, og, logf = _in_fox(
        x, c0, wts["gmix"], wts["w_fox"], wts["qn"], wts["kn"], wts["fb"], wts["gs"], tm)

    C = RWKV_CHUNK
    Tp = -(-T // C) * C
    if Tp != T:
        padt = lambda a: jnp.pad(a, ((0, 0), (0, Tp - T), (0, 0)))
        r_p, lw_p, kh_p, v_p, kk_p, bb_p = (padt(a) for a in (r, lw, kh, v, kkn, bb))
    else:
        r_p, lw_p, kh_p, v_p, kk_p, bb_p = r, lw, kh, v, kkn, bb
    rh, yh, gm, sh = _rwkv_chunks(r_p, lw_p, kh_p, v_p, kk_p, bb_p)
    y, s_fin = _rwkv_scan(rh, yh, gm, sh, _state_to_pairs(s0.astype(F32)), math.gcd(nb, SCAN_BATCH))
    y = y[:, :T]

    if past is None:
        o_att = _fox_attention(q_aug, k_aug, v_b, min(T, FOX_TQ), min(T, FOX_TK), 0)
    else:
        tk = FOX_TK_CACHED
        sk = -(-(P + T) // tk) * tk
        k_aug_past = _fox_aug(_bf(k_past.reshape(nb, P, FOX_WIDTH)), c_past, math.gcd(P, CACHE_TILE))
        k_all = jnp.pad(jnp.concatenate([k_aug_past, k_aug], axis=2), ((0, 0), (0, 0), (0, sk - P - T), (0, 0)))
        v_all = jnp.pad(jnp.concatenate([_bf(v_past.reshape(nb, P, FOX_WIDTH)), v_b], axis=1),
                        ((0, 0), (0, sk - P - T), (0, 0)))
        o_att = _fox_attention(q_aug, k_all, v_all, T, tk, P)

    n = nb * T
    flat = lambda a: a.reshape(n, a.shape[-1])
    feats = (flat(y), flat(bonus), flat(g), flat(o_att), flat(og))
    state = (k_f.reshape(nb, T, N_HEADS, HEAD_DIM), v_f.reshape(nb, T, N_HEADS, HEAD_DIM),
             logf[:, :, :N_HEADS], _pairs_to_state(s_fin), _rwkv_unpad_cols(last))
    return feats, state


def _block_tables(counts):
    blk = MOE_ROWS
    counts = counts.reshape(N_EXPERTS).astype(jnp.int32)
    blk_count = (counts + blk - 1) // blk
    blk_end = jnp.cumsum(blk_count)
    first_blk = blk_end - blk_count
    return ((first_blk * blk).astype(F32).reshape(N_EXPERTS, 1), first_blk.astype(jnp.int32),
            blk_count.astype(jnp.int32), blk_end[-1:].astype(jnp.int32))


def kernel(x_prompt, x_sample, cache_fox_k, cache_fox_v, cache_fox_logf, state_rwkv_wkv, state_rwkv_shift, p_prompt, p_sample, norm_mix_g, w_in, rwkv_mu, rwkv_w0, rwkv_w_up, rwkv_a0, rwkv_a_up, rwkv_g_up, rwkv_k_k, rwkv_k_a, rwkv_r_k, rwkv_ln_w, rwkv_ln_b, fox_q_norm, fox_k_norm, fox_f_bias, w_out, norm_ffn_g, router_w, router_bias, exp_w_gate, exp_w_up, exp_w_down, shared_w_gate, shared_w_up, shared_w_down, ple_norm_g, ple_w_gate, ple_w_proj, final_norm_g):
    assert w_in.shape[0] == 1, "single-layer kernel"
    row = lambda a: a.reshape(1, -1).astype(F32)
    tile_heads = lambda a: jnp.tile(a.reshape(1, HEAD_DIM), (1, N_HEADS)).astype(F32)
    hid = jnp.arange(MXU_TILE) // HEAD_DIM
    w_in0 = w_in[0]
    router_hi = _bf(router_w[0])
    wts = {
        "gmix": row(norm_mix_g[0]),
        "w_rwkv": _bf(_rwkv_pad_cols(w_in0[:, :RWKV_IN])),
        "w_fox": _bf(_pad_cols(w_in0[:, RWKV_IN:], FOX_PAD)),
        "mu": row(_rwkv_pad_cols(rwkv_mu[0])),
        "w0": row(rwkv_w0[0]),
        "wup": _bf(_pad_rows(rwkv_w_up[0], 128)),
        "a0": row(rwkv_a0[0]),
        "aup": _bf(_pad_rows(rwkv_a_up[0], 128)),
        "gup": _bf(rwkv_g_up[0]),
        "k_k": row(rwkv_k_k[0]),
        "k_a": row(rwkv_k_a[0]),
        "r_k": row(rwkv_r_k[0]),
        "gs": _bf((hid[:, None] == hid[None, :]).astype(F32)),
        "qn": tile_heads(fox_q_norm[0]),
        "kn": tile_heads(fox_k_norm[0]),
        "fb": _pad_cols(row(fox_f_bias[0]), 128),
    }
    nbp, Tp, _ = x_prompt.shape
    nbs, Ts, _ = x_sample.shape
    s0_prompt = jnp.zeros((nbp, N_HEADS, HEAD_DIM, HEAD_DIM), F32)
    shift0_prompt = jnp.zeros((nbp, 1, RWKV_IN), F32)
    feats_p, st_p = _mixer(x_prompt, shift0_prompt, s0_prompt, None, wts, min(Tp, TOKEN_TILE))
    feats_s, st_s = _mixer(x_sample, state_rwkv_shift[0], state_rwkv_wkv[0],
                           (cache_fox_k[0], cache_fox_v[0], cache_fox_logf[0]), wts, Ts)

    n_p, n_s = nbp * Tp, nbs * Ts
    n_tok = n_p + n_s
    tm = math.gcd(math.gcd(n_p, n_s), TOKEN_TILE)
    streams = [(x_prompt.reshape(n_p, D_MODEL), x_sample.reshape(n_s, D_MODEL))] + list(zip(feats_p, feats_s))
    h1, xn2, scores = _out_proj(streams, row(rwkv_ln_w[0]), row(rwkv_ln_b[0]), wts["gs"], _bf(w_out[0]),
                                row(norm_ffn_g[0]), router_hi,
                                _bf(router_w[0] - router_hi.astype(F32)), tm)
    eidx_t, wts_t, rank_t, counts = _route(scores, router_bias[0].reshape(N_EXPERTS, 1).astype(F32), tm)
    n_blocks = -(-n_tok * TOP_K // MOE_ROWS) + N_EXPERTS
    start_col, first_blk, blk_count, n_used = _block_tables(counts)
    pos_t = _positions(eidx_t, rank_t, start_col, tm)
    xs, h1s = _dispatch(pos_t, xn2, h1, _bf(shared_w_gate[0]), _bf(shared_w_up[0]), _bf(shared_w_down[0]),
                        n_blocks * MOE_ROWS, tm)
    y_rows = _gmm(first_blk, blk_count, n_used, xs, exp_w_gate[0], exp_w_up[0], exp_w_down[0])
    p_pair = (p_prompt[0].reshape(n_p, PLE_DIM), p_sample[0].reshape(n_s, PLE_DIM))
    y_p, y_s = _final(pos_t, h1s, wts_t, p_pair, y_rows, row(ple_norm_g[0]), _bf(ple_w_gate[0]),
                      _bf(ple_w_proj[0]), row(final_norm_g), tm)
    y_prompt = y_p.reshape(nbp, Tp, D_MODEL)
    y_sample = y_s.reshape(nbs, Ts, D_MODEL)
    lead = lambda t: tuple(a[None] for a in t)
    return (y_prompt, y_sample) + lead(st_p) + lead(st_s)
```

```python
import functools
import math

import jax
import jax.numpy as jnp
from jax import lax
from jax.experimental import pallas as pl
from jax.experimental.pallas import tpu as pltpu

F32 = jnp.float32
BF16 = jnp.bfloat16

D_MODEL = 1024
HEAD_DIM = 64
RWKV_WIDTH = 512
FOX_WIDTH = 512
N_HEADS = 8
N_PAIRS = N_HEADS // 2
DECAY_RANK = 64
ICL_RANK = 64
GATE_RANK = 128
RWKV_IN = 3 * RWKV_WIDTH + DECAY_RANK + ICL_RANK + GATE_RANK
RWKV_PAD = 3 * RWKV_WIDTH + 3 * 128
FOX_PAD = 4 * FOX_WIDTH + 128
ATTN_SCALE = HEAD_DIM ** -0.5
N_EXPERTS = 256
N_GROUPS = 8
GROUP_SIZE = N_EXPERTS // N_GROUPS
TOPK_GROUPS = 4
TOP_K = 8
EXPERT_FF = 256
ROUTED_SCALE = 2.5
PLE_DIM = 256
RMS_EPS = 1e-6
GN_EPS = 64e-5
L2_EPS = 1e-12

LANES = 128
MXU_TILE = 256
ROW_CHUNKS = D_MODEL // LANES
PACK_CHUNKS = ROW_CHUNKS // 2
RWKV_CHUNK = 128
MOE_ROWS = 256
TOKEN_TILE = 256
CACHE_TILE = 512
SCAN_BATCH = 4
GMM_AHEAD = 4
GMM_IN_SLOTS = GMM_AHEAD + 2
GMM_OUT_SLOTS = 4
DMA_QUEUES = 2
COMBINE_UNROLL = 8
FOX_TQ, FOX_TK = 512, 1024
FOX_TK_CACHED = 768
VMEM_LIMIT = 56 * 1024 * 1024


def _cparams(*sem):
    return pltpu.CompilerParams(dimension_semantics=sem, vmem_limit_bytes=VMEM_LIMIT)


def _bf(x):
    return x.astype(BF16)


def _dot(a, b):
    return jnp.dot(a, b, preferred_element_type=F32)


def _dot_nt(a, b):
    return lax.dot_general(a, b, (((1,), (1,)), ((), ())), preferred_element_type=F32)


def _split2(x):
    hi = _bf(x)
    return hi, _bf(x - hi.astype(F32))


def _split3(x):
    hi = _bf(x)
    r1 = x - hi.astype(F32)
    mid = _bf(r1)
    return hi, mid, _bf(r1 - mid.astype(F32))


def _dot_x01(x, w01):
    hi, lo = _split2(x)
    slabs = [slice(c, c + MXU_TILE) for c in range(0, x.shape[1], MXU_TILE)]
    return jnp.concatenate([_dot(hi[:, s], w01) + _dot(lo[:, s], w01) for s in slabs], axis=1)


def _dot3(a, b):
    ah, al = _split2(a)
    bh, bl = _split2(b)
    return _dot(ah, bh) + _dot(al, bh) + _dot(ah, bl)


def _dot3_nt(a, b):
    ah, al = _split2(a)
    bh, bl = _split2(b)
    return _dot_nt(ah, bh) + _dot_nt(al, bh) + _dot_nt(ah, bl)


def _softplus(x):
    return jnp.maximum(x, 0.0) + jnp.log1p(jnp.exp(-jnp.abs(x)))


def _rms(x, g):
    return x * lax.rsqrt(jnp.mean(x * x, axis=-1, keepdims=True) + RMS_EPS) * g


def _full(shape):
    return pl.BlockSpec(shape, lambda *_: (0,) * len(shape))


def _store_chunked(ref, x):
    n = x.shape[0]
    for s in range(ROW_CHUNKS):
        ref[pl.ds(s, n, stride=ROW_CHUNKS), :] = x[:, s * LANES:(s + 1) * LANES]


def _load_chunked(ref, n):
    return jnp.concatenate([ref[pl.ds(s, n, stride=ROW_CHUNKS), :] for s in range(ROW_CHUNKS)], axis=1)


def _store_packed(ref, x):
    n, half = x.shape[0], D_MODEL // 2
    bits = lax.bitcast_convert_type(_bf(x).astype(F32), jnp.uint32)
    packed = (bits[:, :half] >> 16) | (bits[:, half:] & jnp.uint32(0xFFFF0000))
    for s in range(PACK_CHUNKS):
        ref[pl.ds(s, n, stride=PACK_CHUNKS), :] = packed[:, s * LANES:(s + 1) * LANES]


def _load_packed(ref, n):
    packed = jnp.concatenate([ref[pl.ds(s, n, stride=PACK_CHUNKS), :] for s in range(PACK_CHUNKS)], axis=1)
    lo = lax.bitcast_convert_type(packed << 16, F32)
    hi = lax.bitcast_convert_type(packed & jnp.uint32(0xFFFF0000), F32)
    return _bf(jnp.concatenate([lo, hi], axis=1))


def _in_rwkv_kernel(x_ref, shift_ref, gmix_ref, w_ref, mu_ref, w0_ref, wup_ref, a0_ref, aup_ref, gup_ref,
                    kk_ref, ka_ref, rk_ref, gs_ref,
                    r_out, lw_out, kh_out, v_out, kkn_out, bb_out, g_out, bonus_out, last_out,
                    carry_ref):
    j = pl.program_id(1)
    tm = x_ref.shape[1]
    xn = _bf(_rms(x_ref[0], gmix_ref[...]))
    u = _dot(xn, w_ref[...])
    first = jnp.where(j == 0, shift_ref[0], carry_ref[...])
    row = lax.broadcasted_iota(jnp.int32, (tm, 1), 0)
    prev = jnp.where(row == 0, first, pltpu.roll(u, 1, axis=0))
    carry_ref[...] = u[tm - 1:tm, :]
    last_out[0] = u[tm - 1:tm, :]
    xs = u + (prev - u) * mu_ref[...]
    W = RWKV_WIDTH
    r, k, v = xs[:, :W], xs[:, W:2 * W], xs[:, 2 * W:3 * W]
    xw, xa, xg = xs[:, 3 * W:3 * W + 128], xs[:, 3 * W + 128:3 * W + 256], xs[:, 3 * W + 256:]
    w_raw = w0_ref[...] + _dot(_bf(jnp.tanh(xw)), wup_ref[...])
    lw = -jnp.exp(-_softplus(-w_raw) - 0.5)
    a = jax.nn.sigmoid(a0_ref[...] + _dot(_bf(xa), aup_ref[...]))
    g = _dot(_bf(jax.nn.sigmoid(xg)), gup_ref[...])
    gs = gs_ref[...]
    kk = k * kk_ref[...]
    kkn = kk / jnp.maximum(jnp.sqrt(_dot_x01(kk * kk, gs)), L2_EPS)
    kh = k * (1.0 + (a - 1.0) * ka_ref[...])
    r_out[0] = r
    lw_out[0] = lw
    kh_out[0] = kh
    v_out[0] = v
    kkn_out[0] = kkn
    bb_out[0] = kkn * a
    g_out[0] = g
    bonus_out[0] = _dot_x01(r * kh * rk_ref[...], gs) * v


def _in_rwkv(x, shift, gmix, w, mu, w0, wup, a0, aup, gup, k_k, k_a, r_k, gs, tm):
    nb, T, _ = x.shape
    W = RWKV_WIDTH
    tok = lambda width: pl.BlockSpec((1, tm, width), lambda b, j: (b, j, 0))
    outs = [jax.ShapeDtypeStruct((nb, T, W), F32)] * 8 + [jax.ShapeDtypeStruct((nb, 1, RWKV_PAD), F32)]
    return pl.pallas_call(
        _in_rwkv_kernel,
        grid=(nb, T // tm),
        in_specs=[tok(D_MODEL), pl.BlockSpec((1, 1, RWKV_PAD), lambda b, j: (b, 0, 0)),
                  _full((1, D_MODEL)), _full((D_MODEL, RWKV_PAD)), _full((1, RWKV_PAD)),
                  _full((1, W)), _full((128, W)), _full((1, W)), _full((128, W)), _full((128, W)),
                  _full((1, W)), _full((1, W)), _full((1, W)), _full((MXU_TILE, MXU_TILE))],
        out_specs=[tok(W)] * 8 + [pl.BlockSpec((1, 1, RWKV_PAD), lambda b, j: (b, 0, 0))],
        out_shape=outs,
        scratch_shapes=[pltpu.VMEM((1, RWKV_PAD), F32)],
        compiler_params=_cparams("parallel", "arbitrary"),
        name="in_rwkv",
    )(x, shift, gmix, w, mu, w0, wup, a0, aup, gup, k_k, k_a, r_k, gs)


def _aug_rows(x, c, is_query):
    lane = lax.broadcasted_iota(jnp.int32, (1, LANES), 1)
    rows = []
    for p in range(N_PAIRS):
        xp = x[:, p * LANES:(p + 1) * LANES]
        xr = pltpu.roll(xp, HEAD_DIM, axis=1)
        for h in range(2):
            ch = c[:, 2 * p + h:2 * p + h + 1]
            hi = _bf(ch).astype(F32)
            r1 = ch - hi
            mid = _bf(r1).astype(F32)
            lo = _bf(r1 - mid).astype(F32)
            one = jnp.ones_like(ch)
            cols = (hi, mid, lo, one, one, one) if is_query else (one, one, one, -hi, -mid, -lo)
            aug = jnp.zeros_like(xp)
            for n, col in enumerate(cols):
                aug = jnp.where(lane == HEAD_DIM + n, col, aug)
            rows.append(_bf(jnp.where(lane < HEAD_DIM, xp if h == 0 else xr, aug)))
    return rows


def _in_fox_kernel(x_ref, gmix_ref, w_ref, qn_ref, kn_ref, fb_ref, gs_ref, tri_ref, c0_ref,
                   q_out, ka_out, k_out, v_out, vb_out, og_out, lf_out, carry_ref):
    j = pl.program_id(1)
    tm = x_ref.shape[1]

    @pl.when(j == 0)
    def _():
        carry_ref[...] = c0_ref[0]

    xn = _bf(_rms(x_ref[0], gmix_ref[...]))
    u = _dot(xn, w_ref[...])
    W = FOX_WIDTH
    q, k, v, og, fl = u[:, :W], u[:, W:2 * W], u[:, 2 * W:3 * W], u[:, 3 * W:4 * W], u[:, 4 * W:]
    gs = gs_ref[...]
    inv_d = 1.0 / HEAD_DIM
    qn = q * lax.rsqrt(_dot_x01(q * q, gs) * inv_d + RMS_EPS) * qn_ref[...]
    kn = k * lax.rsqrt(_dot_x01(k * k, gs) * inv_d + RMS_EPS) * kn_ref[...]
    logf = -_softplus(-(fl + fb_ref[...]))
    hi, mid, lo = _split3(logf)
    tri = tri_ref[...]
    c = _dot(tri, hi) + _dot(tri, mid) + _dot(tri, lo) + carry_ref[...]
    carry_ref[...] = c[tm - 1:tm, :]
    for h, row in enumerate(_aug_rows(_bf(qn * ATTN_SCALE).astype(F32), c, True)):
        q_out[0, h] = row
    for h, row in enumerate(_aug_rows(_bf(kn).astype(F32), c, False)):
        ka_out[0, h] = row
    k_out[0] = kn
    v_out[0] = v
    vb_out[0] = _bf(v)
    og_out[0] = og
    lf_out[0] = logf


def _in_fox(x, c0, gmix, w, qn, kn, fb, gs, tm):
    nb, T, _ = x.shape
    W = FOX_WIDTH
    tok = lambda width: pl.BlockSpec((1, tm, width), lambda b, j: (b, j, 0))
    aug = pl.BlockSpec((1, N_HEADS, tm, LANES), lambda b, j: (b, 0, j, 0))
    sds = lambda width, dt: jax.ShapeDtypeStruct((nb, T, width), dt)
    aug_shape = jax.ShapeDtypeStruct((nb, N_HEADS, T, LANES), BF16)
    tri = _bf(jnp.tril(jnp.ones((tm, tm), F32)))
    return pl.pallas_call(
        _in_fox_kernel,
        grid=(nb, T // tm),
        in_specs=[tok(D_MODEL), _full((1, D_MODEL)), _full((D_MODEL, FOX_PAD)),
                  _full((1, W)), _full((1, W)), _full((1, 128)), _full((MXU_TILE, MXU_TILE)), _full((tm, tm)),
                  pl.BlockSpec((1, 1, LANES), lambda b, j: (b, 0, 0))],
        out_specs=[aug, aug] + [tok(W)] * 4 + [tok(128)],
        out_shape=[aug_shape, aug_shape, sds(W, F32), sds(W, F32), sds(W, BF16), sds(W, F32), sds(128, F32)],
        scratch_shapes=[pltpu.VMEM((1, LANES), F32)],
        compiler_params=_cparams("parallel", "arbitrary"),
        name="in_fox",
    )(x, gmix, w, qn, kn, fb, gs, tri, c0)


def _cumsum_kernel(x_ref, tri_ref, o_ref, carry_ref):
    j = pl.program_id(1)
    ts = x_ref.shape[1]

    @pl.when(j == 0)
    def _():
        carry_ref[...] = jnp.zeros_like(carry_ref)

    hi, mid, lo = _split3(x_ref[0])
    tri = tri_ref[...]
    c = _dot(tri, hi) + _dot(tri, mid) + _dot(tri, lo) + carry_ref[...]
    o_ref[0] = c
    carry_ref[...] = c[ts - 1:ts, :]


def _cumsum(x, ts):
    nb, S, L = x.shape
    tri = _bf(jnp.tril(jnp.ones((ts, ts), F32)))
    return pl.pallas_call(
        _cumsum_kernel,
        grid=(nb, S // ts),
        in_specs=[pl.BlockSpec((1, ts, L), lambda b, j: (b, j, 0)), _full((ts, ts))],
        out_specs=pl.BlockSpec((1, ts, L), lambda b, j: (b, j, 0)),
        out_shape=jax.ShapeDtypeStruct((nb, S, L), F32),
        scratch_shapes=[pltpu.VMEM((1, L), F32)],
        compiler_params=_cparams("parallel", "arbitrary"),
        name="seq_cumsum",
    )(x, tri)


def _rwkv_chunk_kernel(r_ref, lw_ref, kh_ref, v_ref, kk_ref, bb_ref, tri_ref,
                       rh_out, yh_out, g_out, sh_out):
    C = r_ref.shape[1]
    ti = lax.broadcasted_iota(jnp.int32, (C, C), 0)
    si = lax.broadcasted_iota(jnp.int32, (C, C), 1)
    tx = ti ^ si
    strict = ti > si
    incl = ti >= si
    eye_c = (ti == si).astype(F32)
    lane = lax.broadcasted_iota(jnp.int32, (1, LANES), 1)
    head0 = lane < HEAD_DIM
    pi = lax.broadcasted_iota(jnp.int32, (LANES, LANES), 0)
    pj = lax.broadcasted_iota(jnp.int32, (LANES, LANES), 1)
    same_head = (pi < HEAD_DIM) == (pj < HEAD_DIM)
    eye_p = (pi == pj).astype(F32)
    tri = tri_ref[...]

    pairs = range(N_PAIRS)
    heads = [(p, h) for p in pairs for h in range(2)]
    P = []
    for p in pairs:
        sl = slice(p * LANES, (p + 1) * LANES)
        r, lw, kh, v, kk, bb = (ref[0, :, sl] for ref in (r_ref, lw_ref, kh_ref, v_ref, kk_ref, bb_ref))
        l_hi, l_mid, l_lo = _split3(lw)
        lc = _dot(tri, l_hi) + _dot(tri, l_mid) + _dot(tri, l_lo)
        mid = lc[C // 2 - 1:C // 2, :]
        last = lc[C - 1:C, :]
        e_dn = jnp.exp(mid - lc)
        e_end = jnp.exp(last - lc)
        aa = kk * jnp.exp(lc - lw - mid)
        P.append(dict(sl=sl, v=v, aa=aa, rt=r * jnp.exp(lc - mid), rho=jnp.exp(mid), g_last=jnp.exp(last),
                      bt_b=_bf(bb * e_dn), kt_b=_bf(kh * e_dn), bc_b=_bf(bb * e_end), kc_b=_bf(kh * e_end),
                      aa_b=_bf(aa), v_b=_bf(v)))
    lab, lak, mrb, mrk = {}, {}, {}, {}
    for p, h in heads:
        q = P[p]
        hm = head0 if h == 0 else jnp.logical_not(head0)
        aa_m = _bf(jnp.where(hm, q["aa"], 0.0))
        rt_m = _bf(jnp.where(hm, q["rt"], 0.0))
        lab[p, h] = jnp.where(strict, _dot_nt(aa_m, q["bt_b"]), 0.0)
        lak[p, h] = _bf(jnp.where(strict, _dot_nt(aa_m, q["kt_b"]), 0.0))
        mrb[p, h] = _bf(jnp.where(incl, _dot_nt(rt_m, q["bt_b"]), 0.0))
        mrk[p, h] = _bf(jnp.where(incl, _dot_nt(rt_m, q["kt_b"]), 0.0))
    d = {k: eye_c - jnp.where(tx < 2, lab[k], 0.0) for k in heads}
    s = 2
    while s < C:
        level = (tx >= s) & (tx < 2 * s)
        d_b = {k: _bf(d[k]) for k in heads}
        t1 = {k: _bf(_dot(d_b[k], _bf(jnp.where(level, lab[k], 0.0)))) for k in heads}
        d = {k: d[k] - _dot(t1[k], d_b[k]) for k in heads}
        s *= 2
    d_b = {k: _bf(d[k]) for k in heads}
    w = {k: _bf(_dot(lak[k], P[k[0]]["v_b"])) for k in heads}
    ah = {k: _dot(d_b[k], P[k[0]]["aa_b"]) * P[k[0]]["rho"] for k in heads}
    uh = {k: _dot(d_b[k], w[k]) for k in heads}
    rh = {k: P[k[0]]["rt"] * P[k[0]]["rho"] - _dot(mrb[k], _bf(ah[k])) for k in heads}
    yh = {k: _dot(mrk[k], P[k[0]]["v_b"]) - _dot(mrb[k], _bf(uh[k])) for k in heads}
    for p in pairs:
        q = P[p]
        both = lambda x: jnp.where(head0, x[p, 0], x[p, 1])
        ah_p, uh_p = both(ah), both(uh)
        rh_out[0, :, q["sl"]] = both(rh)
        yh_out[0, :, q["sl"]] = both(yh)
        g_full = eye_p * q["g_last"] - _dot(_bf(ah_p.T), q["bc_b"])
        sh_full = _dot(_bf(q["v"].T), q["kc_b"]) - _dot(_bf(uh_p.T), q["bc_b"])
        g_out[0, 0, p] = jnp.where(same_head, g_full, 0.0)
        sh_out[0, 0, p] = jnp.where(same_head, sh_full, 0.0)


def _rwkv_chunks(r, lw, kh, v, kk, bb):
    nb, T, W = r.shape
    C = RWKV_CHUNK
    nc = T // C
    tri = _bf(jnp.tril(jnp.ones((C, C), F32)))
    tok = pl.BlockSpec((1, C, W), lambda b, c: (b, c, 0))
    mat = pl.BlockSpec((1, 1, N_PAIRS, LANES, LANES), lambda b, c: (b, c, 0, 0, 0))
    mat_shape = jax.ShapeDtypeStruct((nb, nc, N_PAIRS, LANES, LANES), F32)
    return pl.pallas_call(
        _rwkv_chunk_kernel,
        grid=(nb, nc),
        in_specs=[tok] * 6 + [_full((C, C))],
        out_specs=[tok, tok, mat, mat],
        out_shape=[jax.ShapeDtypeStruct((nb, T, W), F32)] * 2 + [mat_shape] * 2,
        compiler_params=_cparams("parallel", "parallel"),
        name="rwkv_chunks",
    )(r, lw, kh, v, kk, bb, tri)


def _rwkv_scan_kernel(rh_ref, yh_ref, g_ref, sh_ref, s0_ref, y_out, sfin_out, s_scr):
    c = pl.program_id(1)
    nbg = rh_ref.shape[0]

    @pl.when(c == 0)
    def _():
        s_scr[...] = s0_ref[...]

    for b in range(nbg):
        for p in range(N_PAIRS):
            sl = slice(p * LANES, (p + 1) * LANES)
            s = s_scr[b, p]
            y_out[b, :, sl] = _dot3_nt(rh_ref[b, :, sl], s) + yh_ref[b, :, sl]
            s_new = _dot3(s, g_ref[b, 0, p]) + sh_ref[b, 0, p]
            s_scr[b, p] = s_new
            sfin_out[b, p] = s_new


def _rwkv_scan(rh, yh, g, sh, s0, nbg):
    nb, T, W = rh.shape
    C = RWKV_CHUNK
    nc = T // C
    tok = pl.BlockSpec((nbg, C, W), lambda i, c: (i, c, 0))
    mat = pl.BlockSpec((nbg, 1, N_PAIRS, LANES, LANES), lambda i, c: (i, c, 0, 0, 0))
    st = pl.BlockSpec((nbg, N_PAIRS, LANES, LANES), lambda i, c: (i, 0, 0, 0))
    return pl.pallas_call(
        _rwkv_scan_kernel,
        grid=(nb // nbg, nc),
        in_specs=[tok, tok, mat, mat, st],
        out_specs=[tok, st],
        out_shape=[jax.ShapeDtypeStruct((nb, T, W), F32),
                   jax.ShapeDtypeStruct((nb, N_PAIRS, LANES, LANES), F32)],
        scratch_shapes=[pltpu.VMEM((nbg, N_PAIRS, LANES, LANES), F32)],
        compiler_params=_cparams("parallel", "arbitrary"),
        name="rwkv_scan",
    )(rh, yh, g, sh, s0)


def _state_to_pairs(s):
    nb = s.shape[0]
    s = s.reshape(nb, N_PAIRS, 2, HEAD_DIM, HEAD_DIM)
    z = jnp.zeros_like(s[:, :, 0])
    top = jnp.concatenate([s[:, :, 0], z], axis=-1)
    bot = jnp.concatenate([z, s[:, :, 1]], axis=-1)
    return jnp.concatenate([top, bot], axis=-2)


def _pairs_to_state(sp):
    nb = sp.shape[0]
    a = sp[:, :, :HEAD_DIM, :HEAD_DIM]
    b = sp[:, :, HEAD_DIM:, HEAD_DIM:]
    return jnp.stack([a, b], axis=2).reshape(nb, N_HEADS, HEAD_DIM, HEAD_DIM)


def _fox_aug_kernel(x_ref, c_ref, o_ref):
    for h, row in enumerate(_aug_rows(x_ref[0].astype(F32), c_ref[0], False)):
        o_ref[0, h] = row


def _fox_aug(x, c, tm):
    nb, S, W = x.shape
    return pl.pallas_call(
        _fox_aug_kernel,
        grid=(nb, S // tm),
        in_specs=[pl.BlockSpec((1, tm, W), lambda b, j: (b, j, 0)),
                  pl.BlockSpec((1, tm, LANES), lambda b, j: (b, j, 0))],
        out_specs=pl.BlockSpec((1, N_HEADS, tm, LANES), lambda b, j: (b, 0, j, 0)),
        out_shape=jax.ShapeDtypeStruct((nb, N_HEADS, S, LANES), BF16),
        compiler_params=_cparams("parallel", "parallel"),
        name="fox_aug",
    )(x, c)


def _fox_kernel(q_ref, k_ref, v_ref, o_ref, m_scr, l_scr, acc_scr, sa_scr, sb_scr, *, tk, q_off):
    i = pl.program_id(2)
    tq = q_ref.shape[2]
    q_start = q_off + i * tq
    m_scr[...] = jnp.full_like(m_scr, -jnp.inf)
    l_scr[...] = jnp.zeros_like(l_scr)
    acc_scr[...] = jnp.zeros_like(acc_scr)
    n_full = (q_start + 1) // tk

    def scores_into(j, dst):
        ks = pl.multiple_of(j * tk, tk)
        for h in range(2):
            dst[h] = _dot_nt(q_ref[0, h], k_ref[0, h, pl.ds(ks, tk), :])

    def update_from(j, src, masked, width=tk):
        ks = pl.multiple_of(j * tk, tk)
        v = v_ref[0, pl.ds(ks, width), :]
        if masked:
            visible = (ks + lax.broadcasted_iota(jnp.int32, (1, width), 1)) <= (
                q_start + lax.broadcasted_iota(jnp.int32, (tq, 1), 0))
        for h in range(2):
            s = src[h, :, :width]
            if masked:
                s = jnp.where(visible, s, -jnp.inf)
            m_old = m_scr[h]
            m_new = jnp.maximum(m_old, jnp.max(s, axis=-1, keepdims=True))
            alpha = jnp.exp(m_old - m_new)
            pr = jnp.exp(s - m_new)
            l_scr[h] = alpha * l_scr[h] + jnp.sum(pr, axis=-1, keepdims=True)
            acc_scr[h] = alpha * acc_scr[h] + _dot(_bf(pr), v)
            m_scr[h] = m_new

    def step(j, src, dst):
        scores_into(j + 1, dst)
        update_from(j, src, False)

    def two_steps(jj, carry):
        step(2 * jj, sa_scr, sb_scr)
        step(2 * jj + 1, sb_scr, sa_scr)
        return carry

    scores_into(0, sa_scr)
    lax.fori_loop(0, n_full // 2, two_steps, 0)
    odd = n_full % 2 == 1

    def diagonal_block(src):
        half = q_start + tq - n_full * tk <= tk // 2

        @pl.when(half)
        def _():
            update_from(n_full, src, True, tk // 2)

        @pl.when(jnp.logical_not(half))
        def _():
            update_from(n_full, src, True)

    @pl.when(odd)
    def _():
        step(n_full - 1, sa_scr, sb_scr)
        diagonal_block(sb_scr)

    @pl.when(jnp.logical_not(odd))
    def _():
        diagonal_block(sa_scr)

    head0 = lax.broadcasted_iota(jnp.int32, (1, LANES), 1) < HEAD_DIM
    o_ref[0] = jnp.where(head0, acc_scr[0] / l_scr[0], acc_scr[1] / l_scr[1])


def _fox_attention(q, k, v, tq, tk, q_off):
    nb, _, sq, _ = q.shape
    sk = k.shape[2]
    for q_start in range(q_off, q_off + sq, tq):
        assert (q_start + 1) // tk + 1 == -(-(q_start + tq) // tk) <= sk // tk, (q_start, tq, tk)
    return pl.pallas_call(
        functools.partial(_fox_kernel, tk=tk, q_off=q_off),
        grid=(nb, N_PAIRS, sq // tq),
        in_specs=[pl.BlockSpec((1, 2, tq, LANES), lambda b, p, i: (b, p, i, 0)),
                  pl.BlockSpec((1, 2, sk, LANES), lambda b, p, i: (b, p, 0, 0)),
                  pl.BlockSpec((1, sk, LANES), lambda b, p, i: (b, 0, p))],
        out_specs=pl.BlockSpec((1, tq, LANES), lambda b, p, i: (b, i, p)),
        out_shape=jax.ShapeDtypeStruct((nb, sq, FOX_WIDTH), F32),
        scratch_shapes=[pltpu.VMEM((2, tq, 1), F32), pltpu.VMEM((2, tq, 1), F32),
                        pltpu.VMEM((2, tq, LANES), F32), pltpu.VMEM((2, tq, tk), F32),
                        pltpu.VMEM((2, tq, tk), F32)],
        compiler_params=_cparams("parallel", "parallel", "arbitrary"),
        name="fox_attention",
    )(q, k, v)


def _out_kernel(*refs, n_first):
    tok_refs, rest = refs[:12], refs[12:]
    lnw_ref, lnb_ref, gs_ref, wout_ref, gffn_ref, rwh_ref, rwl_ref, h_out, xn_out, sc_out = rest
    first = pl.program_id(0) < n_first
    x, y, bonus, g, oa, og = (jnp.where(first, tok_refs[2 * n][...], tok_refs[2 * n + 1][...]) for n in range(6))
    gs = gs_ref[...]
    inv_d = 1.0 / HEAD_DIM
    mean = _dot_x01(y, gs) * inv_d
    d = y - mean
    var = _dot_x01(d * d, gs) * inv_d
    yn = d * lax.rsqrt(var + GN_EPS) * lnw_ref[...] + lnb_ref[...]
    o_rwkv = (yn + bonus) * g
    o_fox = oa * jax.nn.sigmoid(og)
    mix = jnp.concatenate([_bf(o_rwkv), _bf(o_fox)], axis=-1)
    h = x + _dot(mix, wout_ref[...])
    h_out[...] = h
    xn = _rms(h, gffn_ref[...])
    _store_packed(xn_out, xn)
    xh, xl = _split2(xn)
    logits = _dot(xh, rwh_ref[...]) + _dot(xl, rwh_ref[...]) + _dot(xh, rwl_ref[...])
    sc_out[...] = jax.nn.sigmoid(logits)


def _two_streams(width, tm, n_first):
    return [pl.BlockSpec((tm, width), lambda i: (jnp.minimum(i, n_first - 1), 0)),
            pl.BlockSpec((tm, width), lambda i: (jnp.maximum(i - n_first, 0), 0))]


def _out_proj(streams, lnw, lnb, gs, wout, gffn, rwh, rwl, tm):
    n_first = streams[0][0].shape[0] // tm
    T = streams[0][0].shape[0] + streams[0][1].shape[0]
    W = RWKV_WIDTH
    tok = lambda width: pl.BlockSpec((tm, width), lambda i: (i, 0))
    tok_specs, tok_args = [], []
    for a, b in streams:
        tok_specs += _two_streams(a.shape[1], tm, n_first)
        tok_args += [a, b]
    return pl.pallas_call(
        functools.partial(_out_kernel, n_first=n_first),
        grid=(T // tm,),
        in_specs=tok_specs + [_full((1, W)), _full((1, W)), _full((MXU_TILE, MXU_TILE)),
                              _full((D_MODEL, D_MODEL)), _full((1, D_MODEL)),
                              _full((D_MODEL, N_EXPERTS)), _full((D_MODEL, N_EXPERTS))],
        out_specs=[tok(D_MODEL), pl.BlockSpec((tm * PACK_CHUNKS, LANES), lambda i: (i, 0)), tok(N_EXPERTS)],
        out_shape=[jax.ShapeDtypeStruct((T, D_MODEL), F32),
                   jax.ShapeDtypeStruct((T * PACK_CHUNKS, LANES), jnp.uint32),
                   jax.ShapeDtypeStruct((T, N_EXPERTS), F32)],
        compiler_params=_cparams("parallel"),
        name="out_proj",
    )(*tok_args, lnw, lnb, gs, wout, gffn, rwh, rwl)


def _route_kernel(sc_ref, bias_ref, before_ref, idx_out, wt_out, rank_out, cnt_out, cnt_scr):
    tm = sc_ref.shape[0]
    neg = -jnp.inf

    @pl.when(pl.program_id(0) == 0)
    def _():
        cnt_scr[...] = jnp.zeros_like(cnt_scr)

    st = sc_ref[...].T
    sel = st + bias_ref[...]
    gscore = []
    for gi in range(N_GROUPS):
        blk = sel[gi * GROUP_SIZE:(gi + 1) * GROUP_SIZE, :]
        m1 = jnp.max(blk, axis=0, keepdims=True)
        n1 = jnp.sum((blk == m1).astype(F32), axis=0, keepdims=True)
        m2 = jnp.max(jnp.where(blk < m1, blk, neg), axis=0, keepdims=True)
        gscore.append(m1 + jnp.where(n1 > 1.0, m1, m2))
    taken = [jnp.zeros((1, tm), jnp.bool_) for _ in range(N_GROUPS)]
    for _ in range(TOPK_GROUPS):
        avail = [jnp.where(taken[gi], neg, gscore[gi]) for gi in range(N_GROUPS)]
        best = functools.reduce(jnp.maximum, avail)
        found = jnp.zeros((1, tm), jnp.bool_)
        for gi in range(N_GROUPS):
            hit = (avail[gi] == best) & jnp.logical_not(found)
            taken[gi] = taken[gi] | hit
            found = found | hit
    cand = jnp.concatenate(
        [jnp.where(taken[gi], sel[gi * GROUP_SIZE:(gi + 1) * GROUP_SIZE, :], neg) for gi in range(N_GROUPS)], axis=0)
    eid = lax.broadcasted_iota(jnp.int32, (N_EXPERTS, tm), 0).astype(F32)
    idxs, wts = [], []
    onehot = jnp.zeros((N_EXPERTS, tm), F32)
    for _ in range(TOP_K):
        best = jnp.max(cand, axis=0, keepdims=True)
        pick = jnp.min(jnp.where(cand == best, eid, float(N_EXPERTS)), axis=0, keepdims=True)
        chosen = eid == pick
        wts.append(jnp.sum(jnp.where(chosen, st, 0.0), axis=0, keepdims=True))
        idxs.append(pick)
        cand = jnp.where(chosen, neg, cand)
        onehot = jnp.where(chosen, 1.0, onehot)
    w = jnp.concatenate(wts, axis=0)
    idx_out[...] = jnp.concatenate(idxs, axis=0).astype(jnp.int32)
    wt_out[...] = w / jnp.sum(w, axis=0, keepdims=True) * ROUTED_SCALE
    earlier = _dot(_bf(onehot), before_ref[...]) + cnt_scr[...]
    rank_out[...] = jnp.concatenate(
        [jnp.sum(jnp.where(eid == pick, earlier, 0.0), axis=0, keepdims=True) for pick in idxs],
        axis=0).astype(jnp.int32)
    cnt_scr[...] += jnp.sum(onehot, axis=1, keepdims=True)
    cnt_out[...] = cnt_scr[...]


def _route(scores, bias_col, tm):
    T = scores.shape[0]
    before = _bf(jnp.triu(jnp.ones((tm, tm), F32), 1))
    tok = pl.BlockSpec((TOP_K, tm), lambda i: (0, i))
    return pl.pallas_call(
        _route_kernel,
        grid=(T // tm,),
        in_specs=[pl.BlockSpec((tm, N_EXPERTS), lambda i: (i, 0)), _full((N_EXPERTS, 1)), _full((tm, tm))],
        out_specs=[tok, tok, tok, _full((N_EXPERTS, 1))],
        out_shape=[jax.ShapeDtypeStruct((TOP_K, T), jnp.int32), jax.ShapeDtypeStruct((TOP_K, T), F32),
                   jax.ShapeDtypeStruct((TOP_K, T), jnp.int32), jax.ShapeDtypeStruct((N_EXPERTS, 1), F32)],
        scratch_shapes=[pltpu.VMEM((N_EXPERTS, 1), F32)],
        compiler_params=_cparams("arbitrary"),
        name="route",
    )(scores, bias_col, before)


def _pos_kernel(idx_ref, rank_ref, start_ref, pos_out):
    tm = idx_ref.shape[1]
    eid = lax.broadcasted_iota(jnp.int32, (N_EXPERTS, tm), 0)
    idx = idx_ref[...]
    start = start_ref[...]
    base = jnp.concatenate(
        [jnp.sum(jnp.where(eid == idx[k:k + 1, :], start, 0.0), axis=0, keepdims=True) for k in range(TOP_K)],
        axis=0)
    pos_out[...] = rank_ref[...] + base.astype(jnp.int32)


def _positions(eidx_t, rank_t, start_col, tm):
    T = eidx_t.shape[1]
    tok = pl.BlockSpec((TOP_K, tm), lambda i: (0, i))
    return pl.pallas_call(
        _pos_kernel,
        grid=(T // tm,),
        in_specs=[tok, tok, _full((N_EXPERTS, 1))],
        out_specs=tok,
        out_shape=jax.ShapeDtypeStruct((TOP_K, T), jnp.int32),
        compiler_params=_cparams("parallel"),
        name="moe_positions",
    )(eidx_t, rank_t, start_col)


def _dispatch_kernel(pos_ref, x_ref, h_ref, sg_ref, su_ref, sd_ref, xs_in, xs_out, hs_out, sem):
    del xs_in
    tm = pos_ref.shape[1]

    def issue(t, carry):
        src = x_ref.at[pl.ds(pl.multiple_of(t * PACK_CHUNKS, PACK_CHUNKS), PACK_CHUNKS), :]
        for k in range(TOP_K):
            row = pl.multiple_of(pos_ref[k, t] * PACK_CHUNKS, PACK_CHUNKS)
            pltpu.make_async_copy(src, xs_out.at[pl.ds(row, PACK_CHUNKS), :], sem).start(priority=k % DMA_QUEUES)
        return carry

    lax.fori_loop(0, tm, issue, 0)
    xb = _load_packed(x_ref, tm)
    hg = _dot(xb, sg_ref[...])
    hu = _dot(xb, su_ref[...])
    hs_out[...] = h_ref[...] + _dot(_bf(hg * jax.nn.sigmoid(hg) * hu), sd_ref[...])
    for k in range(TOP_K):
        pltpu.make_async_copy(x_ref, xs_out.at[pl.ds(0, tm * PACK_CHUNKS), :], sem).wait()


def _dispatch(pos_t, xn_packed, h, sg, su, sd, n_rows, tm):
    T = pos_t.shape[1]
    xs0 = jnp.zeros((n_rows * PACK_CHUNKS, LANES), jnp.uint32)
    return pl.pallas_call(
        _dispatch_kernel,
        grid=(T // tm,),
        in_specs=[pl.BlockSpec((TOP_K, tm), lambda i: (0, i), memory_space=pltpu.SMEM),
                  pl.BlockSpec((tm * PACK_CHUNKS, LANES), lambda i: (i, 0)),
                  pl.BlockSpec((tm, D_MODEL), lambda i: (i, 0)),
                  _full((D_MODEL, EXPERT_FF)), _full((D_MODEL, EXPERT_FF)), _full((EXPERT_FF, D_MODEL)),
                  pl.BlockSpec(memory_space=pl.ANY)],
        out_specs=[pl.BlockSpec(memory_space=pl.ANY), pl.BlockSpec((tm, D_MODEL), lambda i: (i, 0))],
        out_shape=[jax.ShapeDtypeStruct((n_rows * PACK_CHUNKS, LANES), jnp.uint32),
                   jax.ShapeDtypeStruct((T, D_MODEL), F32)],
        scratch_shapes=[pltpu.SemaphoreType.DMA(())],
        input_output_aliases={6: 0},
        compiler_params=_cparams("arbitrary"),
        name="moe_dispatch",
    )(pos_t, xn_packed, h, sg, su, sd, xs0)


def _gmm_kernel(first_ref, count_ref, used_ref, xs_hbm, wg_ref, wu_ref, wd_ref, y_hbm,
                xbuf, ybuf, wg_b, wu_b, wd_b, in_sems, out_sems):
    e = pl.program_id(0)
    bm = MOE_ROWS
    blk_rows = bm * ROW_CHUNKS
    n_used = used_ref[0]
    n_blocks = y_hbm.shape[0] // blk_rows

    in_rows = bm * PACK_CHUNKS

    def read(g, slot):
        return pltpu.make_async_copy(xs_hbm.at[pl.ds(pl.multiple_of(g * in_rows, in_rows), in_rows), :],
                                     xbuf.at[slot], in_sems.at[slot])

    def write(g, slot):
        return pltpu.make_async_copy(ybuf.at[slot],
                                     y_hbm.at[pl.ds(pl.multiple_of(g * blk_rows, blk_rows), blk_rows), :],
                                     out_sems.at[slot])

    @pl.when(e == 0)
    def _():
        for g0 in range(GMM_AHEAD):
            @pl.when(g0 < n_used)
            def _():
                read(g0, g0 % GMM_IN_SLOTS).start()

    wg_b[...] = _bf(wg_ref[0])
    wu_b[...] = _bf(wu_ref[0])
    wd_b[...] = _bf(wd_ref[0])

    def blocks(g, width):
        for j in range(width):
            read(g + j, (g + j) % GMM_IN_SLOTS).wait()
        for j in range(width):
            @pl.when(g + j + GMM_AHEAD < n_used)
            def _():
                read(g + j + GMM_AHEAD, (g + j + GMM_AHEAD) % GMM_IN_SLOTS).start(priority=1)

            @pl.when(g + j >= GMM_OUT_SLOTS)
            def _():
                write(g + j - GMM_OUT_SLOTS, (g + j) % GMM_OUT_SLOTS).wait()
        for j in range(width):
            xe = _load_packed(xbuf.at[(g + j) % GMM_IN_SLOTS], bm)
            hg = _dot(xe, wg_b[...])
            hu = _dot(xe, wu_b[...])
            _store_chunked(ybuf.at[(g + j) % GMM_OUT_SLOTS], _dot(_bf(hg * jax.nn.sigmoid(hg) * hu), wd_b[...]))
        for j in range(width):
            write(g + j, (g + j) % GMM_OUT_SLOTS).start()

    first, count = first_ref[e], count_ref[e]

    def pair(jj, carry):
        blocks(first + 2 * jj, 2)
        return carry

    lax.fori_loop(0, count // 2, pair, 0)

    @pl.when(count % 2 == 1)
    def _():
        blocks(first + count - 1, 1)

    @pl.when(e == pl.num_programs(0) - 1)
    def _():
        for back in range(GMM_OUT_SLOTS, 0, -1):
            @pl.when(n_used >= back)
            def _():
                write(n_used - back, (n_used - back) % GMM_OUT_SLOTS).wait()
        ybuf[0] = jnp.zeros_like(ybuf[0])

        def fill(g, carry):
            write(g, 0).start()
            return carry

        def drain(g, carry):
            write(g, 0).wait()
            return carry

        lax.fori_loop(n_used, n_blocks, fill, 0)
        lax.fori_loop(n_used, n_blocks, drain, 0)


def _gmm(first_blk, blk_count, n_used, xs, wg, wu, wd):
    bm = MOE_ROWS
    n_rows = xs.shape[0] // PACK_CHUNKS
    wspec = lambda shape: pl.BlockSpec((1,) + shape, lambda e, *_: (e, 0, 0))
    grid_spec = pltpu.PrefetchScalarGridSpec(
        num_scalar_prefetch=3,
        grid=(N_EXPERTS,),
        in_specs=[pl.BlockSpec(memory_space=pl.ANY), wspec((D_MODEL, EXPERT_FF)), wspec((D_MODEL, EXPERT_FF)),
                  wspec((EXPERT_FF, D_MODEL))],
        out_specs=pl.BlockSpec(memory_space=pl.ANY),
        scratch_shapes=[pltpu.VMEM((GMM_IN_SLOTS, bm * PACK_CHUNKS, LANES), jnp.uint32),
                        pltpu.VMEM((GMM_OUT_SLOTS, bm * ROW_CHUNKS, LANES), F32),
                        pltpu.VMEM((D_MODEL, EXPERT_FF), BF16), pltpu.VMEM((D_MODEL, EXPERT_FF), BF16),
                        pltpu.VMEM((EXPERT_FF, D_MODEL), BF16),
                        pltpu.SemaphoreType.DMA((GMM_IN_SLOTS,)), pltpu.SemaphoreType.DMA((GMM_OUT_SLOTS,))],
    )
    return pl.pallas_call(
        _gmm_kernel,
        grid_spec=grid_spec,
        out_shape=jax.ShapeDtypeStruct((n_rows * ROW_CHUNKS, LANES), F32),
        compiler_params=_cparams("arbitrary"),
        name="expert_gmm",
    )(first_blk, blk_count, n_used, xs, wg, wu, wd)


def _final_kernel(pos_ref, nxt_ref, hs_ref, w_ref, pa_ref, pb_ref, ys_ref, gple_ref, wpg_ref, wpp_ref, gfin_ref,
                  ya_out, yb_out, buf, routed_scr, sems, *, n_first):
    i = pl.program_id(0)
    n = pl.num_programs(0)
    tm = hs_ref.shape[0]

    def gather(rows_ref, slot):
        def issue(t, carry):
            dst = pl.ds(pl.multiple_of(t * ROW_CHUNKS, ROW_CHUNKS), ROW_CHUNKS)
            for k in range(TOP_K):
                row = pl.multiple_of(rows_ref[k, t] * ROW_CHUNKS, ROW_CHUNKS)
                pltpu.make_async_copy(ys_ref.at[pl.ds(row, ROW_CHUNKS), :], buf.at[slot, k, dst, :],
                                      sems.at[slot, k]).start(priority=k % DMA_QUEUES)
            return carry

        lax.fori_loop(0, tm, issue, 0)

    slot = i % 2

    @pl.when(i == 0)
    def _():
        gather(pos_ref, 0)

    for nxt in range(2):
        @pl.when((i + 1 < n) & (slot != nxt))
        def _():
            gather(nxt_ref, nxt)

    pp = _dot(_bf(jnp.where(i < n_first, pa_ref[...], pb_ref[...])), wpp_ref[...])
    for k in range(TOP_K):
        pltpu.make_async_copy(ys_ref.at[pl.ds(0, tm * ROW_CHUNKS), :], buf.at[slot, k], sems.at[slot, k]).wait()
    def combine(t, carry):
        tile = pl.ds(pl.multiple_of(t * ROW_CHUNKS, ROW_CHUNKS), ROW_CHUNKS)
        acc = buf[slot, 0, tile, :] * w_ref[0, t]
        for k in range(1, TOP_K):
            acc = acc + buf[slot, k, tile, :] * w_ref[k, t]
        routed_scr[tile, :] = acc
        return carry

    lax.fori_loop(0, tm, combine, 0, unroll=COMBINE_UNROLL)
    h = hs_ref[...] + _load_chunked(routed_scr, tm)
    gate = jax.nn.sigmoid(_dot(_bf(_rms(h, gple_ref[...])), wpg_ref[...]))
    y = _rms(h + gate * pp, gfin_ref[...])

    @pl.when(i < n_first)
    def _():
        ya_out[...] = y

    @pl.when(i >= n_first)
    def _():
        yb_out[...] = y


def _final(pos_t, hs, w, p_pair, ys, gple, wpg, wpp, gfin, tm):
    T = hs.shape[0]
    n = T // tm
    n_first = p_pair[0].shape[0] // tm
    tok = lambda width: pl.BlockSpec((tm, width), lambda i: (i, 0))
    return pl.pallas_call(
        functools.partial(_final_kernel, n_first=n_first),
        grid=(n,),
        in_specs=[pl.BlockSpec((TOP_K, tm), lambda i: (0, i), memory_space=pltpu.SMEM),
                  pl.BlockSpec((TOP_K, tm), lambda i: (0, jnp.minimum(i + 1, n - 1)), memory_space=pltpu.SMEM),
                  tok(D_MODEL), pl.BlockSpec((TOP_K, tm), lambda i: (0, i), memory_space=pltpu.SMEM)]
                 + _two_streams(PLE_DIM, tm, n_first) + [
                  pl.BlockSpec(memory_space=pl.ANY),
                  _full((1, D_MODEL)), _full((D_MODEL, D_MODEL)), _full((PLE_DIM, D_MODEL)),
                  _full((1, D_MODEL))],
        out_specs=_two_streams(D_MODEL, tm, n_first),
        out_shape=[jax.ShapeDtypeStruct((n_first * tm, D_MODEL), F32),
                   jax.ShapeDtypeStruct((T - n_first * tm, D_MODEL), F32)],
        scratch_shapes=[pltpu.VMEM((2, TOP_K, tm * ROW_CHUNKS, LANES), F32),
                        pltpu.VMEM((tm * ROW_CHUNKS, LANES), F32),
                        pltpu.SemaphoreType.DMA((2, TOP_K))],
        compiler_params=_cparams("arbitrary"),
        name="ffn_tail",
    )(pos_t, pos_t, hs, w, *p_pair, ys, gple, wpg, wpp, gfin)


def _pad_cols(a, width):
    return jnp.pad(a, [(0, 0)] * (a.ndim - 1) + [(0, width - a.shape[-1])])


def _rwkv_pad_cols(a):
    W = RWKV_WIDTH
    o1, o2, o3 = 3 * W, 3 * W + DECAY_RANK, 3 * W + DECAY_RANK + ICL_RANK
    return jnp.concatenate([a[..., :o1], _pad_cols(a[..., o1:o2], 128), _pad_cols(a[..., o2:o3], 128),
                            a[..., o3:]], axis=-1)


def _rwkv_unpad_cols(a):
    W = RWKV_WIDTH
    return jnp.concatenate([a[..., :3 * W + DECAY_RANK], a[..., 3 * W + 128:3 * W + 128 + ICL_RANK],
                            a[..., 3 * W + 256:]], axis=-1)


def _pad_rows(a, rows):
    return jnp.pad(a, [(0, rows - a.shape[0])] + [(0, 0)] * (a.ndim - 1))


def _mixer(x, shift, s0, past, wts, tm):
    nb, T, _ = x.shape
    (r, lw, kh, v, kkn, bb, g, bonus, last) = _in_rwkv(
        x, _rwkv_pad_cols(shift), wts["gmix"], wts["w_rwkv"], wts["mu"], wts["w0"], wts["wup"], wts["a0"],
        wts["aup"], wts["gup"], wts["k_k"], wts["k_a"], wts["r_k"], wts["gs"], tm)
    if past is None:
        c0 = jnp.zeros((nb, 1, LANES), F32)
    else:
        k_past, v_past, lf_past = past
        P = k_past.shape[1]
        c_past = _cumsum(_pad_cols(lf_past.astype(F32), LANES), math.gcd(P, CACHE_TILE))
        c0 = c_past[:, P - 1:, :]
    q_aug, k_aug, k_f, v_f, v_b, og, logf = _in_fox(
        x, c0, wts["gmix"], wts["w_fox"], wts["qn"], wts["kn"], wts["fb"], wts["gs"], tm)

    C = RWKV_CHUNK
    Tp = -(-T // C) * C
    if Tp != T:
        padt = lambda a: jnp.pad(a, ((0, 0), (0, Tp - T), (0, 0)))
        r_p, lw_p, kh_p, v_p, kk_p, bb_p = (padt(a) for a in (r, lw, kh, v, kkn, bb))
    else:
        r_p, lw_p, kh_p, v_p, kk_p, bb_p = r, lw, kh, v, kkn, bb
    rh, yh, gm, sh = _rwkv_chunks(r_p, lw_p, kh_p, v_p, kk_p, bb_p)
    y, s_fin = _rwkv_scan(rh, yh, gm, sh, _state_to_pairs(s0.astype(F32)), math.gcd(nb, SCAN_BATCH))
    y = y[:, :T]

    if past is None:
        o_att = _fox_attention(q_aug, k_aug, v_b, min(T, FOX_TQ), min(T, FOX_TK), 0)
    else:
        tk = FOX_TK_CACHED
        sk = -(-(P + T) // tk) * tk
        k_aug_past = _fox_aug(_bf(k_past.reshape(nb, P, FOX_WIDTH)), c_past, math.gcd(P, CACHE_TILE))
        k_all = jnp.pad(jnp.concatenate([k_aug_past, k_aug], axis=2), ((0, 0), (0, 0), (0, sk - P - T), (0, 0)))
        v_all = jnp.pad(jnp.concatenate([_bf(v_past.reshape(nb, P, FOX_WIDTH)), v_b], axis=1),
                        ((0, 0), (0, sk - P - T), (0, 0)))
        o_att = _fox_attention(q_aug, k_all, v_all, T, tk, P)

    n = nb * T
    flat = lambda a: a.reshape(n, a.shape[-1])
    feats = (flat(y), flat(bonus), flat(g), flat(o_att), flat(og))
    state = (k_f.reshape(nb, T, N_HEADS, HEAD_DIM), v_f.reshape(nb, T, N_HEADS, HEAD_DIM),
             logf[:, :, :N_HEADS], _pairs_to_state(s_fin), _rwkv_unpad_cols(last))
    return feats, state


def _block_tables(counts):
    blk = MOE_ROWS
    counts = counts.reshape(N_EXPERTS).astype(jnp.int32)
    blk_count = (counts + blk - 1) // blk
    blk_end = jnp.cumsum(blk_count)
    first_blk = blk_end - blk_count
    return ((first_blk * blk).astype(F32).reshape(N_EXPERTS, 1), first_blk.astype(jnp.int32),
            blk_count.astype(jnp.int32), blk_end[-1:].astype(jnp.int32))


def kernel(x_prompt, x_sample, cache_fox_k, cache_fox_v, cache_fox_logf, state_rwkv_wkv, state_rwkv_shift, p_prompt, p_sample, norm_mix_g, w_in, rwkv_mu, rwkv_w0, rwkv_w_up, rwkv_a0, rwkv_a_up, rwkv_g_up, rwkv_k_k, rwkv_k_a, rwkv_r_k, rwkv_ln_w, rwkv_ln_b, fox_q_norm, fox_k_norm, fox_f_bias, w_out, norm_ffn_g, router_w, router_bias, exp_w_gate, exp_w_up, exp_w_down, shared_w_gate, shared_w_up, shared_w_down, ple_norm_g, ple_w_gate, ple_w_proj, final_norm_g):
    assert w_in.shape[0] == 1, "single-layer kernel"
    row = lambda a: a.reshape(1, -1).astype(F32)
    tile_heads = lambda a: jnp.tile(a.reshape(1, HEAD_DIM), (1, N_HEADS)).astype(F32)
    hid = jnp.arange(MXU_TILE) // HEAD_DIM
    w_in0 = w_in[0]
    router_hi = _bf(router_w[0])
    wts = {
        "gmix": row(norm_mix_g[0]),
        "w_rwkv": _bf(_rwkv_pad_cols(w_in0[:, :RWKV_IN])),
        "w_fox": _bf(_pad_cols(w_in0[:, RWKV_IN:], FOX_PAD)),
        "mu": row(_rwkv_pad_cols(rwkv_mu[0])),
        "w0": row(rwkv_w0[0]),
        "wup": _bf(_pad_rows(rwkv_w_up[0], 128)),
        "a0": row(rwkv_a0[0]),
        "aup": _bf(_pad_rows(rwkv_a_up[0], 128)),
        "gup": _bf(rwkv_g_up[0]),
        "k_k": row(rwkv_k_k[0]),
        "k_a": row(rwkv_k_a[0]),
        "r_k": row(rwkv_r_k[0]),
        "gs": _bf((hid[:, None] == hid[None, :]).astype(F32)),
        "qn": tile_heads(fox_q_norm[0]),
        "kn": tile_heads(fox_k_norm[0]),
        "fb": _pad_cols(row(fox_f_bias[0]), 128),
    }
    nbp, Tp, _ = x_prompt.shape
    nbs, Ts, _ = x_sample.shape
    s0_prompt = jnp.zeros((nbp, N_HEADS, HEAD_DIM, HEAD_DIM), F32)
    shift0_prompt = jnp.zeros((nbp, 1, RWKV_IN), F32)
    feats_p, st_p = _mixer(x_prompt, shift0_prompt, s0_prompt, None, wts, min(Tp, TOKEN_TILE))
    feats_s, st_s = _mixer(x_sample, state_rwkv_shift[0], state_rwkv_wkv[0],
                           (cache_fox_k[0], cache_fox_v[0], cache_fox_logf[0]), wts, Ts)

    n_p, n_s = nbp * Tp, nbs * Ts
    n_tok = n_p + n_s
    tm = math.gcd(math.gcd(n_p, n_s), TOKEN_TILE)
    streams = [(x_prompt.reshape(n_p, D_MODEL), x_sample.reshape(n_s, D_MODEL))] + list(zip(feats_p, feats_s))
    h1, xn2, scores = _out_proj(streams, row(rwkv_ln_w[0]), row(rwkv_ln_b[0]), wts["gs"], _bf(w_out[0]),
                                row(norm_ffn_g[0]), router_hi,
                                _bf(router_w[0] - router_hi.astype(F32)), tm)
    eidx_t, wts_t, rank_t, counts = _route(scores, router_bias[0].reshape(N_EXPERTS, 1).astype(F32), tm)
    n_blocks = -(-n_tok * TOP_K // MOE_ROWS) + N_EXPERTS
    start_col, first_blk, blk_count, n_used = _block_tables(counts)
    pos_t = _positions(eidx_t, rank_t, start_col, tm)
    xs, h1s = _dispatch(pos_t, xn2, h1, _bf(shared_w_gate[0]), _bf(shared_w_up[0]), _bf(shared_w_down[0]),
                        n_blocks * MOE_ROWS, tm)
    y_rows = _gmm(first_blk, blk_count, n_used, xs, exp_w_gate[0], exp_w_up[0], exp_w_down[0])
    p_pair = (p_prompt[0].reshape(n_p, PLE_DIM), p_sample[0].reshape(n_s, PLE_DIM))
    y_p, y_s = _final(pos_t, h1s, wts_t, p_pair, y_rows, row(ple_norm_g[0]), _bf(ple_w_gate[0]),
                      _bf(ple_w_proj[0]), row(final_norm_g), tm)
    y_prompt = y_p.reshape(nbp, Tp, D_MODEL)
    y_sample = y_s.reshape(nbs, Ts, D_MODEL)
    lead = lambda t: tuple(a[None] for a in t)
    return (y_prompt, y_sample) + lead(st_p) + lead(st_s)
```
